```python
import math
import jax, jax.numpy as jnp
from jax import lax
import numpy as np


D_MODEL = 1024
BATCH = 4
SEQ = 8192
DEPTH = 4
DEC_BATCH = 2
DEC_SEQ = 16384
PAST_LEN = 128

A_HEADS = 4
A_QK_DIM = 64
A_V_DIM = 2 * A_QK_DIM
A_QK_WIDTH = A_HEADS * 2 * A_QK_DIM
A_WIDTH = A_HEADS * A_V_DIM
B_HEADS = 4
B_DIM = 128
B_WIDTH = B_HEADS * B_DIM
B_CONV = 3
N_GATES = 4
AB_WIDTHS = (A_QK_WIDTH, A_QK_WIDTH, A_WIDTH, 2 * B_WIDTH, B_WIDTH, B_WIDTH, B_HEADS * N_GATES)
AB_IN = 2 * A_QK_WIDTH + A_WIDTH + 4 * B_WIDTH + B_HEADS * N_GATES
AB_OUT = A_WIDTH + B_WIDTH
C_HEADS = 4
C_QK_DIM = D_MODEL // C_HEADS
C_V_DIM = 2 * C_QK_DIM
C_QK_WIDTH = C_HEADS * C_QK_DIM
C_V_WIDTH = C_HEADS * C_V_DIM
C_WIDTHS = (C_QK_WIDTH, C_QK_WIDTH, C_V_WIDTH, C_V_WIDTH)
C_IN = 2 * C_QK_WIDTH + 2 * C_V_WIDTH
CHUNK = 128
Q_BLOCK = 128
REL_BUCKETS = 32
REL_MAX_DIST = 128
N_EXPERTS = 16
N_GROUPS = 4
EXPERTS_PER_GROUP = N_EXPERTS // N_GROUPS
TOP_K = 2
D_FF = 512
ROPE_BASE = 10000.0
EPS = 1e-6
N_EVEN = (DEPTH + 1) // 2
N_ODD = DEPTH // 2

kernel_name = 'hybrid_bidir_diffattn_mlstm_retnet_moe'


def _split(x, widths):
    offsets = [int(o) for o in np.cumsum(widths)[:-1]]
    return jnp.split(x, offsets, axis=-1)


def rmsnorm(x, g):
    xf = x.astype(jnp.float32)
    y = xf * lax.rsqrt(jnp.mean(xf * xf, axis=-1, keepdims=True) + EPS)
    return (y * g.astype(jnp.float32)).astype(x.dtype)


def t5_bucket(rel):
    nb = REL_BUCKETS // 2
    max_exact = nb // 2
    ret = jnp.where(rel > 0, nb, 0)
    n = jnp.abs(rel)
    nf = jnp.maximum(n, 1).astype(jnp.float32)
    large = max_exact + (jnp.log(nf / max_exact) / math.log(REL_MAX_DIST / max_exact) * (nb - max_exact)).astype(jnp.int32)
    large = jnp.minimum(large, nb - 1)
    return ret + jnp.where(n < max_exact, n, large)


def rope(x):
    S, d = x.shape[1], x.shape[-1]
    inv = ROPE_BASE ** (-jnp.arange(0, d, 2, dtype=jnp.float32) / d)
    ang = jnp.arange(S, dtype=jnp.float32)[:, None] * inv[None, :]
    cos, sin = jnp.cos(ang)[:, None, :], jnp.sin(ang)[:, None, :]
    x1, x2 = jnp.split(x.astype(jnp.float32), 2, axis=-1)
    return jnp.concatenate([x1 * cos - x2 * sin, x1 * sin + x2 * cos], axis=-1).astype(x.dtype)


def centred_conv(x, w, b):
    ch = x.shape[-1]
    y = lax.conv_general_dilated(x, w[:, None, :].astype(x.dtype), window_strides=(1,), padding='SAME',
                                 dimension_numbers=('NWC', 'WIO', 'NWC'), feature_group_count=ch)
    return y + b.astype(x.dtype)


def diff_attention(q, k, v, lam, rel_bias):
    Bn, H, _, S, d = q.shape
    nq = S // Q_BLOCK
    q_blocks = jnp.moveaxis(q.reshape(Bn, H, 2, nq, Q_BLOCK, d), 3, 0)
    starts = jnp.arange(nq, dtype=jnp.int32) * Q_BLOCK
    key_pos = jnp.arange(S, dtype=jnp.int32)
    scale = d ** -0.5

    def block(args):
        qb, start = args
        s = jnp.einsum('bhmqd,bhmkd->bhmqk', qb, k).astype(jnp.float32) * scale
        rel = key_pos[None, :] - (start + jnp.arange(Q_BLOCK, dtype=jnp.int32))[:, None]
        bias = jnp.transpose(rel_bias[t5_bucket(rel)], (2, 0, 1)).astype(jnp.float32)
        p = jax.nn.softmax(s + bias[None, :, None], axis=-1)
        a = p[:, :, 0] - lam * p[:, :, 1]
        return jnp.einsum('bhqk,bhkv->bhqv', a.astype(v.dtype), v)

    out = lax.map(block, (q_blocks, starts))
    return jnp.moveaxis(out, 0, 2).reshape(Bn, H, S, v.shape[-1])


def _chunks(t):
    nc = t.shape[2] // CHUNK
    return jnp.moveaxis(t.reshape(t.shape[:2] + (nc, CHUNK) + t.shape[3:]), 2, 0)


def mlstm_chunkwise(q, k, v, i_pre, log_f):
    Bn, H, S, d = q.shape
    dv = v.shape[-1]
    mask = jnp.tril(jnp.ones((CHUNK, CHUNK), dtype=bool))

    def step(carry, inp):
        C, n, m = carry
        qc, kc, vc, ic, fc = inp
        b = jnp.cumsum(fc, axis=-1)
        log_d = jnp.where(mask, b[..., :, None] - b[..., None, :] + ic[..., None, :], -jnp.inf)
        m_inter = b + m[..., None]
        m_t = jnp.maximum(jnp.max(log_d, axis=-1), m_inter)
        s = jnp.einsum('bhtd,bhsd->bhts', qc, kc) * jnp.exp(log_d - m_t[..., None])
        inter = jnp.exp(m_inter - m_t)
        num = jnp.einsum('bhts,bhsv->bhtv', s, vc) + inter[..., None] * jnp.einsum('bhtd,bhdv->bhtv', qc, C)
        den = jnp.sum(s, axis=-1) + inter * jnp.einsum('bhtd,bhd->bht', qc, n)
        h = num / jnp.maximum(jnp.abs(den), jnp.exp(-m_t))[..., None]
        b_last = b[..., -1]
        log_w = b_last[..., None] - b + ic
        m_new = jnp.maximum(b_last + m, jnp.max(log_w, axis=-1))
        w = jnp.exp(log_w - m_new[..., None])
        decay = jnp.exp(b_last + m - m_new)
        C_new = decay[..., None, None] * C + jnp.einsum('bhsd,bhsv->bhdv', kc * w[..., None], vc)
        n_new = decay[..., None] * n + jnp.einsum('bhs,bhsd->bhd', w, kc)
        return (C_new, n_new, m_new), h

    init = (jnp.zeros((Bn, H, d, dv), jnp.float32), jnp.zeros((Bn, H, d), jnp.float32), jnp.zeros((Bn, H), jnp.float32))
    _, h = lax.scan(step, init, (_chunks(q), _chunks(k), _chunks(v), _chunks(i_pre), _chunks(log_f)))
    return jnp.moveaxis(h, 0, 2).reshape(Bn, H, S, dv)


def retention_chunkwise(q, k, v, log_gamma):
    Bn, H, S, dk = q.shape
    dv = v.shape[-1]
    pos = jnp.arange(CHUNK, dtype=jnp.float32)
    diff = pos[:, None] - pos[None, :]
    intra = jnp.where(diff >= 0, jnp.exp(jnp.maximum(diff, 0.0) * log_gamma[:, None, None]), 0.0)
    q_scale = jnp.exp((pos + 1.0) * log_gamma[:, None])
    k_scale = jnp.exp((CHUNK - 1.0 - pos) * log_gamma[:, None])
    c_decay = jnp.exp(CHUNK * log_gamma)

    def step(R, inp):
        qc, kc, vc = inp
        s = jnp.einsum('bhtd,bhsd->bhts', qc, kc) * intra
        y = jnp.einsum('bhts,bhsv->bhtv', s, vc) + q_scale[..., None] * jnp.einsum('bhtd,bhdv->bhtv', qc, R)
        R_new = c_decay[:, None, None] * R + jnp.einsum('bhsd,bhsv->bhdv', kc * k_scale[..., None], vc)
        return R_new, y

    init = jnp.zeros((Bn, H, dk, dv), jnp.float32)
    _, y = lax.scan(step, init, (_chunks(q), _chunks(k), _chunks(v)))
    return jnp.moveaxis(y, 0, 2).reshape(Bn, H, S, dv)


def _flip(t):
    return jnp.flip(t, axis=2)


def setup_inputs(seed: int = 0) -> dict:
    key = jax.random.key(seed)
    ks = jax.random.split(key, 32)
    f32 = jnp.float32
    nrm = lambda i, shape, s: jax.random.normal(ks[i], shape, f32) * s
    gate_base = np.zeros((B_HEADS, N_GATES), np.float32)
    fb = np.linspace(3.0, 6.0, B_HEADS).astype(np.float32)
    gate_base[:, 1] = fb
    gate_base[:, 3] = fb
    gamma = 1.0 - 2.0 ** (-5.0 - np.arange(C_HEADS, dtype=np.float32))
    decay_base = np.log(gamma / (1.0 - gamma)).astype(np.float32)
    return {
        'x_prompt': nrm(0, (BATCH, SEQ, D_MODEL), 1.0),
        'x_sample': nrm(1, (DEC_BATCH, DEC_SEQ, D_MODEL), 1.0),
        'c_prompt': nrm(2, (BATCH, D_MODEL), 1.0),
        'c_sample': nrm(3, (DEC_BATCH, D_MODEL), 1.0),
        'rel_bias': nrm(4, (REL_BUCKETS, A_HEADS), 0.2),
        'router_w': nrm(5, (D_MODEL, N_EXPERTS), D_MODEL ** -0.5),
        'router_b': nrm(6, (N_EXPERTS,), 0.01),
        'ada_w': nrm(7, (DEPTH, D_MODEL, 6 * D_MODEL), 0.5 * D_MODEL ** -0.5),
        'ada_b': nrm(8, (DEPTH, 6 * D_MODEL), 0.02),
        'norm_mix_g': 1.0 + nrm(9, (DEPTH, D_MODEL), 0.02),
        'norm_ffn_g': 1.0 + nrm(10, (DEPTH, D_MODEL), 0.02),
        'w_in_ab': nrm(11, (N_EVEN, D_MODEL, AB_IN), D_MODEL ** -0.5),
        'w_out_ab': nrm(12, (N_EVEN, AB_OUT, D_MODEL), AB_OUT ** -0.5),
        'q_norm_g': 1.0 + nrm(13, (N_EVEN, A_QK_DIM), 0.02),
        'k_norm_g': 1.0 + nrm(14, (N_EVEN, A_QK_DIM), 0.02),
        'diff_lambda': nrm(15, (N_EVEN, 4, A_QK_DIM), 0.1),
        'diff_norm_g': 1.0 + nrm(16, (N_EVEN, A_V_DIM), 0.02),
        'mlstm_conv_w': nrm(17, (N_EVEN, B_CONV, 2 * B_WIDTH), B_CONV ** -0.5),
        'mlstm_conv_b': nrm(18, (N_EVEN, 2 * B_WIDTH), 0.02),
        'mlstm_gate_b': jnp.asarray(gate_base)[None] + nrm(19, (N_EVEN, B_HEADS, N_GATES), 0.1),
        'mlstm_norm_g': 1.0 + nrm(20, (N_EVEN, B_DIM), 0.02),
        'w_in_c': nrm(21, (N_ODD, D_MODEL, C_IN), D_MODEL ** -0.5),
        'w_out_c': nrm(22, (N_ODD, C_V_WIDTH, D_MODEL), C_V_WIDTH ** -0.5),
        'ret_decay_logit': jnp.asarray(decay_base)[None, None] + nrm(23, (N_ODD, 2, C_HEADS), 0.05),
        'ret_norm_g': 1.0 + nrm(24, (N_ODD, C_V_DIM), 0.02),
        'moe_w1': nrm(25, (DEPTH, N_EXPERTS, D_MODEL, D_FF), D_MODEL ** -0.5),
        'moe_w3': nrm(26, (DEPTH, N_EXPERTS, D_MODEL, D_FF), D_MODEL ** -0.5),
        'moe_w2': nrm(27, (DEPTH, N_EXPERTS, D_FF, D_MODEL), D_FF ** -0.5),
    }


def reference(x_prompt, x_sample, c_prompt, c_sample, rel_bias, router_w, router_b, ada_w, ada_b,
              norm_mix_g, norm_ffn_g, w_in_ab, w_out_ab, q_norm_g, k_norm_g, diff_lambda, diff_norm_g,
              mlstm_conv_w, mlstm_conv_b, mlstm_gate_b, mlstm_norm_g, w_in_c, w_out_c, ret_decay_logit,
              ret_norm_g, moe_w1, moe_w3, moe_w2):
    f32 = jnp.float32

    def mixer_ab(h, j, lam_init):
        Bn, S, _ = h.shape
        qa, ka, va, qkb, vb, ob, gb = _split(h @ w_in_ab[j], AB_WIDTHS)
        qa = rmsnorm(qa.reshape(Bn, S, A_HEADS, 2, A_QK_DIM), q_norm_g[j]).transpose(0, 2, 3, 1, 4)
        ka = rmsnorm(ka.reshape(Bn, S, A_HEADS, 2, A_QK_DIM), k_norm_g[j]).transpose(0, 2, 3, 1, 4)
        va = va.reshape(Bn, S, A_HEADS, A_V_DIM).transpose(0, 2, 1, 3)
        dl = diff_lambda[j].astype(f32)
        lam = jnp.exp(jnp.sum(dl[0] * dl[1])) - jnp.exp(jnp.sum(dl[2] * dl[3])) + lam_init
        ya = diff_attention(qa, ka, va, lam, rel_bias).transpose(0, 2, 1, 3)
        ya = (rmsnorm(ya, diff_norm_g[j]) * (1.0 - lam_init)).reshape(Bn, S, A_WIDTH)
        qkb = jax.nn.silu(centred_conv(qkb, mlstm_conv_w[j], mlstm_conv_b[j]))
        qb, kb = jnp.split(qkb, 2, axis=-1)
        heads = lambda t: t.reshape(Bn, S, B_HEADS, B_DIM).transpose(0, 2, 1, 3).astype(f32)
        qb, kb, vb = heads(qb), heads(kb) * (B_DIM ** -0.5), heads(vb)
        g = (gb.reshape(Bn, S, B_HEADS, N_GATES) + mlstm_gate_b[j]).astype(f32).transpose(3, 0, 2, 1)
        hf = mlstm_chunkwise(qb, kb, vb, g[0], jax.nn.log_sigmoid(g[1]))
        hb = _flip(mlstm_chunkwise(_flip(qb), _flip(kb), _flip(vb), _flip(g[2]), _flip(jax.nn.log_sigmoid(g[3]))))
        yb = (hf + hb).transpose(0, 2, 1, 3).astype(h.dtype)
        yb = rmsnorm(yb, mlstm_norm_g[j]) * jax.nn.sigmoid(ob.reshape(Bn, S, B_HEADS, B_DIM))
        yb = yb.reshape(Bn, S, B_WIDTH)
        return jnp.concatenate([ya, yb], axis=-1) @ w_out_ab[j]

    def mixer_c(h, j):
        Bn, S, _ = h.shape
        q, k, v, gt = _split(h @ w_in_c[j], C_WIDTHS)
        q = rope(q.reshape(Bn, S, C_HEADS, C_QK_DIM)).transpose(0, 2, 1, 3).astype(f32)
        k = rope(k.reshape(Bn, S, C_HEADS, C_QK_DIM)).transpose(0, 2, 1, 3).astype(f32) * (C_QK_DIM ** -0.5)
        v = v.reshape(Bn, S, C_HEADS, C_V_DIM).transpose(0, 2, 1, 3).astype(f32)
        lg = jax.nn.log_sigmoid(ret_decay_logit[j].astype(f32))
        yf = retention_chunkwise(q, k, v, lg[0])
        yb = _flip(retention_chunkwise(_flip(q), _flip(k), _flip(v), lg[1]))
        y = (yf + yb).transpose(0, 2, 1, 3).astype(h.dtype)
        y = rmsnorm(y, ret_norm_g[j]).reshape(Bn, S, C_V_WIDTH) * jax.nn.silu(gt)
        return y @ w_out_c[j]

    def moe(h, l):
        Bn, S, D = h.shape
        xt = h.reshape(Bn * S, D)
        probs = jax.nn.softmax((xt @ router_w + router_b).astype(f32), axis=-1)
        grouped = probs.reshape(-1, N_GROUPS, EXPERTS_PER_GROUP)
        group_score = jnp.sum(lax.top_k(grouped, TOP_K)[0], axis=-1)
        g_sel = jnp.argmax(group_score, axis=-1)
        in_group = jnp.take_along_axis(grouped, g_sel[:, None, None], axis=1)[:, 0]
        top_v, top_i = lax.top_k(in_group, TOP_K)
        weights = top_v / jnp.sum(top_v, axis=-1, keepdims=True)
        expert_id = g_sel[:, None] * EXPERTS_PER_GROUP + top_i
        combine = jnp.einsum('tk,tke->te', weights, jax.nn.one_hot(expert_id, N_EXPERTS, dtype=f32)).astype(h.dtype)
        y = jnp.zeros_like(xt)
        for e in range(N_EXPERTS):
            he = jax.nn.silu(xt @ moe_w1[l, e]) * (xt @ moe_w3[l, e])
            y = y + combine[:, e:e + 1] * (he @ moe_w2[l, e])
        return y.reshape(Bn, S, D)

    def trunk(x, c):
        for l in range(DEPTH):
            mod = (jax.nn.silu(c) @ ada_w[l] + ada_b[l])[:, None, :]
            sh1, sc1, g1, sh2, sc2, g2 = jnp.split(mod, 6, axis=-1)
            h = rmsnorm(x, norm_mix_g[l]) * (1.0 + sc1) + sh1
            if l % 2 == 0:
                lam_init = 0.8 - 0.6 * math.exp(-0.3 * l)
                m = mixer_ab(h, l // 2, lam_init)
            else:
                m = mixer_c(h, l // 2)
            x = x + g1 * m
            h = rmsnorm(x, norm_ffn_g[l]) * (1.0 + sc2) + sh2
            x = x + g2 * moe(h, l)
        return x

    y_prompt = trunk(x_prompt, c_prompt)
    y_sample = trunk(x_sample, c_sample)
    return (y_prompt, y_sample)
```

```python
import functools
import math

import jax
import jax.numpy as jnp
import numpy as np
from jax import lax
from jax.experimental import pallas as pl
from jax.experimental.pallas import tpu as pltpu

F32 = jnp.float32
MXU_DT = jnp.bfloat16
ACT_DT = jnp.bfloat16
HI = lax.Precision.HIGHEST

D_MODEL = 1024
DEPTH = 4
A_HEADS = 4
A_QK_DIM = 64
A_V_DIM = 128
A_QK_WIDTH = 512
A_WIDTH = 512
B_HEADS = 4
B_DIM = 128
B_WIDTH = 512
N_GATES = 4
AB_IN = 3600
AB_IN_PAD = 3712
C_HEADS = 4
C_QK_DIM = 256
C_V_DIM = 512
C_QK_WIDTH = 1024
C_V_WIDTH = 2048
C_IN = 6144
CHUNK = 128
REL_BUCKETS = 32
REL_MAX_DIST = 128
N_EXPERTS = 16
N_GROUPS = 4
EPG = 4
D_FF = 512
ROPE_BASE = 10000.0
EPS = 1e-6
LANES = 128
NEG_BIG = -1e30
VMEM_LIMIT = 56 * 1024 * 1024


def _cparams(sem):
    return pltpu.CompilerParams(dimension_semantics=sem, vmem_limit_bytes=VMEM_LIMIT)


def _mm(a, b):
    return jnp.dot(a.astype(MXU_DT), b.astype(MXU_DT), preferred_element_type=F32)


def _mm_nt(a, b):
    return lax.dot_general(a.astype(MXU_DT), b.astype(MXU_DT), (((1,), (1,)), ((), ())),
                           preferred_element_type=F32)


def _mm_tn(a, b):
    return lax.dot_general(a.astype(MXU_DT), b.astype(MXU_DT), (((0,), (0,)), ((), ())),
                           preferred_element_type=F32)


def _mm_hi(a, b):
    return jnp.dot(a, b, precision=HI, preferred_element_type=F32)


def _silu(x):
    return x * (1.0 / (1.0 + jnp.exp(-x)))


def _sigmoid(x):
    return 1.0 / (1.0 + jnp.exp(-x))


def _log_sigmoid(x):
    return jnp.minimum(x, 0.0) - jnp.log1p(jnp.exp(-jnp.abs(x)))


def _rms(x, g):
    return x * lax.rsqrt(jnp.mean(x * x, axis=-1, keepdims=True) + EPS) * g


class Geo:
    def __init__(self, bp, sp, bs, ss):
        self.bp, self.sp, self.bs, self.ss = bp, sp, bs, ss
        self.tp = bp * sp
        self.t = bp * sp + bs * ss
        self.seg = math.gcd(sp, ss)

    def seq_start(self, row0):
        return jnp.where(row0 < self.tp, row0 % self.sp == 0, (row0 - self.tp) % self.ss == 0)

    def pos_block(self, blk, rows):
        nbp = self.tp // rows
        return jnp.where(blk < nbp, blk % (self.sp // rows), (blk - nbp) % (self.ss // rows))


def _ada_kernel(c_ref, w_ref, b_ref, o_ref):
    o_ref[...] = _mm_hi(_silu(c_ref[...]), w_ref[...]) + b_ref[...]


def _ada_call(c_rows, ada_w, ada_b):
    r = c_rows.shape[0]
    nb = 1536
    out = pl.pallas_call(
        _ada_kernel,
        grid=(DEPTH, 6 * D_MODEL // nb),
        in_specs=[pl.BlockSpec((r, D_MODEL), lambda l, n: (0, 0)),
                  pl.BlockSpec((None, D_MODEL, nb), lambda l, n: (l, 0, n)),
                  pl.BlockSpec((None, 1, nb), lambda l, n: (l, 0, n))],
        out_specs=pl.BlockSpec((None, r, nb), lambda l, n: (l, 0, n)),
        out_shape=jax.ShapeDtypeStruct((DEPTH, r, 6 * D_MODEL), F32),
        compiler_params=_cparams(("arbitrary", "arbitrary")),
        name="ada_mod",
    )(c_rows, ada_w, ada_b.reshape(DEPTH, 1, 6 * D_MODEL))
    return out.reshape(DEPTH, r, 6, D_MODEL)


def _t5_bucket(rel):
    nb = REL_BUCKETS // 2
    max_exact = nb // 2
    ret = jnp.where(rel > 0, nb, 0)
    n = jnp.abs(rel)
    nf = jnp.maximum(n, 1).astype(jnp.float32)
    large = max_exact + (jnp.log(nf / max_exact) / math.log(REL_MAX_DIST / max_exact) * (nb - max_exact)).astype(jnp.int32)
    large = jnp.minimum(large, nb - 1)
    return ret + jnp.where(n < max_exact, n, large)


def _bias_kernel(rb_ref, bk_ref, o_ref):
    h = pl.program_id(0)
    bk = bk_ref[...]
    acc = jnp.zeros(bk.shape, F32)
    for b in range(REL_BUCKETS):
        acc = acc + jnp.where(bk == b, rb_ref[b, h], 0.0)
    o_ref[...] = acc


def _bias_call(rel_bias, tq):
    r = jnp.arange(tq, dtype=jnp.int32)
    rel = (jnp.arange(-1, 2, dtype=jnp.int32) * tq)[:, None, None] + r[None, None, :] - r[None, :, None]
    buckets = _t5_bucket(rel)
    return pl.pallas_call(
        _bias_kernel,
        grid=(A_HEADS, 3),
        in_specs=[pl.BlockSpec(memory_space=pltpu.SMEM),
                  pl.BlockSpec((None, tq, tq), lambda h, o: (o, 0, 0))],
        out_specs=pl.BlockSpec((None, None, tq, tq), lambda h, o: (h, o, 0, 0)),
        out_shape=jax.ShapeDtypeStruct((A_HEADS, 3, tq, tq), F32),
        compiler_params=_cparams(("arbitrary", "arbitrary")),
        name="rel_bias_tiles",
    )(rel_bias, buckets)


def _rope_tables(s):
    d = C_QK_DIM
    inv = ROPE_BASE ** (-jnp.arange(0, d, 2, dtype=jnp.float32) / d)
    ang = jnp.arange(s, dtype=jnp.float32)[:, None] * inv[None, :]
    return jnp.cos(ang), jnp.sin(ang)


def _half_rms(z, g):
    lo_lane = lax.broadcasted_iota(jnp.int32, (1, LANES), 1) < A_QK_DIM
    z2 = z * z
    tot = jnp.sum(z2, axis=-1, keepdims=True)
    lo = jnp.sum(jnp.where(lo_lane, z2, 0.0), axis=-1, keepdims=True)
    ms = jnp.where(lo_lane, lo, tot - lo) * (1.0 / A_QK_DIM)
    return z * lax.rsqrt(ms + EPS) * g


def _inproj_ab_kernel(x_ref, mod_ref, g_ref, w_ref, qg_ref, kg_ref, gb_ref,
                      qa_ref, ka_ref, va_ref, qk_ref, vb_ref, ob_ref, gt_ref):
    x = x_ref[...]
    h = _rms(x, g_ref[...]) * (1.0 + mod_ref[1:2, :]) + mod_ref[0:1, :]
    hb = h.astype(MXU_DT)
    qscale = A_QK_DIM ** -0.5
    for hd in range(A_HEADS):
        c0 = hd * LANES
        q = jnp.dot(hb, w_ref[:, c0:c0 + LANES], preferred_element_type=F32)
        qa_ref[:, c0:c0 + LANES] = (_half_rms(q, qg_ref[...]) * qscale).astype(qa_ref.dtype)
        k = jnp.dot(hb, w_ref[:, A_QK_WIDTH + c0:A_QK_WIDTH + c0 + LANES], preferred_element_type=F32)
        ka_ref[:, c0:c0 + LANES] = _half_rms(k, kg_ref[...]).astype(ka_ref.dtype)
    o = 2 * A_QK_WIDTH
    va_ref[...] = jnp.dot(hb, w_ref[:, o:o + A_WIDTH], preferred_element_type=F32).astype(va_ref.dtype)
    o += A_WIDTH
    qk_ref[...] = jnp.dot(hb, w_ref[:, o:o + 2 * B_WIDTH], preferred_element_type=F32)
    o += 2 * B_WIDTH
    vb_ref[...] = jnp.dot(hb, w_ref[:, o:o + B_WIDTH], preferred_element_type=F32).astype(vb_ref.dtype)
    o += B_WIDTH
    ob_ref[...] = jnp.dot(hb, w_ref[:, o:o + B_WIDTH], preferred_element_type=F32)
    o += B_WIDTH
    gt_ref[...] = jnp.dot(hb, w_ref[:, o:o + LANES], preferred_element_type=F32) + gb_ref[...]


def _inproj_ab_call(geo, tm, x, mod_l, g, w_pad, qg, kg, gate_b):
    t = geo.t
    per_seg = geo.seg // tm
    row = lambda i: (i, 0)
    const = lambda i: (0, 0)
    widths = (A_QK_WIDTH, A_QK_WIDTH, A_WIDTH, 2 * B_WIDTH, B_WIDTH, B_WIDTH, LANES)
    dtypes = (ACT_DT, ACT_DT, ACT_DT, F32, ACT_DT, F32, F32)
    return pl.pallas_call(
        _inproj_ab_kernel,
        grid=(t // tm,),
        in_specs=[pl.BlockSpec((tm, D_MODEL), row),
                  pl.BlockSpec((None, 6, D_MODEL), lambda i: (i // per_seg, 0, 0)),
                  pl.BlockSpec((1, D_MODEL), const),
                  pl.BlockSpec((D_MODEL, AB_IN_PAD), const),
                  pl.BlockSpec((1, LANES), const),
                  pl.BlockSpec((1, LANES), const),
                  pl.BlockSpec((1, LANES), const)],
        out_specs=[pl.BlockSpec((tm, w), row) for w in widths],
        out_shape=[jax.ShapeDtypeStruct((t, w), d) for w, d in zip(widths, dtypes)],
        compiler_params=_cparams(("parallel",)),
        name="inproj_ab",
    )(x, mod_l, g, w_pad, qg, kg, gate_b)


def _conv_kernel(geo, tc, x_ref, prev_ref, next_ref, w_ref, b_ref, q_ref, k_ref):
    i = pl.program_id(0)
    row0 = i * tc
    x = x_ref[...]
    first = geo.seq_start(row0)
    last = geo.seq_start(row0 + tc) | (row0 + tc == geo.t)
    prev_row = jnp.where(first, 0.0, prev_ref[7:8, :])
    next_row = jnp.where(last, 0.0, next_ref[0:1, :])
    ridx = lax.broadcasted_iota(jnp.int32, (tc, 1), 0)
    x_prev = jnp.where(ridx == 0, prev_row, pltpu.roll(x, 1, axis=0))
    x_next = jnp.where(ridx == tc - 1, next_row, pltpu.roll(x, tc - 1, axis=0))
    y = x_prev * w_ref[0:1, :] + x * w_ref[1:2, :] + x_next * w_ref[2:3, :] + b_ref[...]
    y = _silu(y)
    q_ref[...] = y[:, :B_WIDTH].astype(q_ref.dtype)
    k_ref[...] = (y[:, B_WIDTH:] * (B_DIM ** -0.5)).astype(k_ref.dtype)


def _conv_call(geo, tc, qk, w, b):
    t = geo.t
    r8 = tc // 8
    nb8 = t // 8
    return pl.pallas_call(
        functools.partial(_conv_kernel, geo, tc),
        grid=(t // tc,),
        in_specs=[pl.BlockSpec((tc, 2 * B_WIDTH), lambda i: (i, 0)),
                  pl.BlockSpec((8, 2 * B_WIDTH), lambda i: (jnp.maximum(i * r8 - 1, 0), 0)),
                  pl.BlockSpec((8, 2 * B_WIDTH), lambda i: (jnp.minimum((i + 1) * r8, nb8 - 1), 0)),
                  pl.BlockSpec((3, 2 * B_WIDTH), lambda i: (0, 0)),
                  pl.BlockSpec((1, 2 * B_WIDTH), lambda i: (0, 0))],
        out_specs=[pl.BlockSpec((tc, B_WIDTH), lambda i: (i, 0))] * 2,
        out_shape=[jax.ShapeDtypeStruct((t, B_WIDTH), ACT_DT)] * 2,
        compiler_params=_cparams(("parallel",)),
        name="mlstm_conv",
    )(qk, qk, qk, w, b)


def _attn_kernel(lam_init, nk, q_ref, k_ref, v_ref, bias_ref, far_ref, dl_ref, ng_ref, o_ref,
                 m_sc, l_sc, acc_sc):
    h = pl.program_id(1)
    i = pl.program_id(2)
    j = pl.program_id(3)

    @pl.when(j == 0)
    def _():
        m_sc[...] = jnp.full(m_sc.shape, NEG_BIG, F32)
        l_sc[...] = jnp.zeros(l_sc.shape, F32)
        acc_sc[...] = jnp.zeros(acc_sc.shape, F32)

    q = q_ref[...]
    lo_lane = lax.broadcasted_iota(jnp.int32, (1, LANES), 1) < A_QK_DIM
    qsub = (jnp.where(lo_lane, q, jnp.zeros_like(q)), jnp.where(lo_lane, jnp.zeros_like(q), q))
    k = k_ref[...]
    v = v_ref[...]

    def step(near):
        if near:
            shift = 0.0
        else:
            shift = jnp.where(j < i, far_ref[h, 0], far_ref[h, 1])
        for sub in range(2):
            s = _mm_nt(qsub[sub], k)
            if near:
                s = s + bias_ref[...]
            m_old = m_sc[sub]
            m_new = jnp.maximum(m_old, jnp.max(s, axis=-1, keepdims=True) + shift)
            p = jnp.exp(s - (m_new - shift))
            alpha = jnp.exp(m_old - m_new)
            l_sc[sub] = alpha * l_sc[sub] + jnp.sum(p, axis=-1, keepdims=True)
            acc_sc[sub] = alpha * acc_sc[sub] + _mm(p, v)
            m_sc[sub] = m_new

    near = jnp.abs(j - i) <= 1
    pl.when(near)(lambda: step(True))
    pl.when(jnp.logical_not(near))(lambda: step(False))

    @pl.when(j == nk - 1)
    def _():
        dl = dl_ref[...]
        lam = (jnp.exp(jnp.sum(dl[0:1] * dl[1:2], axis=-1, keepdims=True))
               - jnp.exp(jnp.sum(dl[2:3] * dl[3:4], axis=-1, keepdims=True)) + lam_init)
        out = acc_sc[0] / l_sc[0] - lam * (acc_sc[1] / l_sc[1])
        o_ref[...] = (_rms(out, ng_ref[...]) * (1.0 - lam_init)).astype(o_ref.dtype)


def _attn_call(geo, tq, lam_init, qa, ka, va, bias, far, dl, ng):
    outs = []
    for (nb, s, row_off) in ((geo.bp, geo.sp, 0), (geo.bs, geo.ss, geo.tp)):
        nq = s // tq
        off = row_off // tq
        qmap = lambda b, h, i, j, off=off, nq=nq: (off + b * nq + i, h)
        kmap = lambda b, h, i, j, off=off, nq=nq: (off + b * nq + j, h)
        omap = lambda b, h, i, j, nq=nq: (b * nq + i, h)
        outs.append(pl.pallas_call(
            functools.partial(_attn_kernel, lam_init, nq),
            grid=(nb, A_HEADS, nq, nq),
            in_specs=[pl.BlockSpec((tq, LANES), qmap),
                      pl.BlockSpec((tq, LANES), kmap),
                      pl.BlockSpec((tq, LANES), kmap),
                      pl.BlockSpec((None, None, tq, tq), lambda b, h, i, j: (h, jnp.clip(j - i + 1, 0, 2), 0, 0)),
                      pl.BlockSpec(memory_space=pltpu.SMEM),
                      pl.BlockSpec((4, A_QK_DIM), lambda b, h, i, j: (0, 0)),
                      pl.BlockSpec((1, LANES), lambda b, h, i, j: (0, 0))],
            out_specs=pl.BlockSpec((tq, LANES), omap),
            out_shape=jax.ShapeDtypeStruct((nb * s, A_WIDTH), ACT_DT),
            scratch_shapes=[pltpu.VMEM((2, tq, 1), F32), pltpu.VMEM((2, tq, 1), F32),
                            pltpu.VMEM((2, tq, LANES), F32)],
            compiler_params=_cparams(("parallel", "parallel", "parallel", "arbitrary")),
            name="diff_attn",
        )(qa, ka, va, bias, far, dl, ng))
    return outs


def _tri(lower):
    r = lax.broadcasted_iota(jnp.int32, (CHUNK, CHUNK), 0)
    c = lax.broadcasted_iota(jnp.int32, (CHUNK, CHUNK), 1)
    return (c <= r) if lower else (c >= r)


def _mlstm_kernel(geo, tb, reverse, *refs):
    if reverse:
        q_ref, k_ref, v_ref, g_ref, hf_ref, ob_ref, ng_ref, o_ref, c_sc, m_sc = refs
    else:
        q_ref, k_ref, v_ref, g_ref, o_ref, c_sc, m_sc = refs
    step = pl.program_id(0)
    nblk = geo.t // tb
    blk = (nblk - 1 - step) if reverse else step
    row0 = blk * tb
    if reverse:
        fresh = geo.seq_start(row0 + tb) | (row0 + tb == geo.t)
    else:
        fresh = geo.seq_start(row0)

    @pl.when(fresh)
    def _():
        c_sc[...] = jnp.zeros(c_sc.shape, F32)
        m_sc[...] = jnp.zeros(m_sc.shape, F32)

    mask = _tri(not reverse)
    cum_l = mask.astype(F32)
    cum_r = _tri(reverse).astype(F32)
    ones_col = (lax.broadcasted_iota(jnp.int32, (CHUNK, LANES), 1) == 0).astype(MXU_DT)
    nch = tb // CHUNK

    def chunk(ci, carry):
        c_idx = (nch - 1 - ci) if reverse else ci
        r0 = pl.multiple_of(c_idx * CHUNK, CHUNK)
        g = g_ref[pl.ds(r0, CHUNK), :]
        g_t = g.T
        b_col = _mm_hi(cum_l, _log_sigmoid(g))
        b_row = _mm_hi(_log_sigmoid(g_t), cum_r)
        for hd in range(B_HEADS):
            ci_col = hd * N_GATES + (2 if reverse else 0)
            cf_col = ci_col + 1
            lanes = slice(hd * B_DIM, (hd + 1) * B_DIM)
            q = q_ref[pl.ds(r0, CHUNK), lanes]
            k = k_ref[pl.ds(r0, CHUNK), lanes]
            v = v_ref[pl.ds(r0, CHUNK), lanes]
            bc = b_col[:, cf_col:cf_col + 1]
            br = b_row[cf_col:cf_col + 1, :]
            ic = g[:, ci_col:ci_col + 1]
            ir = g_t[ci_col:ci_col + 1, :]
            m_prev = m_sc[hd]
            log_d = jnp.where(mask, bc - br + ir, NEG_BIG)
            m_inter = bc + m_prev
            m_t = jnp.maximum(jnp.max(log_d, axis=-1, keepdims=True), m_inter)
            s = _mm_nt(q, k) * jnp.exp(log_d - m_t)
            inter = jnp.exp(m_inter - m_t)
            c_aug = c_sc[hd]
            qc = _mm(q, c_aug)
            num = _mm(s, v) + inter * qc[:, :B_DIM]
            den = jnp.sum(s, axis=-1, keepdims=True) + inter * qc[:, B_DIM:B_DIM + 1]
            hout = num / jnp.maximum(jnp.abs(den), jnp.exp(-m_t))
            b_last = bc[0:1, :] if reverse else bc[CHUNK - 1:CHUNK, :]
            log_w = b_last - bc + ic
            m_new = jnp.maximum(b_last + m_prev, jnp.max(log_w, axis=0, keepdims=True))
            w = jnp.exp(log_w - m_new)
            decay = jnp.exp(b_last + m_prev - m_new)
            v_aug = jnp.concatenate([v, ones_col], axis=1)
            c_sc[hd] = decay * c_aug + _mm_tn(k.astype(F32) * w, v_aug)
            m_sc[hd] = m_new
            if reverse:
                hsum = hf_ref[pl.ds(r0, CHUNK), lanes] + hout
                y = _rms(hsum, ng_ref[...]) * _sigmoid(ob_ref[pl.ds(r0, CHUNK), lanes])
                o_ref[pl.ds(r0, CHUNK), lanes] = y.astype(o_ref.dtype)
            else:
                o_ref[pl.ds(r0, CHUNK), lanes] = hout
        return carry

    lax.fori_loop(0, nch, chunk, 0)


def _mlstm_call(geo, tb, qb, kb, vb, gates, ob, ng):
    t = geo.t
    nblk = t // tb
    fmap = lambda s: (s, 0)
    rmap = lambda s: (nblk - 1 - s, 0)
    scratch = [pltpu.VMEM((B_HEADS, B_DIM, 2 * B_DIM), F32), pltpu.VMEM((B_HEADS, 1, 1), F32)]
    wide = lambda m: pl.BlockSpec((tb, B_WIDTH), m)
    hf = pl.pallas_call(
        functools.partial(_mlstm_kernel, geo, tb, False),
        grid=(nblk,),
        in_specs=[wide(fmap), wide(fmap), wide(fmap), pl.BlockSpec((tb, LANES), fmap)],
        out_specs=wide(fmap),
        out_shape=jax.ShapeDtypeStruct((t, B_WIDTH), F32),
        scratch_shapes=scratch,
        compiler_params=_cparams(("arbitrary",)),
        name="mlstm_fwd",
    )(qb, kb, vb, gates)
    return pl.pallas_call(
        functools.partial(_mlstm_kernel, geo, tb, True),
        grid=(nblk,),
        in_specs=[wide(rmap), wide(rmap), wide(rmap), pl.BlockSpec((tb, LANES), rmap), wide(rmap), wide(rmap),
                  pl.BlockSpec((1, B_DIM), lambda s: (0, 0))],
        out_specs=wide(rmap),
        out_shape=jax.ShapeDtypeStruct((t, B_WIDTH), ACT_DT),
        scratch_shapes=scratch,
        compiler_params=_cparams(("arbitrary",)),
        name="mlstm_bwd",
    )(qb, kb, vb, gates, hf, ob, ng)


def _inproj_c_kernel(x_ref, mod_ref, g_ref, w_ref, cos_ref, sin_ref, q_ref, k_ref, v_ref, gt_ref, h_sc):
    j = pl.program_id(1)

    @pl.when(j == 0)
    def _():
        x = x_ref[...]
        h = _rms(x, g_ref[...]) * (1.0 + mod_ref[1:2, :]) + mod_ref[0:1, :]
        h_sc[...] = h.astype(h_sc.dtype)

    y = jnp.dot(h_sc[...], w_ref[...], preferred_element_type=F32)

    def rope(scale):
        cos = cos_ref[...]
        sin = sin_ref[...]
        half = C_QK_DIM // 2
        parts = []
        for hd in range(C_HEADS):
            x1 = y[:, hd * C_QK_DIM:hd * C_QK_DIM + half]
            x2 = y[:, hd * C_QK_DIM + half:(hd + 1) * C_QK_DIM]
            parts.append((x1 * cos - x2 * sin) * scale)
            parts.append((x1 * sin + x2 * cos) * scale)
        return jnp.concatenate(parts, axis=1)

    @pl.when(j == 0)
    def _():
        q_ref[...] = rope(1.0).astype(q_ref.dtype)

    @pl.when(j == 1)
    def _():
        k_ref[...] = rope(C_QK_DIM ** -0.5).astype(k_ref.dtype)

    @pl.when((j == 2) | (j == 3))
    def _():
        v_ref[...] = y.astype(v_ref.dtype)

    @pl.when(j >= 4)
    def _():
        gt_ref[...] = _silu(y).astype(gt_ref.dtype)


def _inproj_c_call(geo, tm, x, mod_l, g, w, cos, sin):
    t = geo.t
    per_seg = geo.seg // tm
    nw = D_MODEL
    return pl.pallas_call(
        _inproj_c_kernel,
        grid=(t // tm, C_IN // nw),
        in_specs=[pl.BlockSpec((tm, D_MODEL), lambda i, j: (i, 0)),
                  pl.BlockSpec((None, 6, D_MODEL), lambda i, j: (i // per_seg, 0, 0)),
                  pl.BlockSpec((1, D_MODEL), lambda i, j: (0, 0)),
                  pl.BlockSpec((D_MODEL, nw), lambda i, j: (0, j)),
                  pl.BlockSpec((tm, C_QK_DIM // 2), lambda i, j: (geo.pos_block(i, tm), 0)),
                  pl.BlockSpec((tm, C_QK_DIM // 2), lambda i, j: (geo.pos_block(i, tm), 0))],
        out_specs=[pl.BlockSpec((tm, nw), lambda i, j: (i, 0)),
                   pl.BlockSpec((tm, nw), lambda i, j: (i, 0)),
                   pl.BlockSpec((tm, nw), lambda i, j: (i, jnp.clip(j - 2, 0, 1))),
                   pl.BlockSpec((tm, nw), lambda i, j: (i, jnp.clip(j - 4, 0, 1)))],
        out_shape=[jax.ShapeDtypeStruct((t, C_QK_WIDTH), ACT_DT),
                   jax.ShapeDtypeStruct((t, C_QK_WIDTH), ACT_DT),
                   jax.ShapeDtypeStruct((t, C_V_WIDTH), ACT_DT),
                   jax.ShapeDtypeStruct((t, C_V_WIDTH), ACT_DT)],
        scratch_shapes=[pltpu.VMEM((tm, D_MODEL), MXU_DT)],
        compiler_params=_cparams(("parallel", "arbitrary")),
        name="inproj_c",
    )(x, mod_l, g, w, cos, sin)


def _ret_kernel(geo, tb, reverse, *refs):
    if reverse:
        q_ref, k_ref, v_ref, dlg_ref, yf_ref, gt_ref, ng_ref, o_ref, r_sc = refs
    else:
        q_ref, k_ref, v_ref, dlg_ref, o_ref, r_sc = refs
    step = pl.program_id(0)
    nblk = geo.t // tb
    blk = (nblk - 1 - step) if reverse else step
    row0 = blk * tb
    if reverse:
        fresh = geo.seq_start(row0 + tb) | (row0 + tb == geo.t)
    else:
        fresh = geo.seq_start(row0)

    @pl.when(fresh)
    def _():
        r_sc[...] = jnp.zeros(r_sc.shape, F32)

    d = 1 if reverse else 0
    lg_all = _log_sigmoid(dlg_ref[...])
    ti = lax.broadcasted_iota(jnp.int32, (CHUNK, CHUNK), 0)
    si = lax.broadcasted_iota(jnp.int32, (CHUNK, CHUNK), 1)
    dist = ((si - ti) if reverse else (ti - si)).astype(F32)
    pos = lax.broadcasted_iota(jnp.int32, (CHUNK, 1), 0).astype(F32)
    upos = (CHUNK - 1.0 - pos) if reverse else pos
    nch = tb // CHUNK
    decays = []
    for hd in range(C_HEADS):
        lg = lg_all[d:d + 1, hd:hd + 1]
        intra = jnp.where(dist >= 0, jnp.exp(jnp.maximum(dist, 0.0) * lg), 0.0)
        decays.append((intra, jnp.exp((upos + 1.0) * lg), jnp.exp((CHUNK - 1.0 - upos) * lg),
                       jnp.exp(CHUNK * lg)))

    def chunk(ci, carry):
        c_idx = (nch - 1 - ci) if reverse else ci
        r0 = pl.multiple_of(c_idx * CHUNK, CHUNK)
        for hd in range(C_HEADS):
            intra, q_scale, k_scale, c_decay = decays[hd]
            ql = slice(hd * C_QK_DIM, (hd + 1) * C_QK_DIM)
            vl = slice(hd * C_V_DIM, (hd + 1) * C_V_DIM)
            q = q_ref[pl.ds(r0, CHUNK), ql]
            k = k_ref[pl.ds(r0, CHUNK), ql]
            v = v_ref[pl.ds(r0, CHUNK), vl]
            r_old = r_sc[hd]
            s = _mm_nt(q, k) * intra
            y = _mm(s, v) + q_scale * _mm(q, r_old)
            r_sc[hd] = c_decay * r_old + _mm_tn(k.astype(F32) * k_scale, v)
            if reverse:
                ysum = yf_ref[pl.ds(r0, CHUNK), vl] + y
                out = _rms(ysum, ng_ref[...]) * gt_ref[pl.ds(r0, CHUNK), vl].astype(F32)
                o_ref[pl.ds(r0, CHUNK), vl] = out.astype(o_ref.dtype)
            else:
                o_ref[pl.ds(r0, CHUNK), vl] = y
        return carry

    lax.fori_loop(0, nch, chunk, 0)


def _ret_call(geo, tb, q, k, v, decay_logit, gt, ng):
    t = geo.t
    nblk = t // tb
    fmap = lambda s: (s, 0)
    rmap = lambda s: (nblk - 1 - s, 0)
    scratch = [pltpu.VMEM((C_HEADS, C_QK_DIM, C_V_DIM), F32)]
    qk = lambda m: pl.BlockSpec((tb, C_QK_WIDTH), m)
    vv = lambda m: pl.BlockSpec((tb, C_V_WIDTH), m)
    dspec = pl.BlockSpec((2, C_HEADS), lambda s: (0, 0))
    yf = pl.pallas_call(
        functools.partial(_ret_kernel, geo, tb, False),
        grid=(nblk,),
        in_specs=[qk(fmap), qk(fmap), vv(fmap), dspec],
        out_specs=vv(fmap),
        out_shape=jax.ShapeDtypeStruct((t, C_V_WIDTH), F32),
        scratch_shapes=scratch,
        compiler_params=_cparams(("arbitrary",)),
        name="ret_fwd",
    )(q, k, v, decay_logit)
    return pl.pallas_call(
        functools.partial(_ret_kernel, geo, tb, True),
        grid=(nblk,),
        in_specs=[qk(rmap), qk(rmap), vv(rmap), dspec, vv(rmap), vv(rmap),
                  pl.BlockSpec((1, C_V_DIM), lambda s: (0, 0))],
        out_specs=vv(rmap),
        out_shape=jax.ShapeDtypeStruct((t, C_V_WIDTH), ACT_DT),
        scratch_shapes=scratch,
        compiler_params=_cparams(("arbitrary",)),
        name="ret_bwd",
    )(q, k, v, decay_logit, yf, gt, ng)


def _route(probs):
    p = [probs[e:e + 1, :] for e in range(N_EXPERTS)]
    scores = []
    for g in range(N_GROUPS):
        a, b, c, d = p[EPG * g:EPG * g + EPG]
        hi1, lo1 = jnp.maximum(a, b), jnp.minimum(a, b)
        hi2, lo2 = jnp.maximum(c, d), jnp.minimum(c, d)
        scores.append(jnp.maximum(hi1, hi2) + jnp.maximum(jnp.minimum(hi1, hi2), jnp.maximum(lo1, lo2)))
    g_sel = jnp.zeros(scores[0].shape, jnp.int32)
    best = scores[0]
    for g in range(1, N_GROUPS):
        better = scores[g] > best
        g_sel = jnp.where(better, g, g_sel)
        best = jnp.where(better, scores[g], best)
    vals = []
    for kk in range(EPG):
        v = p[kk]
        for g in range(1, N_GROUPS):
            v = jnp.where(g_sel == g, p[EPG * g + kk], v)
        vals.append(v)

    def argmax4(xs):
        idx = jnp.zeros(xs[0].shape, jnp.int32)
        top = xs[0]
        for kk in range(1, EPG):
            better = xs[kk] > top
            idx = jnp.where(better, kk, idx)
            top = jnp.where(better, xs[kk], top)
        return idx, top

    i1, v1 = argmax4(vals)
    i2, v2 = argmax4([jnp.where(i1 == kk, -1.0, vals[kk]) for kk in range(EPG)])
    tot = v1 + v2
    w1, w2 = v1 / tot, v2 / tot
    e1 = g_sel * EPG + i1
    e2 = g_sel * EPG + i2
    eidx = lax.broadcasted_iota(jnp.int32, probs.shape, 0)
    return jnp.where(eidx == e1, w1, 0.0) + jnp.where(eidx == e2, w2, 0.0)


def _outproj_kernel(nparts, *refs):
    y_refs = refs[:nparts]
    w_refs = refs[nparts:2 * nparts]
    x_ref, mod_ref, g_ref, rw_ref, rb_ref, xo_ref, h_ref, cmb_ref = refs[2 * nparts:]
    m = jnp.dot(y_refs[0][...], w_refs[0][...], preferred_element_type=F32)
    for p in range(1, nparts):
        m = m + jnp.dot(y_refs[p][...], w_refs[p][...], preferred_element_type=F32)
    x = x_ref[...] + mod_ref[2:3, :] * m
    xo_ref[...] = x
    h = _rms(x, g_ref[...]) * (1.0 + mod_ref[4:5, :]) + mod_ref[3:4, :]
    h_ref[...] = h.astype(h_ref.dtype)
    logits = lax.dot_general(rw_ref[...], h, (((1,), (1,)), ((), ())), precision=HI,
                             preferred_element_type=F32) + rb_ref[...]
    z = jnp.exp(logits - jnp.max(logits, axis=0, keepdims=True))
    probs = z / jnp.sum(z, axis=0, keepdims=True)
    cmb_ref[...] = _route(probs).T


def _outproj_call(geo, tm, ys, ws, x, mod_l, g, rw_t, rb):
    t = geo.t
    per_seg = geo.seg // tm
    row = lambda i: (i, 0)
    const = lambda i: (0, 0)
    n = len(ys)
    return pl.pallas_call(
        functools.partial(_outproj_kernel, n),
        grid=(t // tm,),
        in_specs=([pl.BlockSpec((tm, y.shape[1]), row) for y in ys]
                  + [pl.BlockSpec(w.shape, const) for w in ws]
                  + [pl.BlockSpec((tm, D_MODEL), row),
                     pl.BlockSpec((None, 6, D_MODEL), lambda i: (i // per_seg, 0, 0)),
                     pl.BlockSpec((1, D_MODEL), const),
                     pl.BlockSpec((N_EXPERTS, D_MODEL), const),
                     pl.BlockSpec((N_EXPERTS, 1), const)]),
        out_specs=[pl.BlockSpec((tm, D_MODEL), row), pl.BlockSpec((tm, D_MODEL), row),
                   pl.BlockSpec((tm, N_EXPERTS), row)],
        out_shape=[jax.ShapeDtypeStruct((t, D_MODEL), F32), jax.ShapeDtypeStruct((t, D_MODEL), ACT_DT),
                   jax.ShapeDtypeStruct((t, N_EXPERTS), F32)],
        compiler_params=_cparams(("parallel",)),
        name="outproj_router",
    )(*ys, *ws, x, mod_l, g, rw_t, rb)


def _moe_kernel(h_ref, cmb_ref, w1_ref, w3_ref, w2_ref, x_ref, mod_ref, o_ref, acc_sc):
    e = pl.program_id(1)

    @pl.when(e == 0)
    def _():
        acc_sc[...] = jnp.zeros(acc_sc.shape, F32)

    h = h_ref[...]
    a = jnp.dot(h, w1_ref[...], preferred_element_type=F32)
    b = jnp.dot(h, w3_ref[...], preferred_element_type=F32)
    he = _silu(a) * b
    cmb = cmb_ref[...]
    sel = lax.broadcasted_iota(jnp.int32, cmb.shape, 1) == e
    c = jnp.sum(jnp.where(sel, cmb, 0.0), axis=-1, keepdims=True)
    acc_sc[...] += c * _mm(he, w2_ref[...])

    @pl.when(e == N_EXPERTS - 1)
    def _():
        o_ref[...] = x_ref[...] + mod_ref[5:6, :] * acc_sc[...]


def _moe_call(geo, tm, layer, h, cmb, w1, w3, w2, x, mod_l):
    t = geo.t
    per_seg = geo.seg // tm
    return pl.pallas_call(
        _moe_kernel,
        grid=(t // tm, N_EXPERTS),
        in_specs=[pl.BlockSpec((tm, D_MODEL), lambda i, e: (i, 0)),
                  pl.BlockSpec((tm, N_EXPERTS), lambda i, e: (i, 0)),
                  pl.BlockSpec((None, None, D_MODEL, D_FF), lambda i, e: (layer, e, 0, 0)),
                  pl.BlockSpec((None, None, D_MODEL, D_FF), lambda i, e: (layer, e, 0, 0)),
                  pl.BlockSpec((None, None, D_FF, D_MODEL), lambda i, e: (layer, e, 0, 0)),
                  pl.BlockSpec((tm, D_MODEL), lambda i, e: (i, 0)),
                  pl.BlockSpec((None, 6, D_MODEL), lambda i, e: (i // per_seg, 0, 0))],
        out_specs=pl.BlockSpec((tm, D_MODEL), lambda i, e: (i, 0)),
        out_shape=jax.ShapeDtypeStruct((t, D_MODEL), F32),
        scratch_shapes=[pltpu.VMEM((tm, D_MODEL), F32)],
        compiler_params=_cparams(("parallel", "arbitrary")),
        name="moe",
    )(h, cmb, w1, w3, w2, x, mod_l)


def _tiles(geo):
    seg = geo.seg
    return dict(tm=min(512, seg), tq=min(512, seg), tscan=min(512, seg), tmoe=min(1024, seg))


def _forward(geo, tiles, x_prompt, x_sample, c_prompt, c_sample, rel_bias, router_w, router_b, ada_w, ada_b,
             norm_mix_g, norm_ffn_g, w_in_ab, w_out_ab, q_norm_g, k_norm_g, diff_lambda, diff_norm_g,
             mlstm_conv_w, mlstm_conv_b, mlstm_gate_b, mlstm_norm_g, w_in_c, w_out_c, ret_decay_logit,
             ret_norm_g, moe_w1, moe_w3, moe_w2):
    tm, tq, tscan, tmoe = tiles["tm"], tiles["tq"], tiles["tscan"], tiles["tmoe"]
    x = jnp.concatenate([x_prompt.reshape(geo.tp, D_MODEL), x_sample.reshape(geo.t - geo.tp, D_MODEL)], axis=0)
    c_rows = jnp.concatenate([jnp.repeat(c_prompt, geo.sp // geo.seg, axis=0),
                              jnp.repeat(c_sample, geo.ss // geo.seg, axis=0)], axis=0)
    mod = _ada_call(c_rows, ada_w, ada_b)

    bias = _bias_call(rel_bias, tq)
    far = jnp.stack([bias[:, 0, tq - 1, 0], bias[:, 2, 0, tq - 1]], axis=-1)
    cos, sin = _rope_tables(max(geo.sp, geo.ss))
    rw_t = router_w.T
    rb = router_b.reshape(N_EXPERTS, 1)
    w1 = moe_w1.astype(MXU_DT)
    w3 = moe_w3.astype(MXU_DT)
    w2 = moe_w2.astype(MXU_DT)

    for l in range(DEPTH):
        j = l // 2
        mod_l = mod[l]
        g_mix = norm_mix_g[l].reshape(1, D_MODEL)
        g_ffn = norm_ffn_g[l].reshape(1, D_MODEL)
        if l % 2 == 0:
            lam_init = 0.8 - 0.6 * math.exp(-0.3 * l)
            w_pad = jnp.pad(w_in_ab[j], ((0, 0), (0, AB_IN_PAD - AB_IN))).astype(MXU_DT)
            qg = jnp.tile(q_norm_g[j], 2).reshape(1, LANES)
            kg = jnp.tile(k_norm_g[j], 2).reshape(1, LANES)
            gate_b = jnp.pad(mlstm_gate_b[j].reshape(1, B_HEADS * N_GATES), ((0, 0), (0, LANES - B_HEADS * N_GATES)))
            qa, ka, va, qk, vb, ob, gates = _inproj_ab_call(geo, tm, x, mod_l, g_mix, w_pad, qg, kg, gate_b)
            ya_p, ya_s = _attn_call(geo, tq, lam_init, qa, ka, va, bias, far, diff_lambda[j],
                                    diff_norm_g[j].reshape(1, A_V_DIM))
            ya = jnp.concatenate([ya_p, ya_s], axis=0)
            qb, kb = _conv_call(geo, tscan, qk, mlstm_conv_w[j], mlstm_conv_b[j].reshape(1, 2 * B_WIDTH))
            yb = _mlstm_call(geo, tscan, qb, kb, vb, gates, ob, mlstm_norm_g[j].reshape(1, B_DIM))
            w_o = w_out_ab[j].astype(MXU_DT)
            ys, ws = [ya, yb], [w_o[:A_WIDTH], w_o[A_WIDTH:]]
        else:
            q, k, v, gt = _inproj_c_call(geo, tm, x, mod_l, g_mix, w_in_c[j].astype(MXU_DT), cos, sin)
            y = _ret_call(geo, tscan, q, k, v, ret_decay_logit[j], gt, ret_norm_g[j].reshape(1, C_V_DIM))
            ys, ws = [y], [w_out_c[j].astype(MXU_DT)]
        x, h, cmb = _outproj_call(geo, tm, ys, ws, x, mod_l, g_ffn, rw_t, rb)
        x = _moe_call(geo, tmoe, l, h, cmb, w1, w3, w2, x, mod_l)

    y_prompt = x[:geo.tp].reshape(x_prompt.shape)
    y_sample = x[geo.tp:].reshape(x_sample.shape)
    return (y_prompt, y_sample)


def kernel(x_prompt, x_sample, c_prompt, c_sample, rel_bias, router_w, router_b, ada_w, ada_b, norm_mix_g, norm_ffn_g, w_in_ab, w_out_ab, q_norm_g, k_norm_g, diff_lambda, diff_norm_g, mlstm_conv_w, mlstm_conv_b, mlstm_gate_b, mlstm_norm_g, w_in_c, w_out_c, ret_decay_logit, ret_norm_g, moe_w1, moe_w3, moe_w2):
    geo = Geo(x_prompt.shape[0], x_prompt.shape[1], x_sample.shape[0], x_sample.shape[1])
    return _forward(geo, _tiles(geo), x_prompt, x_sample, c_prompt, c_sample, rel_bias, router_w, router_b,
                    ada_w, ada_b, norm_mix_g, norm_ffn_g, w_in_ab, w_out_ab, q_norm_g, k_norm_g, diff_lambda,
                    diff_norm_g, mlstm_conv_w, mlstm_conv_b, mlstm_gate_b, mlstm_norm_g, w_in_c, w_out_c,
                    ret_decay_logit, ret_norm_g, moe_w1, moe_w3, moe_w2)
```

```python
import functools
import math

import jax
import jax.numpy as jnp
import numpy as np
from jax import lax
from jax.experimental import pallas as pl
from jax.experimental.pallas import tpu as pltpu

F32 = jnp.float32
MXU_DT = jnp.bfloat16
ACT_DT = jnp.bfloat16
HI = lax.Precision.HIGHEST

D_MODEL = 1024
DEPTH = 4
A_HEADS = 4
A_QK_DIM = 64
A_V_DIM = 128
A_QK_WIDTH = 512
A_WIDTH = 512
B_HEADS = 4
B_DIM = 128
B_WIDTH = 512
N_GATES = 4
AB_IN = 3600
AB_IN_PAD = 3712
C_HEADS = 4
C_QK_DIM = 256
C_V_DIM = 512
C_QK_WIDTH = 1024
C_V_WIDTH = 2048
C_IN = 6144
CHUNK = 128
REL_BUCKETS = 32
REL_MAX_DIST = 128
N_EXPERTS = 16
N_GROUPS = 4
EPG = 4
D_FF = 512
ROPE_BASE = 10000.0
EPS = 1e-6
LANES = 128
NEG_BIG = -1e30
LOG2E = math.log2(math.e)
VMEM_LIMIT = 56 * 1024 * 1024


def _cparams(sem):
    return pltpu.CompilerParams(dimension_semantics=sem, vmem_limit_bytes=VMEM_LIMIT)


def _mm(a, b):
    return jnp.dot(a.astype(MXU_DT), b.astype(MXU_DT), preferred_element_type=F32)


def _mm_nt(a, b):
    return lax.dot_general(a.astype(MXU_DT), b.astype(MXU_DT), (((1,), (1,)), ((), ())),
                           preferred_element_type=F32)


def _mm_tn(a, b):
    return lax.dot_general(a.astype(MXU_DT), b.astype(MXU_DT), (((0,), (0,)), ((), ())),
                           preferred_element_type=F32)


def _mm_hi(a, b):
    return jnp.dot(a, b, precision=HI, preferred_element_type=F32)


def _silu(x):
    return x * (1.0 / (1.0 + jnp.exp(-x)))


def _sigmoid(x):
    return 1.0 / (1.0 + jnp.exp(-x))


def _log_sigmoid(x):
    return jnp.minimum(x, 0.0) - jnp.log1p(jnp.exp(-jnp.abs(x)))


def _rms(x, g):
    return x * lax.rsqrt(jnp.mean(x * x, axis=-1, keepdims=True) + EPS) * g


class Geo:
    def __init__(self, bp, sp, bs, ss):
        self.bp, self.sp, self.bs, self.ss = bp, sp, bs, ss
        self.tp = bp * sp
        self.t = bp * sp + bs * ss
        self.seg = math.gcd(sp, ss)

    def seq_start(self, row0):
        return jnp.where(row0 < self.tp, row0 % self.sp == 0, (row0 - self.tp) % self.ss == 0)

    def pos_block(self, blk, rows):
        nbp = self.tp // rows
        return jnp.where(blk < nbp, blk % (self.sp // rows), (blk - nbp) % (self.ss // rows))


def _ada_kernel(c_ref, w_ref, b_ref, o_ref):
    o_ref[...] = _mm_hi(_silu(c_ref[...]), w_ref[...]) + b_ref[...]


def _ada_call(c_rows, ada_w, ada_b):
    r = c_rows.shape[0]
    nb = 1536
    out = pl.pallas_call(
        _ada_kernel,
        grid=(DEPTH, 6 * D_MODEL // nb),
        in_specs=[pl.BlockSpec((r, D_MODEL), lambda l, n: (0, 0)),
                  pl.BlockSpec((None, D_MODEL, nb), lambda l, n: (l, 0, n)),
                  pl.BlockSpec((None, 1, nb), lambda l, n: (l, 0, n))],
        out_specs=pl.BlockSpec((None, r, nb), lambda l, n: (l, 0, n)),
        out_shape=jax.ShapeDtypeStruct((DEPTH, r, 6 * D_MODEL), F32),
        compiler_params=_cparams(("arbitrary", "arbitrary")),
        name="ada_mod",
    )(c_rows, ada_w, ada_b.reshape(DEPTH, 1, 6 * D_MODEL))
    return out.reshape(DEPTH, r, 6, D_MODEL)


def _t5_bucket(rel):
    nb = REL_BUCKETS // 2
    max_exact = nb // 2
    ret = jnp.where(rel > 0, nb, 0)
    n = jnp.abs(rel)
    nf = jnp.maximum(n, 1).astype(jnp.float32)
    large = max_exact + (jnp.log(nf / max_exact) / math.log(REL_MAX_DIST / max_exact) * (nb - max_exact)).astype(jnp.int32)
    large = jnp.minimum(large, nb - 1)
    return ret + jnp.where(n < max_exact, n, large)


def _bias_kernel(rb_ref, bk_ref, o_ref):
    h = pl.program_id(0)
    bk = bk_ref[...]
    acc = jnp.zeros(bk.shape, F32)
    bmax = rb_ref[0, h]
    for b in range(REL_BUCKETS):
        acc = acc + jnp.where(bk == b, rb_ref[b, h], 0.0)
        bmax = jnp.maximum(bmax, rb_ref[b, h])
    o_ref[...] = (acc - bmax) * LOG2E


def _bias_call(rel_bias, tq):
    r = jnp.arange(tq, dtype=jnp.int32)
    rel = (jnp.arange(-1, 2, dtype=jnp.int32) * tq)[:, None, None] + r[None, None, :] - r[None, :, None]
    buckets = _t5_bucket(rel)
    return pl.pallas_call(
        _bias_kernel,
        grid=(A_HEADS, 3),
        in_specs=[pl.BlockSpec(memory_space=pltpu.SMEM),
                  pl.BlockSpec((None, tq, tq), lambda h, o: (o, 0, 0))],
        out_specs=pl.BlockSpec((None, None, tq, tq), lambda h, o: (h, o, 0, 0)),
        out_shape=jax.ShapeDtypeStruct((A_HEADS, 3, tq, tq), F32),
        compiler_params=_cparams(("arbitrary", "arbitrary")),
        name="rel_bias_tiles",
    )(rel_bias, buckets)


def _rope_tables(s):
    d = C_QK_DIM
    inv = ROPE_BASE ** (-jnp.arange(0, d, 2, dtype=jnp.float32) / d)
    ang = jnp.arange(s, dtype=jnp.float32)[:, None] * inv[None, :]
    return jnp.cos(ang), jnp.sin(ang)


def _half_rms(z, g):
    lo_lane = lax.broadcasted_iota(jnp.int32, (1, LANES), 1) < A_QK_DIM
    z2 = z * z
    tot = jnp.sum(z2, axis=-1, keepdims=True)
    lo = jnp.sum(jnp.where(lo_lane, z2, 0.0), axis=-1, keepdims=True)
    ms = jnp.where(lo_lane, lo, tot - lo) * (1.0 / A_QK_DIM)
    return z * lax.rsqrt(ms + EPS) * g


def _inproj_ab_kernel(x_ref, mod_ref, g_ref, w_ref, qg_ref, kg_ref, gb_ref,
                      qa_ref, ka_ref, va_ref, qk_ref, vb_ref, ob_ref, gt_ref):
    x = x_ref[...]
    h = _rms(x, g_ref[...]) * (1.0 + mod_ref[1:2, :]) + mod_ref[0:1, :]
    hb = h.astype(MXU_DT)
    qscale = (A_QK_DIM ** -0.5) * LOG2E
    for hd in range(A_HEADS):
        c0 = hd * LANES
        q = jnp.dot(hb, w_ref[:, c0:c0 + LANES], preferred_element_type=F32)
        qa_ref[:, c0:c0 + LANES] = (_half_rms(q, qg_ref[...]) * qscale).astype(qa_ref.dtype)
        k = jnp.dot(hb, w_ref[:, A_QK_WIDTH + c0:A_QK_WIDTH + c0 + LANES], preferred_element_type=F32)
        ka_ref[:, c0:c0 + LANES] = _half_rms(k, kg_ref[...]).astype(ka_ref.dtype)
    o = 2 * A_QK_WIDTH
    va_ref[...] = jnp.dot(hb, w_ref[:, o:o + A_WIDTH], preferred_element_type=F32).astype(va_ref.dtype)
    o += A_WIDTH
    qk_ref[...] = jnp.dot(hb, w_ref[:, o:o + 2 * B_WIDTH], preferred_element_type=F32)
    o += 2 * B_WIDTH
    vb_ref[...] = jnp.dot(hb, w_ref[:, o:o + B_WIDTH], preferred_element_type=F32).astype(vb_ref.dtype)
    o += B_WIDTH
    ob_ref[...] = jnp.dot(hb, w_ref[:, o:o + B_WIDTH], preferred_element_type=F32)
    o += B_WIDTH
    gt_ref[...] = jnp.dot(hb, w_ref[:, o:o + LANES], preferred_element_type=F32) + gb_ref[...]


def _inproj_ab_call(geo, tm, x, mod_l, g, w_pad, qg, kg, gate_b):
    t = geo.t
    per_seg = geo.seg // tm
    row = lambda i: (i, 0)
    const = lambda i: (0, 0)
    widths = (A_QK_WIDTH, A_QK_WIDTH, A_WIDTH, 2 * B_WIDTH, B_WIDTH, B_WIDTH, LANES)
    dtypes = (ACT_DT, ACT_DT, ACT_DT, F32, ACT_DT, F32, F32)
    return pl.pallas_call(
        _inproj_ab_kernel,
        grid=(t // tm,),
        in_specs=[pl.BlockSpec((tm, D_MODEL), row),
                  pl.BlockSpec((None, 6, D_MODEL), lambda i: (i // per_seg, 0, 0)),
                  pl.BlockSpec((1, D_MODEL), const),
                  pl.BlockSpec((D_MODEL, AB_IN_PAD), const),
                  pl.BlockSpec((1, LANES), const),
                  pl.BlockSpec((1, LANES), const),
                  pl.BlockSpec((1, LANES), const)],
        out_specs=[pl.BlockSpec((tm, w), row) for w in widths],
        out_shape=[jax.ShapeDtypeStruct((t, w), d) for w, d in zip(widths, dtypes)],
        compiler_params=_cparams(("parallel",)),
        name="inproj_ab",
    )(x, mod_l, g, w_pad, qg, kg, gate_b)


def _conv_kernel(geo, tc, x_ref, prev_ref, next_ref, w_ref, b_ref, q_ref, k_ref):
    i = pl.program_id(0)
    row0 = i * tc
    x = x_ref[...]
    first = geo.seq_start(row0)
    last = geo.seq_start(row0 + tc) | (row0 + tc == geo.t)
    prev_row = jnp.where(first, 0.0, prev_ref[7:8, :])
    next_row = jnp.where(last, 0.0, next_ref[0:1, :])
    ridx = lax.broadcasted_iota(jnp.int32, (tc, 1), 0)
    x_prev = jnp.where(ridx == 0, prev_row, pltpu.roll(x, 1, axis=0))
    x_next = jnp.where(ridx == tc - 1, next_row, pltpu.roll(x, tc - 1, axis=0))
    y = x_prev * w_ref[0:1, :] + x * w_ref[1:2, :] + x_next * w_ref[2:3, :] + b_ref[...]
    y = _silu(y)
    q_ref[...] = y[:, :B_WIDTH].astype(q_ref.dtype)
    k_ref[...] = (y[:, B_WIDTH:] * (B_DIM ** -0.5)).astype(k_ref.dtype)


def _conv_call(geo, tc, qk, w, b):
    t = geo.t
    r8 = tc // 8
    nb8 = t // 8
    return pl.pallas_call(
        functools.partial(_conv_kernel, geo, tc),
        grid=(t // tc,),
        in_specs=[pl.BlockSpec((tc, 2 * B_WIDTH), lambda i: (i, 0)),
                  pl.BlockSpec((8, 2 * B_WIDTH), lambda i: (jnp.maximum(i * r8 - 1, 0), 0)),
                  pl.BlockSpec((8, 2 * B_WIDTH), lambda i: (jnp.minimum((i + 1) * r8, nb8 - 1), 0)),
                  pl.BlockSpec((3, 2 * B_WIDTH), lambda i: (0, 0)),
                  pl.BlockSpec((1, 2 * B_WIDTH), lambda i: (0, 0))],
        out_specs=[pl.BlockSpec((tc, B_WIDTH), lambda i: (i, 0))] * 2,
        out_shape=[jax.ShapeDtypeStruct((t, B_WIDTH), ACT_DT)] * 2,
        compiler_params=_cparams(("parallel",)),
        name="mlstm_conv",
    )(qk, qk, qk, w, b)


def _attn_finish(lam_init, acc0, l0, acc1, l1, dl_ref, ng_ref, o_ref):
    dl = dl_ref[...]
    lam = (jnp.exp(jnp.sum(dl[0:1] * dl[1:2], axis=-1, keepdims=True))
           - jnp.exp(jnp.sum(dl[2:3] * dl[3:4], axis=-1, keepdims=True)) + lam_init)
    out = acc0 / l0 - lam * (acc1 / l1)
    o_ref[...] = (_rms(out, ng_ref[...]) * (1.0 - lam_init)).astype(o_ref.dtype)


def _attn_kernel(lam_init, nk, q_ref, k_ref, v_ref, bias_ref, sc_ref, dl_ref, ng_ref, o_ref,
                 m_sc, l_sc, acc_sc):
    h = pl.program_id(1)
    i = pl.program_id(2)
    j = pl.program_id(3)

    @pl.when(j == 0)
    def _():
        m_sc[...] = jnp.full(m_sc.shape, NEG_BIG, F32)
        l_sc[...] = jnp.zeros(l_sc.shape, F32)
        acc_sc[...] = jnp.zeros(acc_sc.shape, F32)

    q = q_ref[...]
    lo_lane = lax.broadcasted_iota(jnp.int32, (1, LANES), 1) < A_QK_DIM
    qsub = (jnp.where(lo_lane, q, jnp.zeros_like(q)), jnp.where(lo_lane, jnp.zeros_like(q), q))
    k = k_ref[...]
    v = v_ref[...]

    def step(near):
        if near:
            shift = 0.0
        else:
            shift = jnp.where(j < i, sc_ref[h, 0], sc_ref[h, 1])
        for sub in range(2):
            s = _mm_nt(qsub[sub], k)
            if near:
                s = s + bias_ref[...]
            m_old = m_sc[sub]
            m_new = jnp.maximum(m_old, jnp.max(s, axis=-1, keepdims=True) + shift)
            p = jnp.exp2(s - (m_new - shift))
            alpha = jnp.exp2(m_old - m_new)
            l_sc[sub] = alpha * l_sc[sub] + jnp.sum(p, axis=-1, keepdims=True)
            acc_sc[sub] = alpha * acc_sc[sub] + _mm(p, v)
            m_sc[sub] = m_new

    near = jnp.abs(j - i) <= 1
    pl.when(near)(lambda: step(True))
    pl.when(jnp.logical_not(near))(lambda: step(False))

    @pl.when(j == nk - 1)
    def _():
        _attn_finish(lam_init, acc_sc[0], l_sc[0], acc_sc[1], l_sc[1], dl_ref, ng_ref, o_ref)


def _attn_bounded_kernel(lam_init, nk, q_ref, k_ref, v_ref, bias_ref, sc_ref, dl_ref, ng_ref, o_ref, acc_sc):
    h = pl.program_id(1)
    i = pl.program_id(2)
    j = pl.program_id(3)

    @pl.when(j == 0)
    def _():
        acc_sc[...] = jnp.zeros(acc_sc.shape, F32)

    @pl.when(j == jnp.maximum(i - 1, 0))
    def _():
        acc_sc[...] = acc_sc[...] * sc_ref[h, 2]

    @pl.when(j == i + 2)
    def _():
        acc_sc[...] = acc_sc[...] * sc_ref[h, 3]

    q = q_ref[...]
    k = k_ref[...]
    v = v_ref[...]
    lo = (lax.broadcasted_iota(jnp.int32, q.shape, 1) < A_QK_DIM).astype(F32).astype(q.dtype)
    qsub = (q * lo, q * (1 - lo))
    ones_col = (lax.broadcasted_iota(jnp.int32, v.shape, 1) == 0).astype(v.dtype)
    v_aug = jnp.concatenate([v, ones_col], axis=1)

    def step(near):
        for sub in range(2):
            s = _mm_nt(qsub[sub], k)
            if near:
                s = s + bias_ref[...]
            acc_sc[sub] += _mm(jnp.exp2(s), v_aug)

    near = jnp.abs(j - i) <= 1
    pl.when(near)(lambda: step(True))
    pl.when(jnp.logical_not(near))(lambda: step(False))

    @pl.when(j == nk - 1)
    def _():
        a0 = acc_sc[0]
        a1 = acc_sc[1]
        _attn_finish(lam_init, a0[:, :A_V_DIM], a0[:, A_V_DIM:A_V_DIM + 1], a1[:, :A_V_DIM],
                     a1[:, A_V_DIM:A_V_DIM + 1], dl_ref, ng_ref, o_ref)


def _attn_scalars(bias, tq, q_gain, k_gain):
    far_l = bias[:, 0, tq - 1, 0]
    far_r = bias[:, 2, 0, tq - 1]
    sc = jnp.stack([far_l, far_r, jnp.exp2(far_l), jnp.exp2(-far_r)], axis=-1)
    bound = A_QK_DIM * jnp.max(jnp.abs(q_gain)) * jnp.max(jnp.abs(k_gain)) * (A_QK_DIM ** -0.5) * LOG2E * 1.02
    spread = -jnp.min(bias)
    ok = bound + 2.0 * spread <= 80.0
    return sc, ok


def _attn_call(geo, tq, lam_init, bounded, qa, ka, va, bias, sc, dl, ng):
    outs = []
    for (nb, s, row_off) in ((geo.bp, geo.sp, 0), (geo.bs, geo.ss, geo.tp)):
        nq = s // tq
        off = row_off // tq
        qmap = lambda b, h, i, j, off=off, nq=nq: (off + b * nq + i, h)
        kmap = lambda b, h, i, j, off=off, nq=nq: (off + b * nq + j, h)
        omap = lambda b, h, i, j, nq=nq: (b * nq + i, h)
        if bounded:
            body = functools.partial(_attn_bounded_kernel, lam_init, nq)
            scratch = [pltpu.VMEM((2, tq, 2 * LANES), F32)]
        else:
            body = functools.partial(_attn_kernel, lam_init, nq)
            scratch = [pltpu.VMEM((2, tq, 1), F32), pltpu.VMEM((2, tq, 1), F32), pltpu.VMEM((2, tq, LANES), F32)]
        outs.append(pl.pallas_call(
            body,
            grid=(nb, A_HEADS, nq, nq),
            in_specs=[pl.BlockSpec((tq, LANES), qmap),
                      pl.BlockSpec((tq, LANES), kmap),
                      pl.BlockSpec((tq, LANES), kmap),
                      pl.BlockSpec((None, None, tq, tq), lambda b, h, i, j: (h, jnp.clip(j - i + 1, 0, 2), 0, 0)),
                      pl.BlockSpec(memory_space=pltpu.SMEM),
                      pl.BlockSpec((4, A_QK_DIM), lambda b, h, i, j: (0, 0)),
                      pl.BlockSpec((1, LANES), lambda b, h, i, j: (0, 0))],
            out_specs=pl.BlockSpec((tq, LANES), omap),
            out_shape=jax.ShapeDtypeStruct((nb * s, A_WIDTH), ACT_DT),
            scratch_shapes=scratch,
            compiler_params=_cparams(("parallel", "parallel", "parallel", "arbitrary")),
            name="diff_attn_bounded" if bounded else "diff_attn",
        )(qa, ka, va, bias, sc, dl, ng))
    return jnp.concatenate(outs, axis=0)


def _tri(lower):
    r = lax.broadcasted_iota(jnp.int32, (CHUNK, CHUNK), 0)
    c = lax.broadcasted_iota(jnp.int32, (CHUNK, CHUNK), 1)
    return (c <= r) if lower else (c >= r)


def _mlstm_kernel(geo, tb, reverse, *refs):
    if reverse:
        q_ref, k_ref, v_ref, g_ref, hf_ref, ob_ref, ng_ref, o_ref, c_sc, m_sc = refs
    else:
        q_ref, k_ref, v_ref, g_ref, o_ref, c_sc, m_sc = refs
    step = pl.program_id(0)
    nblk = geo.t // tb
    blk = (nblk - 1 - step) if reverse else step
    row0 = blk * tb
    if reverse:
        fresh = geo.seq_start(row0 + tb) | (row0 + tb == geo.t)
    else:
        fresh = geo.seq_start(row0)

    @pl.when(fresh)
    def _():
        c_sc[...] = jnp.zeros(c_sc.shape, F32)
        m_sc[...] = jnp.zeros(m_sc.shape, F32)

    mask = _tri(not reverse)
    cum_l = mask.astype(F32)
    cum_r = _tri(reverse).astype(F32)
    ones_col = (lax.broadcasted_iota(jnp.int32, (CHUNK, LANES), 1) == 0).astype(MXU_DT)
    nch = tb // CHUNK

    def chunk(ci, carry):
        c_idx = (nch - 1 - ci) if reverse else ci
        r0 = pl.multiple_of(c_idx * CHUNK, CHUNK)
        g = g_ref[pl.ds(r0, CHUNK), :]
        g_t = g.T
        b_col = _mm_hi(cum_l, _log_sigmoid(g))
        b_row = _mm_hi(_log_sigmoid(g_t), cum_r)
        for hd in range(B_HEADS):
            ci_col = hd * N_GATES + (2 if reverse else 0)
            cf_col = ci_col + 1
            lanes = slice(hd * B_DIM, (hd + 1) * B_DIM)
            q = q_ref[pl.ds(r0, CHUNK), lanes]
            k = k_ref[pl.ds(r0, CHUNK), lanes]
            v = v_ref[pl.ds(r0, CHUNK), lanes]
            bc = b_col[:, cf_col:cf_col + 1]
            br = b_row[cf_col:cf_col + 1, :]
            ic = g[:, ci_col:ci_col + 1]
            ir = g_t[ci_col:ci_col + 1, :]
            m_prev = m_sc[hd]
            log_d = jnp.where(mask, bc - br + ir, NEG_BIG)
            m_inter = bc + m_prev
            m_t = jnp.maximum(jnp.max(log_d, axis=-1, keepdims=True), m_inter)
            s = _mm_nt(q, k) * jnp.exp(log_d - m_t)
            inter = jnp.exp(m_inter - m_t)
            c_aug = c_sc[hd]
            qc = _mm(q, c_aug)
            num = _mm(s, v) + inter * qc[:, :B_DIM]
            den = jnp.sum(s, axis=-1, keepdims=True) + inter * qc[:, B_DIM:B_DIM + 1]
            hout = num / jnp.maximum(jnp.abs(den), jnp.exp(-m_t))
            b_last = bc[0:1, :] if reverse else bc[CHUNK - 1:CHUNK, :]
            log_w = b_last - bc + ic
            m_new = jnp.maximum(b_last + m_prev, jnp.max(log_w, axis=0, keepdims=True))
            w = jnp.exp(log_w - m_new)
            decay = jnp.exp(b_last + m_prev - m_new)
            v_aug = jnp.concatenate([v, ones_col], axis=1)
            c_sc[hd] = decay * c_aug + _mm_tn(k.astype(F32) * w, v_aug)
            m_sc[hd] = m_new
            if reverse:
                hsum = hf_ref[pl.ds(r0, CHUNK), lanes] + hout
                y = _rms(hsum, ng_ref[...]) * _sigmoid(ob_ref[pl.ds(r0, CHUNK), lanes])
                o_ref[pl.ds(r0, CHUNK), lanes] = y.astype(o_ref.dtype)
            else:
                o_ref[pl.ds(r0, CHUNK), lanes] = hout
        return carry

    lax.fori_loop(0, nch, chunk, 0)


def _mlstm_call(geo, tb, qb, kb, vb, gates, ob, ng):
    t = geo.t
    nblk = t // tb
    fmap = lambda s: (s, 0)
    rmap = lambda s: (nblk - 1 - s, 0)
    scratch = [pltpu.VMEM((B_HEADS, B_DIM, 2 * B_DIM), F32), pltpu.VMEM((B_HEADS, 1, 1), F32)]
    wide = lambda m: pl.BlockSpec((tb, B_WIDTH), m)
    hf = pl.pallas_call(
        functools.partial(_mlstm_kernel, geo, tb, False),
        grid=(nblk,),
        in_specs=[wide(fmap), wide(fmap), wide(fmap), pl.BlockSpec((tb, LANES), fmap)],
        out_specs=wide(fmap),
        out_shape=jax.ShapeDtypeStruct((t, B_WIDTH), F32),
        scratch_shapes=scratch,
        compiler_params=_cparams(("arbitrary",)),
        name="mlstm_fwd",
    )(qb, kb, vb, gates)
    return pl.pallas_call(
        functools.partial(_mlstm_kernel, geo, tb, True),
        grid=(nblk,),
        in_specs=[wide(rmap), wide(rmap), wide(rmap), pl.BlockSpec((tb, LANES), rmap), wide(rmap), wide(rmap),
                  pl.BlockSpec((1, B_DIM), lambda s: (0, 0))],
        out_specs=wide(rmap),
        out_shape=jax.ShapeDtypeStruct((t, B_WIDTH), ACT_DT),
        scratch_shapes=scratch,
        compiler_params=_cparams(("arbitrary",)),
        name="mlstm_bwd",
    )(qb, kb, vb, gates, hf, ob, ng)


def _inproj_c_kernel(x_ref, mod_ref, g_ref, w_ref, cos_ref, sin_ref, q_ref, k_ref, v_ref, gt_ref, h_sc):
    j = pl.program_id(1)

    @pl.when(j == 0)
    def _():
        x = x_ref[...]
        h = _rms(x, g_ref[...]) * (1.0 + mod_ref[1:2, :]) + mod_ref[0:1, :]
        h_sc[...] = h.astype(h_sc.dtype)

    y = jnp.dot(h_sc[...], w_ref[...], preferred_element_type=F32)

    def rope(scale):
        cos = cos_ref[...]
        sin = sin_ref[...]
        half = C_QK_DIM // 2
        parts = []
        for hd in range(C_HEADS):
            x1 = y[:, hd * C_QK_DIM:hd * C_QK_DIM + half]
            x2 = y[:, hd * C_QK_DIM + half:(hd + 1) * C_QK_DIM]
            parts.append((x1 * cos - x2 * sin) * scale)
            parts.append((x1 * sin + x2 * cos) * scale)
        return jnp.concatenate(parts, axis=1)

    @pl.when(j == 0)
    def _():
        q_ref[...] = rope(1.0).astype(q_ref.dtype)

    @pl.when(j == 1)
    def _():
        k_ref[...] = rope(C_QK_DIM ** -0.5).astype(k_ref.dtype)

    @pl.when((j == 2) | (j == 3))
    def _():
        v_ref[...] = y.astype(v_ref.dtype)

    @pl.when(j >= 4)
    def _():
        gt_ref[...] = _silu(y).astype(gt_ref.dtype)


def _inproj_c_call(geo, tm, x, mod_l, g, w, cos, sin):
    t = geo.t
    per_seg = geo.seg // tm
    nw = D_MODEL
    return pl.pallas_call(
        _inproj_c_kernel,
        grid=(t // tm, C_IN // nw),
        in_specs=[pl.BlockSpec((tm, D_MODEL), lambda i, j: (i, 0)),
                  pl.BlockSpec((None, 6, D_MODEL), lambda i, j: (i // per_seg, 0, 0)),
                  pl.BlockSpec((1, D_MODEL), lambda i, j: (0, 0)),
                  pl.BlockSpec((D_MODEL, nw), lambda i, j: (0, j)),
                  pl.BlockSpec((tm, C_QK_DIM // 2), lambda i, j: (geo.pos_block(i, tm), 0)),
                  pl.BlockSpec((tm, C_QK_DIM // 2), lambda i, j: (geo.pos_block(i, tm), 0))],
        out_specs=[pl.BlockSpec((tm, nw), lambda i, j: (i, 0)),
                   pl.BlockSpec((tm, nw), lambda i, j: (i, 0)),
                   pl.BlockSpec((tm, nw), lambda i, j: (i, jnp.clip(j - 2, 0, 1))),
                   pl.BlockSpec((tm, nw), lambda i, j: (i, jnp.clip(j - 4, 0, 1)))],
        out_shape=[jax.ShapeDtypeStruct((t, C_QK_WIDTH), ACT_DT),
                   jax.ShapeDtypeStruct((t, C_QK_WIDTH), ACT_DT),
                   jax.ShapeDtypeStruct((t, C_V_WIDTH), ACT_DT),
                   jax.ShapeDtypeStruct((t, C_V_WIDTH), ACT_DT)],
        scratch_shapes=[pltpu.VMEM((tm, D_MODEL), MXU_DT)],
        compiler_params=_cparams(("parallel", "arbitrary")),
        name="inproj_c",
    )(x, mod_l, g, w, cos, sin)


def _ret_kernel(geo, tb, reverse, *refs):
    if reverse:
        q_ref, k_ref, v_ref, dlg_ref, yf_ref, gt_ref, ng_ref, o_ref, r_sc = refs
    else:
        q_ref, k_ref, v_ref, dlg_ref, o_ref, r_sc = refs
    step = pl.program_id(0)
    nblk = geo.t // tb
    blk = (nblk - 1 - step) if reverse else step
    row0 = blk * tb
    if reverse:
        fresh = geo.seq_start(row0 + tb) | (row0 + tb == geo.t)
    else:
        fresh = geo.seq_start(row0)

    @pl.when(fresh)
    def _():
        r_sc[...] = jnp.zeros(r_sc.shape, F32)

    d = 1 if reverse else 0
    lg_all = _log_sigmoid(dlg_ref[...])
    ti = lax.broadcasted_iota(jnp.int32, (CHUNK, CHUNK), 0)
    si = lax.broadcasted_iota(jnp.int32, (CHUNK, CHUNK), 1)
    dist = ((si - ti) if reverse else (ti - si)).astype(F32)
    pos = lax.broadcasted_iota(jnp.int32, (CHUNK, 1), 0).astype(F32)
    upos = (CHUNK - 1.0 - pos) if reverse else pos
    nch = tb // CHUNK
    decays = []
    for hd in range(C_HEADS):
        lg = lg_all[d:d + 1, hd:hd + 1]
        intra = jnp.where(dist >= 0, jnp.exp(jnp.maximum(dist, 0.0) * lg), 0.0)
        decays.append((intra, jnp.exp((upos + 1.0) * lg), jnp.exp((CHUNK - 1.0 - upos) * lg),
                       jnp.exp(CHUNK * lg)))

    def chunk(ci, carry):
        c_idx = (nch - 1 - ci) if reverse else ci
        r0 = pl.multiple_of(c_idx * CHUNK, CHUNK)
        for hd in range(C_HEADS):
            intra, q_scale, k_scale, c_decay = decays[hd]
            ql = slice(hd * C_QK_DIM, (hd + 1) * C_QK_DIM)
            vl = slice(hd * C_V_DIM, (hd + 1) * C_V_DIM)
            q = q_ref[pl.ds(r0, CHUNK), ql]
            k = k_ref[pl.ds(r0, CHUNK), ql]
            v = v_ref[pl.ds(r0, CHUNK), vl]
            r_old = r_sc[hd]
            s = _mm_nt(q, k) * intra
            y = _mm(s, v) + q_scale * _mm(q, r_old)
            r_sc[hd] = c_decay * r_old + _mm_tn(k.astype(F32) * k_scale, v)
            if reverse:
                ysum = yf_ref[pl.ds(r0, CHUNK), vl] + y
                out = _rms(ysum, ng_ref[...]) * gt_ref[pl.ds(r0, CHUNK), vl].astype(F32)
                o_ref[pl.ds(r0, CHUNK), vl] = out.astype(o_ref.dtype)
            else:
                o_ref[pl.ds(r0, CHUNK), vl] = y
        return carry

    lax.fori_loop(0, nch, chunk, 0)


def _ret_call(geo, tb, q, k, v, decay_logit, gt, ng):
    t = geo.t
    nblk = t // tb
    fmap = lambda s: (s, 0)
    rmap = lambda s: (nblk - 1 - s, 0)
    scratch = [pltpu.VMEM((C_HEADS, C_QK_DIM, C_V_DIM), F32)]
    qk = lambda m: pl.BlockSpec((tb, C_QK_WIDTH), m)
    vv = lambda m: pl.BlockSpec((tb, C_V_WIDTH), m)
    dspec = pl.BlockSpec((2, C_HEADS), lambda s: (0, 0))
    yf = pl.pallas_call(
        functools.partial(_ret_kernel, geo, tb, False),
        grid=(nblk,),
        in_specs=[qk(fmap), qk(fmap), vv(fmap), dspec],
        out_specs=vv(fmap),
        out_shape=jax.ShapeDtypeStruct((t, C_V_WIDTH), F32),
        scratch_shapes=scratch,
        compiler_params=_cparams(("arbitrary",)),
        name="ret_fwd",
    )(q, k, v, decay_logit)
    return pl.pallas_call(
        functools.partial(_ret_kernel, geo, tb, True),
        grid=(nblk,),
        in_specs=[qk(rmap), qk(rmap), vv(rmap), dspec, vv(rmap), vv(rmap),
                  pl.BlockSpec((1, C_V_DIM), lambda s: (0, 0))],
        out_specs=vv(rmap),
        out_shape=jax.ShapeDtypeStruct((t, C_V_WIDTH), ACT_DT),
        scratch_shapes=scratch,
        compiler_params=_cparams(("arbitrary",)),
        name="ret_bwd",
    )(q, k, v, decay_logit, yf, gt, ng)


def _route(probs):
    p = [probs[e:e + 1, :] for e in range(N_EXPERTS)]
    scores = []
    for g in range(N_GROUPS):
        a, b, c, d = p[EPG * g:EPG * g + EPG]
        hi1, lo1 = jnp.maximum(a, b), jnp.minimum(a, b)
        hi2, lo2 = jnp.maximum(c, d), jnp.minimum(c, d)
        scores.append(jnp.maximum(hi1, hi2) + jnp.maximum(jnp.minimum(hi1, hi2), jnp.maximum(lo1, lo2)))
    g_sel = jnp.zeros(scores[0].shape, jnp.int32)
    best = scores[0]
    for g in range(1, N_GROUPS):
        better = scores[g] > best
        g_sel = jnp.where(better, g, g_sel)
        best = jnp.where(better, scores[g], best)
    vals = []
    for kk in range(EPG):
        v = p[kk]
        for g in range(1, N_GROUPS):
            v = jnp.where(g_sel == g, p[EPG * g + kk], v)
        vals.append(v)

    def argmax4(xs):
        idx = jnp.zeros(xs[0].shape, jnp.int32)
        top = xs[0]
        for kk in range(1, EPG):
            better = xs[kk] > top
            idx = jnp.where(better, kk, idx)
            top = jnp.where(better, xs[kk], top)
        return idx, top

    i1, v1 = argmax4(vals)
    i2, v2 = argmax4([jnp.where(i1 == kk, -1.0, vals[kk]) for kk in range(EPG)])
    tot = v1 + v2
    w1, w2 = v1 / tot, v2 / tot
    e1 = g_sel * EPG + i1
    e2 = g_sel * EPG + i2
    eidx = lax.broadcasted_iota(jnp.int32, probs.shape, 0)
    return jnp.where(eidx == e1, w1, 0.0) + jnp.where(eidx == e2, w2, 0.0)


def _outproj_kernel(nparts, *refs):
    y_refs = refs[:nparts]
    w_refs = refs[nparts:2 * nparts]
    x_ref, mod_ref, g_ref, rw_ref, rb_ref, xo_ref, h_ref, cmb_ref = refs[2 * nparts:]
    m = jnp.dot(y_refs[0][...], w_refs[0][...], preferred_element_type=F32)
    for p in range(1, nparts):
        m = m + jnp.dot(y_refs[p][...], w_refs[p][...], preferred_element_type=F32)
    x = x_ref[...] + mod_ref[2:3, :] * m
    xo_ref[...] = x
    h = _rms(x, g_ref[...]) * (1.0 + mod_ref[4:5, :]) + mod_ref[3:4, :]
    h_ref[...] = h.astype(h_ref.dtype)
    logits = lax.dot_general(rw_ref[...], h, (((1,), (1,)), ((), ())), precision=HI,
                             preferred_element_type=F32) + rb_ref[...]
    z = jnp.exp(logits - jnp.max(logits, axis=0, keepdims=True))
    probs = z / jnp.sum(z, axis=0, keepdims=True)
    cmb_ref[...] = _route(probs).T


def _outproj_call(geo, tm, ys, ws, x, mod_l, g, rw_t, rb):
    t = geo.t
    per_seg = geo.seg // tm
    row = lambda i: (i, 0)
    const = lambda i: (0, 0)
    n = len(ys)
    return pl.pallas_call(
        functools.partial(_outproj_kernel, n),
        grid=(t // tm,),
        in_specs=([pl.BlockSpec((tm, y.shape[1]), row) for y in ys]
                  + [pl.BlockSpec(w.shape, const) for w in ws]
                  + [pl.BlockSpec((tm, D_MODEL), row),
                     pl.BlockSpec((None, 6, D_MODEL), lambda i: (i // per_seg, 0, 0)),
                     pl.BlockSpec((1, D_MODEL), const),
                     pl.BlockSpec((N_EXPERTS, D_MODEL), const),
                     pl.BlockSpec((N_EXPERTS, 1), const)]),
        out_specs=[pl.BlockSpec((tm, D_MODEL), row), pl.BlockSpec((tm, D_MODEL), row),
                   pl.BlockSpec((tm, N_EXPERTS), row)],
        out_shape=[jax.ShapeDtypeStruct((t, D_MODEL), F32), jax.ShapeDtypeStruct((t, D_MODEL), ACT_DT),
                   jax.ShapeDtypeStruct((t, N_EXPERTS), F32)],
        compiler_params=_cparams(("parallel",)),
        name="outproj_router",
    )(*ys, *ws, x, mod_l, g, rw_t, rb)


def _moe_kernel(h_ref, cmb_ref, w1_ref, w3_ref, w2_ref, x_ref, mod_ref, o_ref, acc_sc):
    e = pl.program_id(1)

    @pl.when(e == 0)
    def _():
        acc_sc[...] = jnp.zeros(acc_sc.shape, F32)

    h = h_ref[...]
    a = jnp.dot(h, w1_ref[...], preferred_element_type=F32)
    b = jnp.dot(h, w3_ref[...], preferred_element_type=F32)
    he = _silu(a) * b
    cmb = cmb_ref[...]
    sel = lax.broadcasted_iota(jnp.int32, cmb.shape, 1) == e
    c = jnp.sum(jnp.where(sel, cmb, 0.0), axis=-1, keepdims=True)
    acc_sc[...] += c * _mm(he, w2_ref[...])

    @pl.when(e == N_EXPERTS - 1)
    def _():
        o_ref[...] = x_ref[...] + mod_ref[5:6, :] * acc_sc[...]


def _moe_call(geo, tm, layer, h, cmb, w1, w3, w2, x, mod_l):
    t = geo.t
    per_seg = geo.seg // tm
    return pl.pallas_call(
        _moe_kernel,
        grid=(t // tm, N_EXPERTS),
        in_specs=[pl.BlockSpec((tm, D_MODEL), lambda i, e: (i, 0)),
                  pl.BlockSpec((tm, N_EXPERTS), lambda i, e: (i, 0)),
                  pl.BlockSpec((None, None, D_MODEL, D_FF), lambda i, e: (layer, e, 0, 0)),
                  pl.BlockSpec((None, None, D_MODEL, D_FF), lambda i, e: (layer, e, 0, 0)),
                  pl.BlockSpec((None, None, D_FF, D_MODEL), lambda i, e: (layer, e, 0, 0)),
                  pl.BlockSpec((tm, D_MODEL), lambda i, e: (i, 0)),
                  pl.BlockSpec((None, 6, D_MODEL), lambda i, e: (i // per_seg, 0, 0))],
        out_specs=pl.BlockSpec((tm, D_MODEL), lambda i, e: (i, 0)),
        out_shape=jax.ShapeDtypeStruct((t, D_MODEL), F32),
        scratch_shapes=[pltpu.VMEM((tm, D_MODEL), F32)],
        compiler_params=_cparams(("parallel", "arbitrary")),
        name="moe",
    )(h, cmb, w1, w3, w2, x, mod_l)


def _tiles(geo):
    seg = geo.seg
    return dict(tm=min(512, seg), tq=min(1024, seg), tscan=min(512, seg), tmoe=min(1024, seg))


def _forward(geo, tiles, x_prompt, x_sample, c_prompt, c_sample, rel_bias, router_w, router_b, ada_w, ada_b,
             norm_mix_g, norm_ffn_g, w_in_ab, w_out_ab, q_norm_g, k_norm_g, diff_lambda, diff_norm_g,
             mlstm_conv_w, mlstm_conv_b, mlstm_gate_b, mlstm_norm_g, w_in_c, w_out_c, ret_decay_logit,
             ret_norm_g, moe_w1, moe_w3, moe_w2):
    tm, tq, tscan, tmoe = tiles["tm"], tiles["tq"], tiles["tscan"], tiles["tmoe"]
    x = jnp.concatenate([x_prompt.reshape(geo.tp, D_MODEL), x_sample.reshape(geo.t - geo.tp, D_MODEL)], axis=0)
    c_rows = jnp.concatenate([jnp.repeat(c_prompt, geo.sp // geo.seg, axis=0),
                              jnp.repeat(c_sample, geo.ss // geo.seg, axis=0)], axis=0)
    mod = _ada_call(c_rows, ada_w, ada_b)

    bias = _bias_call(rel_bias, tq)
    cos, sin = _rope_tables(max(geo.sp, geo.ss))
    rw_t = router_w.T
    rb = router_b.reshape(N_EXPERTS, 1)
    w1 = moe_w1.astype(MXU_DT)
    w3 = moe_w3.astype(MXU_DT)
    w2 = moe_w2.astype(MXU_DT)

    for l in range(DEPTH):
        j = l // 2
        mod_l = mod[l]
        g_mix = norm_mix_g[l].reshape(1, D_MODEL)
        g_ffn = norm_ffn_g[l].reshape(1, D_MODEL)
        if l % 2 == 0:
            lam_init = 0.8 - 0.6 * math.exp(-0.3 * l)
            w_pad = jnp.pad(w_in_ab[j], ((0, 0), (0, AB_IN_PAD - AB_IN))).astype(MXU_DT)
            qg = jnp.tile(q_norm_g[j], 2).reshape(1, LANES)
            kg = jnp.tile(k_norm_g[j], 2).reshape(1, LANES)
            gate_b = jnp.pad(mlstm_gate_b[j].reshape(1, B_HEADS * N_GATES), ((0, 0), (0, LANES - B_HEADS * N_GATES)))
            qa, ka, va, qk, vb, ob, gates = _inproj_ab_call(geo, tm, x, mod_l, g_mix, w_pad, qg, kg, gate_b)
            sc, bounded_ok = _attn_scalars(bias, tq, q_norm_g[j], k_norm_g[j])
            ya = lax.cond(bounded_ok,
                          functools.partial(_attn_call, geo, tq, lam_init, True),
                          functools.partial(_attn_call, geo, tq, lam_init, False),
                          qa, ka, va, bias, sc, diff_lambda[j], diff_norm_g[j].reshape(1, A_V_DIM))
            qb, kb = _conv_call(geo, tscan, qk, mlstm_conv_w[j], mlstm_conv_b[j].reshape(1, 2 * B_WIDTH))
            yb = _mlstm_call(geo, tscan, qb, kb, vb, gates, ob, mlstm_norm_g[j].reshape(1, B_DIM))
            w_o = w_out_ab[j].astype(MXU_DT)
            ys, ws = [ya, yb], [w_o[:A_WIDTH], w_o[A_WIDTH:]]
        else:
            q, k, v, gt = _inproj_c_call(geo, tm, x, mod_l, g_mix, w_in_c[j].astype(MXU_DT), cos, sin)
            y = _ret_call(geo, tscan, q, k, v, ret_decay_logit[j], gt, ret_norm_g[j].reshape(1, C_V_DIM))
            ys, ws = [y], [w_out_c[j].astype(MXU_DT)]
        x, h, cmb = _outproj_call(geo, tm, ys, ws, x, mod_l, g_ffn, rw_t, rb)
        x = _moe_call(geo, tmoe, l, h, cmb, w1, w3, w2, x, mod_l)

    y_prompt = x[:geo.tp].reshape(x_prompt.shape)
    y_sample = x[geo.tp:].reshape(x_sample.shape)
    return (y_prompt, y_sample)


def kernel(x_prompt, x_sample, c_prompt, c_sample, rel_bias, router_w, router_b, ada_w, ada_b, norm_mix_g, norm_ffn_g, w_in_ab, w_out_ab, q_norm_g, k_norm_g, diff_lambda, diff_norm_g, mlstm_conv_w, mlstm_conv_b, mlstm_gate_b, mlstm_norm_g, w_in_c, w_out_c, ret_decay_logit, ret_norm_g, moe_w1, moe_w3, moe_w2):
    geo = Geo(x_prompt.shape[0], x_prompt.shape[1], x_sample.shape[0], x_sample.shape[1])
    return _forward(geo, _tiles(geo), x_prompt, x_sample, c_prompt, c_sample, rel_bias, router_w, router_b,
                    ada_w, ada_b, norm_mix_g, norm_ffn_g, w_in_ab, w_out_ab, q_norm_g, k_norm_g, diff_lambda,
                    diff_norm_g, mlstm_conv_w, mlstm_conv_b, mlstm_gate_b, mlstm_norm_g, w_in_c, w_out_c,
                    ret_decay_logit, ret_norm_g, moe_w1, moe_w3, moe_w2)
```

```python
import functools
import math

import jax
import jax.numpy as jnp
import numpy as np
from jax import lax
from jax.experimental import pallas as pl
from jax.experimental.pallas import tpu as pltpu

F32 = jnp.float32
MXU_DT = jnp.bfloat16
ACT_DT = jnp.bfloat16
HI = lax.Precision.HIGHEST

D_MODEL = 1024
DEPTH = 4
A_HEADS = 4
A_QK_DIM = 64
A_V_DIM = 128
A_QK_WIDTH = 512
A_WIDTH = 512
B_HEADS = 4
B_DIM = 128
B_WIDTH = 512
N_GATES = 4
AB_IN = 3600
AB_IN_PAD = 3712
C_HEADS = 4
C_QK_DIM = 256
C_V_DIM = 512
C_QK_WIDTH = 1024
C_V_WIDTH = 2048
C_IN = 6144
CHUNK = 128
REL_BUCKETS = 32
REL_MAX_DIST = 128
N_EXPERTS = 16
N_GROUPS = 4
EPG = 4
D_FF = 512
ROPE_BASE = 10000.0
EPS = 1e-6
LANES = 128
NEG_BIG = -1e30
LOG2E = math.log2(math.e)
HX_W = D_MODEL + LANES
VMEM_LIMIT = 56 * 1024 * 1024


def _cparams(sem):
    return pltpu.CompilerParams(dimension_semantics=sem, vmem_limit_bytes=VMEM_LIMIT)


def _mm(a, b):
    return jnp.dot(a.astype(MXU_DT), b.astype(MXU_DT), preferred_element_type=F32)


def _mm_nt(a, b):
    return lax.dot_general(a.astype(MXU_DT), b.astype(MXU_DT), (((1,), (1,)), ((), ())),
                           preferred_element_type=F32)


def _mm_tn(a, b):
    return lax.dot_general(a.astype(MXU_DT), b.astype(MXU_DT), (((0,), (0,)), ((), ())),
                           preferred_element_type=F32)


def _mm_hi(a, b):
    return jnp.dot(a, b, precision=HI, preferred_element_type=F32)


def _silu(x):
    return x * (1.0 / (1.0 + jnp.exp(-x)))


def _sigmoid(x):
    return 1.0 / (1.0 + jnp.exp(-x))


def _log_sigmoid(x):
    return jnp.minimum(x, 0.0) - jnp.log1p(jnp.exp(-jnp.abs(x)))


def _rms(x, g):
    return x * lax.rsqrt(jnp.mean(x * x, axis=-1, keepdims=True) + EPS) * g


class Geo:
    def __init__(self, bp, sp, bs, ss):
        self.bp, self.sp, self.bs, self.ss = bp, sp, bs, ss
        self.tp = bp * sp
        self.t = bp * sp + bs * ss
        self.seg = math.gcd(sp, ss)

    def seq_start(self, row0):
        return jnp.where(row0 < self.tp, row0 % self.sp == 0, (row0 - self.tp) % self.ss == 0)

    def pos_block(self, blk, rows):
        nbp = self.tp // rows
        return jnp.where(blk < nbp, blk % (self.sp // rows), (blk - nbp) % (self.ss // rows))


def _ada_kernel(c_ref, w_ref, b_ref, o_ref):
    o_ref[...] = _mm_hi(_silu(c_ref[...]), w_ref[...]) + b_ref[...]


def _ada_call(c_rows, ada_w, ada_b):
    r = c_rows.shape[0]
    nb = 1536
    out = pl.pallas_call(
        _ada_kernel,
        grid=(DEPTH, 6 * D_MODEL // nb),
        in_specs=[pl.BlockSpec((r, D_MODEL), lambda l, n: (0, 0)),
                  pl.BlockSpec((None, D_MODEL, nb), lambda l, n: (l, 0, n)),
                  pl.BlockSpec((None, 1, nb), lambda l, n: (l, 0, n))],
        out_specs=pl.BlockSpec((None, r, nb), lambda l, n: (l, 0, n)),
        out_shape=jax.ShapeDtypeStruct((DEPTH, r, 6 * D_MODEL), F32),
        compiler_params=_cparams(("arbitrary", "arbitrary")),
        name="ada_mod",
    )(c_rows, ada_w, ada_b.reshape(DEPTH, 1, 6 * D_MODEL))
    return out.reshape(DEPTH, r, 6, D_MODEL)


def _t5_bucket(rel):
    nb = REL_BUCKETS // 2
    max_exact = nb // 2
    ret = jnp.where(rel > 0, nb, 0)
    n = jnp.abs(rel)
    nf = jnp.maximum(n, 1).astype(jnp.float32)
    large = max_exact + (jnp.log(nf / max_exact) / math.log(REL_MAX_DIST / max_exact) * (nb - max_exact)).astype(jnp.int32)
    large = jnp.minimum(large, nb - 1)
    return ret + jnp.where(n < max_exact, n, large)


def _bias_kernel(rb_ref, bk_ref, o_ref):
    h = pl.program_id(0)
    bk = bk_ref[...]
    acc = jnp.zeros(bk.shape, F32)
    bmax = rb_ref[0, h]
    for b in range(REL_BUCKETS):
        acc = acc + jnp.where(bk == b, rb_ref[b, h], 0.0)
        bmax = jnp.maximum(bmax, rb_ref[b, h])
    o_ref[...] = (acc - bmax) * LOG2E


def _bias_call(rel_bias, tq):
    r = jnp.arange(tq, dtype=jnp.int32)
    rel = (jnp.arange(-1, 2, dtype=jnp.int32) * tq)[:, None, None] + r[None, None, :] - r[None, :, None]
    buckets = _t5_bucket(rel)
    return pl.pallas_call(
        _bias_kernel,
        grid=(A_HEADS, 3),
        in_specs=[pl.BlockSpec(memory_space=pltpu.SMEM),
                  pl.BlockSpec((None, tq, tq), lambda h, o: (o, 0, 0))],
        out_specs=pl.BlockSpec((None, None, tq, tq), lambda h, o: (h, o, 0, 0)),
        out_shape=jax.ShapeDtypeStruct((A_HEADS, 3, tq, tq), F32),
        compiler_params=_cparams(("arbitrary", "arbitrary")),
        name="rel_bias_tiles",
    )(rel_bias, buckets)


def _rope_tables(s):
    d = C_QK_DIM
    inv = ROPE_BASE ** (-jnp.arange(0, d, 2, dtype=jnp.float32) / d)
    ang = jnp.arange(s, dtype=jnp.float32)[:, None] * inv[None, :]
    return jnp.cos(ang), jnp.sin(ang)


def _half_rms(z, g):
    lo_lane = lax.broadcasted_iota(jnp.int32, (1, LANES), 1) < A_QK_DIM
    z2 = z * z
    tot = jnp.sum(z2, axis=-1, keepdims=True)
    lo = jnp.sum(jnp.where(lo_lane, z2, 0.0), axis=-1, keepdims=True)
    ms = jnp.where(lo_lane, lo, tot - lo) * (1.0 / A_QK_DIM)
    return z * lax.rsqrt(ms + EPS) * g


def _inproj_ab_kernel(x_ref, mod_ref, g_ref, w_ref, qg_ref, kg_ref, gb_ref,
                      qa_ref, ka_ref, va_ref, qk_ref, vb_ref, ob_ref, gt_ref):
    x = x_ref[...]
    h = _rms(x, g_ref[...]) * (1.0 + mod_ref[1:2, :]) + mod_ref[0:1, :]
    hb = h.astype(MXU_DT)
    qscale = (A_QK_DIM ** -0.5) * LOG2E
    for hd in range(A_HEADS):
        c0 = hd * LANES
        q = jnp.dot(hb, w_ref[:, c0:c0 + LANES], preferred_element_type=F32)
        qa_ref[:, c0:c0 + LANES] = (_half_rms(q, qg_ref[...]) * qscale).astype(qa_ref.dtype)
        k = jnp.dot(hb, w_ref[:, A_QK_WIDTH + c0:A_QK_WIDTH + c0 + LANES], preferred_element_type=F32)
        ka_ref[:, c0:c0 + LANES] = _half_rms(k, kg_ref[...]).astype(ka_ref.dtype)
    o = 2 * A_QK_WIDTH
    va_ref[...] = jnp.dot(hb, w_ref[:, o:o + A_WIDTH], preferred_element_type=F32).astype(va_ref.dtype)
    o += A_WIDTH
    qk_ref[...] = jnp.dot(hb, w_ref[:, o:o + 2 * B_WIDTH], preferred_element_type=F32)
    o += 2 * B_WIDTH
    vb_ref[...] = jnp.dot(hb, w_ref[:, o:o + B_WIDTH], preferred_element_type=F32).astype(vb_ref.dtype)
    o += B_WIDTH
    ob_ref[...] = jnp.dot(hb, w_ref[:, o:o + B_WIDTH], preferred_element_type=F32)
    o += B_WIDTH
    gt_ref[...] = jnp.dot(hb, w_ref[:, o:o + LANES], preferred_element_type=F32) + gb_ref[...]


def _inproj_ab_call(geo, tm, x, mod_l, g, w_pad, qg, kg, gate_b):
    t = geo.t
    per_seg = geo.seg // tm
    row = lambda i: (i, 0)
    const = lambda i: (0, 0)
    widths = (A_QK_WIDTH, A_QK_WIDTH, A_WIDTH, 2 * B_WIDTH, B_WIDTH, B_WIDTH, LANES)
    dtypes = (ACT_DT, ACT_DT, ACT_DT, F32, ACT_DT, F32, F32)
    return pl.pallas_call(
        _inproj_ab_kernel,
        grid=(t // tm,),
        in_specs=[pl.BlockSpec((tm, D_MODEL), row),
                  pl.BlockSpec((None, 6, D_MODEL), lambda i: (i // per_seg, 0, 0)),
                  pl.BlockSpec((1, D_MODEL), const),
                  pl.BlockSpec((D_MODEL, AB_IN_PAD), const),
                  pl.BlockSpec((1, LANES), const),
                  pl.BlockSpec((1, LANES), const),
                  pl.BlockSpec((1, LANES), const)],
        out_specs=[pl.BlockSpec((tm, w), row) for w in widths],
        out_shape=[jax.ShapeDtypeStruct((t, w), d) for w, d in zip(widths, dtypes)],
        compiler_params=_cparams(("parallel",)),
        name="inproj_ab",
    )(x, mod_l, g, w_pad, qg, kg, gate_b)


def _conv_kernel(geo, tc, x_ref, prev_ref, next_ref, w_ref, b_ref, q_ref, k_ref):
    i = pl.program_id(0)
    row0 = i * tc
    x = x_ref[...]
    first = geo.seq_start(row0)
    last = geo.seq_start(row0 + tc) | (row0 + tc == geo.t)
    prev_row = jnp.where(first, 0.0, prev_ref[7:8, :])
    next_row = jnp.where(last, 0.0, next_ref[0:1, :])
    ridx = lax.broadcasted_iota(jnp.int32, (tc, 1), 0)
    x_prev = jnp.where(ridx == 0, prev_row, pltpu.roll(x, 1, axis=0))
    x_next = jnp.where(ridx == tc - 1, next_row, pltpu.roll(x, tc - 1, axis=0))
    y = x_prev * w_ref[0:1, :] + x * w_ref[1:2, :] + x_next * w_ref[2:3, :] + b_ref[...]
    y = _silu(y)
    q_ref[...] = y[:, :B_WIDTH].astype(q_ref.dtype)
    k_ref[...] = (y[:, B_WIDTH:] * (B_DIM ** -0.5)).astype(k_ref.dtype)


def _conv_call(geo, tc, qk, w, b):
    t = geo.t
    r8 = tc // 8
    nb8 = t // 8
    return pl.pallas_call(
        functools.partial(_conv_kernel, geo, tc),
        grid=(t // tc,),
        in_specs=[pl.BlockSpec((tc, 2 * B_WIDTH), lambda i: (i, 0)),
                  pl.BlockSpec((8, 2 * B_WIDTH), lambda i: (jnp.maximum(i * r8 - 1, 0), 0)),
                  pl.BlockSpec((8, 2 * B_WIDTH), lambda i: (jnp.minimum((i + 1) * r8, nb8 - 1), 0)),
                  pl.BlockSpec((3, 2 * B_WIDTH), lambda i: (0, 0)),
                  pl.BlockSpec((1, 2 * B_WIDTH), lambda i: (0, 0))],
        out_specs=[pl.BlockSpec((tc, B_WIDTH), lambda i: (i, 0))] * 2,
        out_shape=[jax.ShapeDtypeStruct((t, B_WIDTH), ACT_DT)] * 2,
        compiler_params=_cparams(("parallel",)),
        name="mlstm_conv",
    )(qk, qk, qk, w, b)


def _attn_finish(lam_init, acc0, l0, acc1, l1, dl_ref, ng_ref, o_ref):
    dl = dl_ref[...]
    lam = (jnp.exp(jnp.sum(dl[0:1] * dl[1:2], axis=-1, keepdims=True))
           - jnp.exp(jnp.sum(dl[2:3] * dl[3:4], axis=-1, keepdims=True)) + lam_init)
    out = acc0 / l0 - lam * (acc1 / l1)
    o_ref[...] = (_rms(out, ng_ref[...]) * (1.0 - lam_init)).astype(o_ref.dtype)


def _attn_kernel(lam_init, nk, q_ref, k_ref, v_ref, bias_ref, sc_ref, dl_ref, ng_ref, o_ref,
                 m_sc, l_sc, acc_sc):
    h = pl.program_id(1)
    i = pl.program_id(2)
    j = pl.program_id(3)

    @pl.when(j == 0)
    def _():
        m_sc[...] = jnp.full(m_sc.shape, NEG_BIG, F32)
        l_sc[...] = jnp.zeros(l_sc.shape, F32)
        acc_sc[...] = jnp.zeros(acc_sc.shape, F32)

    q = q_ref[...]
    lo_lane = lax.broadcasted_iota(jnp.int32, (1, LANES), 1) < A_QK_DIM
    qsub = (jnp.where(lo_lane, q, jnp.zeros_like(q)), jnp.where(lo_lane, jnp.zeros_like(q), q))
    k = k_ref[...]
    v = v_ref[...]

    def step(near):
        if near:
            shift = 0.0
        else:
            shift = jnp.where(j < i, sc_ref[h, 0], sc_ref[h, 1])
        for sub in range(2):
            s = _mm_nt(qsub[sub], k)
            if near:
                s = s + bias_ref[...]
            m_old = m_sc[sub]
            m_new = jnp.maximum(m_old, jnp.max(s, axis=-1, keepdims=True) + shift)
            p = jnp.exp2(s - (m_new - shift))
            alpha = jnp.exp2(m_old - m_new)
            l_sc[sub] = alpha * l_sc[sub] + jnp.sum(p, axis=-1, keepdims=True)
            acc_sc[sub] = alpha * acc_sc[sub] + _mm(p, v)
            m_sc[sub] = m_new

    near = jnp.abs(j - i) <= 1
    pl.when(near)(lambda: step(True))
    pl.when(jnp.logical_not(near))(lambda: step(False))

    @pl.when(j == nk - 1)
    def _():
        _attn_finish(lam_init, acc_sc[0], l_sc[0], acc_sc[1], l_sc[1], dl_ref, ng_ref, o_ref)


def _attn_bounded_kernel(lam_init, nk, q_ref, k_ref, v_ref, bias_ref, sc_ref, dl_ref, ng_ref, o_ref, acc_sc):
    h = pl.program_id(1)
    i = pl.program_id(2)
    j = pl.program_id(3)

    @pl.when(j == 0)
    def _():
        acc_sc[...] = jnp.zeros(acc_sc.shape, F32)

    @pl.when(j == jnp.maximum(i - 1, 0))
    def _():
        acc_sc[...] = acc_sc[...] * sc_ref[h, 2]

    @pl.when(j == i + 2)
    def _():
        acc_sc[...] = acc_sc[...] * sc_ref[h, 3]

    q = q_ref[...]
    k = k_ref[...]
    v = v_ref[...]
    lo = (lax.broadcasted_iota(jnp.int32, q.shape, 1) < A_QK_DIM).astype(F32).astype(q.dtype)
    qsub = (q * lo, q * (1 - lo))
    ones_col = (lax.broadcasted_iota(jnp.int32, v.shape, 1) == 0).astype(v.dtype)
    v_aug = jnp.concatenate([v, ones_col], axis=1)

    def step(near):
        for sub in range(2):
            s = _mm_nt(qsub[sub], k)
            if near:
                s = s + bias_ref[...]
            acc_sc[sub] += _mm(jnp.exp2(s), v_aug)

    near = jnp.abs(j - i) <= 1
    pl.when(near)(lambda: step(True))
    pl.when(jnp.logical_not(near))(lambda: step(False))

    @pl.when(j == nk - 1)
    def _():
        a0 = acc_sc[0]
        a1 = acc_sc[1]
        _attn_finish(lam_init, a0[:, :A_V_DIM], a0[:, A_V_DIM:A_V_DIM + 1], a1[:, :A_V_DIM],
                     a1[:, A_V_DIM:A_V_DIM + 1], dl_ref, ng_ref, o_ref)


def _attn_scalars(bias, tq, q_gain, k_gain):
    far_l = bias[:, 0, tq - 1, 0]
    far_r = bias[:, 2, 0, tq - 1]
    sc = jnp.stack([far_l, far_r, jnp.exp2(far_l), jnp.exp2(-far_r)], axis=-1)
    bound = A_QK_DIM * jnp.max(jnp.abs(q_gain)) * jnp.max(jnp.abs(k_gain)) * (A_QK_DIM ** -0.5) * LOG2E * 1.02
    spread = -jnp.min(bias)
    ok = bound + 2.0 * spread <= 80.0
    return sc, ok


def _attn_call(geo, tq, lam_init, bounded, qa, ka, va, bias, sc, dl, ng):
    outs = []
    for (nb, s, row_off) in ((geo.bp, geo.sp, 0), (geo.bs, geo.ss, geo.tp)):
        nq = s // tq
        off = row_off // tq
        qmap = lambda b, h, i, j, off=off, nq=nq: (off + b * nq + i, h)
        kmap = lambda b, h, i, j, off=off, nq=nq: (off + b * nq + j, h)
        omap = lambda b, h, i, j, nq=nq: (b * nq + i, h)
        if bounded:
            body = functools.partial(_attn_bounded_kernel, lam_init, nq)
            scratch = [pltpu.VMEM((2, tq, 2 * LANES), F32)]
        else:
            body = functools.partial(_attn_kernel, lam_init, nq)
            scratch = [pltpu.VMEM((2, tq, 1), F32), pltpu.VMEM((2, tq, 1), F32), pltpu.VMEM((2, tq, LANES), F32)]
        outs.append(pl.pallas_call(
            body,
            grid=(nb, A_HEADS, nq, nq),
            in_specs=[pl.BlockSpec((tq, LANES), qmap),
                      pl.BlockSpec((tq, LANES), kmap),
                      pl.BlockSpec((tq, LANES), kmap),
                      pl.BlockSpec((None, None, tq, tq), lambda b, h, i, j: (h, jnp.clip(j - i + 1, 0, 2), 0, 0)),
                      pl.BlockSpec(memory_space=pltpu.SMEM),
                      pl.BlockSpec((4, A_QK_DIM), lambda b, h, i, j: (0, 0)),
                      pl.BlockSpec((1, LANES), lambda b, h, i, j: (0, 0))],
            out_specs=pl.BlockSpec((tq, LANES), omap),
            out_shape=jax.ShapeDtypeStruct((nb * s, A_WIDTH), ACT_DT),
            scratch_shapes=scratch,
            compiler_params=_cparams(("parallel", "parallel", "parallel", "arbitrary")),
            name="diff_attn_bounded" if bounded else "diff_attn",
        )(qa, ka, va, bias, sc, dl, ng))
    return jnp.concatenate(outs, axis=0)


def _tri(lower):
    r = lax.broadcasted_iota(jnp.int32, (CHUNK, CHUNK), 0)
    c = lax.broadcasted_iota(jnp.int32, (CHUNK, CHUNK), 1)
    return (c <= r) if lower else (c >= r)


def _mlstm_kernel(geo, tb, reverse, *refs):
    if reverse:
        q_ref, k_ref, v_ref, g_ref, hf_ref, ob_ref, ng_ref, o_ref, c_sc, m_sc = refs
    else:
        q_ref, k_ref, v_ref, g_ref, o_ref, c_sc, m_sc = refs
    step = pl.program_id(0)
    nblk = geo.t // tb
    blk = (nblk - 1 - step) if reverse else step
    row0 = blk * tb
    if reverse:
        fresh = geo.seq_start(row0 + tb) | (row0 + tb == geo.t)
    else:
        fresh = geo.seq_start(row0)

    @pl.when(fresh)
    def _():
        c_sc[...] = jnp.zeros(c_sc.shape, F32)
        m_sc[...] = jnp.zeros(m_sc.shape, F32)

    mask = _tri(not reverse)
    cum_l = mask.astype(F32)
    cum_r = _tri(reverse).astype(F32)
    ones_col = (lax.broadcasted_iota(jnp.int32, (CHUNK, LANES), 1) == 0).astype(MXU_DT)
    nch = tb // CHUNK

    def chunk(ci, carry):
        c_idx = (nch - 1 - ci) if reverse else ci
        r0 = pl.multiple_of(c_idx * CHUNK, CHUNK)
        g = g_ref[pl.ds(r0, CHUNK), :]
        g_t = g.T
        b_col = _mm_hi(cum_l, _log_sigmoid(g))
        b_row = _mm_hi(_log_sigmoid(g_t), cum_r)
        for hd in range(B_HEADS):
            ci_col = hd * N_GATES + (2 if reverse else 0)
            cf_col = ci_col + 1
            lanes = slice(hd * B_DIM, (hd + 1) * B_DIM)
            q = q_ref[pl.ds(r0, CHUNK), lanes]
            k = k_ref[pl.ds(r0, CHUNK), lanes]
            v = v_ref[pl.ds(r0, CHUNK), lanes]
            bc = b_col[:, cf_col:cf_col + 1]
            br = b_row[cf_col:cf_col + 1, :]
            ic = g[:, ci_col:ci_col + 1]
            ir = g_t[ci_col:ci_col + 1, :]
            m_prev = m_sc[hd]
            log_d = jnp.where(mask, bc - br + ir, NEG_BIG)
            m_inter = bc + m_prev
            m_t = jnp.maximum(jnp.max(log_d, axis=-1, keepdims=True), m_inter)
            s = _mm_nt(q, k) * jnp.exp(log_d - m_t)
            inter = jnp.exp(m_inter - m_t)
            c_aug = c_sc[hd]
            qc = _mm(q, c_aug)
            num = _mm(s, v) + inter * qc[:, :B_DIM]
            den = jnp.sum(s, axis=-1, keepdims=True) + inter * qc[:, B_DIM:B_DIM + 1]
            hout = num / jnp.maximum(jnp.abs(den), jnp.exp(-m_t))
            b_last = bc[0:1, :] if reverse else bc[CHUNK - 1:CHUNK, :]
            log_w = b_last - bc + ic
            m_new = jnp.maximum(b_last + m_prev, jnp.max(log_w, axis=0, keepdims=True))
            w = jnp.exp(log_w - m_new)
            decay = jnp.exp(b_last + m_prev - m_new)
            v_aug = jnp.concatenate([v, ones_col], axis=1)
            c_sc[hd] = decay * c_aug + _mm_tn(k.astype(F32) * w, v_aug)
            m_sc[hd] = m_new
            if reverse:
                hsum = hf_ref[pl.ds(r0, CHUNK), lanes] + hout
                y = _rms(hsum, ng_ref[...]) * _sigmoid(ob_ref[pl.ds(r0, CHUNK), lanes])
                o_ref[pl.ds(r0, CHUNK), lanes] = y.astype(o_ref.dtype)
            else:
                o_ref[pl.ds(r0, CHUNK), lanes] = hout
        return carry

    lax.fori_loop(0, nch, chunk, 0)


def _mlstm_call(geo, tb, qb, kb, vb, gates, ob, ng):
    t = geo.t
    nblk = t // tb
    fmap = lambda s: (s, 0)
    rmap = lambda s: (nblk - 1 - s, 0)
    scratch = [pltpu.VMEM((B_HEADS, B_DIM, 2 * B_DIM), F32), pltpu.VMEM((B_HEADS, 1, 1), F32)]
    wide = lambda m: pl.BlockSpec((tb, B_WIDTH), m)
    hf = pl.pallas_call(
        functools.partial(_mlstm_kernel, geo, tb, False),
        grid=(nblk,),
        in_specs=[wide(fmap), wide(fmap), wide(fmap), pl.BlockSpec((tb, LANES), fmap)],
        out_specs=wide(fmap),
        out_shape=jax.ShapeDtypeStruct((t, B_WIDTH), F32),
        scratch_shapes=scratch,
        compiler_params=_cparams(("arbitrary",)),
        name="mlstm_fwd",
    )(qb, kb, vb, gates)
    return pl.pallas_call(
        functools.partial(_mlstm_kernel, geo, tb, True),
        grid=(nblk,),
        in_specs=[wide(rmap), wide(rmap), wide(rmap), pl.BlockSpec((tb, LANES), rmap), wide(rmap), wide(rmap),
                  pl.BlockSpec((1, B_DIM), lambda s: (0, 0))],
        out_specs=wide(rmap),
        out_shape=jax.ShapeDtypeStruct((t, B_WIDTH), ACT_DT),
        scratch_shapes=scratch,
        compiler_params=_cparams(("arbitrary",)),
        name="mlstm_bwd",
    )(qb, kb, vb, gates, hf, ob, ng)


def _inproj_c_kernel(x_ref, mod_ref, g_ref, w_ref, cos_ref, sin_ref, q_ref, k_ref, v_ref, gt_ref, h_sc):
    j = pl.program_id(1)

    @pl.when(j == 0)
    def _():
        x = x_ref[...]
        h = _rms(x, g_ref[...]) * (1.0 + mod_ref[1:2, :]) + mod_ref[0:1, :]
        h_sc[...] = h.astype(h_sc.dtype)

    y = jnp.dot(h_sc[...], w_ref[...], preferred_element_type=F32)

    def rope(scale):
        cos = cos_ref[...]
        sin = sin_ref[...]
        half = C_QK_DIM // 2
        parts = []
        for hd in range(C_HEADS):
            x1 = y[:, hd * C_QK_DIM:hd * C_QK_DIM + half]
            x2 = y[:, hd * C_QK_DIM + half:(hd + 1) * C_QK_DIM]
            parts.append((x1 * cos - x2 * sin) * scale)
            parts.append((x1 * sin + x2 * cos) * scale)
        return jnp.concatenate(parts, axis=1)

    @pl.when(j == 0)
    def _():
        q_ref[...] = rope(1.0).astype(q_ref.dtype)

    @pl.when(j == 1)
    def _():
        k_ref[...] = rope(C_QK_DIM ** -0.5).astype(k_ref.dtype)

    @pl.when((j == 2) | (j == 3))
    def _():
        v_ref[...] = y.astype(v_ref.dtype)

    @pl.when(j >= 4)
    def _():
        gt_ref[...] = _silu(y).astype(gt_ref.dtype)


def _inproj_c_call(geo, tm, x, mod_l, g, w, cos, sin):
    t = geo.t
    per_seg = geo.seg // tm
    nw = D_MODEL
    return pl.pallas_call(
        _inproj_c_kernel,
        grid=(t // tm, C_IN // nw),
        in_specs=[pl.BlockSpec((tm, D_MODEL), lambda i, j: (i, 0)),
                  pl.BlockSpec((None, 6, D_MODEL), lambda i, j: (i // per_seg, 0, 0)),
                  pl.BlockSpec((1, D_MODEL), lambda i, j: (0, 0)),
                  pl.BlockSpec((D_MODEL, nw), lambda i, j: (0, j)),
                  pl.BlockSpec((tm, C_QK_DIM // 2), lambda i, j: (geo.pos_block(i, tm), 0)),
                  pl.BlockSpec((tm, C_QK_DIM // 2), lambda i, j: (geo.pos_block(i, tm), 0))],
        out_specs=[pl.BlockSpec((tm, nw), lambda i, j: (i, 0)),
                   pl.BlockSpec((tm, nw), lambda i, j: (i, 0)),
                   pl.BlockSpec((tm, nw), lambda i, j: (i, jnp.clip(j - 2, 0, 1))),
                   pl.BlockSpec((tm, nw), lambda i, j: (i, jnp.clip(j - 4, 0, 1)))],
        out_shape=[jax.ShapeDtypeStruct((t, C_QK_WIDTH), ACT_DT),
                   jax.ShapeDtypeStruct((t, C_QK_WIDTH), ACT_DT),
                   jax.ShapeDtypeStruct((t, C_V_WIDTH), ACT_DT),
                   jax.ShapeDtypeStruct((t, C_V_WIDTH), ACT_DT)],
        scratch_shapes=[pltpu.VMEM((tm, D_MODEL), MXU_DT)],
        compiler_params=_cparams(("parallel", "arbitrary")),
        name="inproj_c",
    )(x, mod_l, g, w, cos, sin)


def _ret_kernel(geo, tb, reverse, *refs):
    if reverse:
        q_ref, k_ref, v_ref, dlg_ref, yf_ref, gt_ref, ng_ref, o_ref, r_sc = refs
    else:
        q_ref, k_ref, v_ref, dlg_ref, o_ref, r_sc = refs
    step = pl.program_id(0)
    nblk = geo.t // tb
    blk = (nblk - 1 - step) if reverse else step
    row0 = blk * tb
    if reverse:
        fresh = geo.seq_start(row0 + tb) | (row0 + tb == geo.t)
    else:
        fresh = geo.seq_start(row0)

    @pl.when(fresh)
    def _():
        r_sc[...] = jnp.zeros(r_sc.shape, F32)

    d = 1 if reverse else 0
    lg_all = _log_sigmoid(dlg_ref[...])
    ti = lax.broadcasted_iota(jnp.int32, (CHUNK, CHUNK), 0)
    si = lax.broadcasted_iota(jnp.int32, (CHUNK, CHUNK), 1)
    dist = ((si - ti) if reverse else (ti - si)).astype(F32)
    pos = lax.broadcasted_iota(jnp.int32, (CHUNK, 1), 0).astype(F32)
    upos = (CHUNK - 1.0 - pos) if reverse else pos
    nch = tb // CHUNK
    decays = []
    for hd in range(C_HEADS):
        lg = lg_all[d:d + 1, hd:hd + 1]
        intra = jnp.where(dist >= 0, jnp.exp(jnp.maximum(dist, 0.0) * lg), 0.0)
        decays.append((intra, jnp.exp((upos + 1.0) * lg), jnp.exp((CHUNK - 1.0 - upos) * lg),
                       jnp.exp(CHUNK * lg)))

    def chunk(ci, carry):
        c_idx = (nch - 1 - ci) if reverse else ci
        r0 = pl.multiple_of(c_idx * CHUNK, CHUNK)
        for hd in range(C_HEADS):
            intra, q_scale, k_scale, c_decay = decays[hd]
            ql = slice(hd * C_QK_DIM, (hd + 1) * C_QK_DIM)
            vl = slice(hd * C_V_DIM, (hd + 1) * C_V_DIM)
            q = q_ref[pl.ds(r0, CHUNK), ql]
            k = k_ref[pl.ds(r0, CHUNK), ql]
            v = v_ref[pl.ds(r0, CHUNK), vl]
            r_old = r_sc[hd]
            s = _mm_nt(q, k) * intra
            y = _mm(s, v) + q_scale * _mm(q, r_old)
            r_sc[hd] = c_decay * r_old + _mm_tn(k.astype(F32) * k_scale, v)
            if reverse:
                ysum = yf_ref[pl.ds(r0, CHUNK), vl] + y
                out = _rms(ysum, ng_ref[...]) * gt_ref[pl.ds(r0, CHUNK), vl].astype(F32)
                o_ref[pl.ds(r0, CHUNK), vl] = out.astype(o_ref.dtype)
            else:
                o_ref[pl.ds(r0, CHUNK), vl] = y
        return carry

    lax.fori_loop(0, nch, chunk, 0)


def _ret_call(geo, tb, q, k, v, decay_logit, gt, ng):
    t = geo.t
    nblk = t // tb
    fmap = lambda s: (s, 0)
    rmap = lambda s: (nblk - 1 - s, 0)
    scratch = [pltpu.VMEM((C_HEADS, C_QK_DIM, C_V_DIM), F32)]
    qk = lambda m: pl.BlockSpec((tb, C_QK_WIDTH), m)
    vv = lambda m: pl.BlockSpec((tb, C_V_WIDTH), m)
    dspec = pl.BlockSpec((2, C_HEADS), lambda s: (0, 0))
    yf = pl.pallas_call(
        functools.partial(_ret_kernel, geo, tb, False),
        grid=(nblk,),
        in_specs=[qk(fmap), qk(fmap), vv(fmap), dspec],
        out_specs=vv(fmap),
        out_shape=jax.ShapeDtypeStruct((t, C_V_WIDTH), F32),
        scratch_shapes=scratch,
        compiler_params=_cparams(("arbitrary",)),
        name="ret_fwd",
    )(q, k, v, decay_logit)
    return pl.pallas_call(
        functools.partial(_ret_kernel, geo, tb, True),
        grid=(nblk,),
        in_specs=[qk(rmap), qk(rmap), vv(rmap), dspec, vv(rmap), vv(rmap),
                  pl.BlockSpec((1, C_V_DIM), lambda s: (0, 0))],
        out_specs=vv(rmap),
        out_shape=jax.ShapeDtypeStruct((t, C_V_WIDTH), ACT_DT),
        scratch_shapes=scratch,
        compiler_params=_cparams(("arbitrary",)),
        name="ret_bwd",
    )(q, k, v, decay_logit, yf, gt, ng)


def _route(probs):
    p = [probs[e:e + 1, :] for e in range(N_EXPERTS)]
    scores = []
    for g in range(N_GROUPS):
        a, b, c, d = p[EPG * g:EPG * g + EPG]
        hi1, lo1 = jnp.maximum(a, b), jnp.minimum(a, b)
        hi2, lo2 = jnp.maximum(c, d), jnp.minimum(c, d)
        scores.append(jnp.maximum(hi1, hi2) + jnp.maximum(jnp.minimum(hi1, hi2), jnp.maximum(lo1, lo2)))
    g_sel = jnp.zeros(scores[0].shape, jnp.int32)
    best = scores[0]
    for g in range(1, N_GROUPS):
        better = scores[g] > best
        g_sel = jnp.where(better, g, g_sel)
        best = jnp.where(better, scores[g], best)
    vals = []
    for kk in range(EPG):
        v = p[kk]
        for g in range(1, N_GROUPS):
            v = jnp.where(g_sel == g, p[EPG * g + kk], v)
        vals.append(v)

    def argmax4(xs):
        idx = jnp.zeros(xs[0].shape, jnp.int32)
        top = xs[0]
        for kk in range(1, EPG):
            better = xs[kk] > top
            idx = jnp.where(better, kk, idx)
            top = jnp.where(better, xs[kk], top)
        return idx, top

    i1, v1 = argmax4(vals)
    i2, v2 = argmax4([jnp.where(i1 == kk, -1.0, vals[kk]) for kk in range(EPG)])
    tot = v1 + v2
    w1, w2 = v1 / tot, v2 / tot
    e1 = g_sel * EPG + i1
    e2 = g_sel * EPG + i2
    eidx = lax.broadcasted_iota(jnp.int32, probs.shape, 0)
    return jnp.where(eidx == e1, w1, 0.0) + jnp.where(eidx == e2, w2, 0.0), g_sel


def _outproj_kernel(nparts, *refs):
    y_refs = refs[:nparts]
    w_refs = refs[nparts:2 * nparts]
    x_ref, mod_ref, g_ref, rw_ref, rb_ref, xo_ref, hx_ref, grp_ref = refs[2 * nparts:]
    m = jnp.dot(y_refs[0][...], w_refs[0][...], preferred_element_type=F32)
    for p in range(1, nparts):
        m = m + jnp.dot(y_refs[p][...], w_refs[p][...], preferred_element_type=F32)
    x = x_ref[...] + mod_ref[2:3, :] * m
    xo_ref[...] = x
    h = _rms(x, g_ref[...]) * (1.0 + mod_ref[4:5, :]) + mod_ref[3:4, :]
    hx_ref[:, :D_MODEL] = h
    logits = lax.dot_general(rw_ref[...], h, (((1,), (1,)), ((), ())), precision=HI,
                             preferred_element_type=F32) + rb_ref[...]
    z = jnp.exp(logits - jnp.max(logits, axis=0, keepdims=True))
    probs = z / jnp.sum(z, axis=0, keepdims=True)
    cmb, g_sel = _route(probs)
    pad = jnp.zeros((LANES - N_EXPERTS, cmb.shape[1]), F32)
    hx_ref[:, D_MODEL:] = jnp.concatenate([cmb, pad], axis=0).T
    grp_ref[...] = g_sel


def _outproj_call(geo, tm, ys, ws, x, mod_l, g, rw_t, rb):
    t = geo.t
    per_seg = geo.seg // tm
    row = lambda i: (i, 0)
    const = lambda i: (0, 0)
    n = len(ys)
    return pl.pallas_call(
        functools.partial(_outproj_kernel, n),
        grid=(t // tm,),
        in_specs=([pl.BlockSpec((tm, y.shape[1]), row) for y in ys]
                  + [pl.BlockSpec(w.shape, const) for w in ws]
                  + [pl.BlockSpec((tm, D_MODEL), row),
                     pl.BlockSpec((None, 6, D_MODEL), lambda i: (i // per_seg, 0, 0)),
                     pl.BlockSpec((1, D_MODEL), const),
                     pl.BlockSpec((N_EXPERTS, D_MODEL), const),
                     pl.BlockSpec((N_EXPERTS, 1), const)]),
        out_specs=[pl.BlockSpec((tm, D_MODEL), row), pl.BlockSpec((tm, HX_W), row),
                   pl.BlockSpec((None, 1, tm), lambda i: (i, 0, 0))],
        out_shape=[jax.ShapeDtypeStruct((t, D_MODEL), F32), jax.ShapeDtypeStruct((t, HX_W), F32),
                   jax.ShapeDtypeStruct((t // tm, 1, tm), jnp.int32)],
        compiler_params=_cparams(("parallel",)),
        name="outproj_router",
    )(*ys, *ws, x, mod_l, g, rw_t, rb)


def _plan_kernel(tm, grp_ref, pos_ref, tg_ref, nv_ref):
    nblk, _, blk = grp_ref.shape
    gid = lax.broadcasted_iota(jnp.int32, (8, blk), 0)
    r = lax.broadcasted_iota(jnp.int32, (blk, blk), 0)
    c = lax.broadcasted_iota(jnp.int32, (blk, blk), 1)
    prefix = (r <= c).astype(MXU_DT)

    def count(b, acc):
        return acc + jnp.sum((grp_ref[b] == gid).astype(F32), axis=1, keepdims=True)

    counts = lax.fori_loop(0, nblk, count, jnp.zeros((8, 1), F32))
    padded = jnp.floor((counts + (tm - 1.0)) / tm) * tm
    row8 = lax.broadcasted_iota(jnp.int32, (8, 1), 0)
    offs = jnp.zeros((8, 1), F32)
    ends = []
    run = jnp.zeros((1, 1), F32)
    for g in range(N_GROUPS):
        offs = jnp.where(row8 == g, run, offs)
        run = run + padded[g:g + 1, :]
        ends.append(run)

    def place(b, carry):
        oh = (grp_ref[b] == gid).astype(F32)
        pre = jnp.dot(oh.astype(MXU_DT), prefix, preferred_element_type=F32)
        pos = jnp.sum(oh * (offs + carry + pre - 1.0), axis=0, keepdims=True)
        pos_ref[b] = pos.astype(jnp.int32)
        return carry + pre[:, blk - 1:blk]

    lax.fori_loop(0, nblk, place, jnp.zeros((8, 1), F32))
    start = lax.broadcasted_iota(jnp.int32, tg_ref.shape, 1).astype(F32) * tm
    tg = jnp.zeros(tg_ref.shape, F32)
    for g in range(N_GROUPS - 1):
        tg = tg + (ends[g] <= start).astype(F32)
    tg_ref[...] = tg.astype(jnp.int32)
    nv_ref[...] = jnp.broadcast_to(run / tm, nv_ref.shape).astype(jnp.int32)


def _plan_call(tm, grp):
    nblk, _, blk = grp.shape
    nt = nblk * blk // tm + N_GROUPS
    ntp = -(-nt // LANES) * LANES
    pos, tg, nv = pl.pallas_call(
        functools.partial(_plan_kernel, tm),
        out_shape=[jax.ShapeDtypeStruct(grp.shape, jnp.int32), jax.ShapeDtypeStruct((1, ntp), jnp.int32),
                   jax.ShapeDtypeStruct((1, LANES), jnp.int32)],
        compiler_params=pltpu.CompilerParams(vmem_limit_bytes=VMEM_LIMIT),
        name="moe_plan",
    )(grp)
    return pos, tg[0, :nt], nv[0, :1]


def _dispatch_kernel(tb, nsteps, pos_ref, hx_ref, init_ref, xs_ref, sem):
    del init_ref
    i = pl.program_id(0)
    row0 = i * tb

    def issue(r, carry):
        pltpu.make_async_copy(hx_ref.at[pl.ds(row0 + r, 1)], xs_ref.at[pl.ds(pos_ref[0, r], 1)], sem).start()
        return carry

    lax.fori_loop(0, tb, issue, 0, unroll=8)

    def wait_rows():
        pltpu.make_async_copy(hx_ref.at[pl.ds(0, tb)], xs_ref.at[pl.ds(0, tb)], sem).wait()

    pl.when(i > 0)(wait_rows)
    pl.when(i == nsteps - 1)(wait_rows)


def _dispatch_call(tb, pos, hx, xs_init):
    nsteps = hx.shape[0] // tb
    return pl.pallas_call(
        functools.partial(_dispatch_kernel, tb, nsteps),
        grid=(nsteps,),
        in_specs=[pl.BlockSpec((None, 1, tb), lambda i: (i, 0, 0), memory_space=pltpu.SMEM),
                  pl.BlockSpec(memory_space=pl.ANY),
                  pl.BlockSpec(memory_space=pl.ANY)],
        out_specs=pl.BlockSpec(memory_space=pl.ANY),
        out_shape=jax.ShapeDtypeStruct(xs_init.shape, xs_init.dtype),
        scratch_shapes=[pltpu.SemaphoreType.DMA(())],
        input_output_aliases={2: 0},
        compiler_params=pltpu.CompilerParams(dimension_semantics=("arbitrary",), disable_bounds_checks=True,
                                             has_side_effects=True),
        name="moe_dispatch",
    )(pos, hx, xs_init)


def _moe_kernel(tg_ref, nv_ref, xs_ref, w1_ref, w3_ref, w2_ref, o_ref, acc_sc):
    n = pl.program_id(0)
    e = pl.program_id(1)

    @pl.when(n < nv_ref[0])
    def _():
        @pl.when(e == 0)
        def _():
            acc_sc[...] = jnp.zeros(acc_sc.shape, F32)

        h = xs_ref[:, :D_MODEL].astype(MXU_DT)
        a = jnp.dot(h, w1_ref[...], preferred_element_type=F32)
        b = jnp.dot(h, w3_ref[...], preferred_element_type=F32)
        he = _silu(a) * b
        cmb = xs_ref[:, D_MODEL:]
        sel = lax.broadcasted_iota(jnp.int32, cmb.shape, 1) == tg_ref[n] * EPG + e
        c = jnp.sum(jnp.where(sel, cmb, 0.0), axis=-1, keepdims=True)
        acc_sc[...] += c * _mm(he, w2_ref[...])

        @pl.when(e == EPG - 1)
        def _():
            o_ref[...] = acc_sc[...]

    @pl.when((n >= nv_ref[0]) & (e == EPG - 1))
    def _():
        o_ref[...] = jnp.zeros(o_ref.shape, F32)


def _moe_call(tm, layer, tg, nv, xs, w1, w3, w2):
    rows = xs.shape[0]
    nt = rows // tm
    tile = lambda n, e, tg, nv: (jnp.minimum(n, nv[0] - 1), 0)
    out_tile = lambda n, e, tg, nv: (n, 0)

    def wmap(n, e, tg, nv):
        return (layer, tg[jnp.minimum(n, nv[0] - 1)] * EPG + jnp.where(n < nv[0], e, EPG - 1), 0, 0)

    return pl.pallas_call(
        _moe_kernel,
        grid_spec=pltpu.PrefetchScalarGridSpec(
            num_scalar_prefetch=2,
            grid=(nt, EPG),
            in_specs=[pl.BlockSpec((tm, HX_W), tile),
                      pl.BlockSpec((None, None, D_MODEL, D_FF), wmap),
                      pl.BlockSpec((None, None, D_MODEL, D_FF), wmap),
                      pl.BlockSpec((None, None, D_FF, D_MODEL), wmap)],
            out_specs=pl.BlockSpec((tm, D_MODEL), out_tile),
            scratch_shapes=[pltpu.VMEM((tm, D_MODEL), F32)]),
        out_shape=jax.ShapeDtypeStruct((rows, D_MODEL), F32),
        compiler_params=_cparams(("arbitrary", "arbitrary")),
        name="moe",
    )(tg, nv, xs, w1, w3, w2)


def _collect_kernel(tb, nsteps, pos_ref, posn_ref, ys_ref, x_ref, mod_ref, o_ref, buf, sem):
    i = pl.program_id(0)
    slot = i % 2

    def issue(p_ref, s):
        def body(r, carry):
            pltpu.make_async_copy(ys_ref.at[pl.ds(p_ref[0, r], 1)], buf.at[s, pl.ds(r, 1)], sem.at[s]).start()
            return carry
        lax.fori_loop(0, tb, body, 0, unroll=8)

    pl.when(i == 0)(lambda: issue(pos_ref, 0))
    pl.when(i + 1 < nsteps)(lambda: issue(posn_ref, 1 - slot))
    pltpu.make_async_copy(ys_ref.at[pl.ds(0, tb)], buf.at[slot], sem.at[slot]).wait()
    o_ref[...] = x_ref[...] + mod_ref[5:6, :] * buf[slot]


def _collect_call(geo, tb, pos, ys, x, mod_l):
    t = geo.t
    nsteps = t // tb
    per_seg = geo.seg // tb
    return pl.pallas_call(
        functools.partial(_collect_kernel, tb, nsteps),
        grid=(nsteps,),
        in_specs=[pl.BlockSpec((None, 1, tb), lambda i: (i, 0, 0), memory_space=pltpu.SMEM),
                  pl.BlockSpec((None, 1, tb), lambda i: (jnp.minimum(i + 1, nsteps - 1), 0, 0),
                               memory_space=pltpu.SMEM),
                  pl.BlockSpec(memory_space=pl.ANY),
                  pl.BlockSpec((tb, D_MODEL), lambda i: (i, 0)),
                  pl.BlockSpec((None, 6, D_MODEL), lambda i: (i // per_seg, 0, 0))],
        out_specs=pl.BlockSpec((tb, D_MODEL), lambda i: (i, 0)),
        out_shape=jax.ShapeDtypeStruct((t, D_MODEL), F32),
        scratch_shapes=[pltpu.VMEM((2, tb, D_MODEL), F32), pltpu.SemaphoreType.DMA((2,))],
        compiler_params=pltpu.CompilerParams(dimension_semantics=("arbitrary",), vmem_limit_bytes=VMEM_LIMIT,
                                             disable_bounds_checks=True),
        name="moe_collect",
    )(pos, pos, ys, x, mod_l)


def _tiles(geo):
    seg = geo.seg
    return dict(tm=min(512, seg), tq=min(1024, seg), tscan=min(512, seg), tmoe=min(1024, seg))


def _forward(geo, tiles, x_prompt, x_sample, c_prompt, c_sample, rel_bias, router_w, router_b, ada_w, ada_b,
             norm_mix_g, norm_ffn_g, w_in_ab, w_out_ab, q_norm_g, k_norm_g, diff_lambda, diff_norm_g,
             mlstm_conv_w, mlstm_conv_b, mlstm_gate_b, mlstm_norm_g, w_in_c, w_out_c, ret_decay_logit,
             ret_norm_g, moe_w1, moe_w3, moe_w2):
    tm, tq, tscan, tmoe = tiles["tm"], tiles["tq"], tiles["tscan"], tiles["tmoe"]
    x = jnp.concatenate([x_prompt.reshape(geo.tp, D_MODEL), x_sample.reshape(geo.t - geo.tp, D_MODEL)], axis=0)
    c_rows = jnp.concatenate([jnp.repeat(c_prompt, geo.sp // geo.seg, axis=0),
                              jnp.repeat(c_sample, geo.ss // geo.seg, axis=0)], axis=0)
    mod = _ada_call(c_rows, ada_w, ada_b)

    bias = _bias_call(rel_bias, tq)
    cos, sin = _rope_tables(max(geo.sp, geo.ss))
    rw_t = router_w.T
    rb = router_b.reshape(N_EXPERTS, 1)
    w1 = moe_w1.astype(MXU_DT)
    w3 = moe_w3.astype(MXU_DT)
    w2 = moe_w2.astype(MXU_DT)
    xs = jnp.zeros((geo.t + N_GROUPS * tmoe, HX_W), F32)

    for l in range(DEPTH):
        j = l // 2
        mod_l = mod[l]
        g_mix = norm_mix_g[l].reshape(1, D_MODEL)
        g_ffn = norm_ffn_g[l].reshape(1, D_MODEL)
        if l % 2 == 0:
            lam_init = 0.8 - 0.6 * math.exp(-0.3 * l)
            w_pad = jnp.pad(w_in_ab[j], ((0, 0), (0, AB_IN_PAD - AB_IN))).astype(MXU_DT)
            qg = jnp.tile(q_norm_g[j], 2).reshape(1, LANES)
            kg = jnp.tile(k_norm_g[j], 2).reshape(1, LANES)
            gate_b = jnp.pad(mlstm_gate_b[j].reshape(1, B_HEADS * N_GATES), ((0, 0), (0, LANES - B_HEADS * N_GATES)))
            qa, ka, va, qk, vb, ob, gates = _inproj_ab_call(geo, tm, x, mod_l, g_mix, w_pad, qg, kg, gate_b)
            sc, bounded_ok = _attn_scalars(bias, tq, q_norm_g[j], k_norm_g[j])
            ya = lax.cond(bounded_ok,
                          functools.partial(_attn_call, geo, tq, lam_init, True),
                          functools.partial(_attn_call, geo, tq, lam_init, False),
                          qa, ka, va, bias, sc, diff_lambda[j], diff_norm_g[j].reshape(1, A_V_DIM))
            qb, kb = _conv_call(geo, tscan, qk, mlstm_conv_w[j], mlstm_conv_b[j].reshape(1, 2 * B_WIDTH))
            yb = _mlstm_call(geo, tscan, qb, kb, vb, gates, ob, mlstm_norm_g[j].reshape(1, B_DIM))
            w_o = w_out_ab[j].astype(MXU_DT)
            ys, ws = [ya, yb], [w_o[:A_WIDTH], w_o[A_WIDTH:]]
        else:
            q, k, v, gt = _inproj_c_call(geo, tm, x, mod_l, g_mix, w_in_c[j].astype(MXU_DT), cos, sin)
            y = _ret_call(geo, tscan, q, k, v, ret_decay_logit[j], gt, ret_norm_g[j].reshape(1, C_V_DIM))
            ys, ws = [y], [w_out_c[j].astype(MXU_DT)]
        x, hx, grp = _outproj_call(geo, tm, ys, ws, x, mod_l, g_ffn, rw_t, rb)
        pos, tile_group, n_tiles = _plan_call(tmoe, grp)
        xs = _dispatch_call(tm, pos, hx, xs)
        ysort = _moe_call(tmoe, l, tile_group, n_tiles, xs, w1, w3, w2)
        x = _collect_call(geo, tm, pos, ysort, x, mod_l)

    y_prompt = x[:geo.tp].reshape(x_prompt.shape)
    y_sample = x[geo.tp:].reshape(x_sample.shape)
    return (y_prompt, y_sample)


def kernel(x_prompt, x_sample, c_prompt, c_sample, rel_bias, router_w, router_b, ada_w, ada_b, norm_mix_g, norm_ffn_g, w_in_ab, w_out_ab, q_norm_g, k_norm_g, diff_lambda, diff_norm_g, mlstm_conv_w, mlstm_conv_b, mlstm_gate_b, mlstm_norm_g, w_in_c, w_out_c, ret_decay_logit, ret_norm_g, moe_w1, moe_w3, moe_w2):
    geo = Geo(x_prompt.shape[0], x_prompt.shape[1], x_sample.shape[0], x_sample.shape[1])
    return _forward(geo, _tiles(geo), x_prompt, x_sample, c_prompt, c_sample, rel_bias, router_w, router_b,
                    ada_w, ada_b, norm_mix_g, norm_ffn_g, w_in_ab, w_out_ab, q_norm_g, k_norm_g, diff_lambda,
                    diff_norm_g, mlstm_conv_w, mlstm_conv_b, mlstm_gate_b, mlstm_norm_g, w_in_c, w_out_c,
                    ret_decay_logit, ret_norm_g, moe_w1, moe_w3, moe_w2)
```

```python
import functools
import math

import jax
import jax.numpy as jnp
import numpy as np
from jax import lax
from jax.experimental import pallas as pl
from jax.experimental.pallas import tpu as pltpu

F32 = jnp.float32
MXU_DT = jnp.bfloat16
ACT_DT = jnp.bfloat16
HI = lax.Precision.HIGHEST

D_MODEL = 1024
DEPTH = 4
A_HEADS = 4
A_QK_DIM = 64
A_V_DIM = 128
A_QK_WIDTH = 512
A_WIDTH = 512
B_HEADS = 4
B_DIM = 128
B_WIDTH = 512
N_GATES = 4
AB_IN = 3600
AB_IN_PAD = 3712
C_HEADS = 4
C_QK_DIM = 256
C_V_DIM = 512
C_QK_WIDTH = 1024
C_V_WIDTH = 2048
C_IN = 6144
CHUNK = 128
REL_BUCKETS = 32
REL_MAX_DIST = 128
N_EXPERTS = 16
N_GROUPS = 4
EPG = 4
D_FF = 512
ROPE_BASE = 10000.0
EPS = 1e-6
LANES = 128
NEG_BIG = -1e30
LOG2E = math.log2(math.e)
HX_W = D_MODEL + LANES
VMEM_LIMIT = 56 * 1024 * 1024


def _cparams(sem):
    return pltpu.CompilerParams(dimension_semantics=sem, vmem_limit_bytes=VMEM_LIMIT)


def _mm(a, b):
    return jnp.dot(a.astype(MXU_DT), b.astype(MXU_DT), preferred_element_type=F32)


def _mm_nt(a, b):
    return lax.dot_general(a.astype(MXU_DT), b.astype(MXU_DT), (((1,), (1,)), ((), ())),
                           preferred_element_type=F32)


def _mm_tn(a, b):
    return lax.dot_general(a.astype(MXU_DT), b.astype(MXU_DT), (((0,), (0,)), ((), ())),
                           preferred_element_type=F32)


def _mm_hi(a, b):
    return jnp.dot(a, b, precision=HI, preferred_element_type=F32)


def _silu(x):
    return x * (1.0 / (1.0 + jnp.exp(-x)))


def _sigmoid(x):
    return 1.0 / (1.0 + jnp.exp(-x))


def _log_sigmoid(x):
    return jnp.minimum(x, 0.0) - jnp.log1p(jnp.exp(-jnp.abs(x)))


def _rms(x, g):
    return x * lax.rsqrt(jnp.mean(x * x, axis=-1, keepdims=True) + EPS) * g


class Geo:
    def __init__(self, bp, sp, bs, ss):
        self.bp, self.sp, self.bs, self.ss = bp, sp, bs, ss
        self.tp = bp * sp
        self.t = bp * sp + bs * ss
        self.seg = math.gcd(sp, ss)

    def seq_start(self, row0):
        return jnp.where(row0 < self.tp, row0 % self.sp == 0, (row0 - self.tp) % self.ss == 0)

    def pos_block(self, blk, rows):
        nbp = self.tp // rows
        return jnp.where(blk < nbp, blk % (self.sp // rows), (blk - nbp) % (self.ss // rows))


def _ada_kernel(c_ref, w_ref, b_ref, o_ref):
    o_ref[...] = _mm_hi(_silu(c_ref[...]), w_ref[...]) + b_ref[...]


def _ada_call(c_rows, ada_w, ada_b):
    r = c_rows.shape[0]
    nb = 1536
    out = pl.pallas_call(
        _ada_kernel,
        grid=(DEPTH, 6 * D_MODEL // nb),
        in_specs=[pl.BlockSpec((r, D_MODEL), lambda l, n: (0, 0)),
                  pl.BlockSpec((None, D_MODEL, nb), lambda l, n: (l, 0, n)),
                  pl.BlockSpec((None, 1, nb), lambda l, n: (l, 0, n))],
        out_specs=pl.BlockSpec((None, r, nb), lambda l, n: (l, 0, n)),
        out_shape=jax.ShapeDtypeStruct((DEPTH, r, 6 * D_MODEL), F32),
        compiler_params=_cparams(("arbitrary", "arbitrary")),
        name="ada_mod",
    )(c_rows, ada_w, ada_b.reshape(DEPTH, 1, 6 * D_MODEL))
    return out.reshape(DEPTH, r, 6, D_MODEL)


def _t5_bucket(rel):
    nb = REL_BUCKETS // 2
    max_exact = nb // 2
    ret = jnp.where(rel > 0, nb, 0)
    n = jnp.abs(rel)
    nf = jnp.maximum(n, 1).astype(jnp.float32)
    large = max_exact + (jnp.log(nf / max_exact) / math.log(REL_MAX_DIST / max_exact) * (nb - max_exact)).astype(jnp.int32)
    large = jnp.minimum(large, nb - 1)
    return ret + jnp.where(n < max_exact, n, large)


def _bias_kernel(rb_ref, bk_ref, o_ref):
    h = pl.program_id(0)
    bk = bk_ref[...]
    acc = jnp.zeros(bk.shape, F32)
    bmax = rb_ref[0, h]
    for b in range(REL_BUCKETS):
        acc = acc + jnp.where(bk == b, rb_ref[b, h], 0.0)
        bmax = jnp.maximum(bmax, rb_ref[b, h])
    o_ref[...] = (acc - bmax) * LOG2E


def _bias_call(rel_bias, tq):
    r = jnp.arange(tq, dtype=jnp.int32)
    rel = (jnp.arange(-1, 2, dtype=jnp.int32) * tq)[:, None, None] + r[None, None, :] - r[None, :, None]
    buckets = _t5_bucket(rel)
    return pl.pallas_call(
        _bias_kernel,
        grid=(A_HEADS, 3),
        in_specs=[pl.BlockSpec(memory_space=pltpu.SMEM),
                  pl.BlockSpec((None, tq, tq), lambda h, o: (o, 0, 0))],
        out_specs=pl.BlockSpec((None, None, tq, tq), lambda h, o: (h, o, 0, 0)),
        out_shape=jax.ShapeDtypeStruct((A_HEADS, 3, tq, tq), F32),
        compiler_params=_cparams(("arbitrary", "arbitrary")),
        name="rel_bias_tiles",
    )(rel_bias, buckets)


def _rope_tables(s):
    d = C_QK_DIM
    inv = ROPE_BASE ** (-jnp.arange(0, d, 2, dtype=jnp.float32) / d)
    ang = jnp.arange(s, dtype=jnp.float32)[:, None] * inv[None, :]
    return jnp.cos(ang), jnp.sin(ang)


def _half_rms(z, g):
    lo_lane = lax.broadcasted_iota(jnp.int32, (1, LANES), 1) < A_QK_DIM
    z2 = z * z
    tot = jnp.sum(z2, axis=-1, keepdims=True)
    lo = jnp.sum(jnp.where(lo_lane, z2, 0.0), axis=-1, keepdims=True)
    ms = jnp.where(lo_lane, lo, tot - lo) * (1.0 / A_QK_DIM)
    return z * lax.rsqrt(ms + EPS) * g


def _inproj_ab_kernel(x_ref, mod_ref, g_ref, w_ref, qg_ref, kg_ref, gb_ref,
                      qa_ref, ka_ref, va_ref, qk_ref, vb_ref, ob_ref, gt_ref):
    x = x_ref[...]
    h = _rms(x, g_ref[...]) * (1.0 + mod_ref[1:2, :]) + mod_ref[0:1, :]
    hb = h.astype(MXU_DT)
    qscale = (A_QK_DIM ** -0.5) * LOG2E
    for hd in range(A_HEADS):
        c0 = hd * LANES
        q = jnp.dot(hb, w_ref[:, c0:c0 + LANES], preferred_element_type=F32)
        qa_ref[:, c0:c0 + LANES] = (_half_rms(q, qg_ref[...]) * qscale).astype(qa_ref.dtype)
        k = jnp.dot(hb, w_ref[:, A_QK_WIDTH + c0:A_QK_WIDTH + c0 + LANES], preferred_element_type=F32)
        ka_ref[:, c0:c0 + LANES] = _half_rms(k, kg_ref[...]).astype(ka_ref.dtype)
    o = 2 * A_QK_WIDTH
    va_ref[...] = jnp.dot(hb, w_ref[:, o:o + A_WIDTH], preferred_element_type=F32).astype(va_ref.dtype)
    o += A_WIDTH
    qk_ref[...] = jnp.dot(hb, w_ref[:, o:o + 2 * B_WIDTH], preferred_element_type=F32)
    o += 2 * B_WIDTH
    vb_ref[...] = jnp.dot(hb, w_ref[:, o:o + B_WIDTH], preferred_element_type=F32).astype(vb_ref.dtype)
    o += B_WIDTH
    ob_ref[...] = jnp.dot(hb, w_ref[:, o:o + B_WIDTH], preferred_element_type=F32)
    o += B_WIDTH
    gt_ref[...] = jnp.dot(hb, w_ref[:, o:o + LANES], preferred_element_type=F32) + gb_ref[...]


def _inproj_ab_call(geo, tm, x, mod_l, g, w_pad, qg, kg, gate_b):
    t = geo.t
    per_seg = geo.seg // tm
    row = lambda i: (i, 0)
    const = lambda i: (0, 0)
    widths = (A_QK_WIDTH, A_QK_WIDTH, A_WIDTH, 2 * B_WIDTH, B_WIDTH, B_WIDTH, LANES)
    dtypes = (ACT_DT, ACT_DT, ACT_DT, F32, ACT_DT, F32, F32)
    return pl.pallas_call(
        _inproj_ab_kernel,
        grid=(t // tm,),
        in_specs=[pl.BlockSpec((tm, D_MODEL), row),
                  pl.BlockSpec((None, 6, D_MODEL), lambda i: (i // per_seg, 0, 0)),
                  pl.BlockSpec((1, D_MODEL), const),
                  pl.BlockSpec((D_MODEL, AB_IN_PAD), const),
                  pl.BlockSpec((1, LANES), const),
                  pl.BlockSpec((1, LANES), const),
                  pl.BlockSpec((1, LANES), const)],
        out_specs=[pl.BlockSpec((tm, w), row) for w in widths],
        out_shape=[jax.ShapeDtypeStruct((t, w), d) for w, d in zip(widths, dtypes)],
        compiler_params=_cparams(("parallel",)),
        name="inproj_ab",
    )(x, mod_l, g, w_pad, qg, kg, gate_b)


def _conv_kernel(geo, tc, x_ref, prev_ref, next_ref, w_ref, b_ref, q_ref, k_ref):
    i = pl.program_id(0)
    row0 = i * tc
    x = x_ref[...]
    first = geo.seq_start(row0)
    last = geo.seq_start(row0 + tc) | (row0 + tc == geo.t)
    prev_row = jnp.where(first, 0.0, prev_ref[7:8, :])
    next_row = jnp.where(last, 0.0, next_ref[0:1, :])
    ridx = lax.broadcasted_iota(jnp.int32, (tc, 1), 0)
    x_prev = jnp.where(ridx == 0, prev_row, pltpu.roll(x, 1, axis=0))
    x_next = jnp.where(ridx == tc - 1, next_row, pltpu.roll(x, tc - 1, axis=0))
    y = x_prev * w_ref[0:1, :] + x * w_ref[1:2, :] + x_next * w_ref[2:3, :] + b_ref[...]
    y = _silu(y)
    q_ref[...] = y[:, :B_WIDTH].astype(q_ref.dtype)
    k_ref[...] = (y[:, B_WIDTH:] * (B_DIM ** -0.5)).astype(k_ref.dtype)


def _conv_call(geo, tc, qk, w, b):
    t = geo.t
    r8 = tc // 8
    nb8 = t // 8
    return pl.pallas_call(
        functools.partial(_conv_kernel, geo, tc),
        grid=(t // tc,),
        in_specs=[pl.BlockSpec((tc, 2 * B_WIDTH), lambda i: (i, 0)),
                  pl.BlockSpec((8, 2 * B_WIDTH), lambda i: (jnp.maximum(i * r8 - 1, 0), 0)),
                  pl.BlockSpec((8, 2 * B_WIDTH), lambda i: (jnp.minimum((i + 1) * r8, nb8 - 1), 0)),
                  pl.BlockSpec((3, 2 * B_WIDTH), lambda i: (0, 0)),
                  pl.BlockSpec((1, 2 * B_WIDTH), lambda i: (0, 0))],
        out_specs=[pl.BlockSpec((tc, B_WIDTH), lambda i: (i, 0))] * 2,
        out_shape=[jax.ShapeDtypeStruct((t, B_WIDTH), ACT_DT)] * 2,
        compiler_params=_cparams(("parallel",)),
        name="mlstm_conv",
    )(qk, qk, qk, w, b)


def _attn_finish(lam_init, acc0, l0, acc1, l1, dl_ref, ng_ref, o_ref):
    dl = dl_ref[...]
    lam = (jnp.exp(jnp.sum(dl[0:1] * dl[1:2], axis=-1, keepdims=True))
           - jnp.exp(jnp.sum(dl[2:3] * dl[3:4], axis=-1, keepdims=True)) + lam_init)
    out = acc0 / l0 - lam * (acc1 / l1)
    o_ref[...] = (_rms(out, ng_ref[...]) * (1.0 - lam_init)).astype(o_ref.dtype)


def _attn_kernel(lam_init, nk, q_ref, k_ref, v_ref, bias_ref, sc_ref, dl_ref, ng_ref, o_ref,
                 m_sc, l_sc, acc_sc):
    h = pl.program_id(1)
    i = pl.program_id(2)
    j = pl.program_id(3)

    @pl.when(j == 0)
    def _():
        m_sc[...] = jnp.full(m_sc.shape, NEG_BIG, F32)
        l_sc[...] = jnp.zeros(l_sc.shape, F32)
        acc_sc[...] = jnp.zeros(acc_sc.shape, F32)

    q = q_ref[...]
    lo_lane = lax.broadcasted_iota(jnp.int32, (1, LANES), 1) < A_QK_DIM
    qsub = (jnp.where(lo_lane, q, jnp.zeros_like(q)), jnp.where(lo_lane, jnp.zeros_like(q), q))
    k = k_ref[...]
    v = v_ref[...]

    def step(near):
        if near:
            shift = 0.0
        else:
            shift = jnp.where(j < i, sc_ref[h, 0], sc_ref[h, 1])
        for sub in range(2):
            s = _mm_nt(qsub[sub], k)
            if near:
                s = s + bias_ref[...]
            m_old = m_sc[sub]
            m_new = jnp.maximum(m_old, jnp.max(s, axis=-1, keepdims=True) + shift)
            p = jnp.exp2(s - (m_new - shift))
            alpha = jnp.exp2(m_old - m_new)
            l_sc[sub] = alpha * l_sc[sub] + jnp.sum(p, axis=-1, keepdims=True)
            acc_sc[sub] = alpha * acc_sc[sub] + _mm(p, v)
            m_sc[sub] = m_new

    near = jnp.abs(j - i) <= 1
    pl.when(near)(lambda: step(True))
    pl.when(jnp.logical_not(near))(lambda: step(False))

    @pl.when(j == nk - 1)
    def _():
        _attn_finish(lam_init, acc_sc[0], l_sc[0], acc_sc[1], l_sc[1], dl_ref, ng_ref, o_ref)


def _attn_bounded_kernel(lam_init, nk, q_ref, k_ref, v_ref, bias_ref, sc_ref, dl_ref, ng_ref, o_ref, acc_sc):
    h = pl.program_id(1)
    i = pl.program_id(2)
    j = pl.program_id(3)

    @pl.when(j == 0)
    def _():
        acc_sc[...] = jnp.zeros(acc_sc.shape, F32)

    @pl.when(j == jnp.maximum(i - 1, 0))
    def _():
        acc_sc[...] = acc_sc[...] * sc_ref[h, 2]

    @pl.when(j == i + 2)
    def _():
        acc_sc[...] = acc_sc[...] * sc_ref[h, 3]

    q = q_ref[...]
    k = k_ref[...]
    v = v_ref[...]
    lo = (lax.broadcasted_iota(jnp.int32, q.shape, 1) < A_QK_DIM).astype(F32).astype(q.dtype)
    qsub = (q * lo, q * (1 - lo))
    ones_col = (lax.broadcasted_iota(jnp.int32, v.shape, 1) == 0).astype(v.dtype)
    v_aug = jnp.concatenate([v, ones_col], axis=1)

    def step(near):
        for sub in range(2):
            s = _mm_nt(qsub[sub], k)
            if near:
                s = s + bias_ref[...]
            acc_sc[sub] += _mm(jnp.exp2(s), v_aug)

    near = jnp.abs(j - i) <= 1
    pl.when(near)(lambda: step(True))
    pl.when(jnp.logical_not(near))(lambda: step(False))

    @pl.when(j == nk - 1)
    def _():
        a0 = acc_sc[0]
        a1 = acc_sc[1]
        _attn_finish(lam_init, a0[:, :A_V_DIM], a0[:, A_V_DIM:A_V_DIM + 1], a1[:, :A_V_DIM],
                     a1[:, A_V_DIM:A_V_DIM + 1], dl_ref, ng_ref, o_ref)


def _attn_scalars(bias, tq, q_gain, k_gain):
    far_l = bias[:, 0, tq - 1, 0]
    far_r = bias[:, 2, 0, tq - 1]
    sc = jnp.stack([far_l, far_r, jnp.exp2(far_l), jnp.exp2(-far_r)], axis=-1)
    bound = A_QK_DIM * jnp.max(jnp.abs(q_gain)) * jnp.max(jnp.abs(k_gain)) * (A_QK_DIM ** -0.5) * LOG2E * 1.02
    spread = -jnp.min(bias)
    ok = bound + 2.0 * spread <= 80.0
    return sc, ok


def _attn_call(geo, tq, lam_init, bounded, qa, ka, va, bias, sc, dl, ng):
    outs = []
    for (nb, s, row_off) in ((geo.bp, geo.sp, 0), (geo.bs, geo.ss, geo.tp)):
        nq = s // tq
        off = row_off // tq
        qmap = lambda b, h, i, j, off=off, nq=nq: (off + b * nq + i, h)
        kmap = lambda b, h, i, j, off=off, nq=nq: (off + b * nq + j, h)
        omap = lambda b, h, i, j, nq=nq: (b * nq + i, h)
        if bounded:
            body = functools.partial(_attn_bounded_kernel, lam_init, nq)
            scratch = [pltpu.VMEM((2, tq, 2 * LANES), F32)]
        else:
            body = functools.partial(_attn_kernel, lam_init, nq)
            scratch = [pltpu.VMEM((2, tq, 1), F32), pltpu.VMEM((2, tq, 1), F32), pltpu.VMEM((2, tq, LANES), F32)]
        outs.append(pl.pallas_call(
            body,
            grid=(nb, A_HEADS, nq, nq),
            in_specs=[pl.BlockSpec((tq, LANES), qmap),
                      pl.BlockSpec((tq, LANES), kmap),
                      pl.BlockSpec((tq, LANES), kmap),
                      pl.BlockSpec((None, None, tq, tq), lambda b, h, i, j: (h, jnp.clip(j - i + 1, 0, 2), 0, 0)),
                      pl.BlockSpec(memory_space=pltpu.SMEM),
                      pl.BlockSpec((4, A_QK_DIM), lambda b, h, i, j: (0, 0)),
                      pl.BlockSpec((1, LANES), lambda b, h, i, j: (0, 0))],
            out_specs=pl.BlockSpec((tq, LANES), omap),
            out_shape=jax.ShapeDtypeStruct((nb * s, A_WIDTH), ACT_DT),
            scratch_shapes=scratch,
            compiler_params=_cparams(("parallel", "parallel", "parallel", "arbitrary")),
            name="diff_attn_bounded" if bounded else "diff_attn",
        )(qa, ka, va, bias, sc, dl, ng))
    return jnp.concatenate(outs, axis=0)


def _tri(lower):
    r = lax.broadcasted_iota(jnp.int32, (CHUNK, CHUNK), 0)
    c = lax.broadcasted_iota(jnp.int32, (CHUNK, CHUNK), 1)
    return (c <= r) if lower else (c >= r)


def _mlstm_kernel(geo, tb, reverse, *refs):
    if reverse:
        q_ref, k_ref, v_ref, g_ref, hf_ref, ob_ref, ng_ref, o_ref, c_sc, m_sc = refs
    else:
        q_ref, k_ref, v_ref, g_ref, o_ref, c_sc, m_sc = refs
    step = pl.program_id(0)
    nblk = geo.t // tb
    blk = (nblk - 1 - step) if reverse else step
    row0 = blk * tb
    if reverse:
        fresh = geo.seq_start(row0 + tb) | (row0 + tb == geo.t)
    else:
        fresh = geo.seq_start(row0)

    @pl.when(fresh)
    def _():
        c_sc[...] = jnp.zeros(c_sc.shape, F32)
        m_sc[...] = jnp.zeros(m_sc.shape, F32)

    mask = _tri(not reverse)
    cum_l = mask.astype(F32)
    cum_r = _tri(reverse).astype(F32)
    ones_col = (lax.broadcasted_iota(jnp.int32, (CHUNK, LANES), 1) == 0).astype(MXU_DT)
    nch = tb // CHUNK

    def chunk(ci, carry):
        c_idx = (nch - 1 - ci) if reverse else ci
        r0 = pl.multiple_of(c_idx * CHUNK, CHUNK)
        g = g_ref[pl.ds(r0, CHUNK), :]
        g_t = g.T
        b_col = _mm_hi(cum_l, _log_sigmoid(g))
        b_row = _mm_hi(_log_sigmoid(g_t), cum_r)
        for hd in range(B_HEADS):
            ci_col = hd * N_GATES + (2 if reverse else 0)
            cf_col = ci_col + 1
            lanes = slice(hd * B_DIM, (hd + 1) * B_DIM)
            q = q_ref[pl.ds(r0, CHUNK), lanes]
            k = k_ref[pl.ds(r0, CHUNK), lanes]
            v = v_ref[pl.ds(r0, CHUNK), lanes]
            bc = b_col[:, cf_col:cf_col + 1]
            br = b_row[cf_col:cf_col + 1, :]
            ic = g[:, ci_col:ci_col + 1]
            ir = g_t[ci_col:ci_col + 1, :]
            m_prev = m_sc[hd]
            log_d = jnp.where(mask, bc - br + ir, NEG_BIG)
            m_inter = bc + m_prev
            m_t = jnp.maximum(jnp.max(log_d, axis=-1, keepdims=True), m_inter)
            s = _mm_nt(q, k) * jnp.exp(log_d - m_t)
            inter = jnp.exp(m_inter - m_t)
            c_aug = c_sc[hd]
            qc = _mm(q, c_aug)
            num = _mm(s, v) + inter * qc[:, :B_DIM]
            den = jnp.sum(s, axis=-1, keepdims=True) + inter * qc[:, B_DIM:B_DIM + 1]
            hout = num / jnp.maximum(jnp.abs(den), jnp.exp(-m_t))
            b_last = bc[0:1, :] if reverse else bc[CHUNK - 1:CHUNK, :]
            log_w = b_last - bc + ic
            m_new = jnp.maximum(b_last + m_prev, jnp.max(log_w, axis=0, keepdims=True))
            w = jnp.exp(log_w - m_new)
            decay = jnp.exp(b_last + m_prev - m_new)
            v_aug = jnp.concatenate([v, ones_col], axis=1)
            c_sc[hd] = decay * c_aug + _mm_tn(k.astype(F32) * w, v_aug)
            m_sc[hd] = m_new
            if reverse:
                hsum = hf_ref[pl.ds(r0, CHUNK), lanes] + hout
                y = _rms(hsum, ng_ref[...]) * _sigmoid(ob_ref[pl.ds(r0, CHUNK), lanes])
                o_ref[pl.ds(r0, CHUNK), lanes] = y.astype(o_ref.dtype)
            else:
                o_ref[pl.ds(r0, CHUNK), lanes] = hout
        return carry

    lax.fori_loop(0, nch, chunk, 0)


def _mlstm_call(geo, tb, qb, kb, vb, gates, ob, ng):
    t = geo.t
    nblk = t // tb
    fmap = lambda s: (s, 0)
    rmap = lambda s: (nblk - 1 - s, 0)
    scratch = [pltpu.VMEM((B_HEADS, B_DIM, 2 * B_DIM), F32), pltpu.VMEM((B_HEADS, 1, 1), F32)]
    wide = lambda m: pl.BlockSpec((tb, B_WIDTH), m)
    hf = pl.pallas_call(
        functools.partial(_mlstm_kernel, geo, tb, False),
        grid=(nblk,),
        in_specs=[wide(fmap), wide(fmap), wide(fmap), pl.BlockSpec((tb, LANES), fmap)],
        out_specs=wide(fmap),
        out_shape=jax.ShapeDtypeStruct((t, B_WIDTH), F32),
        scratch_shapes=scratch,
        compiler_params=_cparams(("arbitrary",)),
        name="mlstm_fwd",
    )(qb, kb, vb, gates)
    return pl.pallas_call(
        functools.partial(_mlstm_kernel, geo, tb, True),
        grid=(nblk,),
        in_specs=[wide(rmap), wide(rmap), wide(rmap), pl.BlockSpec((tb, LANES), rmap), wide(rmap), wide(rmap),
                  pl.BlockSpec((1, B_DIM), lambda s: (0, 0))],
        out_specs=wide(rmap),
        out_shape=jax.ShapeDtypeStruct((t, B_WIDTH), ACT_DT),
        scratch_shapes=scratch,
        compiler_params=_cparams(("arbitrary",)),
        name="mlstm_bwd",
    )(qb, kb, vb, gates, hf, ob, ng)


def _inproj_c_kernel(x_ref, mod_ref, g_ref, w_ref, cos_ref, sin_ref, q_ref, k_ref, v_ref, gt_ref, h_sc):
    j = pl.program_id(1)

    @pl.when(j == 0)
    def _():
        x = x_ref[...]
        h = _rms(x, g_ref[...]) * (1.0 + mod_ref[1:2, :]) + mod_ref[0:1, :]
        h_sc[...] = h.astype(h_sc.dtype)

    y = jnp.dot(h_sc[...], w_ref[...], preferred_element_type=F32)

    def rope(scale):
        cos = cos_ref[...]
        sin = sin_ref[...]
        half = C_QK_DIM // 2
        parts = []
        for hd in range(C_HEADS):
            x1 = y[:, hd * C_QK_DIM:hd * C_QK_DIM + half]
            x2 = y[:, hd * C_QK_DIM + half:(hd + 1) * C_QK_DIM]
            parts.append((x1 * cos - x2 * sin) * scale)
            parts.append((x1 * sin + x2 * cos) * scale)
        return jnp.concatenate(parts, axis=1)

    @pl.when(j == 0)
    def _():
        q_ref[...] = rope(1.0).astype(q_ref.dtype)

    @pl.when(j == 1)
    def _():
        k_ref[...] = rope(C_QK_DIM ** -0.5).astype(k_ref.dtype)

    @pl.when((j == 2) | (j == 3))
    def _():
        v_ref[...] = y.astype(v_ref.dtype)

    @pl.when(j >= 4)
    def _():
        gt_ref[...] = _silu(y).astype(gt_ref.dtype)


def _inproj_c_call(geo, tm, x, mod_l, g, w, cos, sin):
    t = geo.t
    per_seg = geo.seg // tm
    nw = D_MODEL
    return pl.pallas_call(
        _inproj_c_kernel,
        grid=(t // tm, C_IN // nw),
        in_specs=[pl.BlockSpec((tm, D_MODEL), lambda i, j: (i, 0)),
                  pl.BlockSpec((None, 6, D_MODEL), lambda i, j: (i // per_seg, 0, 0)),
                  pl.BlockSpec((1, D_MODEL), lambda i, j: (0, 0)),
                  pl.BlockSpec((D_MODEL, nw), lambda i, j: (0, j)),
                  pl.BlockSpec((tm, C_QK_DIM // 2), lambda i, j: (geo.pos_block(i, tm), 0)),
                  pl.BlockSpec((tm, C_QK_DIM // 2), lambda i, j: (geo.pos_block(i, tm), 0))],
        out_specs=[pl.BlockSpec((tm, nw), lambda i, j: (i, 0)),
                   pl.BlockSpec((tm, nw), lambda i, j: (i, 0)),
                   pl.BlockSpec((tm, nw), lambda i, j: (i, jnp.clip(j - 2, 0, 1))),
                   pl.BlockSpec((tm, nw), lambda i, j: (i, jnp.clip(j - 4, 0, 1)))],
        out_shape=[jax.ShapeDtypeStruct((t, C_QK_WIDTH), ACT_DT),
                   jax.ShapeDtypeStruct((t, C_QK_WIDTH), ACT_DT),
                   jax.ShapeDtypeStruct((t, C_V_WIDTH), ACT_DT),
                   jax.ShapeDtypeStruct((t, C_V_WIDTH), ACT_DT)],
        scratch_shapes=[pltpu.VMEM((tm, D_MODEL), MXU_DT)],
        compiler_params=_cparams(("parallel", "arbitrary")),
        name="inproj_c",
    )(x, mod_l, g, w, cos, sin)


def _ret_kernel(geo, tb, reverse, *refs):
    if reverse:
        q_ref, k_ref, v_ref, dlg_ref, yf_ref, gt_ref, ng_ref, o_ref, r_sc = refs
    else:
        q_ref, k_ref, v_ref, dlg_ref, o_ref, r_sc = refs
    step = pl.program_id(0)
    nblk = geo.t // tb
    blk = (nblk - 1 - step) if reverse else step
    row0 = blk * tb
    if reverse:
        fresh = geo.seq_start(row0 + tb) | (row0 + tb == geo.t)
    else:
        fresh = geo.seq_start(row0)

    @pl.when(fresh)
    def _():
        r_sc[...] = jnp.zeros(r_sc.shape, F32)

    d = 1 if reverse else 0
    lg_all = _log_sigmoid(dlg_ref[...])
    ti = lax.broadcasted_iota(jnp.int32, (CHUNK, CHUNK), 0)
    si = lax.broadcasted_iota(jnp.int32, (CHUNK, CHUNK), 1)
    dist = ((si - ti) if reverse else (ti - si)).astype(F32)
    pos = lax.broadcasted_iota(jnp.int32, (CHUNK, 1), 0).astype(F32)
    upos = (CHUNK - 1.0 - pos) if reverse else pos
    nch = tb // CHUNK
    decays = []
    for hd in range(C_HEADS):
        lg = lg_all[d:d + 1, hd:hd + 1]
        intra = jnp.where(dist >= 0, jnp.exp(jnp.maximum(dist, 0.0) * lg), 0.0)
        decays.append((intra, jnp.exp((upos + 1.0) * lg), jnp.exp((CHUNK - 1.0 - upos) * lg),
                       jnp.exp(CHUNK * lg)))

    def chunk(ci, carry):
        c_idx = (nch - 1 - ci) if reverse else ci
        r0 = pl.multiple_of(c_idx * CHUNK, CHUNK)
        for hd in range(C_HEADS):
            intra, q_scale, k_scale, c_decay = decays[hd]
            ql = slice(hd * C_QK_DIM, (hd + 1) * C_QK_DIM)
            vl = slice(hd * C_V_DIM, (hd + 1) * C_V_DIM)
            q = q_ref[pl.ds(r0, CHUNK), ql]
            k = k_ref[pl.ds(r0, CHUNK), ql]
            v = v_ref[pl.ds(r0, CHUNK), vl]
            r_old = r_sc[hd]
            s = _mm_nt(q, k) * intra
            y = _mm(s, v) + q_scale * _mm(q, r_old)
            r_sc[hd] = c_decay * r_old + _mm_tn(k.astype(F32) * k_scale, v)
            if reverse:
                ysum = yf_ref[pl.ds(r0, CHUNK), vl] + y
                out = _rms(ysum, ng_ref[...]) * gt_ref[pl.ds(r0, CHUNK), vl].astype(F32)
                o_ref[pl.ds(r0, CHUNK), vl] = out.astype(o_ref.dtype)
            else:
                o_ref[pl.ds(r0, CHUNK), vl] = y
        return carry

    lax.fori_loop(0, nch, chunk, 0)


def _ret_call(geo, tb, q, k, v, decay_logit, gt, ng):
    t = geo.t
    nblk = t // tb
    fmap = lambda s: (s, 0)
    rmap = lambda s: (nblk - 1 - s, 0)
    scratch = [pltpu.VMEM((C_HEADS, C_QK_DIM, C_V_DIM), F32)]
    qk = lambda m: pl.BlockSpec((tb, C_QK_WIDTH), m)
    vv = lambda m: pl.BlockSpec((tb, C_V_WIDTH), m)
    dspec = pl.BlockSpec((2, C_HEADS), lambda s: (0, 0))
    yf = pl.pallas_call(
        functools.partial(_ret_kernel, geo, tb, False),
        grid=(nblk,),
        in_specs=[qk(fmap), qk(fmap), vv(fmap), dspec],
        out_specs=vv(fmap),
        out_shape=jax.ShapeDtypeStruct((t, C_V_WIDTH), F32),
        scratch_shapes=scratch,
        compiler_params=_cparams(("arbitrary",)),
        name="ret_fwd",
    )(q, k, v, decay_logit)
    return pl.pallas_call(
        functools.partial(_ret_kernel, geo, tb, True),
        grid=(nblk,),
        in_specs=[qk(rmap), qk(rmap), vv(rmap), dspec, vv(rmap), vv(rmap),
                  pl.BlockSpec((1, C_V_DIM), lambda s: (0, 0))],
        out_specs=vv(rmap),
        out_shape=jax.ShapeDtypeStruct((t, C_V_WIDTH), ACT_DT),
        scratch_shapes=scratch,
        compiler_params=_cparams(("arbitrary",)),
        name="ret_bwd",
    )(q, k, v, decay_logit, yf, gt, ng)


def _route(probs):
    p = [probs[e:e + 1, :] for e in range(N_EXPERTS)]
    scores = []
    for g in range(N_GROUPS):
        a, b, c, d = p[EPG * g:EPG * g + EPG]
        hi1, lo1 = jnp.maximum(a, b), jnp.minimum(a, b)
        hi2, lo2 = jnp.maximum(c, d), jnp.minimum(c, d)
        scores.append(jnp.maximum(hi1, hi2) + jnp.maximum(jnp.minimum(hi1, hi2), jnp.maximum(lo1, lo2)))
    g_sel = jnp.zeros(scores[0].shape, jnp.int32)
    best = scores[0]
    for g in range(1, N_GROUPS):
        better = scores[g] > best
        g_sel = jnp.where(better, g, g_sel)
        best = jnp.where(better, scores[g], best)
    vals = []
    for kk in range(EPG):
        v = p[kk]
        for g in range(1, N_GROUPS):
            v = jnp.where(g_sel == g, p[EPG * g + kk], v)
        vals.append(v)

    def argmax4(xs):
        idx = jnp.zeros(xs[0].shape, jnp.int32)
        top = xs[0]
        for kk in range(1, EPG):
            better = xs[kk] > top
            idx = jnp.where(better, kk, idx)
            top = jnp.where(better, xs[kk], top)
        return idx, top

    i1, v1 = argmax4(vals)
    i2, v2 = argmax4([jnp.where(i1 == kk, -1.0, vals[kk]) for kk in range(EPG)])
    tot = v1 + v2
    w1, w2 = v1 / tot, v2 / tot
    e1 = g_sel * EPG + i1
    e2 = g_sel * EPG + i2
    eidx = lax.broadcasted_iota(jnp.int32, probs.shape, 0)
    return jnp.where(eidx == e1, w1, 0.0) + jnp.where(eidx == e2, w2, 0.0), g_sel


def _outproj_kernel(nparts, *refs):
    y_refs = refs[:nparts]
    w_refs = refs[nparts:2 * nparts]
    x_ref, mod_ref, g_ref, rw_ref, rb_ref, xo_ref, hx_ref, grp_ref = refs[2 * nparts:]
    m = jnp.dot(y_refs[0][...], w_refs[0][...], preferred_element_type=F32)
    for p in range(1, nparts):
        m = m + jnp.dot(y_refs[p][...], w_refs[p][...], preferred_element_type=F32)
    x = x_ref[...] + mod_ref[2:3, :] * m
    xo_ref[...] = x
    h = _rms(x, g_ref[...]) * (1.0 + mod_ref[4:5, :]) + mod_ref[3:4, :]
    hx_ref[:, :D_MODEL] = h
    logits = lax.dot_general(rw_ref[...], h, (((1,), (1,)), ((), ())), precision=HI,
                             preferred_element_type=F32) + rb_ref[...]
    z = jnp.exp(logits - jnp.max(logits, axis=0, keepdims=True))
    probs = z / jnp.sum(z, axis=0, keepdims=True)
    cmb, g_sel = _route(probs)
    pad = jnp.zeros((LANES - N_EXPERTS, cmb.shape[1]), F32)
    hx_ref[:, D_MODEL:] = jnp.concatenate([cmb, pad], axis=0).T
    grp_ref[...] = g_sel


def _outproj_call(geo, tm, ys, ws, x, mod_l, g, rw_t, rb):
    t = geo.t
    per_seg = geo.seg // tm
    row = lambda i: (i, 0)
    const = lambda i: (0, 0)
    n = len(ys)
    return pl.pallas_call(
        functools.partial(_outproj_kernel, n),
        grid=(t // tm,),
        in_specs=([pl.BlockSpec((tm, y.shape[1]), row) for y in ys]
                  + [pl.BlockSpec(w.shape, const) for w in ws]
                  + [pl.BlockSpec((tm, D_MODEL), row),
                     pl.BlockSpec((None, 6, D_MODEL), lambda i: (i // per_seg, 0, 0)),
                     pl.BlockSpec((1, D_MODEL), const),
                     pl.BlockSpec((N_EXPERTS, D_MODEL), const),
                     pl.BlockSpec((N_EXPERTS, 1), const)]),
        out_specs=[pl.BlockSpec((tm, D_MODEL), row), pl.BlockSpec((tm, HX_W), row),
                   pl.BlockSpec((None, 1, tm), lambda i: (i, 0, 0))],
        out_shape=[jax.ShapeDtypeStruct((t, D_MODEL), F32), jax.ShapeDtypeStruct((t, HX_W), F32),
                   jax.ShapeDtypeStruct((t // tm, 1, tm), jnp.int32)],
        compiler_params=_cparams(("parallel",)),
        name="outproj_router",
    )(*ys, *ws, x, mod_l, g, rw_t, rb)


def _plan_kernel(tm, grp_ref, pos_ref, tg_ref, nv_ref):
    nblk, _, blk = grp_ref.shape
    gid = lax.broadcasted_iota(jnp.int32, (8, blk), 0)
    r = lax.broadcasted_iota(jnp.int32, (blk, blk), 0)
    c = lax.broadcasted_iota(jnp.int32, (blk, blk), 1)
    prefix = (r <= c).astype(MXU_DT)

    def count(b, acc):
        return acc + jnp.sum((grp_ref[b] == gid).astype(F32), axis=1, keepdims=True)

    counts = lax.fori_loop(0, nblk, count, jnp.zeros((8, 1), F32))
    padded = jnp.floor((counts + (tm - 1.0)) / tm) * tm
    row8 = lax.broadcasted_iota(jnp.int32, (8, 1), 0)
    offs = jnp.zeros((8, 1), F32)
    ends = []
    run = jnp.zeros((1, 1), F32)
    for g in range(N_GROUPS):
        offs = jnp.where(row8 == g, run, offs)
        run = run + padded[g:g + 1, :]
        ends.append(run)

    def place(b, carry):
        oh = (grp_ref[b] == gid).astype(F32)
        pre = jnp.dot(oh.astype(MXU_DT), prefix, preferred_element_type=F32)
        pos = jnp.sum(oh * (offs + carry + pre - 1.0), axis=0, keepdims=True)
        pos_ref[b] = pos.astype(jnp.int32)
        return carry + pre[:, blk - 1:blk]

    lax.fori_loop(0, nblk, place, jnp.zeros((8, 1), F32))
    start = lax.broadcasted_iota(jnp.int32, tg_ref.shape, 1).astype(F32) * tm
    tg = jnp.zeros(tg_ref.shape, F32)
    for g in range(N_GROUPS - 1):
        tg = tg + (ends[g] <= start).astype(F32)
    tg_ref[...] = tg.astype(jnp.int32)
    nv_ref[...] = jnp.broadcast_to(run / tm, nv_ref.shape).astype(jnp.int32)


def _plan_call(tm, grp):
    nblk, _, blk = grp.shape
    nt = nblk * blk // tm + N_GROUPS
    ntp = -(-nt // LANES) * LANES
    pos, tg, nv = pl.pallas_call(
        functools.partial(_plan_kernel, tm),
        out_shape=[jax.ShapeDtypeStruct(grp.shape, jnp.int32), jax.ShapeDtypeStruct((1, ntp), jnp.int32),
                   jax.ShapeDtypeStruct((1, LANES), jnp.int32)],
        compiler_params=pltpu.CompilerParams(vmem_limit_bytes=VMEM_LIMIT),
        name="moe_plan",
    )(grp)
    return pos, tg[0, :nt], nv[0, :1]


def _dispatch_kernel(tb, pos_ref, hx_ref, init_ref, xs_ref, sem):
    del init_ref

    def issue(r, carry):
        pltpu.make_async_copy(hx_ref.at[pl.ds(r, 1)], xs_ref.at[pl.ds(pos_ref[0, r], 1)], sem).start()
        return carry

    lax.fori_loop(0, tb, issue, 0, unroll=8)
    pltpu.make_async_copy(hx_ref, xs_ref.at[pl.ds(0, tb)], sem).wait()


def _dispatch_call(tb, pos, hx, xs_init):
    nsteps = hx.shape[0] // tb
    return pl.pallas_call(
        functools.partial(_dispatch_kernel, tb),
        grid=(nsteps,),
        in_specs=[pl.BlockSpec((None, 1, tb), lambda i: (i, 0, 0), memory_space=pltpu.SMEM),
                  pl.BlockSpec((tb, HX_W), lambda i: (i, 0)),
                  pl.BlockSpec(memory_space=pl.ANY)],
        out_specs=pl.BlockSpec(memory_space=pl.ANY),
        out_shape=jax.ShapeDtypeStruct(xs_init.shape, xs_init.dtype),
        scratch_shapes=[pltpu.SemaphoreType.DMA(())],
        input_output_aliases={2: 0},
        compiler_params=pltpu.CompilerParams(dimension_semantics=("arbitrary",), disable_bounds_checks=True,
                                             has_side_effects=True),
        name="moe_dispatch",
    )(pos, hx, xs_init)


def _moe_kernel(tg_ref, nv_ref, xs_ref, w1_ref, w3_ref, w2_ref, o_ref, acc_sc):
    n = pl.program_id(0)
    e = pl.program_id(1)

    @pl.when(n < nv_ref[0])
    def _():
        @pl.when(e == 0)
        def _():
            acc_sc[...] = jnp.zeros(acc_sc.shape, F32)

        h = xs_ref[:, :D_MODEL].astype(MXU_DT)
        a = jnp.dot(h, w1_ref[...], preferred_element_type=F32)
        b = jnp.dot(h, w3_ref[...], preferred_element_type=F32)
        he = _silu(a) * b
        cmb = xs_ref[:, D_MODEL:]
        sel = lax.broadcasted_iota(jnp.int32, cmb.shape, 1) == tg_ref[n] * EPG + e
        c = jnp.sum(jnp.where(sel, cmb, 0.0), axis=-1, keepdims=True)
        acc_sc[...] += c * _mm(he, w2_ref[...])

        @pl.when(e == EPG - 1)
        def _():
            o_ref[...] = acc_sc[...]

    @pl.when((n >= nv_ref[0]) & (e == EPG - 1))
    def _():
        o_ref[...] = jnp.zeros(o_ref.shape, F32)


def _moe_call(tm, layer, tg, nv, xs, w1, w3, w2):
    rows = xs.shape[0]
    nt = rows // tm
    tile = lambda n, e, tg, nv: (jnp.minimum(n, nv[0] - 1), 0)
    out_tile = lambda n, e, tg, nv: (n, 0)

    def wmap(n, e, tg, nv):
        return (layer, tg[jnp.minimum(n, nv[0] - 1)] * EPG + jnp.where(n < nv[0], e, EPG - 1), 0, 0)

    return pl.pallas_call(
        _moe_kernel,
        grid_spec=pltpu.PrefetchScalarGridSpec(
            num_scalar_prefetch=2,
            grid=(nt, EPG),
            in_specs=[pl.BlockSpec((tm, HX_W), tile),
                      pl.BlockSpec((None, None, D_MODEL, D_FF), wmap),
                      pl.BlockSpec((None, None, D_MODEL, D_FF), wmap),
                      pl.BlockSpec((None, None, D_FF, D_MODEL), wmap)],
            out_specs=pl.BlockSpec((tm, D_MODEL), out_tile),
            scratch_shapes=[pltpu.VMEM((tm, D_MODEL), F32)]),
        out_shape=jax.ShapeDtypeStruct((rows, D_MODEL), F32),
        compiler_params=_cparams(("arbitrary", "arbitrary")),
        name="moe",
    )(tg, nv, xs, w1, w3, w2)


def _collect_kernel(tb, nsteps, pos_ref, posn_ref, ys_ref, x_ref, mod_ref, o_ref, buf, sem):
    i = pl.program_id(0)
    slot = i % 2

    def issue(p_ref, s):
        def body(r, carry):
            pltpu.make_async_copy(ys_ref.at[pl.ds(p_ref[0, r], 1)], buf.at[s, pl.ds(r, 1)], sem.at[s]).start()
            return carry
        lax.fori_loop(0, tb, body, 0, unroll=8)

    pl.when(i == 0)(lambda: issue(pos_ref, 0))
    pl.when(i + 1 < nsteps)(lambda: issue(posn_ref, 1 - slot))
    pltpu.make_async_copy(ys_ref.at[pl.ds(0, tb)], buf.at[slot], sem.at[slot]).wait()
    o_ref[...] = x_ref[...] + mod_ref[5:6, :] * buf[slot]


def _collect_call(geo, tb, pos, ys, x, mod_l):
    t = geo.t
    nsteps = t // tb
    per_seg = geo.seg // tb
    return pl.pallas_call(
        functools.partial(_collect_kernel, tb, nsteps),
        grid=(nsteps,),
        in_specs=[pl.BlockSpec((None, 1, tb), lambda i: (i, 0, 0), memory_space=pltpu.SMEM),
                  pl.BlockSpec((None, 1, tb), lambda i: (jnp.minimum(i + 1, nsteps - 1), 0, 0),
                               memory_space=pltpu.SMEM),
                  pl.BlockSpec(memory_space=pl.ANY),
                  pl.BlockSpec((tb, D_MODEL), lambda i: (i, 0)),
                  pl.BlockSpec((None, 6, D_MODEL), lambda i: (i // per_seg, 0, 0))],
        out_specs=pl.BlockSpec((tb, D_MODEL), lambda i: (i, 0)),
        out_shape=jax.ShapeDtypeStruct((t, D_MODEL), F32),
        scratch_shapes=[pltpu.VMEM((2, tb, D_MODEL), F32), pltpu.SemaphoreType.DMA((2,))],
        compiler_params=pltpu.CompilerParams(dimension_semantics=("arbitrary",), vmem_limit_bytes=VMEM_LIMIT,
                                             disable_bounds_checks=True),
        name="moe_collect",
    )(pos, pos, ys, x, mod_l)


def _tiles(geo):
    seg = geo.seg
    return dict(tm=min(512, seg), tq=min(1024, seg), tscan=min(512, seg), tmoe=min(1024, seg))


def _forward(geo, tiles, x_prompt, x_sample, c_prompt, c_sample, rel_bias, router_w, router_b, ada_w, ada_b,
             norm_mix_g, norm_ffn_g, w_in_ab, w_out_ab, q_norm_g, k_norm_g, diff_lambda, diff_norm_g,
             mlstm_conv_w, mlstm_conv_b, mlstm_gate_b, mlstm_norm_g, w_in_c, w_out_c, ret_decay_logit,
             ret_norm_g, moe_w1, moe_w3, moe_w2):
    tm, tq, tscan, tmoe = tiles["tm"], tiles["tq"], tiles["tscan"], tiles["tmoe"]
    x = jnp.concatenate([x_prompt.reshape(geo.tp, D_MODEL), x_sample.reshape(geo.t - geo.tp, D_MODEL)], axis=0)
    c_rows = jnp.concatenate([jnp.repeat(c_prompt, geo.sp // geo.seg, axis=0),
                              jnp.repeat(c_sample, geo.ss // geo.seg, axis=0)], axis=0)
    mod = _ada_call(c_rows, ada_w, ada_b)

    bias = _bias_call(rel_bias, tq)
    cos, sin = _rope_tables(max(geo.sp, geo.ss))
    rw_t = router_w.T
    rb = router_b.reshape(N_EXPERTS, 1)
    w1 = moe_w1.astype(MXU_DT)
    w3 = moe_w3.astype(MXU_DT)
    w2 = moe_w2.astype(MXU_DT)
    xs = jnp.zeros((geo.t + N_GROUPS * tmoe, HX_W), F32)

    for l in range(DEPTH):
        j = l // 2
        mod_l = mod[l]
        g_mix = norm_mix_g[l].reshape(1, D_MODEL)
        g_ffn = norm_ffn_g[l].reshape(1, D_MODEL)
        if l % 2 == 0:
            lam_init = 0.8 - 0.6 * math.exp(-0.3 * l)
            w_pad = jnp.pad(w_in_ab[j], ((0, 0), (0, AB_IN_PAD - AB_IN))).astype(MXU_DT)
            qg = jnp.tile(q_norm_g[j], 2).reshape(1, LANES)
            kg = jnp.tile(k_norm_g[j], 2).reshape(1, LANES)
            gate_b = jnp.pad(mlstm_gate_b[j].reshape(1, B_HEADS * N_GATES), ((0, 0), (0, LANES - B_HEADS * N_GATES)))
            qa, ka, va, qk, vb, ob, gates = _inproj_ab_call(geo, tm, x, mod_l, g_mix, w_pad, qg, kg, gate_b)
            sc, bounded_ok = _attn_scalars(bias, tq, q_norm_g[j], k_norm_g[j])
            ya = lax.cond(bounded_ok,
                          functools.partial(_attn_call, geo, tq, lam_init, True),
                          functools.partial(_attn_call, geo, tq, lam_init, False),
                          qa, ka, va, bias, sc, diff_lambda[j], diff_norm_g[j].reshape(1, A_V_DIM))
            qb, kb = _conv_call(geo, tscan, qk, mlstm_conv_w[j], mlstm_conv_b[j].reshape(1, 2 * B_WIDTH))
            yb = _mlstm_call(geo, tscan, qb, kb, vb, gates, ob, mlstm_norm_g[j].reshape(1, B_DIM))
            w_o = w_out_ab[j].astype(MXU_DT)
            ys, ws = [ya, yb], [w_o[:A_WIDTH], w_o[A_WIDTH:]]
        else:
            q, k, v, gt = _inproj_c_call(geo, tm, x, mod_l, g_mix, w_in_c[j].astype(MXU_DT), cos, sin)
            y = _ret_call(geo, tscan, q, k, v, ret_decay_logit[j], gt, ret_norm_g[j].reshape(1, C_V_DIM))
            ys, ws = [y], [w_out_c[j].astype(MXU_DT)]
        x, hx, grp = _outproj_call(geo, tm, ys, ws, x, mod_l, g_ffn, rw_t, rb)
        pos, tile_group, n_tiles = _plan_call(tmoe, grp)
        xs = _dispatch_call(tm, pos, hx, xs)
        ysort = _moe_call(tmoe, l, tile_group, n_tiles, xs, w1, w3, w2)
        x = _collect_call(geo, tm, pos, ysort, x, mod_l)

    y_prompt = x[:geo.tp].reshape(x_prompt.shape)
    y_sample = x[geo.tp:].reshape(x_sample.shape)
    return (y_prompt, y_sample)


def kernel(x_prompt, x_sample, c_prompt, c_sample, rel_bias, router_w, router_b, ada_w, ada_b, norm_mix_g, norm_ffn_g, w_in_ab, w_out_ab, q_norm_g, k_norm_g, diff_lambda, diff_norm_g, mlstm_conv_w, mlstm_conv_b, mlstm_gate_b, mlstm_norm_g, w_in_c, w_out_c, ret_decay_logit, ret_norm_g, moe_w1, moe_w3, moe_w2):
    geo = Geo(x_prompt.shape[0], x_prompt.shape[1], x_sample.shape[0], x_sample.shape[1])
    return _forward(geo, _tiles(geo), x_prompt, x_sample, c_prompt, c_sample, rel_bias, router_w, router_b,
                    ada_w, ada_b, norm_mix_g, norm_ffn_g, w_in_ab, w_out_ab, q_norm_g, k_norm_g, diff_lambda,
                    diff_norm_g, mlstm_conv_w, mlstm_conv_b, mlstm_gate_b, mlstm_norm_g, w_in_c, w_out_c,
                    ret_decay_logit, ret_norm_g, moe_w1, moe_w3, moe_w2)
```

```python
import functools
import math

import jax
import jax.numpy as jnp
import numpy as np
from jax import lax
from jax.experimental import pallas as pl
from jax.experimental.pallas import tpu as pltpu

F32 = jnp.float32
MXU_DT = jnp.bfloat16
ACT_DT = jnp.bfloat16
HI = lax.Precision.HIGHEST

D_MODEL = 1024
DEPTH = 4
A_HEADS = 4
A_QK_DIM = 64
A_V_DIM = 128
A_QK_WIDTH = 512
A_WIDTH = 512
B_HEADS = 4
B_DIM = 128
B_WIDTH = 512
N_GATES = 4
AB_IN = 3600
AB_IN_PAD = 3712
C_HEADS = 4
C_QK_DIM = 256
C_V_DIM = 512
C_QK_WIDTH = 1024
C_V_WIDTH = 2048
C_IN = 6144
CHUNK = 128
RET_CHUNK = 256
REL_BUCKETS = 32
REL_MAX_DIST = 128
N_EXPERTS = 16
N_GROUPS = 4
EPG = 4
D_FF = 512
ROPE_BASE = 10000.0
EPS = 1e-6
LANES = 128
NEG_BIG = -1e30
LOG2E = math.log2(math.e)
HX_W = D_MODEL + LANES
VMEM_LIMIT = 56 * 1024 * 1024


def _cparams(sem):
    return pltpu.CompilerParams(dimension_semantics=sem, vmem_limit_bytes=VMEM_LIMIT)


def _mm(a, b):
    return jnp.dot(a.astype(MXU_DT), b.astype(MXU_DT), preferred_element_type=F32)


def _mm_nt(a, b):
    return lax.dot_general(a.astype(MXU_DT), b.astype(MXU_DT), (((1,), (1,)), ((), ())),
                           preferred_element_type=F32)


def _mm_tn(a, b):
    return lax.dot_general(a.astype(MXU_DT), b.astype(MXU_DT), (((0,), (0,)), ((), ())),
                           preferred_element_type=F32)


def _mm_hi(a, b):
    return jnp.dot(a, b, precision=HI, preferred_element_type=F32)


def _silu(x):
    return x * (1.0 / (1.0 + jnp.exp(-x)))


def _sigmoid(x):
    return 1.0 / (1.0 + jnp.exp(-x))


def _log_sigmoid(x):
    return jnp.minimum(x, 0.0) - jnp.log1p(jnp.exp(-jnp.abs(x)))


def _rms(x, g):
    return x * lax.rsqrt(jnp.mean(x * x, axis=-1, keepdims=True) + EPS) * g


class Geo:
    def __init__(self, bp, sp, bs, ss):
        self.bp, self.sp, self.bs, self.ss = bp, sp, bs, ss
        self.tp = bp * sp
        self.t = bp * sp + bs * ss
        self.seg = math.gcd(sp, ss)

    def seq_start(self, row0):
        return jnp.where(row0 < self.tp, row0 % self.sp == 0, (row0 - self.tp) % self.ss == 0)

    def pos_block(self, blk, rows):
        nbp = self.tp // rows
        return jnp.where(blk < nbp, blk % (self.sp // rows), (blk - nbp) % (self.ss // rows))


def _ada_kernel(c_ref, w_ref, b_ref, o_ref):
    o_ref[...] = _mm_hi(_silu(c_ref[...]), w_ref[...]) + b_ref[...]


def _ada_call(c_rows, ada_w, ada_b):
    r = c_rows.shape[0]
    nb = 1536
    out = pl.pallas_call(
        _ada_kernel,
        grid=(DEPTH, 6 * D_MODEL // nb),
        in_specs=[pl.BlockSpec((r, D_MODEL), lambda l, n: (0, 0)),
                  pl.BlockSpec((None, D_MODEL, nb), lambda l, n: (l, 0, n)),
                  pl.BlockSpec((None, 1, nb), lambda l, n: (l, 0, n))],
        out_specs=pl.BlockSpec((None, r, nb), lambda l, n: (l, 0, n)),
        out_shape=jax.ShapeDtypeStruct((DEPTH, r, 6 * D_MODEL), F32),
        compiler_params=_cparams(("arbitrary", "arbitrary")),
        name="ada_mod",
    )(c_rows, ada_w, ada_b.reshape(DEPTH, 1, 6 * D_MODEL))
    return out.reshape(DEPTH, r, 6, D_MODEL)


def _t5_bucket(rel):
    nb = REL_BUCKETS // 2
    max_exact = nb // 2
    ret = jnp.where(rel > 0, nb, 0)
    n = jnp.abs(rel)
    nf = jnp.maximum(n, 1).astype(jnp.float32)
    large = max_exact + (jnp.log(nf / max_exact) / math.log(REL_MAX_DIST / max_exact) * (nb - max_exact)).astype(jnp.int32)
    large = jnp.minimum(large, nb - 1)
    return ret + jnp.where(n < max_exact, n, large)


def _bias_kernel(rb_ref, bk_ref, o_ref):
    h = pl.program_id(0)
    bk = bk_ref[...]
    acc = jnp.zeros(bk.shape, F32)
    bmax = rb_ref[0, h]
    for b in range(REL_BUCKETS):
        acc = acc + jnp.where(bk == b, rb_ref[b, h], 0.0)
        bmax = jnp.maximum(bmax, rb_ref[b, h])
    o_ref[...] = (acc - bmax) * LOG2E


def _bias_call(rel_bias, tq):
    r = jnp.arange(tq, dtype=jnp.int32)
    rel = (jnp.arange(-1, 2, dtype=jnp.int32) * tq)[:, None, None] + r[None, None, :] - r[None, :, None]
    buckets = _t5_bucket(rel)
    return pl.pallas_call(
        _bias_kernel,
        grid=(A_HEADS, 3),
        in_specs=[pl.BlockSpec(memory_space=pltpu.SMEM),
                  pl.BlockSpec((None, tq, tq), lambda h, o: (o, 0, 0))],
        out_specs=pl.BlockSpec((None, None, tq, tq), lambda h, o: (h, o, 0, 0)),
        out_shape=jax.ShapeDtypeStruct((A_HEADS, 3, tq, tq), F32),
        compiler_params=_cparams(("arbitrary", "arbitrary")),
        name="rel_bias_tiles",
    )(rel_bias, buckets)


def _rope_tables(s):
    d = C_QK_DIM
    inv = ROPE_BASE ** (-jnp.arange(0, d, 2, dtype=jnp.float32) / d)
    ang = jnp.arange(s, dtype=jnp.float32)[:, None] * inv[None, :]
    return jnp.cos(ang), jnp.sin(ang)


def _half_rms(z, g):
    lo_lane = lax.broadcasted_iota(jnp.int32, (1, LANES), 1) < A_QK_DIM
    z2 = z * z
    tot = jnp.sum(z2, axis=-1, keepdims=True)
    lo = jnp.sum(jnp.where(lo_lane, z2, 0.0), axis=-1, keepdims=True)
    ms = jnp.where(lo_lane, lo, tot - lo) * (1.0 / A_QK_DIM)
    return z * lax.rsqrt(ms + EPS) * g


def _inproj_ab_kernel(x_ref, mod_ref, g_ref, w_ref, qg_ref, kg_ref, gb_ref,
                      qa_ref, ka_ref, va_ref, qk_ref, vb_ref, ob_ref, gt_ref):
    x = x_ref[...]
    h = _rms(x, g_ref[...]) * (1.0 + mod_ref[1:2, :]) + mod_ref[0:1, :]
    hb = h.astype(MXU_DT)
    qscale = (A_QK_DIM ** -0.5) * LOG2E
    for hd in range(A_HEADS):
        c0 = hd * LANES
        q = jnp.dot(hb, w_ref[:, c0:c0 + LANES], preferred_element_type=F32)
        qa_ref[:, c0:c0 + LANES] = (_half_rms(q, qg_ref[...]) * qscale).astype(qa_ref.dtype)
        k = jnp.dot(hb, w_ref[:, A_QK_WIDTH + c0:A_QK_WIDTH + c0 + LANES], preferred_element_type=F32)
        ka_ref[:, c0:c0 + LANES] = _half_rms(k, kg_ref[...]).astype(ka_ref.dtype)
    o = 2 * A_QK_WIDTH
    va_ref[...] = jnp.dot(hb, w_ref[:, o:o + A_WIDTH], preferred_element_type=F32).astype(va_ref.dtype)
    o += A_WIDTH
    qk_ref[...] = jnp.dot(hb, w_ref[:, o:o + 2 * B_WIDTH], preferred_element_type=F32)
    o += 2 * B_WIDTH
    vb_ref[...] = jnp.dot(hb, w_ref[:, o:o + B_WIDTH], preferred_element_type=F32).astype(vb_ref.dtype)
    o += B_WIDTH
    ob_ref[...] = jnp.dot(hb, w_ref[:, o:o + B_WIDTH], preferred_element_type=F32)
    o += B_WIDTH
    gt_ref[...] = jnp.dot(hb, w_ref[:, o:o + LANES], preferred_element_type=F32) + gb_ref[...]


def _inproj_ab_call(geo, tm, x, mod_l, g, w_pad, qg, kg, gate_b):
    t = geo.t
    per_seg = geo.seg // tm
    row = lambda i: (i, 0)
    const = lambda i: (0, 0)
    widths = (A_QK_WIDTH, A_QK_WIDTH, A_WIDTH, 2 * B_WIDTH, B_WIDTH, B_WIDTH, LANES)
    dtypes = (ACT_DT, ACT_DT, ACT_DT, F32, ACT_DT, F32, F32)
    return pl.pallas_call(
        _inproj_ab_kernel,
        grid=(t // tm,),
        in_specs=[pl.BlockSpec((tm, D_MODEL), row),
                  pl.BlockSpec((None, 6, D_MODEL), lambda i: (i // per_seg, 0, 0)),
                  pl.BlockSpec((1, D_MODEL), const),
                  pl.BlockSpec((D_MODEL, AB_IN_PAD), const),
                  pl.BlockSpec((1, LANES), const),
                  pl.BlockSpec((1, LANES), const),
                  pl.BlockSpec((1, LANES), const)],
        out_specs=[pl.BlockSpec((tm, w), row) for w in widths],
        out_shape=[jax.ShapeDtypeStruct((t, w), d) for w, d in zip(widths, dtypes)],
        compiler_params=_cparams(("parallel",)),
        name="inproj_ab",
    )(x, mod_l, g, w_pad, qg, kg, gate_b)


def _conv_kernel(geo, tc, x_ref, prev_ref, next_ref, w_ref, b_ref, q_ref, k_ref):
    i = pl.program_id(0)
    row0 = i * tc
    x = x_ref[...]
    first = geo.seq_start(row0)
    last = geo.seq_start(row0 + tc) | (row0 + tc == geo.t)
    prev_row = jnp.where(first, 0.0, prev_ref[7:8, :])
    next_row = jnp.where(last, 0.0, next_ref[0:1, :])
    ridx = lax.broadcasted_iota(jnp.int32, (tc, 1), 0)
    x_prev = jnp.where(ridx == 0, prev_row, pltpu.roll(x, 1, axis=0))
    x_next = jnp.where(ridx == tc - 1, next_row, pltpu.roll(x, tc - 1, axis=0))
    y = x_prev * w_ref[0:1, :] + x * w_ref[1:2, :] + x_next * w_ref[2:3, :] + b_ref[...]
    y = _silu(y)
    q_ref[...] = y[:, :B_WIDTH].astype(q_ref.dtype)
    k_ref[...] = (y[:, B_WIDTH:] * (B_DIM ** -0.5)).astype(k_ref.dtype)


def _conv_call(geo, tc, qk, w, b):
    t = geo.t
    r8 = tc // 8
    nb8 = t // 8
    return pl.pallas_call(
        functools.partial(_conv_kernel, geo, tc),
        grid=(t // tc,),
        in_specs=[pl.BlockSpec((tc, 2 * B_WIDTH), lambda i: (i, 0)),
                  pl.BlockSpec((8, 2 * B_WIDTH), lambda i: (jnp.maximum(i * r8 - 1, 0), 0)),
                  pl.BlockSpec((8, 2 * B_WIDTH), lambda i: (jnp.minimum((i + 1) * r8, nb8 - 1), 0)),
                  pl.BlockSpec((3, 2 * B_WIDTH), lambda i: (0, 0)),
                  pl.BlockSpec((1, 2 * B_WIDTH), lambda i: (0, 0))],
        out_specs=[pl.BlockSpec((tc, B_WIDTH), lambda i: (i, 0))] * 2,
        out_shape=[jax.ShapeDtypeStruct((t, B_WIDTH), ACT_DT)] * 2,
        compiler_params=_cparams(("parallel",)),
        name="mlstm_conv",
    )(qk, qk, qk, w, b)


def _attn_finish(lam_init, acc0, l0, acc1, l1, dl_ref, ng_ref, o_ref):
    dl = dl_ref[...]
    lam = (jnp.exp(jnp.sum(dl[0:1] * dl[1:2], axis=-1, keepdims=True))
           - jnp.exp(jnp.sum(dl[2:3] * dl[3:4], axis=-1, keepdims=True)) + lam_init)
    out = acc0 / l0 - lam * (acc1 / l1)
    o_ref[...] = (_rms(out, ng_ref[...]) * (1.0 - lam_init)).astype(o_ref.dtype)


def _attn_kernel(lam_init, nk, q_ref, k_ref, v_ref, bias_ref, sc_ref, dl_ref, ng_ref, o_ref,
                 m_sc, l_sc, acc_sc):
    h = pl.program_id(1)
    i = pl.program_id(2)
    j = pl.program_id(3)

    @pl.when(j == 0)
    def _():
        m_sc[...] = jnp.full(m_sc.shape, NEG_BIG, F32)
        l_sc[...] = jnp.zeros(l_sc.shape, F32)
        acc_sc[...] = jnp.zeros(acc_sc.shape, F32)

    q = q_ref[...]
    lo_lane = lax.broadcasted_iota(jnp.int32, (1, LANES), 1) < A_QK_DIM
    qsub = (jnp.where(lo_lane, q, jnp.zeros_like(q)), jnp.where(lo_lane, jnp.zeros_like(q), q))
    k = k_ref[...]
    v = v_ref[...]

    def step(near):
        if near:
            shift = 0.0
        else:
            shift = jnp.where(j < i, sc_ref[h, 0], sc_ref[h, 1])
        for sub in range(2):
            s = _mm_nt(qsub[sub], k)
            if near:
                s = s + bias_ref[...]
            m_old = m_sc[sub]
            m_new = jnp.maximum(m_old, jnp.max(s, axis=-1, keepdims=True) + shift)
            p = jnp.exp2(s - (m_new - shift))
            alpha = jnp.exp2(m_old - m_new)
            l_sc[sub] = alpha * l_sc[sub] + jnp.sum(p, axis=-1, keepdims=True)
            acc_sc[sub] = alpha * acc_sc[sub] + _mm(p, v)
            m_sc[sub] = m_new

    near = jnp.abs(j - i) <= 1
    pl.when(near)(lambda: step(True))
    pl.when(jnp.logical_not(near))(lambda: step(False))

    @pl.when(j == nk - 1)
    def _():
        _attn_finish(lam_init, acc_sc[0], l_sc[0], acc_sc[1], l_sc[1], dl_ref, ng_ref, o_ref)


def _attn_bounded_kernel(lam_init, nk, q_ref, k_ref, v_ref, bias_ref, sc_ref, dl_ref, ng_ref, o_ref, acc_sc):
    h = pl.program_id(1)
    i = pl.program_id(2)
    j = pl.program_id(3)

    @pl.when(j == 0)
    def _():
        acc_sc[...] = jnp.zeros(acc_sc.shape, F32)

    @pl.when(j == jnp.maximum(i - 1, 0))
    def _():
        acc_sc[...] = acc_sc[...] * sc_ref[h, 2]

    @pl.when(j == i + 2)
    def _():
        acc_sc[...] = acc_sc[...] * sc_ref[h, 3]

    q = q_ref[...]
    k = k_ref[...]
    v = v_ref[...]
    lo = (lax.broadcasted_iota(jnp.int32, q.shape, 1) < A_QK_DIM).astype(F32).astype(q.dtype)
    qsub = (q * lo, q * (1 - lo))
    ones_col = (lax.broadcasted_iota(jnp.int32, v.shape, 1) == 0).astype(v.dtype)
    v_aug = jnp.concatenate([v, ones_col], axis=1)

    def step(near):
        for sub in range(2):
            s = _mm_nt(qsub[sub], k)
            if near:
                s = s + bias_ref[...]
            acc_sc[sub] += _mm(jnp.exp2(s), v_aug)

    near = jnp.abs(j - i) <= 1
    pl.when(near)(lambda: step(True))
    pl.when(jnp.logical_not(near))(lambda: step(False))

    @pl.when(j == nk - 1)
    def _():
        a0 = acc_sc[0]
        a1 = acc_sc[1]
        _attn_finish(lam_init, a0[:, :A_V_DIM], a0[:, A_V_DIM:A_V_DIM + 1], a1[:, :A_V_DIM],
                     a1[:, A_V_DIM:A_V_DIM + 1], dl_ref, ng_ref, o_ref)


def _attn_scalars(bias, tq, q_gain, k_gain):
    far_l = bias[:, 0, tq - 1, 0]
    far_r = bias[:, 2, 0, tq - 1]
    sc = jnp.stack([far_l, far_r, jnp.exp2(far_l), jnp.exp2(-far_r)], axis=-1)
    bound = A_QK_DIM * jnp.max(jnp.abs(q_gain)) * jnp.max(jnp.abs(k_gain)) * (A_QK_DIM ** -0.5) * LOG2E * 1.02
    spread = -jnp.min(bias)
    ok = bound + 2.0 * spread <= 80.0
    return sc, ok


def _attn_call(geo, tq, lam_init, bounded, qa, ka, va, bias, sc, dl, ng):
    outs = []
    for (nb, s, row_off) in ((geo.bp, geo.sp, 0), (geo.bs, geo.ss, geo.tp)):
        nq = s // tq
        off = row_off // tq
        qmap = lambda b, h, i, j, off=off, nq=nq: (off + b * nq + i, h)
        kmap = lambda b, h, i, j, off=off, nq=nq: (off + b * nq + j, h)
        omap = lambda b, h, i, j, nq=nq: (b * nq + i, h)
        if bounded:
            body = functools.partial(_attn_bounded_kernel, lam_init, nq)
            scratch = [pltpu.VMEM((2, tq, 2 * LANES), F32)]
        else:
            body = functools.partial(_attn_kernel, lam_init, nq)
            scratch = [pltpu.VMEM((2, tq, 1), F32), pltpu.VMEM((2, tq, 1), F32), pltpu.VMEM((2, tq, LANES), F32)]
        outs.append(pl.pallas_call(
            body,
            grid=(nb, A_HEADS, nq, nq),
            in_specs=[pl.BlockSpec((tq, LANES), qmap),
                      pl.BlockSpec((tq, LANES), kmap),
                      pl.BlockSpec((tq, LANES), kmap),
                      pl.BlockSpec((None, None, tq, tq), lambda b, h, i, j: (h, jnp.clip(j - i + 1, 0, 2), 0, 0)),
                      pl.BlockSpec(memory_space=pltpu.SMEM),
                      pl.BlockSpec((4, A_QK_DIM), lambda b, h, i, j: (0, 0)),
                      pl.BlockSpec((1, LANES), lambda b, h, i, j: (0, 0))],
            out_specs=pl.BlockSpec((tq, LANES), omap),
            out_shape=jax.ShapeDtypeStruct((nb * s, A_WIDTH), ACT_DT),
            scratch_shapes=scratch,
            compiler_params=_cparams(("parallel", "parallel", "parallel", "arbitrary")),
            name="diff_attn_bounded" if bounded else "diff_attn",
        )(qa, ka, va, bias, sc, dl, ng))
    return jnp.concatenate(outs, axis=0)


def _tri(lower):
    r = lax.broadcasted_iota(jnp.int32, (CHUNK, CHUNK), 0)
    c = lax.broadcasted_iota(jnp.int32, (CHUNK, CHUNK), 1)
    return (c <= r) if lower else (c >= r)


def _mlstm_kernel(geo, tb, reverse, *refs):
    if reverse:
        q_ref, k_ref, v_ref, g_ref, hf_ref, ob_ref, ng_ref, o_ref, c_sc, m_sc = refs
    else:
        q_ref, k_ref, v_ref, g_ref, o_ref, c_sc, m_sc = refs
    step = pl.program_id(0)
    nblk = geo.t // tb
    blk = (nblk - 1 - step) if reverse else step
    row0 = blk * tb
    if reverse:
        fresh = geo.seq_start(row0 + tb) | (row0 + tb == geo.t)
    else:
        fresh = geo.seq_start(row0)

    @pl.when(fresh)
    def _():
        c_sc[...] = jnp.zeros(c_sc.shape, F32)
        m_sc[...] = jnp.zeros(m_sc.shape, F32)

    mask = _tri(not reverse)
    cum_l = mask.astype(F32)
    cum_r = _tri(reverse).astype(F32)
    ones_col = (lax.broadcasted_iota(jnp.int32, (CHUNK, LANES), 1) == 0).astype(MXU_DT)
    nch = tb // CHUNK

    def chunk(ci, carry):
        c_idx = (nch - 1 - ci) if reverse else ci
        r0 = pl.multiple_of(c_idx * CHUNK, CHUNK)
        g = g_ref[pl.ds(r0, CHUNK), :]
        g_t = g.T
        b_col = _mm_hi(cum_l, _log_sigmoid(g))
        b_row = _mm_hi(_log_sigmoid(g_t), cum_r)
        for hd in range(B_HEADS):
            ci_col = hd * N_GATES + (2 if reverse else 0)
            cf_col = ci_col + 1
            lanes = slice(hd * B_DIM, (hd + 1) * B_DIM)
            q = q_ref[pl.ds(r0, CHUNK), lanes]
            k = k_ref[pl.ds(r0, CHUNK), lanes]
            v = v_ref[pl.ds(r0, CHUNK), lanes]
            bc = b_col[:, cf_col:cf_col + 1]
            br = b_row[cf_col:cf_col + 1, :]
            ic = g[:, ci_col:ci_col + 1]
            ir = g_t[ci_col:ci_col + 1, :]
            m_prev = m_sc[hd]
            log_d = jnp.where(mask, bc - br + ir, NEG_BIG)
            m_inter = bc + m_prev
            m_t = jnp.maximum(jnp.max(log_d, axis=-1, keepdims=True), m_inter)
            s = _mm_nt(q, k) * jnp.exp(log_d - m_t)
            inter = jnp.exp(m_inter - m_t)
            c_aug = c_sc[hd]
            qc = _mm(q, c_aug)
            num = _mm(s, v) + inter * qc[:, :B_DIM]
            den = jnp.sum(s, axis=-1, keepdims=True) + inter * qc[:, B_DIM:B_DIM + 1]
            hout = num / jnp.maximum(jnp.abs(den), jnp.exp(-m_t))
            b_last = bc[0:1, :] if reverse else bc[CHUNK - 1:CHUNK, :]
            log_w = b_last - bc + ic
            m_new = jnp.maximum(b_last + m_prev, jnp.max(log_w, axis=0, keepdims=True))
            w = jnp.exp(log_w - m_new)
            decay = jnp.exp(b_last + m_prev - m_new)
            v_aug = jnp.concatenate([v, ones_col], axis=1)
            c_sc[hd] = decay * c_aug + _mm_tn(k.astype(F32) * w, v_aug)
            m_sc[hd] = m_new
            if reverse:
                hsum = hf_ref[pl.ds(r0, CHUNK), lanes] + hout
                y = _rms(hsum, ng_ref[...]) * _sigmoid(ob_ref[pl.ds(r0, CHUNK), lanes])
                o_ref[pl.ds(r0, CHUNK), lanes] = y.astype(o_ref.dtype)
            else:
                o_ref[pl.ds(r0, CHUNK), lanes] = hout
        return carry

    lax.fori_loop(0, nch, chunk, 0)


def _mlstm_call(geo, tb, qb, kb, vb, gates, ob, ng):
    t = geo.t
    nblk = t // tb
    fmap = lambda s: (s, 0)
    rmap = lambda s: (nblk - 1 - s, 0)
    scratch = [pltpu.VMEM((B_HEADS, B_DIM, 2 * B_DIM), F32), pltpu.VMEM((B_HEADS, 1, 1), F32)]
    wide = lambda m: pl.BlockSpec((tb, B_WIDTH), m)
    hf = pl.pallas_call(
        functools.partial(_mlstm_kernel, geo, tb, False),
        grid=(nblk,),
        in_specs=[wide(fmap), wide(fmap), wide(fmap), pl.BlockSpec((tb, LANES), fmap)],
        out_specs=wide(fmap),
        out_shape=jax.ShapeDtypeStruct((t, B_WIDTH), F32),
        scratch_shapes=scratch,
        compiler_params=_cparams(("arbitrary",)),
        name="mlstm_fwd",
    )(qb, kb, vb, gates)
    return pl.pallas_call(
        functools.partial(_mlstm_kernel, geo, tb, True),
        grid=(nblk,),
        in_specs=[wide(rmap), wide(rmap), wide(rmap), pl.BlockSpec((tb, LANES), rmap), wide(rmap), wide(rmap),
                  pl.BlockSpec((1, B_DIM), lambda s: (0, 0))],
        out_specs=wide(rmap),
        out_shape=jax.ShapeDtypeStruct((t, B_WIDTH), ACT_DT),
        scratch_shapes=scratch,
        compiler_params=_cparams(("arbitrary",)),
        name="mlstm_bwd",
    )(qb, kb, vb, gates, hf, ob, ng)


def _inproj_c_kernel(x_ref, mod_ref, g_ref, w_ref, cos_ref, sin_ref, q_ref, k_ref, v_ref, gt_ref, h_sc):
    j = pl.program_id(1)

    @pl.when(j == 0)
    def _():
        x = x_ref[...]
        h = _rms(x, g_ref[...]) * (1.0 + mod_ref[1:2, :]) + mod_ref[0:1, :]
        h_sc[...] = h.astype(h_sc.dtype)

    y = jnp.dot(h_sc[...], w_ref[...], preferred_element_type=F32)

    def rope(scale):
        cos = cos_ref[...]
        sin = sin_ref[...]
        half = C_QK_DIM // 2
        parts = []
        for hd in range(C_HEADS):
            x1 = y[:, hd * C_QK_DIM:hd * C_QK_DIM + half]
            x2 = y[:, hd * C_QK_DIM + half:(hd + 1) * C_QK_DIM]
            parts.append((x1 * cos - x2 * sin) * scale)
            parts.append((x1 * sin + x2 * cos) * scale)
        return jnp.concatenate(parts, axis=1)

    @pl.when(j == 0)
    def _():
        q_ref[...] = rope(1.0).astype(q_ref.dtype)

    @pl.when(j == 1)
    def _():
        k_ref[...] = rope(C_QK_DIM ** -0.5).astype(k_ref.dtype)

    @pl.when((j == 2) | (j == 3))
    def _():
        v_ref[...] = y.astype(v_ref.dtype)

    @pl.when(j >= 4)
    def _():
        gt_ref[...] = _silu(y).astype(gt_ref.dtype)


def _inproj_c_call(geo, tm, x, mod_l, g, w, cos, sin):
    t = geo.t
    per_seg = geo.seg // tm
    nw = D_MODEL
    return pl.pallas_call(
        _inproj_c_kernel,
        grid=(t // tm, C_IN // nw),
        in_specs=[pl.BlockSpec((tm, D_MODEL), lambda i, j: (i, 0)),
                  pl.BlockSpec((None, 6, D_MODEL), lambda i, j: (i // per_seg, 0, 0)),
                  pl.BlockSpec((1, D_MODEL), lambda i, j: (0, 0)),
                  pl.BlockSpec((D_MODEL, nw), lambda i, j: (0, j)),
                  pl.BlockSpec((tm, C_QK_DIM // 2), lambda i, j: (geo.pos_block(i, tm), 0)),
                  pl.BlockSpec((tm, C_QK_DIM // 2), lambda i, j: (geo.pos_block(i, tm), 0))],
        out_specs=[pl.BlockSpec((tm, nw), lambda i, j: (i, 0)),
                   pl.BlockSpec((tm, nw), lambda i, j: (i, 0)),
                   pl.BlockSpec((tm, nw), lambda i, j: (i, jnp.clip(j - 2, 0, 1))),
                   pl.BlockSpec((tm, nw), lambda i, j: (i, jnp.clip(j - 4, 0, 1)))],
        out_shape=[jax.ShapeDtypeStruct((t, C_QK_WIDTH), ACT_DT),
                   jax.ShapeDtypeStruct((t, C_QK_WIDTH), ACT_DT),
                   jax.ShapeDtypeStruct((t, C_V_WIDTH), ACT_DT),
                   jax.ShapeDtypeStruct((t, C_V_WIDTH), ACT_DT)],
        scratch_shapes=[pltpu.VMEM((tm, D_MODEL), MXU_DT)],
        compiler_params=_cparams(("parallel", "arbitrary")),
        name="inproj_c",
    )(x, mod_l, g, w, cos, sin)


def _ret_kernel(geo, tb, reverse, *refs):
    if reverse:
        q_ref, k_ref, v_ref, dlg_ref, yf_ref, gt_ref, ng_ref, o_ref, r_sc, intra_sc, vec_sc = refs
    else:
        q_ref, k_ref, v_ref, dlg_ref, o_ref, r_sc, intra_sc, vec_sc = refs
    step = pl.program_id(0)
    nblk = geo.t // tb
    blk = (nblk - 1 - step) if reverse else step
    row0 = blk * tb
    if reverse:
        fresh = geo.seq_start(row0 + tb) | (row0 + tb == geo.t)
    else:
        fresh = geo.seq_start(row0)

    @pl.when(fresh)
    def _():
        r_sc[...] = jnp.zeros(r_sc.shape, F32)

    @pl.when(step == 0)
    def _():
        lg_all = _log_sigmoid(dlg_ref[...])
        ti = lax.broadcasted_iota(jnp.int32, (RET_CHUNK, RET_CHUNK), 0)
        si = lax.broadcasted_iota(jnp.int32, (RET_CHUNK, RET_CHUNK), 1)
        dist = ((si - ti) if reverse else (ti - si)).astype(F32)
        pos = lax.broadcasted_iota(jnp.int32, (RET_CHUNK, LANES), 0).astype(F32)
        upos = (RET_CHUNK - 1.0 - pos) if reverse else pos
        lane = lax.broadcasted_iota(jnp.int32, (RET_CHUNK, LANES), 1)
        d = 1 if reverse else 0
        for hd in range(C_HEADS):
            lg = lg_all[d:d + 1, hd:hd + 1]
            intra_sc[hd] = jnp.where(dist >= 0, jnp.exp(jnp.maximum(dist, 0.0) * lg), 0.0)
            vec_sc[hd] = jnp.where(lane == 0, jnp.exp((upos + 1.0) * lg),
                                   jnp.where(lane == 1, jnp.exp((RET_CHUNK - 1.0 - upos) * lg),
                                             jnp.exp(RET_CHUNK * lg)))

    nch = tb // RET_CHUNK

    def chunk(ci, carry):
        c_idx = (nch - 1 - ci) if reverse else ci
        r0 = pl.multiple_of(c_idx * RET_CHUNK, RET_CHUNK)
        for hd in range(C_HEADS):
            vec = vec_sc[hd]
            q_scale, k_scale, c_decay = vec[:, 0:1], vec[:, 1:2], vec[0:1, 2:3]
            ql = slice(hd * C_QK_DIM, (hd + 1) * C_QK_DIM)
            vl = slice(hd * C_V_DIM, (hd + 1) * C_V_DIM)
            q = q_ref[pl.ds(r0, RET_CHUNK), ql]
            k = k_ref[pl.ds(r0, RET_CHUNK), ql]
            v = v_ref[pl.ds(r0, RET_CHUNK), vl]
            r_old = r_sc[hd]
            s = _mm_nt(q, k) * intra_sc[hd]
            y = _mm(s, v) + q_scale * _mm(q, r_old)
            r_sc[hd] = c_decay * r_old + _mm_tn(k.astype(F32) * k_scale, v)
            if reverse:
                ysum = yf_ref[pl.ds(r0, RET_CHUNK), vl] + y
                out = _rms(ysum, ng_ref[...]) * gt_ref[pl.ds(r0, RET_CHUNK), vl].astype(F32)
                o_ref[pl.ds(r0, RET_CHUNK), vl] = out.astype(o_ref.dtype)
            else:
                o_ref[pl.ds(r0, RET_CHUNK), vl] = y
        return carry

    lax.fori_loop(0, nch, chunk, 0)


def _ret_call(geo, tb, q, k, v, decay_logit, gt, ng):
    t = geo.t
    nblk = t // tb
    fmap = lambda s: (s, 0)
    rmap = lambda s: (nblk - 1 - s, 0)
    scratch = [pltpu.VMEM((C_HEADS, C_QK_DIM, C_V_DIM), F32), pltpu.VMEM((C_HEADS, RET_CHUNK, RET_CHUNK), F32),
               pltpu.VMEM((C_HEADS, RET_CHUNK, LANES), F32)]
    qk = lambda m: pl.BlockSpec((tb, C_QK_WIDTH), m)
    vv = lambda m: pl.BlockSpec((tb, C_V_WIDTH), m)
    dspec = pl.BlockSpec((2, C_HEADS), lambda s: (0, 0))
    yf = pl.pallas_call(
        functools.partial(_ret_kernel, geo, tb, False),
        grid=(nblk,),
        in_specs=[qk(fmap), qk(fmap), vv(fmap), dspec],
        out_specs=vv(fmap),
        out_shape=jax.ShapeDtypeStruct((t, C_V_WIDTH), F32),
        scratch_shapes=scratch,
        compiler_params=_cparams(("arbitrary",)),
        name="ret_fwd",
    )(q, k, v, decay_logit)
    return pl.pallas_call(
        functools.partial(_ret_kernel, geo, tb, True),
        grid=(nblk,),
        in_specs=[qk(rmap), qk(rmap), vv(rmap), dspec, vv(rmap), vv(rmap),
                  pl.BlockSpec((1, C_V_DIM), lambda s: (0, 0))],
        out_specs=vv(rmap),
        out_shape=jax.ShapeDtypeStruct((t, C_V_WIDTH), ACT_DT),
        scratch_shapes=scratch,
        compiler_params=_cparams(("arbitrary",)),
        name="ret_bwd",
    )(q, k, v, decay_logit, yf, gt, ng)


def _route(probs):
    p = [probs[e:e + 1, :] for e in range(N_EXPERTS)]
    scores = []
    for g in range(N_GROUPS):
        a, b, c, d = p[EPG * g:EPG * g + EPG]
        hi1, lo1 = jnp.maximum(a, b), jnp.minimum(a, b)
        hi2, lo2 = jnp.maximum(c, d), jnp.minimum(c, d)
        scores.append(jnp.maximum(hi1, hi2) + jnp.maximum(jnp.minimum(hi1, hi2), jnp.maximum(lo1, lo2)))
    g_sel = jnp.zeros(scores[0].shape, jnp.int32)
    best = scores[0]
    for g in range(1, N_GROUPS):
        better = scores[g] > best
        g_sel = jnp.where(better, g, g_sel)
        best = jnp.where(better, scores[g], best)
    vals = []
    for kk in range(EPG):
        v = p[kk]
        for g in range(1, N_GROUPS):
            v = jnp.where(g_sel == g, p[EPG * g + kk], v)
        vals.append(v)

    def argmax4(xs):
        idx = jnp.zeros(xs[0].shape, jnp.int32)
        top = xs[0]
        for kk in range(1, EPG):
            better = xs[kk] > top
            idx = jnp.where(better, kk, idx)
            top = jnp.where(better, xs[kk], top)
        return idx, top

    i1, v1 = argmax4(vals)
    i2, v2 = argmax4([jnp.where(i1 == kk, -1.0, vals[kk]) for kk in range(EPG)])
    tot = v1 + v2
    w1, w2 = v1 / tot, v2 / tot
    e1 = g_sel * EPG + i1
    e2 = g_sel * EPG + i2
    eidx = lax.broadcasted_iota(jnp.int32, probs.shape, 0)
    return jnp.where(eidx == e1, w1, 0.0) + jnp.where(eidx == e2, w2, 0.0), g_sel


def _outproj_kernel(nparts, *refs):
    y_refs = refs[:nparts]
    w_refs = refs[nparts:2 * nparts]
    x_ref, mod_ref, g_ref, rw_ref, rb_ref, xo_ref, hx_ref, grp_ref = refs[2 * nparts:]
    m = jnp.dot(y_refs[0][...], w_refs[0][...], preferred_element_type=F32)
    for p in range(1, nparts):
        m = m + jnp.dot(y_refs[p][...], w_refs[p][...], preferred_element_type=F32)
    x = x_ref[...] + mod_ref[2:3, :] * m
    xo_ref[...] = x
    h = _rms(x, g_ref[...]) * (1.0 + mod_ref[4:5, :]) + mod_ref[3:4, :]
    hx_ref[:, :D_MODEL] = h
    logits = lax.dot_general(rw_ref[...], h, (((1,), (1,)), ((), ())), precision=HI,
                             preferred_element_type=F32) + rb_ref[...]
    z = jnp.exp(logits - jnp.max(logits, axis=0, keepdims=True))
    probs = z / jnp.sum(z, axis=0, keepdims=True)
    cmb, g_sel = _route(probs)
    pad = jnp.zeros((LANES - N_EXPERTS, cmb.shape[1]), F32)
    hx_ref[:, D_MODEL:] = jnp.concatenate([cmb, pad], axis=0).T
    grp_ref[...] = g_sel


def _outproj_call(geo, tm, ys, ws, x, mod_l, g, rw_t, rb):
    t = geo.t
    per_seg = geo.seg // tm
    row = lambda i: (i, 0)
    const = lambda i: (0, 0)
    n = len(ys)
    return pl.pallas_call(
        functools.partial(_outproj_kernel, n),
        grid=(t // tm,),
        in_specs=([pl.BlockSpec((tm, y.shape[1]), row) for y in ys]
                  + [pl.BlockSpec(w.shape, const) for w in ws]
                  + [pl.BlockSpec((tm, D_MODEL), row),
                     pl.BlockSpec((None, 6, D_MODEL), lambda i: (i // per_seg, 0, 0)),
                     pl.BlockSpec((1, D_MODEL), const),
                     pl.BlockSpec((N_EXPERTS, D_MODEL), const),
                     pl.BlockSpec((N_EXPERTS, 1), const)]),
        out_specs=[pl.BlockSpec((tm, D_MODEL), row), pl.BlockSpec((tm, HX_W), row),
                   pl.BlockSpec((None, 1, tm), lambda i: (i, 0, 0))],
        out_shape=[jax.ShapeDtypeStruct((t, D_MODEL), F32), jax.ShapeDtypeStruct((t, HX_W), F32),
                   jax.ShapeDtypeStruct((t // tm, 1, tm), jnp.int32)],
        compiler_params=_cparams(("parallel",)),
        name="outproj_router",
    )(*ys, *ws, x, mod_l, g, rw_t, rb)


def _plan_kernel(tm, grp_ref, pos_ref, tg_ref, nv_ref):
    nblk, _, blk = grp_ref.shape
    gid = lax.broadcasted_iota(jnp.int32, (8, blk), 0)
    r = lax.broadcasted_iota(jnp.int32, (blk, blk), 0)
    c = lax.broadcasted_iota(jnp.int32, (blk, blk), 1)
    prefix = (r <= c).astype(MXU_DT)

    def count(b, acc):
        return acc + jnp.sum((grp_ref[b] == gid).astype(F32), axis=1, keepdims=True)

    counts = lax.fori_loop(0, nblk, count, jnp.zeros((8, 1), F32))
    padded = jnp.floor((counts + (tm - 1.0)) / tm) * tm
    row8 = lax.broadcasted_iota(jnp.int32, (8, 1), 0)
    offs = jnp.zeros((8, 1), F32)
    ends = []
    run = jnp.zeros((1, 1), F32)
    for g in range(N_GROUPS):
        offs = jnp.where(row8 == g, run, offs)
        run = run + padded[g:g + 1, :]
        ends.append(run)

    def place(b, carry):
        oh = (grp_ref[b] == gid).astype(F32)
        pre = jnp.dot(oh.astype(MXU_DT), prefix, preferred_element_type=F32)
        pos = jnp.sum(oh * (offs + carry + pre - 1.0), axis=0, keepdims=True)
        pos_ref[b] = pos.astype(jnp.int32)
        return carry + pre[:, blk - 1:blk]

    lax.fori_loop(0, nblk, place, jnp.zeros((8, 1), F32))
    start = lax.broadcasted_iota(jnp.int32, tg_ref.shape, 1).astype(F32) * tm
    tg = jnp.zeros(tg_ref.shape, F32)
    for g in range(N_GROUPS - 1):
        tg = tg + (ends[g] <= start).astype(F32)
    tg_ref[...] = tg.astype(jnp.int32)
    nv_ref[...] = jnp.broadcast_to(run / tm, nv_ref.shape).astype(jnp.int32)


def _plan_call(tm, grp):
    nblk, _, blk = grp.shape
    nt = nblk * blk // tm + N_GROUPS
    ntp = -(-nt // LANES) * LANES
    pos, tg, nv = pl.pallas_call(
        functools.partial(_plan_kernel, tm),
        out_shape=[jax.ShapeDtypeStruct(grp.shape, jnp.int32), jax.ShapeDtypeStruct((1, ntp), jnp.int32),
                   jax.ShapeDtypeStruct((1, LANES), jnp.int32)],
        compiler_params=pltpu.CompilerParams(vmem_limit_bytes=VMEM_LIMIT),
        name="moe_plan",
    )(grp)
    return pos, tg[0, :nt], nv[0, :1]


def _dispatch_kernel(tb, pos_ref, hx_ref, init_ref, xs_ref, sem):
    del init_ref

    def issue(r, carry):
        pltpu.make_async_copy(hx_ref.at[pl.ds(r, 1)], xs_ref.at[pl.ds(pos_ref[0, r], 1)], sem).start()
        return carry

    lax.fori_loop(0, tb, issue, 0, unroll=8)
    pltpu.make_async_copy(hx_ref, xs_ref.at[pl.ds(0, tb)], sem).wait()


def _dispatch_call(tb, pos, hx, xs_init):
    nsteps = hx.shape[0] // tb
    return pl.pallas_call(
        functools.partial(_dispatch_kernel, tb),
        grid=(nsteps,),
        in_specs=[pl.BlockSpec((None, 1, tb), lambda i: (i, 0, 0), memory_space=pltpu.SMEM),
                  pl.BlockSpec((tb, HX_W), lambda i: (i, 0)),
                  pl.BlockSpec(memory_space=pl.ANY)],
        out_specs=pl.BlockSpec(memory_space=pl.ANY),
        out_shape=jax.ShapeDtypeStruct(xs_init.shape, xs_init.dtype),
        scratch_shapes=[pltpu.SemaphoreType.DMA(())],
        input_output_aliases={2: 0},
        compiler_params=pltpu.CompilerParams(dimension_semantics=("arbitrary",), disable_bounds_checks=True,
                                             has_side_effects=True),
        name="moe_dispatch",
    )(pos, hx, xs_init)


def _moe_kernel(tg_ref, nv_ref, xs_ref, w1_ref, w3_ref, w2_ref, o_ref, acc_sc):
    n = pl.program_id(0)
    e = pl.program_id(1)

    @pl.when(n < nv_ref[0])
    def _():
        @pl.when(e == 0)
        def _():
            acc_sc[...] = jnp.zeros(acc_sc.shape, F32)

        h = xs_ref[:, :D_MODEL].astype(MXU_DT)
        a = jnp.dot(h, w1_ref[...], preferred_element_type=F32)
        b = jnp.dot(h, w3_ref[...], preferred_element_type=F32)
        he = _silu(a) * b
        cmb = xs_ref[:, D_MODEL:]
        sel = lax.broadcasted_iota(jnp.int32, cmb.shape, 1) == tg_ref[n] * EPG + e
        c = jnp.sum(jnp.where(sel, cmb, 0.0), axis=-1, keepdims=True)
        acc_sc[...] += c * _mm(he, w2_ref[...])

        @pl.when(e == EPG - 1)
        def _():
            o_ref[...] = acc_sc[...]

    @pl.when((n >= nv_ref[0]) & (e == EPG - 1))
    def _():
        o_ref[...] = jnp.zeros(o_ref.shape, F32)


def _moe_call(tm, layer, tg, nv, xs, w1, w3, w2):
    rows = xs.shape[0]
    nt = rows // tm
    tile = lambda n, e, tg, nv: (jnp.minimum(n, nv[0] - 1), 0)
    out_tile = lambda n, e, tg, nv: (n, 0)

    def wmap(n, e, tg, nv):
        return (layer, tg[jnp.minimum(n, nv[0] - 1)] * EPG + jnp.where(n < nv[0], e, EPG - 1), 0, 0)

    return pl.pallas_call(
        _moe_kernel,
        grid_spec=pltpu.PrefetchScalarGridSpec(
            num_scalar_prefetch=2,
            grid=(nt, EPG),
            in_specs=[pl.BlockSpec((tm, HX_W), tile),
                      pl.BlockSpec((None, None, D_MODEL, D_FF), wmap),
                      pl.BlockSpec((None, None, D_MODEL, D_FF), wmap),
                      pl.BlockSpec((None, None, D_FF, D_MODEL), wmap)],
            out_specs=pl.BlockSpec((tm, D_MODEL), out_tile),
            scratch_shapes=[pltpu.VMEM((tm, D_MODEL), F32)]),
        out_shape=jax.ShapeDtypeStruct((rows, D_MODEL), F32),
        compiler_params=_cparams(("arbitrary", "arbitrary")),
        name="moe",
    )(tg, nv, xs, w1, w3, w2)


def _collect_kernel(tb, nsteps, pos_ref, posn_ref, ys_ref, x_ref, mod_ref, o_ref, buf, sem):
    i = pl.program_id(0)
    slot = i % 2

    def issue(p_ref, s):
        def body(r, carry):
            pltpu.make_async_copy(ys_ref.at[pl.ds(p_ref[0, r], 1)], buf.at[s, pl.ds(r, 1)], sem.at[s]).start()
            return carry
        lax.fori_loop(0, tb, body, 0, unroll=8)

    pl.when(i == 0)(lambda: issue(pos_ref, 0))
    pl.when(i + 1 < nsteps)(lambda: issue(posn_ref, 1 - slot))
    pltpu.make_async_copy(ys_ref.at[pl.ds(0, tb)], buf.at[slot], sem.at[slot]).wait()
    o_ref[...] = x_ref[...] + mod_ref[5:6, :] * buf[slot]


def _collect_call(geo, tb, pos, ys, x, mod_l):
    t = geo.t
    nsteps = t // tb
    per_seg = geo.seg // tb
    return pl.pallas_call(
        functools.partial(_collect_kernel, tb, nsteps),
        grid=(nsteps,),
        in_specs=[pl.BlockSpec((None, 1, tb), lambda i: (i, 0, 0), memory_space=pltpu.SMEM),
                  pl.BlockSpec((None, 1, tb), lambda i: (jnp.minimum(i + 1, nsteps - 1), 0, 0),
                               memory_space=pltpu.SMEM),
                  pl.BlockSpec(memory_space=pl.ANY),
                  pl.BlockSpec((tb, D_MODEL), lambda i: (i, 0)),
                  pl.BlockSpec((None, 6, D_MODEL), lambda i: (i // per_seg, 0, 0))],
        out_specs=pl.BlockSpec((tb, D_MODEL), lambda i: (i, 0)),
        out_shape=jax.ShapeDtypeStruct((t, D_MODEL), F32),
        scratch_shapes=[pltpu.VMEM((2, tb, D_MODEL), F32), pltpu.SemaphoreType.DMA((2,))],
        compiler_params=pltpu.CompilerParams(dimension_semantics=("arbitrary",), vmem_limit_bytes=VMEM_LIMIT,
                                             disable_bounds_checks=True),
        name="moe_collect",
    )(pos, pos, ys, x, mod_l)


def _tiles(geo):
    seg = geo.seg
    return dict(tm=min(512, seg), tmc=min(1024, seg), tq=min(1024, seg), tscan=min(512, seg), tmoe=min(1024, seg))


def _forward(geo, tiles, x_prompt, x_sample, c_prompt, c_sample, rel_bias, router_w, router_b, ada_w, ada_b,
             norm_mix_g, norm_ffn_g, w_in_ab, w_out_ab, q_norm_g, k_norm_g, diff_lambda, diff_norm_g,
             mlstm_conv_w, mlstm_conv_b, mlstm_gate_b, mlstm_norm_g, w_in_c, w_out_c, ret_decay_logit,
             ret_norm_g, moe_w1, moe_w3, moe_w2):
    tm, tmc, tq, tscan, tmoe = tiles["tm"], tiles["tmc"], tiles["tq"], tiles["tscan"], tiles["tmoe"]
    x = jnp.concatenate([x_prompt.reshape(geo.tp, D_MODEL), x_sample.reshape(geo.t - geo.tp, D_MODEL)], axis=0)
    c_rows = jnp.concatenate([jnp.repeat(c_prompt, geo.sp // geo.seg, axis=0),
                              jnp.repeat(c_sample, geo.ss // geo.seg, axis=0)], axis=0)
    mod = _ada_call(c_rows, ada_w, ada_b)

    bias = _bias_call(rel_bias, tq)
    cos, sin = _rope_tables(max(geo.sp, geo.ss))
    rw_t = router_w.T
    rb = router_b.reshape(N_EXPERTS, 1)
    w1 = moe_w1.astype(MXU_DT)
    w3 = moe_w3.astype(MXU_DT)
    w2 = moe_w2.astype(MXU_DT)
    xs = jnp.zeros((geo.t + N_GROUPS * tmoe, HX_W), F32)

    for l in range(DEPTH):
        j = l // 2
        mod_l = mod[l]
        g_mix = norm_mix_g[l].reshape(1, D_MODEL)
        g_ffn = norm_ffn_g[l].reshape(1, D_MODEL)
        if l % 2 == 0:
            lam_init = 0.8 - 0.6 * math.exp(-0.3 * l)
            w_pad = jnp.pad(w_in_ab[j], ((0, 0), (0, AB_IN_PAD - AB_IN))).astype(MXU_DT)
            qg = jnp.tile(q_norm_g[j], 2).reshape(1, LANES)
            kg = jnp.tile(k_norm_g[j], 2).reshape(1, LANES)
            gate_b = jnp.pad(mlstm_gate_b[j].reshape(1, B_HEADS * N_GATES), ((0, 0), (0, LANES - B_HEADS * N_GATES)))
            qa, ka, va, qk, vb, ob, gates = _inproj_ab_call(geo, tm, x, mod_l, g_mix, w_pad, qg, kg, gate_b)
            sc, bounded_ok = _attn_scalars(bias, tq, q_norm_g[j], k_norm_g[j])
            ya = lax.cond(bounded_ok,
                          functools.partial(_attn_call, geo, tq, lam_init, True),
                          functools.partial(_attn_call, geo, tq, lam_init, False),
                          qa, ka, va, bias, sc, diff_lambda[j], diff_norm_g[j].reshape(1, A_V_DIM))
            qb, kb = _conv_call(geo, tscan, qk, mlstm_conv_w[j], mlstm_conv_b[j].reshape(1, 2 * B_WIDTH))
            yb = _mlstm_call(geo, tscan, qb, kb, vb, gates, ob, mlstm_norm_g[j].reshape(1, B_DIM))
            w_o = w_out_ab[j].astype(MXU_DT)
            ys, ws = [ya, yb], [w_o[:A_WIDTH], w_o[A_WIDTH:]]
        else:
            q, k, v, gt = _inproj_c_call(geo, tmc, x, mod_l, g_mix, w_in_c[j].astype(MXU_DT), cos, sin)
            y = _ret_call(geo, tscan, q, k, v, ret_decay_logit[j], gt, ret_norm_g[j].reshape(1, C_V_DIM))
            ys, ws = [y], [w_out_c[j].astype(MXU_DT)]
        x, hx, grp = _outproj_call(geo, tm, ys, ws, x, mod_l, g_ffn, rw_t, rb)
        pos, tile_group, n_tiles = _plan_call(tmoe, grp)
        xs = _dispatch_call(tm, pos, hx, xs)
        ysort = _moe_call(tmoe, l, tile_group, n_tiles, xs, w1, w3, w2)
        x = _collect_call(geo, tm, pos, ysort, x, mod_l)

    y_prompt = x[:geo.tp].reshape(x_prompt.shape)
    y_sample = x[geo.tp:].reshape(x_sample.shape)
    return (y_prompt, y_sample)


def kernel(x_prompt, x_sample, c_prompt, c_sample, rel_bias, router_w, router_b, ada_w, ada_b, norm_mix_g, norm_ffn_g, w_in_ab, w_out_ab, q_norm_g, k_norm_g, diff_lambda, diff_norm_g, mlstm_conv_w, mlstm_conv_b, mlstm_gate_b, mlstm_norm_g, w_in_c, w_out_c, ret_decay_logit, ret_norm_g, moe_w1, moe_w3, moe_w2):
    geo = Geo(x_prompt.shape[0], x_prompt.shape[1], x_sample.shape[0], x_sample.shape[1])
    return _forward(geo, _tiles(geo), x_prompt, x_sample, c_prompt, c_sample, rel_bias, router_w, router_b,
                    ada_w, ada_b, norm_mix_g, norm_ffn_g, w_in_ab, w_out_ab, q_norm_g, k_norm_g, diff_lambda,
                    diff_norm_g, mlstm_conv_w, mlstm_conv_b, mlstm_gate_b, mlstm_norm_g, w_in_c, w_out_c,
                    ret_decay_logit, ret_norm_g, moe_w1, moe_w3, moe_w2)
```

```python
import functools
import math

import jax
import jax.numpy as jnp
import numpy as np
from jax import lax
from jax.experimental import pallas as pl
from jax.experimental.pallas import tpu as pltpu

F32 = jnp.float32
MXU_DT = jnp.bfloat16
ACT_DT = jnp.bfloat16
HI = lax.Precision.HIGHEST

D_MODEL = 1024
DEPTH = 4
A_HEADS = 4
A_QK_DIM = 64
A_V_DIM = 128
A_QK_WIDTH = 512
A_WIDTH = 512
B_HEADS = 4
B_DIM = 128
B_WIDTH = 512
N_GATES = 4
AB_IN = 3600
AB_IN_PAD = 3712
C_HEADS = 4
C_QK_DIM = 256
C_V_DIM = 512
C_QK_WIDTH = 1024
C_V_WIDTH = 2048
C_IN = 6144
CHUNK = 128
RET_CHUNK = 256
REL_BUCKETS = 32
REL_MAX_DIST = 128
N_EXPERTS = 16
N_GROUPS = 4
EPG = 4
N_PAIRS = 6
N_BUCKETS = N_GROUPS * N_PAIRS
D_FF = 512
ROPE_BASE = 10000.0
EPS = 1e-6
LANES = 128
NEG_BIG = -1e30
LOG2E = math.log2(math.e)
HX_W = D_MODEL + LANES
VMEM_LIMIT = 56 * 1024 * 1024


def _cparams(sem):
    return pltpu.CompilerParams(dimension_semantics=sem, vmem_limit_bytes=VMEM_LIMIT)


def _mm(a, b):
    return jnp.dot(a.astype(MXU_DT), b.astype(MXU_DT), preferred_element_type=F32)


def _mm_nt(a, b):
    return lax.dot_general(a.astype(MXU_DT), b.astype(MXU_DT), (((1,), (1,)), ((), ())),
                           preferred_element_type=F32)


def _mm_tn(a, b):
    return lax.dot_general(a.astype(MXU_DT), b.astype(MXU_DT), (((0,), (0,)), ((), ())),
                           preferred_element_type=F32)


def _mm_hi(a, b):
    return jnp.dot(a, b, precision=HI, preferred_element_type=F32)


def _silu(x):
    return x * (1.0 / (1.0 + jnp.exp(-x)))


def _sigmoid(x):
    return 1.0 / (1.0 + jnp.exp(-x))


def _log_sigmoid(x):
    return jnp.minimum(x, 0.0) - jnp.log1p(jnp.exp(-jnp.abs(x)))


def _rms(x, g):
    return x * lax.rsqrt(jnp.mean(x * x, axis=-1, keepdims=True) + EPS) * g


class Geo:
    def __init__(self, bp, sp, bs, ss):
        self.bp, self.sp, self.bs, self.ss = bp, sp, bs, ss
        self.tp = bp * sp
        self.t = bp * sp + bs * ss
        self.seg = math.gcd(sp, ss)

    def seq_start(self, row0):
        return jnp.where(row0 < self.tp, row0 % self.sp == 0, (row0 - self.tp) % self.ss == 0)

    def pos_block(self, blk, rows):
        nbp = self.tp // rows
        return jnp.where(blk < nbp, blk % (self.sp // rows), (blk - nbp) % (self.ss // rows))


def _ada_kernel(c_ref, w_ref, b_ref, o_ref):
    o_ref[...] = _mm_hi(_silu(c_ref[...]), w_ref[...]) + b_ref[...]


def _ada_call(c_rows, ada_w, ada_b):
    r = c_rows.shape[0]
    nb = 1536
    out = pl.pallas_call(
        _ada_kernel,
        grid=(DEPTH, 6 * D_MODEL // nb),
        in_specs=[pl.BlockSpec((r, D_MODEL), lambda l, n: (0, 0)),
                  pl.BlockSpec((None, D_MODEL, nb), lambda l, n: (l, 0, n)),
                  pl.BlockSpec((None, 1, nb), lambda l, n: (l, 0, n))],
        out_specs=pl.BlockSpec((None, r, nb), lambda l, n: (l, 0, n)),
        out_shape=jax.ShapeDtypeStruct((DEPTH, r, 6 * D_MODEL), F32),
        compiler_params=_cparams(("arbitrary", "arbitrary")),
        name="ada_mod",
    )(c_rows, ada_w, ada_b.reshape(DEPTH, 1, 6 * D_MODEL))
    return out.reshape(DEPTH, r, 6, D_MODEL)


def _t5_bucket(rel):
    nb = REL_BUCKETS // 2
    max_exact = nb // 2
    ret = jnp.where(rel > 0, nb, 0)
    n = jnp.abs(rel)
    nf = jnp.maximum(n, 1).astype(jnp.float32)
    large = max_exact + (jnp.log(nf / max_exact) / math.log(REL_MAX_DIST / max_exact) * (nb - max_exact)).astype(jnp.int32)
    large = jnp.minimum(large, nb - 1)
    return ret + jnp.where(n < max_exact, n, large)


def _bias_kernel(rb_ref, bk_ref, o_ref):
    h = pl.program_id(0)
    bk = bk_ref[...]
    acc = jnp.zeros(bk.shape, F32)
    bmax = rb_ref[0, h]
    for b in range(REL_BUCKETS):
        acc = acc + jnp.where(bk == b, rb_ref[b, h], 0.0)
        bmax = jnp.maximum(bmax, rb_ref[b, h])
    o_ref[...] = (acc - bmax) * LOG2E


def _bias_call(rel_bias, tq):
    r = jnp.arange(tq, dtype=jnp.int32)
    rel = (jnp.arange(-1, 2, dtype=jnp.int32) * tq)[:, None, None] + r[None, None, :] - r[None, :, None]
    buckets = _t5_bucket(rel)
    return pl.pallas_call(
        _bias_kernel,
        grid=(A_HEADS, 3),
        in_specs=[pl.BlockSpec(memory_space=pltpu.SMEM),
                  pl.BlockSpec((None, tq, tq), lambda h, o: (o, 0, 0))],
        out_specs=pl.BlockSpec((None, None, tq, tq), lambda h, o: (h, o, 0, 0)),
        out_shape=jax.ShapeDtypeStruct((A_HEADS, 3, tq, tq), F32),
        compiler_params=_cparams(("arbitrary", "arbitrary")),
        name="rel_bias_tiles",
    )(rel_bias, buckets)


def _rope_tables(s):
    d = C_QK_DIM
    inv = ROPE_BASE ** (-jnp.arange(0, d, 2, dtype=jnp.float32) / d)
    ang = jnp.arange(s, dtype=jnp.float32)[:, None] * inv[None, :]
    return jnp.cos(ang), jnp.sin(ang)


def _half_rms(z, g):
    lo_lane = lax.broadcasted_iota(jnp.int32, (1, LANES), 1) < A_QK_DIM
    z2 = z * z
    tot = jnp.sum(z2, axis=-1, keepdims=True)
    lo = jnp.sum(jnp.where(lo_lane, z2, 0.0), axis=-1, keepdims=True)
    ms = jnp.where(lo_lane, lo, tot - lo) * (1.0 / A_QK_DIM)
    return z * lax.rsqrt(ms + EPS) * g


def _inproj_ab_kernel(x_ref, mod_ref, g_ref, w_ref, qg_ref, kg_ref, gb_ref,
                      qa_ref, ka_ref, va_ref, qk_ref, vb_ref, ob_ref, gt_ref):
    x = x_ref[...]
    h = _rms(x, g_ref[...]) * (1.0 + mod_ref[1:2, :]) + mod_ref[0:1, :]
    hb = h.astype(MXU_DT)
    qscale = (A_QK_DIM ** -0.5) * LOG2E
    for hd in range(A_HEADS):
        c0 = hd * LANES
        q = jnp.dot(hb, w_ref[:, c0:c0 + LANES], preferred_element_type=F32)
        qa_ref[:, c0:c0 + LANES] = (_half_rms(q, qg_ref[...]) * qscale).astype(qa_ref.dtype)
        k = jnp.dot(hb, w_ref[:, A_QK_WIDTH + c0:A_QK_WIDTH + c0 + LANES], preferred_element_type=F32)
        ka_ref[:, c0:c0 + LANES] = _half_rms(k, kg_ref[...]).astype(ka_ref.dtype)
    o = 2 * A_QK_WIDTH
    va_ref[...] = jnp.dot(hb, w_ref[:, o:o + A_WIDTH], preferred_element_type=F32).astype(va_ref.dtype)
    o += A_WIDTH
    qk_ref[...] = jnp.dot(hb, w_ref[:, o:o + 2 * B_WIDTH], preferred_element_type=F32)
    o += 2 * B_WIDTH
    vb_ref[...] = jnp.dot(hb, w_ref[:, o:o + B_WIDTH], preferred_element_type=F32).astype(vb_ref.dtype)
    o += B_WIDTH
    ob_ref[...] = jnp.dot(hb, w_ref[:, o:o + B_WIDTH], preferred_element_type=F32)
    o += B_WIDTH
    gt_ref[...] = jnp.dot(hb, w_ref[:, o:o + LANES], preferred_element_type=F32) + gb_ref[...]


def _inproj_ab_call(geo, tm, x, mod_l, g, w_pad, qg, kg, gate_b):
    t = geo.t
    per_seg = geo.seg // tm
    row = lambda i: (i, 0)
    const = lambda i: (0, 0)
    widths = (A_QK_WIDTH, A_QK_WIDTH, A_WIDTH, 2 * B_WIDTH, B_WIDTH, B_WIDTH, LANES)
    dtypes = (ACT_DT, ACT_DT, ACT_DT, F32, ACT_DT, F32, F32)
    return pl.pallas_call(
        _inproj_ab_kernel,
        grid=(t // tm,),
        in_specs=[pl.BlockSpec((tm, D_MODEL), row),
                  pl.BlockSpec((None, 6, D_MODEL), lambda i: (i // per_seg, 0, 0)),
                  pl.BlockSpec((1, D_MODEL), const),
                  pl.BlockSpec((D_MODEL, AB_IN_PAD), const),
                  pl.BlockSpec((1, LANES), const),
                  pl.BlockSpec((1, LANES), const),
                  pl.BlockSpec((1, LANES), const)],
        out_specs=[pl.BlockSpec((tm, w), row) for w in widths],
        out_shape=[jax.ShapeDtypeStruct((t, w), d) for w, d in zip(widths, dtypes)],
        compiler_params=_cparams(("parallel",)),
        name="inproj_ab",
    )(x, mod_l, g, w_pad, qg, kg, gate_b)


def _conv_kernel(geo, tc, x_ref, prev_ref, next_ref, w_ref, b_ref, q_ref, k_ref):
    i = pl.program_id(0)
    row0 = i * tc
    x = x_ref[...]
    first = geo.seq_start(row0)
    last = geo.seq_start(row0 + tc) | (row0 + tc == geo.t)
    prev_row = jnp.where(first, 0.0, prev_ref[7:8, :])
    next_row = jnp.where(last, 0.0, next_ref[0:1, :])
    ridx = lax.broadcasted_iota(jnp.int32, (tc, 1), 0)
    x_prev = jnp.where(ridx == 0, prev_row, pltpu.roll(x, 1, axis=0))
    x_next = jnp.where(ridx == tc - 1, next_row, pltpu.roll(x, tc - 1, axis=0))
    y = x_prev * w_ref[0:1, :] + x * w_ref[1:2, :] + x_next * w_ref[2:3, :] + b_ref[...]
    y = _silu(y)
    q_ref[...] = y[:, :B_WIDTH].astype(q_ref.dtype)
    k_ref[...] = (y[:, B_WIDTH:] * (B_DIM ** -0.5)).astype(k_ref.dtype)


def _conv_call(geo, tc, qk, w, b):
    t = geo.t
    r8 = tc // 8
    nb8 = t // 8
    return pl.pallas_call(
        functools.partial(_conv_kernel, geo, tc),
        grid=(t // tc,),
        in_specs=[pl.BlockSpec((tc, 2 * B_WIDTH), lambda i: (i, 0)),
                  pl.BlockSpec((8, 2 * B_WIDTH), lambda i: (jnp.maximum(i * r8 - 1, 0), 0)),
                  pl.BlockSpec((8, 2 * B_WIDTH), lambda i: (jnp.minimum((i + 1) * r8, nb8 - 1), 0)),
                  pl.BlockSpec((3, 2 * B_WIDTH), lambda i: (0, 0)),
                  pl.BlockSpec((1, 2 * B_WIDTH), lambda i: (0, 0))],
        out_specs=[pl.BlockSpec((tc, B_WIDTH), lambda i: (i, 0))] * 2,
        out_shape=[jax.ShapeDtypeStruct((t, B_WIDTH), ACT_DT)] * 2,
        compiler_params=_cparams(("parallel",)),
        name="mlstm_conv",
    )(qk, qk, qk, w, b)


def _attn_finish(lam_init, acc0, l0, acc1, l1, dl_ref, ng_ref, o_ref):
    dl = dl_ref[...]
    lam = (jnp.exp(jnp.sum(dl[0:1] * dl[1:2], axis=-1, keepdims=True))
           - jnp.exp(jnp.sum(dl[2:3] * dl[3:4], axis=-1, keepdims=True)) + lam_init)
    out = acc0 / l0 - lam * (acc1 / l1)
    o_ref[...] = (_rms(out, ng_ref[...]) * (1.0 - lam_init)).astype(o_ref.dtype)


def _attn_kernel(lam_init, nk, q_ref, k_ref, v_ref, bias_ref, sc_ref, dl_ref, ng_ref, o_ref,
                 m_sc, l_sc, acc_sc):
    h = pl.program_id(1)
    i = pl.program_id(2)
    j = pl.program_id(3)

    @pl.when(j == 0)
    def _():
        m_sc[...] = jnp.full(m_sc.shape, NEG_BIG, F32)
        l_sc[...] = jnp.zeros(l_sc.shape, F32)
        acc_sc[...] = jnp.zeros(acc_sc.shape, F32)

    q = q_ref[...]
    lo_lane = lax.broadcasted_iota(jnp.int32, (1, LANES), 1) < A_QK_DIM
    qsub = (jnp.where(lo_lane, q, jnp.zeros_like(q)), jnp.where(lo_lane, jnp.zeros_like(q), q))
    k = k_ref[...]
    v = v_ref[...]

    def step(near):
        if near:
            shift = 0.0
        else:
            shift = jnp.where(j < i, sc_ref[h, 0], sc_ref[h, 1])
        for sub in range(2):
            s = _mm_nt(qsub[sub], k)
            if near:
                s = s + bias_ref[...]
            m_old = m_sc[sub]
            m_new = jnp.maximum(m_old, jnp.max(s, axis=-1, keepdims=True) + shift)
            p = jnp.exp2(s - (m_new - shift))
            alpha = jnp.exp2(m_old - m_new)
            l_sc[sub] = alpha * l_sc[sub] + jnp.sum(p, axis=-1, keepdims=True)
            acc_sc[sub] = alpha * acc_sc[sub] + _mm(p, v)
            m_sc[sub] = m_new

    near = jnp.abs(j - i) <= 1
    pl.when(near)(lambda: step(True))
    pl.when(jnp.logical_not(near))(lambda: step(False))

    @pl.when(j == nk - 1)
    def _():
        _attn_finish(lam_init, acc_sc[0], l_sc[0], acc_sc[1], l_sc[1], dl_ref, ng_ref, o_ref)


def _attn_bounded_kernel(lam_init, nk, q_ref, k_ref, v_ref, bias_ref, sc_ref, dl_ref, ng_ref, o_ref, acc_sc):
    h = pl.program_id(1)
    i = pl.program_id(2)
    j = pl.program_id(3)

    @pl.when(j == 0)
    def _():
        acc_sc[...] = jnp.zeros(acc_sc.shape, F32)

    @pl.when(j == jnp.maximum(i - 1, 0))
    def _():
        acc_sc[...] = acc_sc[...] * sc_ref[h, 2]

    @pl.when(j == i + 2)
    def _():
        acc_sc[...] = acc_sc[...] * sc_ref[h, 3]

    q = q_ref[...]
    k = k_ref[...]
    v = v_ref[...]
    lo = (lax.broadcasted_iota(jnp.int32, q.shape, 1) < A_QK_DIM).astype(F32).astype(q.dtype)
    qsub = (q * lo, q * (1 - lo))
    ones_col = (lax.broadcasted_iota(jnp.int32, v.shape, 1) == 0).astype(v.dtype)
    v_aug = jnp.concatenate([v, ones_col], axis=1)

    def step(near):
        for sub in range(2):
            s = _mm_nt(qsub[sub], k)
            if near:
                s = s + bias_ref[...]
            acc_sc[sub] += _mm(jnp.exp2(s), v_aug)

    near = jnp.abs(j - i) <= 1
    pl.when(near)(lambda: step(True))
    pl.when(jnp.logical_not(near))(lambda: step(False))

    @pl.when(j == nk - 1)
    def _():
        a0 = acc_sc[0]
        a1 = acc_sc[1]
        _attn_finish(lam_init, a0[:, :A_V_DIM], a0[:, A_V_DIM:A_V_DIM + 1], a1[:, :A_V_DIM],
                     a1[:, A_V_DIM:A_V_DIM + 1], dl_ref, ng_ref, o_ref)


def _attn_scalars(bias, tq, q_gain, k_gain):
    far_l = bias[:, 0, tq - 1, 0]
    far_r = bias[:, 2, 0, tq - 1]
    sc = jnp.stack([far_l, far_r, jnp.exp2(far_l), jnp.exp2(-far_r)], axis=-1)
    bound = A_QK_DIM * jnp.max(jnp.abs(q_gain)) * jnp.max(jnp.abs(k_gain)) * (A_QK_DIM ** -0.5) * LOG2E * 1.02
    spread = -jnp.min(bias)
    ok = bound + 2.0 * spread <= 80.0
    return sc, ok


def _attn_call(geo, tq, lam_init, bounded, qa, ka, va, bias, sc, dl, ng):
    outs = []
    for (nb, s, row_off) in ((geo.bp, geo.sp, 0), (geo.bs, geo.ss, geo.tp)):
        nq = s // tq
        off = row_off // tq
        qmap = lambda b, h, i, j, off=off, nq=nq: (off + b * nq + i, h)
        kmap = lambda b, h, i, j, off=off, nq=nq: (off + b * nq + j, h)
        omap = lambda b, h, i, j, nq=nq: (b * nq + i, h)
        if bounded:
            body = functools.partial(_attn_bounded_kernel, lam_init, nq)
            scratch = [pltpu.VMEM((2, tq, 2 * LANES), F32)]
        else:
            body = functools.partial(_attn_kernel, lam_init, nq)
            scratch = [pltpu.VMEM((2, tq, 1), F32), pltpu.VMEM((2, tq, 1), F32), pltpu.VMEM((2, tq, LANES), F32)]
        outs.append(pl.pallas_call(
            body,
            grid=(nb, A_HEADS, nq, nq),
            in_specs=[pl.BlockSpec((tq, LANES), qmap),
                      pl.BlockSpec((tq, LANES), kmap),
                      pl.BlockSpec((tq, LANES), kmap),
                      pl.BlockSpec((None, None, tq, tq), lambda b, h, i, j: (h, jnp.clip(j - i + 1, 0, 2), 0, 0)),
                      pl.BlockSpec(memory_space=pltpu.SMEM),
                      pl.BlockSpec((4, A_QK_DIM), lambda b, h, i, j: (0, 0)),
                      pl.BlockSpec((1, LANES), lambda b, h, i, j: (0, 0))],
            out_specs=pl.BlockSpec((tq, LANES), omap),
            out_shape=jax.ShapeDtypeStruct((nb * s, A_WIDTH), ACT_DT),
            scratch_shapes=scratch,
            compiler_params=_cparams(("parallel", "parallel", "parallel", "arbitrary")),
            name="diff_attn_bounded" if bounded else "diff_attn",
        )(qa, ka, va, bias, sc, dl, ng))
    return jnp.concatenate(outs, axis=0)


def _tri(lower):
    r = lax.broadcasted_iota(jnp.int32, (CHUNK, CHUNK), 0)
    c = lax.broadcasted_iota(jnp.int32, (CHUNK, CHUNK), 1)
    return (c <= r) if lower else (c >= r)


def _mlstm_kernel(geo, tb, reverse, *refs):
    if reverse:
        q_ref, k_ref, v_ref, g_ref, hf_ref, ob_ref, ng_ref, o_ref, c_sc, m_sc = refs
    else:
        q_ref, k_ref, v_ref, g_ref, o_ref, c_sc, m_sc = refs
    step = pl.program_id(0)
    nblk = geo.t // tb
    blk = (nblk - 1 - step) if reverse else step
    row0 = blk * tb
    if reverse:
        fresh = geo.seq_start(row0 + tb) | (row0 + tb == geo.t)
    else:
        fresh = geo.seq_start(row0)

    @pl.when(fresh)
    def _():
        c_sc[...] = jnp.zeros(c_sc.shape, F32)
        m_sc[...] = jnp.zeros(m_sc.shape, F32)

    mask = _tri(not reverse)
    cum_l = mask.astype(F32)
    cum_r = _tri(reverse).astype(F32)
    ones_col = (lax.broadcasted_iota(jnp.int32, (CHUNK, LANES), 1) == 0).astype(MXU_DT)
    nch = tb // CHUNK

    def chunk(ci, carry):
        c_idx = (nch - 1 - ci) if reverse else ci
        r0 = pl.multiple_of(c_idx * CHUNK, CHUNK)
        g = g_ref[pl.ds(r0, CHUNK), :]
        g_t = g.T
        b_col = _mm_hi(cum_l, _log_sigmoid(g))
        b_row = _mm_hi(_log_sigmoid(g_t), cum_r)
        for hd in range(B_HEADS):
            ci_col = hd * N_GATES + (2 if reverse else 0)
            cf_col = ci_col + 1
            lanes = slice(hd * B_DIM, (hd + 1) * B_DIM)
            q = q_ref[pl.ds(r0, CHUNK), lanes]
            k = k_ref[pl.ds(r0, CHUNK), lanes]
            v = v_ref[pl.ds(r0, CHUNK), lanes]
            bc = b_col[:, cf_col:cf_col + 1]
            br = b_row[cf_col:cf_col + 1, :]
            ic = g[:, ci_col:ci_col + 1]
            ir = g_t[ci_col:ci_col + 1, :]
            m_prev = m_sc[hd]
            log_d = jnp.where(mask, bc - br + ir, NEG_BIG)
            m_inter = bc + m_prev
            m_t = jnp.maximum(jnp.max(log_d, axis=-1, keepdims=True), m_inter)
            s = _mm_nt(q, k) * jnp.exp(log_d - m_t)
            inter = jnp.exp(m_inter - m_t)
            c_aug = c_sc[hd]
            qc = _mm(q, c_aug)
            num = _mm(s, v) + inter * qc[:, :B_DIM]
            den = jnp.sum(s, axis=-1, keepdims=True) + inter * qc[:, B_DIM:B_DIM + 1]
            hout = num / jnp.maximum(jnp.abs(den), jnp.exp(-m_t))
            b_last = bc[0:1, :] if reverse else bc[CHUNK - 1:CHUNK, :]
            log_w = b_last - bc + ic
            m_new = jnp.maximum(b_last + m_prev, jnp.max(log_w, axis=0, keepdims=True))
            w = jnp.exp(log_w - m_new)
            decay = jnp.exp(b_last + m_prev - m_new)
            v_aug = jnp.concatenate([v, ones_col], axis=1)
            c_sc[hd] = decay * c_aug + _mm_tn(k.astype(F32) * w, v_aug)
            m_sc[hd] = m_new
            if reverse:
                hsum = hf_ref[pl.ds(r0, CHUNK), lanes] + hout
                y = _rms(hsum, ng_ref[...]) * _sigmoid(ob_ref[pl.ds(r0, CHUNK), lanes])
                o_ref[pl.ds(r0, CHUNK), lanes] = y.astype(o_ref.dtype)
            else:
                o_ref[pl.ds(r0, CHUNK), lanes] = hout
        return carry

    lax.fori_loop(0, nch, chunk, 0)


def _mlstm_call(geo, tb, qb, kb, vb, gates, ob, ng):
    t = geo.t
    nblk = t // tb
    fmap = lambda s: (s, 0)
    rmap = lambda s: (nblk - 1 - s, 0)
    scratch = [pltpu.VMEM((B_HEADS, B_DIM, 2 * B_DIM), F32), pltpu.VMEM((B_HEADS, 1, 1), F32)]
    wide = lambda m: pl.BlockSpec((tb, B_WIDTH), m)
    hf = pl.pallas_call(
        functools.partial(_mlstm_kernel, geo, tb, False),
        grid=(nblk,),
        in_specs=[wide(fmap), wide(fmap), wide(fmap), pl.BlockSpec((tb, LANES), fmap)],
        out_specs=wide(fmap),
        out_shape=jax.ShapeDtypeStruct((t, B_WIDTH), F32),
        scratch_shapes=scratch,
        compiler_params=_cparams(("arbitrary",)),
        name="mlstm_fwd",
    )(qb, kb, vb, gates)
    return pl.pallas_call(
        functools.partial(_mlstm_kernel, geo, tb, True),
        grid=(nblk,),
        in_specs=[wide(rmap), wide(rmap), wide(rmap), pl.BlockSpec((tb, LANES), rmap), wide(rmap), wide(rmap),
                  pl.BlockSpec((1, B_DIM), lambda s: (0, 0))],
        out_specs=wide(rmap),
        out_shape=jax.ShapeDtypeStruct((t, B_WIDTH), ACT_DT),
        scratch_shapes=scratch,
        compiler_params=_cparams(("arbitrary",)),
        name="mlstm_bwd",
    )(qb, kb, vb, gates, hf, ob, ng)


def _inproj_c_kernel(x_ref, mod_ref, g_ref, w_ref, cos_ref, sin_ref, q_ref, k_ref, v_ref, gt_ref, h_sc):
    j = pl.program_id(1)

    @pl.when(j == 0)
    def _():
        x = x_ref[...]
        h = _rms(x, g_ref[...]) * (1.0 + mod_ref[1:2, :]) + mod_ref[0:1, :]
        h_sc[...] = h.astype(h_sc.dtype)

    y = jnp.dot(h_sc[...], w_ref[...], preferred_element_type=F32)

    def rope(scale):
        cos = cos_ref[...]
        sin = sin_ref[...]
        half = C_QK_DIM // 2
        parts = []
        for hd in range(C_HEADS):
            x1 = y[:, hd * C_QK_DIM:hd * C_QK_DIM + half]
            x2 = y[:, hd * C_QK_DIM + half:(hd + 1) * C_QK_DIM]
            parts.append((x1 * cos - x2 * sin) * scale)
            parts.append((x1 * sin + x2 * cos) * scale)
        return jnp.concatenate(parts, axis=1)

    @pl.when(j == 0)
    def _():
        q_ref[...] = rope(1.0).astype(q_ref.dtype)

    @pl.when(j == 1)
    def _():
        k_ref[...] = rope(C_QK_DIM ** -0.5).astype(k_ref.dtype)

    @pl.when((j == 2) | (j == 3))
    def _():
        v_ref[...] = y.astype(v_ref.dtype)

    @pl.when(j >= 4)
    def _():
        gt_ref[...] = _silu(y).astype(gt_ref.dtype)


def _inproj_c_call(geo, tm, x, mod_l, g, w, cos, sin):
    t = geo.t
    per_seg = geo.seg // tm
    nw = D_MODEL
    return pl.pallas_call(
        _inproj_c_kernel,
        grid=(t // tm, C_IN // nw),
        in_specs=[pl.BlockSpec((tm, D_MODEL), lambda i, j: (i, 0)),
                  pl.BlockSpec((None, 6, D_MODEL), lambda i, j: (i // per_seg, 0, 0)),
                  pl.BlockSpec((1, D_MODEL), lambda i, j: (0, 0)),
                  pl.BlockSpec((D_MODEL, nw), lambda i, j: (0, j)),
                  pl.BlockSpec((tm, C_QK_DIM // 2), lambda i, j: (geo.pos_block(i, tm), 0)),
                  pl.BlockSpec((tm, C_QK_DIM // 2), lambda i, j: (geo.pos_block(i, tm), 0))],
        out_specs=[pl.BlockSpec((tm, nw), lambda i, j: (i, 0)),
                   pl.BlockSpec((tm, nw), lambda i, j: (i, 0)),
                   pl.BlockSpec((tm, nw), lambda i, j: (i, jnp.clip(j - 2, 0, 1))),
                   pl.BlockSpec((tm, nw), lambda i, j: (i, jnp.clip(j - 4, 0, 1)))],
        out_shape=[jax.ShapeDtypeStruct((t, C_QK_WIDTH), ACT_DT),
                   jax.ShapeDtypeStruct((t, C_QK_WIDTH), ACT_DT),
                   jax.ShapeDtypeStruct((t, C_V_WIDTH), ACT_DT),
                   jax.ShapeDtypeStruct((t, C_V_WIDTH), ACT_DT)],
        scratch_shapes=[pltpu.VMEM((tm, D_MODEL), MXU_DT)],
        compiler_params=_cparams(("parallel", "arbitrary")),
        name="inproj_c",
    )(x, mod_l, g, w, cos, sin)


def _ret_kernel(geo, tb, reverse, *refs):
    if reverse:
        q_ref, k_ref, v_ref, dlg_ref, yf_ref, gt_ref, ng_ref, o_ref, r_sc, intra_sc, vec_sc = refs
    else:
        q_ref, k_ref, v_ref, dlg_ref, o_ref, r_sc, intra_sc, vec_sc = refs
    step = pl.program_id(0)
    nblk = geo.t // tb
    blk = (nblk - 1 - step) if reverse else step
    row0 = blk * tb
    if reverse:
        fresh = geo.seq_start(row0 + tb) | (row0 + tb == geo.t)
    else:
        fresh = geo.seq_start(row0)

    @pl.when(fresh)
    def _():
        r_sc[...] = jnp.zeros(r_sc.shape, F32)

    @pl.when(step == 0)
    def _():
        lg_all = _log_sigmoid(dlg_ref[...])
        ti = lax.broadcasted_iota(jnp.int32, (RET_CHUNK, RET_CHUNK), 0)
        si = lax.broadcasted_iota(jnp.int32, (RET_CHUNK, RET_CHUNK), 1)
        dist = ((si - ti) if reverse else (ti - si)).astype(F32)
        pos = lax.broadcasted_iota(jnp.int32, (RET_CHUNK, LANES), 0).astype(F32)
        upos = (RET_CHUNK - 1.0 - pos) if reverse else pos
        lane = lax.broadcasted_iota(jnp.int32, (RET_CHUNK, LANES), 1)
        d = 1 if reverse else 0
        for hd in range(C_HEADS):
            lg = lg_all[d:d + 1, hd:hd + 1]
            intra_sc[hd] = jnp.where(dist >= 0, jnp.exp(jnp.maximum(dist, 0.0) * lg), 0.0)
            vec_sc[hd] = jnp.where(lane == 0, jnp.exp((upos + 1.0) * lg),
                                   jnp.where(lane == 1, jnp.exp((RET_CHUNK - 1.0 - upos) * lg),
                                             jnp.exp(RET_CHUNK * lg)))

    nch = tb // RET_CHUNK

    def chunk(ci, carry):
        c_idx = (nch - 1 - ci) if reverse else ci
        r0 = pl.multiple_of(c_idx * RET_CHUNK, RET_CHUNK)
        for hd in range(C_HEADS):
            vec = vec_sc[hd]
            q_scale, k_scale, c_decay = vec[:, 0:1], vec[:, 1:2], vec[0:1, 2:3]
            ql = slice(hd * C_QK_DIM, (hd + 1) * C_QK_DIM)
            vl = slice(hd * C_V_DIM, (hd + 1) * C_V_DIM)
            q = q_ref[pl.ds(r0, RET_CHUNK), ql]
            k = k_ref[pl.ds(r0, RET_CHUNK), ql]
            v = v_ref[pl.ds(r0, RET_CHUNK), vl]
            r_old = r_sc[hd]
            s = _mm_nt(q, k) * intra_sc[hd]
            y = _mm(s, v) + q_scale * _mm(q, r_old)
            r_sc[hd] = c_decay * r_old + _mm_tn(k.astype(F32) * k_scale, v)
            if reverse:
                ysum = yf_ref[pl.ds(r0, RET_CHUNK), vl] + y
                out = _rms(ysum, ng_ref[...]) * gt_ref[pl.ds(r0, RET_CHUNK), vl].astype(F32)
                o_ref[pl.ds(r0, RET_CHUNK), vl] = out.astype(o_ref.dtype)
            else:
                o_ref[pl.ds(r0, RET_CHUNK), vl] = y
        return carry

    lax.fori_loop(0, nch, chunk, 0)


def _ret_call(geo, tb, q, k, v, decay_logit, gt, ng):
    t = geo.t
    nblk = t // tb
    fmap = lambda s: (s, 0)
    rmap = lambda s: (nblk - 1 - s, 0)
    scratch = [pltpu.VMEM((C_HEADS, C_QK_DIM, C_V_DIM), F32), pltpu.VMEM((C_HEADS, RET_CHUNK, RET_CHUNK), F32),
               pltpu.VMEM((C_HEADS, RET_CHUNK, LANES), F32)]
    qk = lambda m: pl.BlockSpec((tb, C_QK_WIDTH), m)
    vv = lambda m: pl.BlockSpec((tb, C_V_WIDTH), m)
    dspec = pl.BlockSpec((2, C_HEADS), lambda s: (0, 0))
    yf = pl.pallas_call(
        functools.partial(_ret_kernel, geo, tb, False),
        grid=(nblk,),
        in_specs=[qk(fmap), qk(fmap), vv(fmap), dspec],
        out_specs=vv(fmap),
        out_shape=jax.ShapeDtypeStruct((t, C_V_WIDTH), F32),
        scratch_shapes=scratch,
        compiler_params=_cparams(("arbitrary",)),
        name="ret_fwd",
    )(q, k, v, decay_logit)
    return pl.pallas_call(
        functools.partial(_ret_kernel, geo, tb, True),
        grid=(nblk,),
        in_specs=[qk(rmap), qk(rmap), vv(rmap), dspec, vv(rmap), vv(rmap),
                  pl.BlockSpec((1, C_V_DIM), lambda s: (0, 0))],
        out_specs=vv(rmap),
        out_shape=jax.ShapeDtypeStruct((t, C_V_WIDTH), ACT_DT),
        scratch_shapes=scratch,
        compiler_params=_cparams(("arbitrary",)),
        name="ret_bwd",
    )(q, k, v, decay_logit, yf, gt, ng)


def _route(probs):
    p = [probs[e:e + 1, :] for e in range(N_EXPERTS)]
    scores = []
    for g in range(N_GROUPS):
        a, b, c, d = p[EPG * g:EPG * g + EPG]
        hi1, lo1 = jnp.maximum(a, b), jnp.minimum(a, b)
        hi2, lo2 = jnp.maximum(c, d), jnp.minimum(c, d)
        scores.append(jnp.maximum(hi1, hi2) + jnp.maximum(jnp.minimum(hi1, hi2), jnp.maximum(lo1, lo2)))
    g_sel = jnp.zeros(scores[0].shape, jnp.int32)
    best = scores[0]
    for g in range(1, N_GROUPS):
        better = scores[g] > best
        g_sel = jnp.where(better, g, g_sel)
        best = jnp.where(better, scores[g], best)
    vals = []
    for kk in range(EPG):
        v = p[kk]
        for g in range(1, N_GROUPS):
            v = jnp.where(g_sel == g, p[EPG * g + kk], v)
        vals.append(v)

    def argmax4(xs):
        idx = jnp.zeros(xs[0].shape, jnp.int32)
        top = xs[0]
        for kk in range(1, EPG):
            better = xs[kk] > top
            idx = jnp.where(better, kk, idx)
            top = jnp.where(better, xs[kk], top)
        return idx, top

    i1, v1 = argmax4(vals)
    i2, v2 = argmax4([jnp.where(i1 == kk, -1.0, vals[kk]) for kk in range(EPG)])
    tot = v1 + v2
    w1, w2 = v1 / tot, v2 / tot
    e1 = g_sel * EPG + i1
    e2 = g_sel * EPG + i2
    eidx = lax.broadcasted_iota(jnp.int32, probs.shape, 0)
    lo, hi = jnp.minimum(i1, i2), jnp.maximum(i1, i2)
    pair = jnp.where(lo == 0, hi - 1, jnp.where(lo == 1, hi + 1, N_PAIRS - 1))
    return jnp.where(eidx == e1, w1, 0.0) + jnp.where(eidx == e2, w2, 0.0), g_sel * N_PAIRS + pair


def _outproj_kernel(nparts, *refs):
    y_refs = refs[:nparts]
    w_refs = refs[nparts:2 * nparts]
    x_ref, mod_ref, g_ref, rw_ref, rb_ref, xo_ref, hx_ref, bkt_ref = refs[2 * nparts:]
    m = jnp.dot(y_refs[0][...], w_refs[0][...], preferred_element_type=F32)
    for p in range(1, nparts):
        m = m + jnp.dot(y_refs[p][...], w_refs[p][...], preferred_element_type=F32)
    x = x_ref[...] + mod_ref[2:3, :] * m
    xo_ref[...] = x
    h = _rms(x, g_ref[...]) * (1.0 + mod_ref[4:5, :]) + mod_ref[3:4, :]
    hx_ref[:, :D_MODEL] = h
    logits = lax.dot_general(rw_ref[...], h, (((1,), (1,)), ((), ())), precision=HI,
                             preferred_element_type=F32) + rb_ref[...]
    z = jnp.exp(logits - jnp.max(logits, axis=0, keepdims=True))
    probs = z / jnp.sum(z, axis=0, keepdims=True)
    cmb, bucket = _route(probs)
    pad = jnp.zeros((LANES - N_EXPERTS, cmb.shape[1]), F32)
    hx_ref[:, D_MODEL:] = jnp.concatenate([cmb, pad], axis=0).T
    bkt_ref[...] = bucket


def _outproj_call(geo, tm, ys, ws, x, mod_l, g, rw_t, rb):
    t = geo.t
    per_seg = geo.seg // tm
    row = lambda i: (i, 0)
    const = lambda i: (0, 0)
    n = len(ys)
    return pl.pallas_call(
        functools.partial(_outproj_kernel, n),
        grid=(t // tm,),
        in_specs=([pl.BlockSpec((tm, y.shape[1]), row) for y in ys]
                  + [pl.BlockSpec(w.shape, const) for w in ws]
                  + [pl.BlockSpec((tm, D_MODEL), row),
                     pl.BlockSpec((None, 6, D_MODEL), lambda i: (i // per_seg, 0, 0)),
                     pl.BlockSpec((1, D_MODEL), const),
                     pl.BlockSpec((N_EXPERTS, D_MODEL), const),
                     pl.BlockSpec((N_EXPERTS, 1), const)]),
        out_specs=[pl.BlockSpec((tm, D_MODEL), row), pl.BlockSpec((tm, HX_W), row),
                   pl.BlockSpec((None, 1, tm), lambda i: (i, 0, 0))],
        out_shape=[jax.ShapeDtypeStruct((t, D_MODEL), F32), jax.ShapeDtypeStruct((t, HX_W), F32),
                   jax.ShapeDtypeStruct((t // tm, 1, tm), jnp.int32)],
        compiler_params=_cparams(("parallel",)),
        name="outproj_router",
    )(*ys, *ws, x, mod_l, g, rw_t, rb)


def _plan_kernel(tm, bkt_ref, pos_ref, te_ref, nv_ref):
    nblk, _, blk = bkt_ref.shape
    nrow = 32
    bid = lax.broadcasted_iota(jnp.int32, (nrow, blk), 0)
    r = lax.broadcasted_iota(jnp.int32, (blk, blk), 0)
    c = lax.broadcasted_iota(jnp.int32, (blk, blk), 1)
    prefix = (r <= c).astype(MXU_DT)

    def count(b, acc):
        return acc + jnp.sum((bkt_ref[b] == bid).astype(F32), axis=1, keepdims=True)

    counts = lax.fori_loop(0, nblk, count, jnp.zeros((nrow, 1), F32))
    padded = jnp.floor((counts + (tm - 1.0)) / tm) * tm
    rows = lax.broadcasted_iota(jnp.int32, (nrow, 1), 0)
    offs = jnp.zeros((nrow, 1), F32)
    ends = []
    run = jnp.zeros((1, 1), F32)
    for b in range(N_BUCKETS):
        offs = jnp.where(rows == b, run, offs)
        run = run + padded[b:b + 1, :]
        ends.append(run)

    def place(b, carry):
        oh = (bkt_ref[b] == bid).astype(F32)
        pre = jnp.dot(oh.astype(MXU_DT), prefix, preferred_element_type=F32)
        pos = jnp.sum(oh * (offs + carry + pre - 1.0), axis=0, keepdims=True)
        pos_ref[b] = pos.astype(jnp.int32)
        return carry + pre[:, blk - 1:blk]

    lax.fori_loop(0, nblk, place, jnp.zeros((nrow, 1), F32))
    start = lax.broadcasted_iota(jnp.int32, (1, te_ref.shape[1]), 1).astype(F32) * tm
    tb = jnp.zeros(start.shape, F32)
    for b in range(N_BUCKETS - 1):
        tb = tb + (ends[b] <= start).astype(F32)
    grp = jnp.floor((tb + 0.5) / N_PAIRS)
    pair = tb - grp * N_PAIRS
    lo = (pair >= 3).astype(F32) + (pair >= 5).astype(F32)
    hi = jnp.where(pair == 0, 1.0, jnp.where((pair == 1) | (pair == 3), 2.0, 3.0))
    te_ref[0:1, :] = (grp * EPG + lo).astype(jnp.int32)
    te_ref[1:2, :] = (grp * EPG + hi).astype(jnp.int32)
    nv_ref[...] = jnp.broadcast_to(run / tm, nv_ref.shape).astype(jnp.int32)


def _plan_call(tm, bkt):
    nblk, _, blk = bkt.shape
    nt = nblk * blk // tm + N_BUCKETS
    ntp = -(-nt // LANES) * LANES
    pos, te, nv = pl.pallas_call(
        functools.partial(_plan_kernel, tm),
        out_shape=[jax.ShapeDtypeStruct(bkt.shape, jnp.int32), jax.ShapeDtypeStruct((2, ntp), jnp.int32),
                   jax.ShapeDtypeStruct((1, LANES), jnp.int32)],
        compiler_params=pltpu.CompilerParams(vmem_limit_bytes=VMEM_LIMIT),
        name="moe_plan",
    )(bkt)
    return pos, te[0, :nt], te[1, :nt], nv[0, :1]


def _dispatch_kernel(tb, pos_ref, hx_ref, init_ref, xs_ref, sem):
    del init_ref

    def issue(r, carry):
        pltpu.make_async_copy(hx_ref.at[pl.ds(r, 1)], xs_ref.at[pl.ds(pos_ref[0, r], 1)], sem).start()
        return carry

    lax.fori_loop(0, tb, issue, 0, unroll=8)
    pltpu.make_async_copy(hx_ref, xs_ref.at[pl.ds(0, tb)], sem).wait()


def _dispatch_call(tb, pos, hx, xs_init):
    nsteps = hx.shape[0] // tb
    return pl.pallas_call(
        functools.partial(_dispatch_kernel, tb),
        grid=(nsteps,),
        in_specs=[pl.BlockSpec((None, 1, tb), lambda i: (i, 0, 0), memory_space=pltpu.SMEM),
                  pl.BlockSpec((tb, HX_W), lambda i: (i, 0)),
                  pl.BlockSpec(memory_space=pl.ANY)],
        out_specs=pl.BlockSpec(memory_space=pl.ANY),
        out_shape=jax.ShapeDtypeStruct(xs_init.shape, xs_init.dtype),
        scratch_shapes=[pltpu.SemaphoreType.DMA(())],
        input_output_aliases={2: 0},
        compiler_params=pltpu.CompilerParams(dimension_semantics=("arbitrary",), disable_bounds_checks=True,
                                             has_side_effects=True),
        name="moe_dispatch",
    )(pos, hx, xs_init)


def _moe_kernel(te0_ref, te1_ref, nv_ref, xs_ref, w1a_ref, w3a_ref, w2a_ref, w1b_ref, w3b_ref, w2b_ref, o_ref):
    n = pl.program_id(0)

    @pl.when(n < nv_ref[0])
    def _():
        h = xs_ref[:, :D_MODEL].astype(MXU_DT)
        cmb = xs_ref[:, D_MODEL:]
        lane = lax.broadcasted_iota(jnp.int32, cmb.shape, 1)

        def expert(e_id, w1_ref, w3_ref, w2_ref):
            a = jnp.dot(h, w1_ref[...], preferred_element_type=F32)
            b = jnp.dot(h, w3_ref[...], preferred_element_type=F32)
            c = jnp.sum(jnp.where(lane == e_id, cmb, 0.0), axis=-1, keepdims=True)
            return c * _mm(_silu(a) * b, w2_ref[...])

        o_ref[...] = (expert(te0_ref[n], w1a_ref, w3a_ref, w2a_ref)
                      + expert(te1_ref[n], w1b_ref, w3b_ref, w2b_ref))

    @pl.when(n >= nv_ref[0])
    def _():
        o_ref[...] = jnp.zeros(o_ref.shape, F32)


def _moe_call(tm, layer, te0, te1, nv, xs, w1, w3, w2):
    rows = xs.shape[0]
    nt = rows // tm
    tile = lambda n, te0, te1, nv: (jnp.minimum(n, nv[0] - 1), 0)
    wa = lambda n, te0, te1, nv: (layer, te0[jnp.minimum(n, nv[0] - 1)], 0, 0)
    wb = lambda n, te0, te1, nv: (layer, te1[jnp.minimum(n, nv[0] - 1)], 0, 0)
    up = lambda m: pl.BlockSpec((None, None, D_MODEL, D_FF), m)
    down = lambda m: pl.BlockSpec((None, None, D_FF, D_MODEL), m)
    return pl.pallas_call(
        _moe_kernel,
        grid_spec=pltpu.PrefetchScalarGridSpec(
            num_scalar_prefetch=3,
            grid=(nt,),
            in_specs=[pl.BlockSpec((tm, HX_W), tile), up(wa), up(wa), down(wa), up(wb), up(wb), down(wb)],
            out_specs=pl.BlockSpec((tm, D_MODEL), lambda n, te0, te1, nv: (n, 0))),
        out_shape=jax.ShapeDtypeStruct((rows, D_MODEL), F32),
        compiler_params=_cparams(("arbitrary",)),
        name="moe",
    )(te0, te1, nv, xs, w1, w3, w2, w1, w3, w2)


def _collect_kernel(tb, nsteps, pos_ref, posn_ref, ys_ref, x_ref, mod_ref, o_ref, buf, sem):
    i = pl.program_id(0)
    slot = i % 2

    def issue(p_ref, s):
        def body(r, carry):
            pltpu.make_async_copy(ys_ref.at[pl.ds(p_ref[0, r], 1)], buf.at[s, pl.ds(r, 1)], sem.at[s]).start()
            return carry
        lax.fori_loop(0, tb, body, 0, unroll=8)

    pl.when(i == 0)(lambda: issue(pos_ref, 0))
    pl.when(i + 1 < nsteps)(lambda: issue(posn_ref, 1 - slot))
    pltpu.make_async_copy(ys_ref.at[pl.ds(0, tb)], buf.at[slot], sem.at[slot]).wait()
    o_ref[...] = x_ref[...] + mod_ref[5:6, :] * buf[slot]


def _collect_call(geo, tb, pos, ys, x, mod_l):
    t = geo.t
    nsteps = t // tb
    per_seg = geo.seg // tb
    return pl.pallas_call(
        functools.partial(_collect_kernel, tb, nsteps),
        grid=(nsteps,),
        in_specs=[pl.BlockSpec((None, 1, tb), lambda i: (i, 0, 0), memory_space=pltpu.SMEM),
                  pl.BlockSpec((None, 1, tb), lambda i: (jnp.minimum(i + 1, nsteps - 1), 0, 0),
                               memory_space=pltpu.SMEM),
                  pl.BlockSpec(memory_space=pl.ANY),
                  pl.BlockSpec((tb, D_MODEL), lambda i: (i, 0)),
                  pl.BlockSpec((None, 6, D_MODEL), lambda i: (i // per_seg, 0, 0))],
        out_specs=pl.BlockSpec((tb, D_MODEL), lambda i: (i, 0)),
        out_shape=jax.ShapeDtypeStruct((t, D_MODEL), F32),
        scratch_shapes=[pltpu.VMEM((2, tb, D_MODEL), F32), pltpu.SemaphoreType.DMA((2,))],
        compiler_params=pltpu.CompilerParams(dimension_semantics=("arbitrary",), vmem_limit_bytes=VMEM_LIMIT,
                                             disable_bounds_checks=True),
        name="moe_collect",
    )(pos, pos, ys, x, mod_l)


def _tiles(geo):
    seg = geo.seg
    return dict(tm=min(512, seg), tmc=min(1024, seg), tq=min(1024, seg), tscan=min(512, seg), tmoe=min(512, seg))


def _forward(geo, tiles, x_prompt, x_sample, c_prompt, c_sample, rel_bias, router_w, router_b, ada_w, ada_b,
             norm_mix_g, norm_ffn_g, w_in_ab, w_out_ab, q_norm_g, k_norm_g, diff_lambda, diff_norm_g,
             mlstm_conv_w, mlstm_conv_b, mlstm_gate_b, mlstm_norm_g, w_in_c, w_out_c, ret_decay_logit,
             ret_norm_g, moe_w1, moe_w3, moe_w2):
    tm, tmc, tq, tscan, tmoe = tiles["tm"], tiles["tmc"], tiles["tq"], tiles["tscan"], tiles["tmoe"]
    x = jnp.concatenate([x_prompt.reshape(geo.tp, D_MODEL), x_sample.reshape(geo.t - geo.tp, D_MODEL)], axis=0)
    c_rows = jnp.concatenate([jnp.repeat(c_prompt, geo.sp // geo.seg, axis=0),
                              jnp.repeat(c_sample, geo.ss // geo.seg, axis=0)], axis=0)
    mod = _ada_call(c_rows, ada_w, ada_b)

    bias = _bias_call(rel_bias, tq)
    cos, sin = _rope_tables(max(geo.sp, geo.ss))
    rw_t = router_w.T
    rb = router_b.reshape(N_EXPERTS, 1)
    w1 = moe_w1.astype(MXU_DT)
    w3 = moe_w3.astype(MXU_DT)
    w2 = moe_w2.astype(MXU_DT)
    xs = jnp.zeros((geo.t + N_BUCKETS * tmoe, HX_W), F32)

    for l in range(DEPTH):
        j = l // 2
        mod_l = mod[l]
        g_mix = norm_mix_g[l].reshape(1, D_MODEL)
        g_ffn = norm_ffn_g[l].reshape(1, D_MODEL)
        if l % 2 == 0:
            lam_init = 0.8 - 0.6 * math.exp(-0.3 * l)
            w_pad = jnp.pad(w_in_ab[j], ((0, 0), (0, AB_IN_PAD - AB_IN))).astype(MXU_DT)
            qg = jnp.tile(q_norm_g[j], 2).reshape(1, LANES)
            kg = jnp.tile(k_norm_g[j], 2).reshape(1, LANES)
            gate_b = jnp.pad(mlstm_gate_b[j].reshape(1, B_HEADS * N_GATES), ((0, 0), (0, LANES - B_HEADS * N_GATES)))
            qa, ka, va, qk, vb, ob, gates = _inproj_ab_call(geo, tm, x, mod_l, g_mix, w_pad, qg, kg, gate_b)
            sc, bounded_ok = _attn_scalars(bias, tq, q_norm_g[j], k_norm_g[j])
            ya = lax.cond(bounded_ok,
                          functools.partial(_attn_call, geo, tq, lam_init, True),
                          functools.partial(_attn_call, geo, tq, lam_init, False),
                          qa, ka, va, bias, sc, diff_lambda[j], diff_norm_g[j].reshape(1, A_V_DIM))
            qb, kb = _conv_call(geo, tscan, qk, mlstm_conv_w[j], mlstm_conv_b[j].reshape(1, 2 * B_WIDTH))
            yb = _mlstm_call(geo, tscan, qb, kb, vb, gates, ob, mlstm_norm_g[j].reshape(1, B_DIM))
            w_o = w_out_ab[j].astype(MXU_DT)
            ys, ws = [ya, yb], [w_o[:A_WIDTH], w_o[A_WIDTH:]]
        else:
            q, k, v, gt = _inproj_c_call(geo, tmc, x, mod_l, g_mix, w_in_c[j].astype(MXU_DT), cos, sin)
            y = _ret_call(geo, tscan, q, k, v, ret_decay_logit[j], gt, ret_norm_g[j].reshape(1, C_V_DIM))
            ys, ws = [y], [w_out_c[j].astype(MXU_DT)]
        x, hx, bucket = _outproj_call(geo, tm, ys, ws, x, mod_l, g_ffn, rw_t, rb)
        pos, te0, te1, n_tiles = _plan_call(tmoe, bucket)
        xs = _dispatch_call(tm, pos, hx, xs)
        ysort = _moe_call(tmoe, l, te0, te1, n_tiles, xs, w1, w3, w2)
        x = _collect_call(geo, tm, pos, ysort, x, mod_l)

    y_prompt = x[:geo.tp].reshape(x_prompt.shape)
    y_sample = x[geo.tp:].reshape(x_sample.shape)
    return (y_prompt, y_sample)


def kernel(x_prompt, x_sample, c_prompt, c_sample, rel_bias, router_w, router_b, ada_w, ada_b, norm_mix_g, norm_ffn_g, w_in_ab, w_out_ab, q_norm_g, k_norm_g, diff_lambda, diff_norm_g, mlstm_conv_w, mlstm_conv_b, mlstm_gate_b, mlstm_norm_g, w_in_c, w_out_c, ret_decay_logit, ret_norm_g, moe_w1, moe_w3, moe_w2):
    geo = Geo(x_prompt.shape[0], x_prompt.shape[1], x_sample.shape[0], x_sample.shape[1])
    return _forward(geo, _tiles(geo), x_prompt, x_sample, c_prompt, c_sample, rel_bias, router_w, router_b,
                    ada_w, ada_b, norm_mix_g, norm_ffn_g, w_in_ab, w_out_ab, q_norm_g, k_norm_g, diff_lambda,
                    diff_norm_g, mlstm_conv_w, mlstm_conv_b, mlstm_gate_b, mlstm_norm_g, w_in_c, w_out_c,
                    ret_decay_logit, ret_norm_g, moe_w1, moe_w3, moe_w2)
```

```python
import functools
import math

import jax
import jax.numpy as jnp
import numpy as np
from jax import lax
from jax.experimental import pallas as pl
from jax.experimental.pallas import tpu as pltpu

F32 = jnp.float32
MXU_DT = jnp.bfloat16
ACT_DT = jnp.bfloat16
HI = lax.Precision.HIGHEST

D_MODEL = 1024
DEPTH = 4
A_HEADS = 4
A_QK_DIM = 64
A_V_DIM = 128
A_QK_WIDTH = 512
A_WIDTH = 512
B_HEADS = 4
B_DIM = 128
B_WIDTH = 512
N_GATES = 4
AB_IN = 3600
AB_IN_PAD = 3712
C_HEADS = 4
C_QK_DIM = 256
C_V_DIM = 512
C_QK_WIDTH = 1024
C_V_WIDTH = 2048
C_IN = 6144
CHUNK = 128
RET_CHUNK = 256
REL_BUCKETS = 32
REL_MAX_DIST = 128
N_EXPERTS = 16
N_GROUPS = 4
EPG = 4
N_PAIRS = 6
N_BUCKETS = N_GROUPS * N_PAIRS
D_FF = 512
ROPE_BASE = 10000.0
EPS = 1e-6
LANES = 128
NEG_BIG = -1e30
LOG2E = math.log2(math.e)
HX_W = D_MODEL + LANES
VMEM_LIMIT = 56 * 1024 * 1024


def _cparams(sem):
    return pltpu.CompilerParams(dimension_semantics=sem, vmem_limit_bytes=VMEM_LIMIT)


def _mm(a, b):
    return jnp.dot(a.astype(MXU_DT), b.astype(MXU_DT), preferred_element_type=F32)


def _mm_nt(a, b):
    return lax.dot_general(a.astype(MXU_DT), b.astype(MXU_DT), (((1,), (1,)), ((), ())),
                           preferred_element_type=F32)


def _mm_tn(a, b):
    return lax.dot_general(a.astype(MXU_DT), b.astype(MXU_DT), (((0,), (0,)), ((), ())),
                           preferred_element_type=F32)


def _mm_hi(a, b):
    return jnp.dot(a, b, precision=HI, preferred_element_type=F32)


def _silu(x):
    return x * (1.0 / (1.0 + jnp.exp(-x)))


def _sigmoid(x):
    return 1.0 / (1.0 + jnp.exp(-x))


def _log_sigmoid(x):
    return jnp.minimum(x, 0.0) - jnp.log1p(jnp.exp(-jnp.abs(x)))


def _rms(x, g):
    return x * lax.rsqrt(jnp.mean(x * x, axis=-1, keepdims=True) + EPS) * g


class Geo:
    def __init__(self, bp, sp, bs, ss):
        self.bp, self.sp, self.bs, self.ss = bp, sp, bs, ss
        self.tp = bp * sp
        self.t = bp * sp + bs * ss
        self.seg = math.gcd(sp, ss)

    def seq_start(self, row0):
        return jnp.where(row0 < self.tp, row0 % self.sp == 0, (row0 - self.tp) % self.ss == 0)

    def pos_block(self, blk, rows):
        nbp = self.tp // rows
        return jnp.where(blk < nbp, blk % (self.sp // rows), (blk - nbp) % (self.ss // rows))


def _ada_kernel(c_ref, w_ref, b_ref, o_ref):
    o_ref[...] = _mm_hi(_silu(c_ref[...]), w_ref[...]) + b_ref[...]


def _ada_call(c_rows, ada_w, ada_b):
    r = c_rows.shape[0]
    nb = 1536
    out = pl.pallas_call(
        _ada_kernel,
        grid=(DEPTH, 6 * D_MODEL // nb),
        in_specs=[pl.BlockSpec((r, D_MODEL), lambda l, n: (0, 0)),
                  pl.BlockSpec((None, D_MODEL, nb), lambda l, n: (l, 0, n)),
                  pl.BlockSpec((None, 1, nb), lambda l, n: (l, 0, n))],
        out_specs=pl.BlockSpec((None, r, nb), lambda l, n: (l, 0, n)),
        out_shape=jax.ShapeDtypeStruct((DEPTH, r, 6 * D_MODEL), F32),
        compiler_params=_cparams(("arbitrary", "arbitrary")),
        name="ada_mod",
    )(c_rows, ada_w, ada_b.reshape(DEPTH, 1, 6 * D_MODEL))
    return out.reshape(DEPTH, r, 6, D_MODEL)


def _t5_bucket(rel):
    nb = REL_BUCKETS // 2
    max_exact = nb // 2
    ret = jnp.where(rel > 0, nb, 0)
    n = jnp.abs(rel)
    nf = jnp.maximum(n, 1).astype(jnp.float32)
    large = max_exact + (jnp.log(nf / max_exact) / math.log(REL_MAX_DIST / max_exact) * (nb - max_exact)).astype(jnp.int32)
    large = jnp.minimum(large, nb - 1)
    return ret + jnp.where(n < max_exact, n, large)


def _bias_kernel(tq, rb_ref, bk_ref, o_ref):
    h = pl.program_id(0)
    bk = bk_ref[...]
    row = jnp.zeros(bk.shape, F32)
    bmax = rb_ref[0, h]
    for b in range(REL_BUCKETS):
        row = row + jnp.where(bk == b, rb_ref[b, h], 0.0)
        bmax = jnp.maximum(bmax, rb_ref[b, h])
    row = (row - bmax) * LOG2E
    table = jnp.broadcast_to(row, (tq, 2 * tq))
    o_ref[...] = pltpu.roll(table, tq + 1, axis=1, stride=1, stride_axis=0)[:, :tq]


def _bias_call(rel_bias, tq):
    rel = (jnp.arange(-1, 2, dtype=jnp.int32) * tq)[:, None, None] + (jnp.arange(2 * tq, dtype=jnp.int32) - (tq - 1))
    buckets = _t5_bucket(rel)
    return pl.pallas_call(
        functools.partial(_bias_kernel, tq),
        grid=(A_HEADS, 3),
        in_specs=[pl.BlockSpec(memory_space=pltpu.SMEM),
                  pl.BlockSpec((None, 1, 2 * tq), lambda h, o: (o, 0, 0))],
        out_specs=pl.BlockSpec((None, None, tq, tq), lambda h, o: (h, o, 0, 0)),
        out_shape=jax.ShapeDtypeStruct((A_HEADS, 3, tq, tq), F32),
        compiler_params=_cparams(("arbitrary", "arbitrary")),
        name="rel_bias_tiles",
    )(rel_bias, buckets)


def _rope_tables(s):
    d = C_QK_DIM
    inv = ROPE_BASE ** (-jnp.arange(0, d, 2, dtype=jnp.float32) / d)
    ang = jnp.arange(s, dtype=jnp.float32)[:, None] * inv[None, :]
    return jnp.cos(ang), jnp.sin(ang)


def _half_rms(z, g):
    lo_lane = lax.broadcasted_iota(jnp.int32, (1, LANES), 1) < A_QK_DIM
    z2 = z * z
    tot = jnp.sum(z2, axis=-1, keepdims=True)
    lo = jnp.sum(jnp.where(lo_lane, z2, 0.0), axis=-1, keepdims=True)
    ms = jnp.where(lo_lane, lo, tot - lo) * (1.0 / A_QK_DIM)
    return z * lax.rsqrt(ms + EPS) * g


def _inproj_ab_kernel(x_ref, mod_ref, g_ref, w_ref, qg_ref, kg_ref, gb_ref,
                      qa_ref, ka_ref, va_ref, qk_ref, vb_ref, ob_ref, gt_ref):
    x = x_ref[...]
    h = _rms(x, g_ref[...]) * (1.0 + mod_ref[1:2, :]) + mod_ref[0:1, :]
    hb = h.astype(MXU_DT)
    qscale = (A_QK_DIM ** -0.5) * LOG2E
    for hd in range(A_HEADS):
        c0 = hd * LANES
        q = jnp.dot(hb, w_ref[:, c0:c0 + LANES], preferred_element_type=F32)
        qa_ref[:, c0:c0 + LANES] = (_half_rms(q, qg_ref[...]) * qscale).astype(qa_ref.dtype)
        k = jnp.dot(hb, w_ref[:, A_QK_WIDTH + c0:A_QK_WIDTH + c0 + LANES], preferred_element_type=F32)
        ka_ref[:, c0:c0 + LANES] = _half_rms(k, kg_ref[...]).astype(ka_ref.dtype)
    o = 2 * A_QK_WIDTH
    va_ref[...] = jnp.dot(hb, w_ref[:, o:o + A_WIDTH], preferred_element_type=F32).astype(va_ref.dtype)
    o += A_WIDTH
    qk_ref[...] = jnp.dot(hb, w_ref[:, o:o + 2 * B_WIDTH], preferred_element_type=F32)
    o += 2 * B_WIDTH
    vb_ref[...] = jnp.dot(hb, w_ref[:, o:o + B_WIDTH], preferred_element_type=F32).astype(vb_ref.dtype)
    o += B_WIDTH
    ob_ref[...] = jnp.dot(hb, w_ref[:, o:o + B_WIDTH], preferred_element_type=F32)
    o += B_WIDTH
    gt_ref[...] = jnp.dot(hb, w_ref[:, o:o + LANES], preferred_element_type=F32) + gb_ref[...]


def _inproj_ab_call(geo, tm, x, mod_l, g, w_pad, qg, kg, gate_b):
    t = geo.t
    per_seg = geo.seg // tm
    row = lambda i: (i, 0)
    const = lambda i: (0, 0)
    widths = (A_QK_WIDTH, A_QK_WIDTH, A_WIDTH, 2 * B_WIDTH, B_WIDTH, B_WIDTH, LANES)
    dtypes = (ACT_DT, ACT_DT, ACT_DT, F32, ACT_DT, F32, F32)
    return pl.pallas_call(
        _inproj_ab_kernel,
        grid=(t // tm,),
        in_specs=[pl.BlockSpec((tm, D_MODEL), row),
                  pl.BlockSpec((None, 6, D_MODEL), lambda i: (i // per_seg, 0, 0)),
                  pl.BlockSpec((1, D_MODEL), const),
                  pl.BlockSpec((D_MODEL, AB_IN_PAD), const),
                  pl.BlockSpec((1, LANES), const),
                  pl.BlockSpec((1, LANES), const),
                  pl.BlockSpec((1, LANES), const)],
        out_specs=[pl.BlockSpec((tm, w), row) for w in widths],
        out_shape=[jax.ShapeDtypeStruct((t, w), d) for w, d in zip(widths, dtypes)],
        compiler_params=_cparams(("parallel",)),
        name="inproj_ab",
    )(x, mod_l, g, w_pad, qg, kg, gate_b)


def _conv_kernel(geo, tc, x_ref, prev_ref, next_ref, w_ref, b_ref, q_ref, k_ref):
    i = pl.program_id(0)
    row0 = i * tc
    x = x_ref[...]
    first = geo.seq_start(row0)
    last = geo.seq_start(row0 + tc) | (row0 + tc == geo.t)
    prev_row = jnp.where(first, 0.0, prev_ref[7:8, :])
    next_row = jnp.where(last, 0.0, next_ref[0:1, :])
    ridx = lax.broadcasted_iota(jnp.int32, (tc, 1), 0)
    x_prev = jnp.where(ridx == 0, prev_row, pltpu.roll(x, 1, axis=0))
    x_next = jnp.where(ridx == tc - 1, next_row, pltpu.roll(x, tc - 1, axis=0))
    y = x_prev * w_ref[0:1, :] + x * w_ref[1:2, :] + x_next * w_ref[2:3, :] + b_ref[...]
    y = _silu(y)
    q_ref[...] = y[:, :B_WIDTH].astype(q_ref.dtype)
    k_ref[...] = (y[:, B_WIDTH:] * (B_DIM ** -0.5)).astype(k_ref.dtype)


def _conv_call(geo, tc, qk, w, b):
    t = geo.t
    r8 = tc // 8
    nb8 = t // 8
    return pl.pallas_call(
        functools.partial(_conv_kernel, geo, tc),
        grid=(t // tc,),
        in_specs=[pl.BlockSpec((tc, 2 * B_WIDTH), lambda i: (i, 0)),
                  pl.BlockSpec((8, 2 * B_WIDTH), lambda i: (jnp.maximum(i * r8 - 1, 0), 0)),
                  pl.BlockSpec((8, 2 * B_WIDTH), lambda i: (jnp.minimum((i + 1) * r8, nb8 - 1), 0)),
                  pl.BlockSpec((3, 2 * B_WIDTH), lambda i: (0, 0)),
                  pl.BlockSpec((1, 2 * B_WIDTH), lambda i: (0, 0))],
        out_specs=[pl.BlockSpec((tc, B_WIDTH), lambda i: (i, 0))] * 2,
        out_shape=[jax.ShapeDtypeStruct((t, B_WIDTH), ACT_DT)] * 2,
        compiler_params=_cparams(("parallel",)),
        name="mlstm_conv",
    )(qk, qk, qk, w, b)


def _attn_finish(lam_init, acc0, l0, acc1, l1, dl_ref, ng_ref, o_ref):
    dl = dl_ref[...]
    lam = (jnp.exp(jnp.sum(dl[0:1] * dl[1:2], axis=-1, keepdims=True))
           - jnp.exp(jnp.sum(dl[2:3] * dl[3:4], axis=-1, keepdims=True)) + lam_init)
    out = acc0 / l0 - lam * (acc1 / l1)
    o_ref[...] = (_rms(out, ng_ref[...]) * (1.0 - lam_init)).astype(o_ref.dtype)


def _attn_kernel(lam_init, nk, q_ref, k_ref, v_ref, bias_ref, sc_ref, dl_ref, ng_ref, o_ref,
                 m_sc, l_sc, acc_sc):
    h = pl.program_id(1)
    i = pl.program_id(2)
    j = pl.program_id(3)

    @pl.when(j == 0)
    def _():
        m_sc[...] = jnp.full(m_sc.shape, NEG_BIG, F32)
        l_sc[...] = jnp.zeros(l_sc.shape, F32)
        acc_sc[...] = jnp.zeros(acc_sc.shape, F32)

    q = q_ref[...]
    lo_lane = lax.broadcasted_iota(jnp.int32, (1, LANES), 1) < A_QK_DIM
    qsub = (jnp.where(lo_lane, q, jnp.zeros_like(q)), jnp.where(lo_lane, jnp.zeros_like(q), q))
    k = k_ref[...]
    v = v_ref[...]

    def step(near):
        if near:
            shift = 0.0
        else:
            shift = jnp.where(j < i, sc_ref[h, 0], sc_ref[h, 1])
        for sub in range(2):
            s = _mm_nt(qsub[sub], k)
            if near:
                s = s + bias_ref[...]
            m_old = m_sc[sub]
            m_new = jnp.maximum(m_old, jnp.max(s, axis=-1, keepdims=True) + shift)
            p = jnp.exp2(s - (m_new - shift))
            alpha = jnp.exp2(m_old - m_new)
            l_sc[sub] = alpha * l_sc[sub] + jnp.sum(p, axis=-1, keepdims=True)
            acc_sc[sub] = alpha * acc_sc[sub] + _mm(p, v)
            m_sc[sub] = m_new

    near = jnp.abs(j - i) <= 1
    pl.when(near)(lambda: step(True))
    pl.when(jnp.logical_not(near))(lambda: step(False))

    @pl.when(j == nk - 1)
    def _():
        _attn_finish(lam_init, acc_sc[0], l_sc[0], acc_sc[1], l_sc[1], dl_ref, ng_ref, o_ref)


def _attn_bounded_kernel(lam_init, nk, q_ref, k_ref, v_ref, bias_ref, sc_ref, dl_ref, ng_ref, o_ref, acc_sc):
    h = pl.program_id(1)
    i = pl.program_id(2)
    j = pl.program_id(3)

    @pl.when(j == 0)
    def _():
        acc_sc[...] = jnp.zeros(acc_sc.shape, F32)

    @pl.when(j == jnp.maximum(i - 1, 0))
    def _():
        acc_sc[...] = acc_sc[...] * sc_ref[h, 2]

    @pl.when(j == i + 2)
    def _():
        acc_sc[...] = acc_sc[...] * sc_ref[h, 3]

    q = q_ref[...]
    k = k_ref[...]
    v = v_ref[...]
    lo = (lax.broadcasted_iota(jnp.int32, q.shape, 1) < A_QK_DIM).astype(F32).astype(q.dtype)
    qsub = (q * lo, q * (1 - lo))
    ones_col = (lax.broadcasted_iota(jnp.int32, v.shape, 1) == 0).astype(v.dtype)
    v_aug = jnp.concatenate([v, ones_col], axis=1)

    def step(near):
        for sub in range(2):
            s = _mm_nt(qsub[sub], k)
            if near:
                s = s + bias_ref[...]
            acc_sc[sub] += _mm(jnp.exp2(s), v_aug)

    near = jnp.abs(j - i) <= 1
    pl.when(near)(lambda: step(True))
    pl.when(jnp.logical_not(near))(lambda: step(False))

    @pl.when(j == nk - 1)
    def _():
        a0 = acc_sc[0]
        a1 = acc_sc[1]
        _attn_finish(lam_init, a0[:, :A_V_DIM], a0[:, A_V_DIM:A_V_DIM + 1], a1[:, :A_V_DIM],
                     a1[:, A_V_DIM:A_V_DIM + 1], dl_ref, ng_ref, o_ref)


def _attn_scalars(bias, tq, q_gain, k_gain):
    far_l = bias[:, 0, tq - 1, 0]
    far_r = bias[:, 2, 0, tq - 1]
    sc = jnp.stack([far_l, far_r, jnp.exp2(far_l), jnp.exp2(-far_r)], axis=-1)
    bound = A_QK_DIM * jnp.max(jnp.abs(q_gain)) * jnp.max(jnp.abs(k_gain)) * (A_QK_DIM ** -0.5) * LOG2E * 1.02
    spread = -jnp.min(bias)
    ok = bound + 2.0 * spread <= 80.0
    return sc, ok


def _attn_call(geo, tq, lam_init, bounded, qa, ka, va, bias, sc, dl, ng):
    outs = []
    for (nb, s, row_off) in ((geo.bp, geo.sp, 0), (geo.bs, geo.ss, geo.tp)):
        nq = s // tq
        off = row_off // tq
        qmap = lambda b, h, i, j, off=off, nq=nq: (off + b * nq + i, h)
        kmap = lambda b, h, i, j, off=off, nq=nq: (off + b * nq + j, h)
        omap = lambda b, h, i, j, nq=nq: (b * nq + i, h)
        if bounded:
            body = functools.partial(_attn_bounded_kernel, lam_init, nq)
            scratch = [pltpu.VMEM((2, tq, 2 * LANES), F32)]
        else:
            body = functools.partial(_attn_kernel, lam_init, nq)
            scratch = [pltpu.VMEM((2, tq, 1), F32), pltpu.VMEM((2, tq, 1), F32), pltpu.VMEM((2, tq, LANES), F32)]
        outs.append(pl.pallas_call(
            body,
            grid=(nb, A_HEADS, nq, nq),
            in_specs=[pl.BlockSpec((tq, LANES), qmap),
                      pl.BlockSpec((tq, LANES), kmap),
                      pl.BlockSpec((tq, LANES), kmap),
                      pl.BlockSpec((None, None, tq, tq), lambda b, h, i, j: (h, jnp.clip(j - i + 1, 0, 2), 0, 0)),
                      pl.BlockSpec(memory_space=pltpu.SMEM),
                      pl.BlockSpec((4, A_QK_DIM), lambda b, h, i, j: (0, 0)),
                      pl.BlockSpec((1, LANES), lambda b, h, i, j: (0, 0))],
            out_specs=pl.BlockSpec((tq, LANES), omap),
            out_shape=jax.ShapeDtypeStruct((nb * s, A_WIDTH), ACT_DT),
            scratch_shapes=scratch,
            compiler_params=_cparams(("parallel", "parallel", "parallel", "arbitrary")),
            name="diff_attn_bounded" if bounded else "diff_attn",
        )(qa, ka, va, bias, sc, dl, ng))
    return jnp.concatenate(outs, axis=0)


def _tri(lower):
    r = lax.broadcasted_iota(jnp.int32, (CHUNK, CHUNK), 0)
    c = lax.broadcasted_iota(jnp.int32, (CHUNK, CHUNK), 1)
    return (c <= r) if lower else (c >= r)


def _mlstm_kernel(geo, tb, reverse, *refs):
    if reverse:
        q_ref, k_ref, v_ref, g_ref, hf_ref, ob_ref, ng_ref, o_ref, c_sc, m_sc = refs
    else:
        q_ref, k_ref, v_ref, g_ref, o_ref, c_sc, m_sc = refs
    step = pl.program_id(0)
    nblk = geo.t // tb
    blk = (nblk - 1 - step) if reverse else step
    row0 = blk * tb
    if reverse:
        fresh = geo.seq_start(row0 + tb) | (row0 + tb == geo.t)
    else:
        fresh = geo.seq_start(row0)

    @pl.when(fresh)
    def _():
        c_sc[...] = jnp.zeros(c_sc.shape, F32)
        m_sc[...] = jnp.zeros(m_sc.shape, F32)

    mask = _tri(not reverse)
    cum_l = mask.astype(F32)
    cum_r = _tri(reverse).astype(F32)
    ones_col = (lax.broadcasted_iota(jnp.int32, (CHUNK, LANES), 1) == 0).astype(MXU_DT)
    nch = tb // CHUNK

    def chunk(ci, carry):
        c_idx = (nch - 1 - ci) if reverse else ci
        r0 = pl.multiple_of(c_idx * CHUNK, CHUNK)
        g = g_ref[pl.ds(r0, CHUNK), :]
        g_t = g.T
        b_col = _mm_hi(cum_l, _log_sigmoid(g))
        b_row = _mm_hi(_log_sigmoid(g_t), cum_r)
        for hd in range(B_HEADS):
            ci_col = hd * N_GATES + (2 if reverse else 0)
            cf_col = ci_col + 1
            lanes = slice(hd * B_DIM, (hd + 1) * B_DIM)
            q = q_ref[pl.ds(r0, CHUNK), lanes]
            k = k_ref[pl.ds(r0, CHUNK), lanes]
            v = v_ref[pl.ds(r0, CHUNK), lanes]
            bc = b_col[:, cf_col:cf_col + 1]
            br = b_row[cf_col:cf_col + 1, :]
            ic = g[:, ci_col:ci_col + 1]
            ir = g_t[ci_col:ci_col + 1, :]
            m_prev = m_sc[hd]
            log_d = jnp.where(mask, bc - br + ir, NEG_BIG)
            m_inter = bc + m_prev
            m_t = jnp.maximum(jnp.max(log_d, axis=-1, keepdims=True), m_inter)
            s = _mm_nt(q, k) * jnp.exp(log_d - m_t)
            inter = jnp.exp(m_inter - m_t)
            c_aug = c_sc[hd]
            qc = _mm(q, c_aug)
            num = _mm(s, v) + inter * qc[:, :B_DIM]
            den = jnp.sum(s, axis=-1, keepdims=True) + inter * qc[:, B_DIM:B_DIM + 1]
            hout = num / jnp.maximum(jnp.abs(den), jnp.exp(-m_t))
            b_last = bc[0:1, :] if reverse else bc[CHUNK - 1:CHUNK, :]
            log_w = b_last - bc + ic
            m_new = jnp.maximum(b_last + m_prev, jnp.max(log_w, axis=0, keepdims=True))
            w = jnp.exp(log_w - m_new)
            decay = jnp.exp(b_last + m_prev - m_new)
            v_aug = jnp.concatenate([v, ones_col], axis=1)
            c_sc[hd] = decay * c_aug + _mm_tn(k.astype(F32) * w, v_aug)
            m_sc[hd] = m_new
            if reverse:
                hsum = hf_ref[pl.ds(r0, CHUNK), lanes] + hout
                y = _rms(hsum, ng_ref[...]) * _sigmoid(ob_ref[pl.ds(r0, CHUNK), lanes])
                o_ref[pl.ds(r0, CHUNK), lanes] = y.astype(o_ref.dtype)
            else:
                o_ref[pl.ds(r0, CHUNK), lanes] = hout
        return carry

    lax.fori_loop(0, nch, chunk, 0)


def _mlstm_call(geo, tb, qb, kb, vb, gates, ob, ng):
    t = geo.t
    nblk = t // tb
    fmap = lambda s: (s, 0)
    rmap = lambda s: (nblk - 1 - s, 0)
    scratch = [pltpu.VMEM((B_HEADS, B_DIM, 2 * B_DIM), F32), pltpu.VMEM((B_HEADS, 1, 1), F32)]
    wide = lambda m: pl.BlockSpec((tb, B_WIDTH), m)
    hf = pl.pallas_call(
        functools.partial(_mlstm_kernel, geo, tb, False),
        grid=(nblk,),
        in_specs=[wide(fmap), wide(fmap), wide(fmap), pl.BlockSpec((tb, LANES), fmap)],
        out_specs=wide(fmap),
        out_shape=jax.ShapeDtypeStruct((t, B_WIDTH), F32),
        scratch_shapes=scratch,
        compiler_params=_cparams(("arbitrary",)),
        name="mlstm_fwd",
    )(qb, kb, vb, gates)
    return pl.pallas_call(
        functools.partial(_mlstm_kernel, geo, tb, True),
        grid=(nblk,),
        in_specs=[wide(rmap), wide(rmap), wide(rmap), pl.BlockSpec((tb, LANES), rmap), wide(rmap), wide(rmap),
                  pl.BlockSpec((1, B_DIM), lambda s: (0, 0))],
        out_specs=wide(rmap),
        out_shape=jax.ShapeDtypeStruct((t, B_WIDTH), ACT_DT),
        scratch_shapes=scratch,
        compiler_params=_cparams(("arbitrary",)),
        name="mlstm_bwd",
    )(qb, kb, vb, gates, hf, ob, ng)


def _inproj_c_kernel(x_ref, mod_ref, g_ref, w_ref, cos_ref, sin_ref, q_ref, k_ref, v_ref, gt_ref, h_sc):
    j = pl.program_id(1)
    nsub = C_HEADS
    sub = w_ref.shape[1] // nsub

    def proj(c):
        return jnp.dot(h_sc[...], w_ref[:, c * sub:(c + 1) * sub], preferred_element_type=F32)

    def rope(o_ref, scale):
        cos = cos_ref[...] * scale
        sin = sin_ref[...] * scale
        half = C_QK_DIM // 2
        for hd in range(C_HEADS):
            y = proj(hd)
            x1, x2 = y[:, :half], y[:, half:]
            o_ref[:, hd * C_QK_DIM:hd * C_QK_DIM + half] = (x1 * cos - x2 * sin).astype(o_ref.dtype)
            o_ref[:, hd * C_QK_DIM + half:(hd + 1) * C_QK_DIM] = (x1 * sin + x2 * cos).astype(o_ref.dtype)

    @pl.when(j == 0)
    def _():
        x = x_ref[...]
        h = _rms(x, g_ref[...]) * (1.0 + mod_ref[1:2, :]) + mod_ref[0:1, :]
        h_sc[...] = h.astype(h_sc.dtype)
        rope(q_ref, 1.0)

    @pl.when(j == 1)
    def _():
        rope(k_ref, C_QK_DIM ** -0.5)

    @pl.when((j == 2) | (j == 3))
    def _():
        for c in range(nsub):
            v_ref[:, c * sub:(c + 1) * sub] = proj(c).astype(v_ref.dtype)

    @pl.when(j >= 4)
    def _():
        for c in range(nsub):
            gt_ref[:, c * sub:(c + 1) * sub] = _silu(proj(c)).astype(gt_ref.dtype)


def _inproj_c_call(geo, tm, x, mod_l, g, w, cos, sin):
    t = geo.t
    per_seg = geo.seg // tm
    nw = D_MODEL
    return pl.pallas_call(
        _inproj_c_kernel,
        grid=(t // tm, C_IN // nw),
        in_specs=[pl.BlockSpec((tm, D_MODEL), lambda i, j: (i, 0)),
                  pl.BlockSpec((None, 6, D_MODEL), lambda i, j: (i // per_seg, 0, 0)),
                  pl.BlockSpec((1, D_MODEL), lambda i, j: (0, 0)),
                  pl.BlockSpec((D_MODEL, nw), lambda i, j: (0, j)),
                  pl.BlockSpec((tm, C_QK_DIM // 2), lambda i, j: (geo.pos_block(i, tm), 0)),
                  pl.BlockSpec((tm, C_QK_DIM // 2), lambda i, j: (geo.pos_block(i, tm), 0))],
        out_specs=[pl.BlockSpec((tm, nw), lambda i, j: (i, 0)),
                   pl.BlockSpec((tm, nw), lambda i, j: (i, 0)),
                   pl.BlockSpec((tm, nw), lambda i, j: (i, jnp.clip(j - 2, 0, 1))),
                   pl.BlockSpec((tm, nw), lambda i, j: (i, jnp.clip(j - 4, 0, 1)))],
        out_shape=[jax.ShapeDtypeStruct((t, C_QK_WIDTH), ACT_DT),
                   jax.ShapeDtypeStruct((t, C_QK_WIDTH), ACT_DT),
                   jax.ShapeDtypeStruct((t, C_V_WIDTH), ACT_DT),
                   jax.ShapeDtypeStruct((t, C_V_WIDTH), ACT_DT)],
        scratch_shapes=[pltpu.VMEM((tm, D_MODEL), MXU_DT)],
        compiler_params=_cparams(("parallel", "arbitrary")),
        name="inproj_c",
    )(x, mod_l, g, w, cos, sin)


def _ret_kernel(geo, tb, reverse, *refs):
    if reverse:
        q_ref, k_ref, v_ref, dlg_ref, yf_ref, gt_ref, ng_ref, o_ref, r_sc, intra_sc, vec_sc = refs
    else:
        q_ref, k_ref, v_ref, dlg_ref, o_ref, r_sc, intra_sc, vec_sc = refs
    step = pl.program_id(0)
    nblk = geo.t // tb
    blk = (nblk - 1 - step) if reverse else step
    row0 = blk * tb
    if reverse:
        fresh = geo.seq_start(row0 + tb) | (row0 + tb == geo.t)
    else:
        fresh = geo.seq_start(row0)

    @pl.when(fresh)
    def _():
        r_sc[...] = jnp.zeros(r_sc.shape, F32)

    @pl.when(step == 0)
    def _():
        lg_all = _log_sigmoid(dlg_ref[...])
        ti = lax.broadcasted_iota(jnp.int32, (RET_CHUNK, RET_CHUNK), 0)
        si = lax.broadcasted_iota(jnp.int32, (RET_CHUNK, RET_CHUNK), 1)
        dist = ((si - ti) if reverse else (ti - si)).astype(F32)
        pos = lax.broadcasted_iota(jnp.int32, (RET_CHUNK, LANES), 0).astype(F32)
        upos = (RET_CHUNK - 1.0 - pos) if reverse else pos
        lane = lax.broadcasted_iota(jnp.int32, (RET_CHUNK, LANES), 1)
        d = 1 if reverse else 0
        for hd in range(C_HEADS):
            lg = lg_all[d:d + 1, hd:hd + 1]
            intra_sc[hd] = jnp.where(dist >= 0, jnp.exp(jnp.maximum(dist, 0.0) * lg), 0.0)
            vec_sc[hd] = jnp.where(lane == 0, jnp.exp((upos + 1.0) * lg),
                                   jnp.where(lane == 1, jnp.exp((RET_CHUNK - 1.0 - upos) * lg),
                                             jnp.exp(RET_CHUNK * lg)))

    nch = tb // RET_CHUNK

    def chunk(ci, carry):
        c_idx = (nch - 1 - ci) if reverse else ci
        r0 = pl.multiple_of(c_idx * RET_CHUNK, RET_CHUNK)
        for hd in range(C_HEADS):
            vec = vec_sc[hd]
            q_scale, k_scale, c_decay = vec[:, 0:1], vec[:, 1:2], vec[0:1, 2:3]
            ql = slice(hd * C_QK_DIM, (hd + 1) * C_QK_DIM)
            vl = slice(hd * C_V_DIM, (hd + 1) * C_V_DIM)
            q = q_ref[pl.ds(r0, RET_CHUNK), ql]
            k = k_ref[pl.ds(r0, RET_CHUNK), ql]
            v = v_ref[pl.ds(r0, RET_CHUNK), vl]
            r_old = r_sc[hd]
            s = _mm_nt(q, k) * intra_sc[hd]
            y = _mm(s, v) + q_scale * _mm(q, r_old)
            r_sc[hd] = c_decay * r_old + _mm_tn(k.astype(F32) * k_scale, v)
            if reverse:
                ysum = yf_ref[pl.ds(r0, RET_CHUNK), vl] + y
                out = _rms(ysum, ng_ref[...]) * gt_ref[pl.ds(r0, RET_CHUNK), vl].astype(F32)
                o_ref[pl.ds(r0, RET_CHUNK), vl] = out.astype(o_ref.dtype)
            else:
                o_ref[pl.ds(r0, RET_CHUNK), vl] = y
        return carry

    lax.fori_loop(0, nch, chunk, 0)


def _ret_call(geo, tb, q, k, v, decay_logit, gt, ng):
    t = geo.t
    nblk = t // tb
    fmap = lambda s: (s, 0)
    rmap = lambda s: (nblk - 1 - s, 0)
    scratch = [pltpu.VMEM((C_HEADS, C_QK_DIM, C_V_DIM), F32), pltpu.VMEM((C_HEADS, RET_CHUNK, RET_CHUNK), F32),
               pltpu.VMEM((C_HEADS, RET_CHUNK, LANES), F32)]
    qk = lambda m: pl.BlockSpec((tb, C_QK_WIDTH), m)
    vv = lambda m: pl.BlockSpec((tb, C_V_WIDTH), m)
    dspec = pl.BlockSpec((2, C_HEADS), lambda s: (0, 0))
    yf = pl.pallas_call(
        functools.partial(_ret_kernel, geo, tb, False),
        grid=(nblk,),
        in_specs=[qk(fmap), qk(fmap), vv(fmap), dspec],
        out_specs=vv(fmap),
        out_shape=jax.ShapeDtypeStruct((t, C_V_WIDTH), F32),
        scratch_shapes=scratch,
        compiler_params=_cparams(("arbitrary",)),
        name="ret_fwd",
    )(q, k, v, decay_logit)
    return pl.pallas_call(
        functools.partial(_ret_kernel, geo, tb, True),
        grid=(nblk,),
        in_specs=[qk(rmap), qk(rmap), vv(rmap), dspec, vv(rmap), vv(rmap),
                  pl.BlockSpec((1, C_V_DIM), lambda s: (0, 0))],
        out_specs=vv(rmap),
        out_shape=jax.ShapeDtypeStruct((t, C_V_WIDTH), ACT_DT),
        scratch_shapes=scratch,
        compiler_params=_cparams(("arbitrary",)),
        name="ret_bwd",
    )(q, k, v, decay_logit, yf, gt, ng)


def _route(probs):
    p = [probs[e:e + 1, :] for e in range(N_EXPERTS)]
    scores = []
    for g in range(N_GROUPS):
        a, b, c, d = p[EPG * g:EPG * g + EPG]
        hi1, lo1 = jnp.maximum(a, b), jnp.minimum(a, b)
        hi2, lo2 = jnp.maximum(c, d), jnp.minimum(c, d)
        scores.append(jnp.maximum(hi1, hi2) + jnp.maximum(jnp.minimum(hi1, hi2), jnp.maximum(lo1, lo2)))
    g_sel = jnp.zeros(scores[0].shape, jnp.int32)
    best = scores[0]
    for g in range(1, N_GROUPS):
        better = scores[g] > best
        g_sel = jnp.where(better, g, g_sel)
        best = jnp.where(better, scores[g], best)
    vals = []
    for kk in range(EPG):
        v = p[kk]
        for g in range(1, N_GROUPS):
            v = jnp.where(g_sel == g, p[EPG * g + kk], v)
        vals.append(v)

    def argmax4(xs):
        idx = jnp.zeros(xs[0].shape, jnp.int32)
        top = xs[0]
        for kk in range(1, EPG):
            better = xs[kk] > top
            idx = jnp.where(better, kk, idx)
            top = jnp.where(better, xs[kk], top)
        return idx, top

    i1, v1 = argmax4(vals)
    i2, v2 = argmax4([jnp.where(i1 == kk, -1.0, vals[kk]) for kk in range(EPG)])
    tot = v1 + v2
    w1, w2 = v1 / tot, v2 / tot
    e1 = g_sel * EPG + i1
    e2 = g_sel * EPG + i2
    eidx = lax.broadcasted_iota(jnp.int32, probs.shape, 0)
    lo, hi = jnp.minimum(i1, i2), jnp.maximum(i1, i2)
    pair = jnp.where(lo == 0, hi - 1, jnp.where(lo == 1, hi + 1, N_PAIRS - 1))
    return jnp.where(eidx == e1, w1, 0.0) + jnp.where(eidx == e2, w2, 0.0), g_sel * N_PAIRS + pair


def _outproj_kernel(nparts, *refs):
    y_refs = refs[:nparts]
    w_refs = refs[nparts:2 * nparts]
    x_ref, mod_ref, g_ref, rw_ref, rb_ref, xo_ref, hx_ref, bkt_ref = refs[2 * nparts:]
    m = jnp.dot(y_refs[0][...], w_refs[0][...], preferred_element_type=F32)
    for p in range(1, nparts):
        m = m + jnp.dot(y_refs[p][...], w_refs[p][...], preferred_element_type=F32)
    x = x_ref[...] + mod_ref[2:3, :] * m
    xo_ref[...] = x
    h = _rms(x, g_ref[...]) * (1.0 + mod_ref[4:5, :]) + mod_ref[3:4, :]
    hx_ref[:, :D_MODEL] = h
    logits = lax.dot_general(rw_ref[...], h, (((1,), (1,)), ((), ())), precision=HI,
                             preferred_element_type=F32) + rb_ref[...]
    z = jnp.exp(logits - jnp.max(logits, axis=0, keepdims=True))
    probs = z / jnp.sum(z, axis=0, keepdims=True)
    cmb, bucket = _route(probs)
    pad = jnp.zeros((LANES - N_EXPERTS, cmb.shape[1]), F32)
    hx_ref[:, D_MODEL:] = jnp.concatenate([cmb, pad], axis=0).T
    bkt_ref[...] = bucket


def _outproj_call(geo, tm, ys, ws, x, mod_l, g, rw_t, rb):
    t = geo.t
    per_seg = geo.seg // tm
    row = lambda i: (i, 0)
    const = lambda i: (0, 0)
    n = len(ys)
    return pl.pallas_call(
        functools.partial(_outproj_kernel, n),
        grid=(t // tm,),
        in_specs=([pl.BlockSpec((tm, y.shape[1]), row) for y in ys]
                  + [pl.BlockSpec(w.shape, const) for w in ws]
                  + [pl.BlockSpec((tm, D_MODEL), row),
                     pl.BlockSpec((None, 6, D_MODEL), lambda i: (i // per_seg, 0, 0)),
                     pl.BlockSpec((1, D_MODEL), const),
                     pl.BlockSpec((N_EXPERTS, D_MODEL), const),
                     pl.BlockSpec((N_EXPERTS, 1), const)]),
        out_specs=[pl.BlockSpec((tm, D_MODEL), row), pl.BlockSpec((tm, HX_W), row),
                   pl.BlockSpec((None, 1, tm), lambda i: (i, 0, 0))],
        out_shape=[jax.ShapeDtypeStruct((t, D_MODEL), F32), jax.ShapeDtypeStruct((t, HX_W), F32),
                   jax.ShapeDtypeStruct((t // tm, 1, tm), jnp.int32)],
        compiler_params=_cparams(("parallel",)),
        name="outproj_router",
    )(*ys, *ws, x, mod_l, g, rw_t, rb)


def _plan_kernel(tm, bkt_ref, pos_ref, te_ref, nv_ref):
    nblk, _, blk = bkt_ref.shape
    nrow = 32
    bid = lax.broadcasted_iota(jnp.int32, (nrow, blk), 0)
    r = lax.broadcasted_iota(jnp.int32, (blk, blk), 0)
    c = lax.broadcasted_iota(jnp.int32, (blk, blk), 1)
    prefix = (r <= c).astype(MXU_DT)

    def count(b, acc):
        return acc + jnp.sum((bkt_ref[b] == bid).astype(F32), axis=1, keepdims=True)

    counts = lax.fori_loop(0, nblk, count, jnp.zeros((nrow, 1), F32))
    padded = jnp.floor((counts + (tm - 1.0)) / tm) * tm
    rows = lax.broadcasted_iota(jnp.int32, (nrow, 1), 0)
    offs = jnp.zeros((nrow, 1), F32)
    ends = []
    run = jnp.zeros((1, 1), F32)
    for b in range(N_BUCKETS):
        offs = jnp.where(rows == b, run, offs)
        run = run + padded[b:b + 1, :]
        ends.append(run)

    def place(b, carry):
        oh = (bkt_ref[b] == bid).astype(F32)
        pre = jnp.dot(oh.astype(MXU_DT), prefix, preferred_element_type=F32)
        pos = jnp.sum(oh * (offs + carry + pre - 1.0), axis=0, keepdims=True)
        pos_ref[b] = pos.astype(jnp.int32)
        return carry + pre[:, blk - 1:blk]

    lax.fori_loop(0, nblk, place, jnp.zeros((nrow, 1), F32))
    start = lax.broadcasted_iota(jnp.int32, (1, te_ref.shape[1]), 1).astype(F32) * tm
    tb = jnp.zeros(start.shape, F32)
    for b in range(N_BUCKETS - 1):
        tb = tb + (ends[b] <= start).astype(F32)
    grp = jnp.floor((tb + 0.5) / N_PAIRS)
    pair = tb - grp * N_PAIRS
    lo = (pair >= 3).astype(F32) + (pair >= 5).astype(F32)
    hi = jnp.where(pair == 0, 1.0, jnp.where((pair == 1) | (pair == 3), 2.0, 3.0))
    te_ref[0:1, :] = (grp * EPG + lo).astype(jnp.int32)
    te_ref[1:2, :] = (grp * EPG + hi).astype(jnp.int32)
    nv_ref[...] = jnp.broadcast_to(run / tm, nv_ref.shape).astype(jnp.int32)


def _plan_call(tm, bkt):
    nblk, _, blk = bkt.shape
    nt = nblk * blk // tm + N_BUCKETS
    ntp = -(-nt // LANES) * LANES
    pos, te, nv = pl.pallas_call(
        functools.partial(_plan_kernel, tm),
        out_shape=[jax.ShapeDtypeStruct(bkt.shape, jnp.int32), jax.ShapeDtypeStruct((2, ntp), jnp.int32),
                   jax.ShapeDtypeStruct((1, LANES), jnp.int32)],
        compiler_params=pltpu.CompilerParams(vmem_limit_bytes=VMEM_LIMIT),
        name="moe_plan",
    )(bkt)
    return pos, te[0, :nt], te[1, :nt], nv[0, :1]


def _dispatch_kernel(tb, pos_ref, hx_ref, init_ref, xs_ref, sem):
    del init_ref

    def issue(r, carry):
        pltpu.make_async_copy(hx_ref.at[pl.ds(r, 1)], xs_ref.at[pl.ds(pos_ref[0, r], 1)], sem).start()
        return carry

    lax.fori_loop(0, tb, issue, 0, unroll=8)
    pltpu.make_async_copy(hx_ref, xs_ref.at[pl.ds(0, tb)], sem).wait()


def _dispatch_call(tb, pos, hx, xs_init):
    nsteps = hx.shape[0] // tb
    return pl.pallas_call(
        functools.partial(_dispatch_kernel, tb),
        grid=(nsteps,),
        in_specs=[pl.BlockSpec((None, 1, tb), lambda i: (i, 0, 0), memory_space=pltpu.SMEM),
                  pl.BlockSpec((tb, HX_W), lambda i: (i, 0)),
                  pl.BlockSpec(memory_space=pl.ANY)],
        out_specs=pl.BlockSpec(memory_space=pl.ANY),
        out_shape=jax.ShapeDtypeStruct(xs_init.shape, xs_init.dtype),
        scratch_shapes=[pltpu.SemaphoreType.DMA(())],
        input_output_aliases={2: 0},
        compiler_params=pltpu.CompilerParams(dimension_semantics=("arbitrary",), disable_bounds_checks=True,
                                             has_side_effects=True),
        name="moe_dispatch",
    )(pos, hx, xs_init)


def _moe_kernel(te0_ref, te1_ref, nv_ref, xs_ref, w1a_ref, w3a_ref, w2a_ref, w1b_ref, w3b_ref, w2b_ref, o_ref):
    n = pl.program_id(0)

    @pl.when(n < nv_ref[0])
    def _():
        h = xs_ref[:, :D_MODEL].astype(MXU_DT)
        cmb = xs_ref[:, D_MODEL:]
        lane = lax.broadcasted_iota(jnp.int32, cmb.shape, 1)

        def expert(e_id, w1_ref, w3_ref, w2_ref):
            a = jnp.dot(h, w1_ref[...], preferred_element_type=F32)
            b = jnp.dot(h, w3_ref[...], preferred_element_type=F32)
            c = jnp.sum(jnp.where(lane == e_id, cmb, 0.0), axis=-1, keepdims=True)
            return c * _mm(_silu(a) * b, w2_ref[...])

        o_ref[...] = (expert(te0_ref[n], w1a_ref, w3a_ref, w2a_ref)
                      + expert(te1_ref[n], w1b_ref, w3b_ref, w2b_ref))

    @pl.when(n >= nv_ref[0])
    def _():
        o_ref[...] = jnp.zeros(o_ref.shape, F32)


def _moe_call(tm, layer, te0, te1, nv, xs, w1, w3, w2):
    rows = xs.shape[0]
    nt = rows // tm
    tile = lambda n, te0, te1, nv: (jnp.minimum(n, nv[0] - 1), 0)
    wa = lambda n, te0, te1, nv: (layer, te0[jnp.minimum(n, nv[0] - 1)], 0, 0)
    wb = lambda n, te0, te1, nv: (layer, te1[jnp.minimum(n, nv[0] - 1)], 0, 0)
    up = lambda m: pl.BlockSpec((None, None, D_MODEL, D_FF), m)
    down = lambda m: pl.BlockSpec((None, None, D_FF, D_MODEL), m)
    return pl.pallas_call(
        _moe_kernel,
        grid_spec=pltpu.PrefetchScalarGridSpec(
            num_scalar_prefetch=3,
            grid=(nt,),
            in_specs=[pl.BlockSpec((tm, HX_W), tile), up(wa), up(wa), down(wa), up(wb), up(wb), down(wb)],
            out_specs=pl.BlockSpec((tm, D_MODEL), lambda n, te0, te1, nv: (n, 0))),
        out_shape=jax.ShapeDtypeStruct((rows, D_MODEL), F32),
        compiler_params=_cparams(("arbitrary",)),
        name="moe",
    )(te0, te1, nv, xs, w1, w3, w2, w1, w3, w2)


def _collect_kernel(tb, nsteps, pos_ref, posn_ref, ys_ref, x_ref, mod_ref, o_ref, buf, sem):
    i = pl.program_id(0)
    slot = i % 2

    def issue(p_ref, s):
        def body(r, carry):
            pltpu.make_async_copy(ys_ref.at[pl.ds(p_ref[0, r], 1)], buf.at[s, pl.ds(r, 1)], sem.at[s]).start()
            return carry
        lax.fori_loop(0, tb, body, 0, unroll=8)

    pl.when(i == 0)(lambda: issue(pos_ref, 0))
    pl.when(i + 1 < nsteps)(lambda: issue(posn_ref, 1 - slot))
    pltpu.make_async_copy(ys_ref.at[pl.ds(0, tb)], buf.at[slot], sem.at[slot]).wait()
    o_ref[...] = x_ref[...] + mod_ref[5:6, :] * buf[slot]


def _collect_call(geo, tb, pos, ys, x, mod_l):
    t = geo.t
    nsteps = t // tb
    per_seg = geo.seg // tb
    return pl.pallas_call(
        functools.partial(_collect_kernel, tb, nsteps),
        grid=(nsteps,),
        in_specs=[pl.BlockSpec((None, 1, tb), lambda i: (i, 0, 0), memory_space=pltpu.SMEM),
                  pl.BlockSpec((None, 1, tb), lambda i: (jnp.minimum(i + 1, nsteps - 1), 0, 0),
                               memory_space=pltpu.SMEM),
                  pl.BlockSpec(memory_space=pl.ANY),
                  pl.BlockSpec((tb, D_MODEL), lambda i: (i, 0)),
                  pl.BlockSpec((None, 6, D_MODEL), lambda i: (i // per_seg, 0, 0))],
        out_specs=pl.BlockSpec((tb, D_MODEL), lambda i: (i, 0)),
        out_shape=jax.ShapeDtypeStruct((t, D_MODEL), F32),
        scratch_shapes=[pltpu.VMEM((2, tb, D_MODEL), F32), pltpu.SemaphoreType.DMA((2,))],
        compiler_params=pltpu.CompilerParams(dimension_semantics=("arbitrary",), vmem_limit_bytes=VMEM_LIMIT,
                                             disable_bounds_checks=True),
        name="moe_collect",
    )(pos, pos, ys, x, mod_l)


def _tiles(geo):
    seg = geo.seg
    return dict(tm=min(512, seg), tmc=min(1024, seg), tq=min(1024, seg), tscan=min(512, seg), tmoe=min(512, seg))


def _forward(geo, tiles, x_prompt, x_sample, c_prompt, c_sample, rel_bias, router_w, router_b, ada_w, ada_b,
             norm_mix_g, norm_ffn_g, w_in_ab, w_out_ab, q_norm_g, k_norm_g, diff_lambda, diff_norm_g,
             mlstm_conv_w, mlstm_conv_b, mlstm_gate_b, mlstm_norm_g, w_in_c, w_out_c, ret_decay_logit,
             ret_norm_g, moe_w1, moe_w3, moe_w2):
    tm, tmc, tq, tscan, tmoe = tiles["tm"], tiles["tmc"], tiles["tq"], tiles["tscan"], tiles["tmoe"]
    x = jnp.concatenate([x_prompt.reshape(geo.tp, D_MODEL), x_sample.reshape(geo.t - geo.tp, D_MODEL)], axis=0)
    c_rows = jnp.concatenate([jnp.repeat(c_prompt, geo.sp // geo.seg, axis=0),
                              jnp.repeat(c_sample, geo.ss // geo.seg, axis=0)], axis=0)
    mod = _ada_call(c_rows, ada_w, ada_b)

    bias = _bias_call(rel_bias, tq)
    cos, sin = _rope_tables(max(geo.sp, geo.ss))
    rw_t = router_w.T
    rb = router_b.reshape(N_EXPERTS, 1)
    w1 = moe_w1.astype(MXU_DT)
    w3 = moe_w3.astype(MXU_DT)
    w2 = moe_w2.astype(MXU_DT)
    xs = jnp.zeros((geo.t + N_BUCKETS * tmoe, HX_W), F32)

    for l in range(DEPTH):
        j = l // 2
        mod_l = mod[l]
        g_mix = norm_mix_g[l].reshape(1, D_MODEL)
        g_ffn = norm_ffn_g[l].reshape(1, D_MODEL)
        if l % 2 == 0:
            lam_init = 0.8 - 0.6 * math.exp(-0.3 * l)
            w_pad = jnp.pad(w_in_ab[j], ((0, 0), (0, AB_IN_PAD - AB_IN))).astype(MXU_DT)
            qg = jnp.tile(q_norm_g[j], 2).reshape(1, LANES)
            kg = jnp.tile(k_norm_g[j], 2).reshape(1, LANES)
            gate_b = jnp.pad(mlstm_gate_b[j].reshape(1, B_HEADS * N_GATES), ((0, 0), (0, LANES - B_HEADS * N_GATES)))
            qa, ka, va, qk, vb, ob, gates = _inproj_ab_call(geo, tm, x, mod_l, g_mix, w_pad, qg, kg, gate_b)
            sc, bounded_ok = _attn_scalars(bias, tq, q_norm_g[j], k_norm_g[j])
            ya = lax.cond(bounded_ok,
                          functools.partial(_attn_call, geo, tq, lam_init, True),
                          functools.partial(_attn_call, geo, tq, lam_init, False),
                          qa, ka, va, bias, sc, diff_lambda[j], diff_norm_g[j].reshape(1, A_V_DIM))
            qb, kb = _conv_call(geo, tscan, qk, mlstm_conv_w[j], mlstm_conv_b[j].reshape(1, 2 * B_WIDTH))
            yb = _mlstm_call(geo, tscan, qb, kb, vb, gates, ob, mlstm_norm_g[j].reshape(1, B_DIM))
            w_o = w_out_ab[j].astype(MXU_DT)
            ys, ws = [ya, yb], [w_o[:A_WIDTH], w_o[A_WIDTH:]]
        else:
            q, k, v, gt = _inproj_c_call(geo, tmc, x, mod_l, g_mix, w_in_c[j].astype(MXU_DT), cos, sin)
            y = _ret_call(geo, tscan, q, k, v, ret_decay_logit[j], gt, ret_norm_g[j].reshape(1, C_V_DIM))
            ys, ws = [y], [w_out_c[j].astype(MXU_DT)]
        x, hx, bucket = _outproj_call(geo, tm, ys, ws, x, mod_l, g_ffn, rw_t, rb)
        pos, te0, te1, n_tiles = _plan_call(tmoe, bucket)
        xs = _dispatch_call(tm, pos, hx, xs)
        ysort = _moe_call(tmoe, l, te0, te1, n_tiles, xs, w1, w3, w2)
        x = _collect_call(geo, tm, pos, ysort, x, mod_l)

    y_prompt = x[:geo.tp].reshape(x_prompt.shape)
    y_sample = x[geo.tp:].reshape(x_sample.shape)
    return (y_prompt, y_sample)


def kernel(x_prompt, x_sample, c_prompt, c_sample, rel_bias, router_w, router_b, ada_w, ada_b, norm_mix_g, norm_ffn_g, w_in_ab, w_out_ab, q_norm_g, k_norm_g, diff_lambda, diff_norm_g, mlstm_conv_w, mlstm_conv_b, mlstm_gate_b, mlstm_norm_g, w_in_c, w_out_c, ret_decay_logit, ret_norm_g, moe_w1, moe_w3, moe_w2):
    geo = Geo(x_prompt.shape[0], x_prompt.shape[1], x_sample.shape[0], x_sample.shape[1])
    return _forward(geo, _tiles(geo), x_prompt, x_sample, c_prompt, c_sample, rel_bias, router_w, router_b,
                    ada_w, ada_b, norm_mix_g, norm_ffn_g, w_in_ab, w_out_ab, q_norm_g, k_norm_g, diff_lambda,
                    diff_norm_g, mlstm_conv_w, mlstm_conv_b, mlstm_gate_b, mlstm_norm_g, w_in_c, w_out_c,
                    ret_decay_logit, ret_norm_g, moe_w1, moe_w3, moe_w2)
```

```python
import functools
import math

import jax
import jax.numpy as jnp
import numpy as np
from jax import lax
from jax.experimental import pallas as pl
from jax.experimental.pallas import tpu as pltpu

F32 = jnp.float32
MXU_DT = jnp.bfloat16
ACT_DT = jnp.bfloat16
HI = lax.Precision.HIGHEST

D_MODEL = 1024
DEPTH = 4
A_HEADS = 4
A_QK_DIM = 64
A_V_DIM = 128
A_QK_WIDTH = 512
A_WIDTH = 512
B_HEADS = 4
B_DIM = 128
B_WIDTH = 512
N_GATES = 4
AB_IN = 3600
AB_IN_PAD = 3712
C_HEADS = 4
C_QK_DIM = 256
C_V_DIM = 512
C_QK_WIDTH = 1024
C_V_WIDTH = 2048
C_IN = 6144
CHUNK = 128
RET_CHUNK = 256
REL_BUCKETS = 32
REL_MAX_DIST = 128
N_EXPERTS = 16
N_GROUPS = 4
EPG = 4
N_PAIRS = 6
N_BUCKETS = N_GROUPS * N_PAIRS
D_FF = 512
ROPE_BASE = 10000.0
EPS = 1e-6
LANES = 128
SUBLANES = 8
NEG_BIG = -1e30
LOG2E = math.log2(math.e)
HX_W = D_MODEL + LANES
VMEM_LIMIT = 56 * 1024 * 1024


def _cparams(sem):
    return pltpu.CompilerParams(dimension_semantics=sem, vmem_limit_bytes=VMEM_LIMIT)


def _mm(a, b):
    return jnp.dot(a.astype(MXU_DT), b.astype(MXU_DT), preferred_element_type=F32)


def _mm_nt(a, b):
    return lax.dot_general(a.astype(MXU_DT), b.astype(MXU_DT), (((1,), (1,)), ((), ())),
                           preferred_element_type=F32)


def _mm_tn(a, b):
    return lax.dot_general(a.astype(MXU_DT), b.astype(MXU_DT), (((0,), (0,)), ((), ())),
                           preferred_element_type=F32)


def _mm_hi(a, b):
    return jnp.dot(a, b, precision=HI, preferred_element_type=F32)


def _silu(x):
    return x * (1.0 / (1.0 + jnp.exp(-x)))


def _sigmoid(x):
    return 1.0 / (1.0 + jnp.exp(-x))


def _log_sigmoid(x):
    return jnp.minimum(x, 0.0) - jnp.log1p(jnp.exp(-jnp.abs(x)))


def _rms(x, g):
    return x * lax.rsqrt(jnp.mean(x * x, axis=-1, keepdims=True) + EPS) * g


class Geo:
    def __init__(self, bp, sp, bs, ss):
        self.bp, self.sp, self.bs, self.ss = bp, sp, bs, ss
        self.tp = bp * sp
        self.t = bp * sp + bs * ss
        self.seg = math.gcd(sp, ss)

    def seq_start(self, row0):
        return jnp.where(row0 < self.tp, row0 % self.sp == 0, (row0 - self.tp) % self.ss == 0)

    def pos_block(self, blk, rows):
        nbp = self.tp // rows
        return jnp.where(blk < nbp, blk % (self.sp // rows), (blk - nbp) % (self.ss // rows))


def _ada_kernel(c_ref, w_ref, b_ref, o_ref):
    o_ref[...] = _mm_hi(_silu(c_ref[...]), w_ref[...]) + b_ref[...]


def _ada_call(c_rows, ada_w, ada_b):
    r = c_rows.shape[0]
    nb = 1536
    out = pl.pallas_call(
        _ada_kernel,
        grid=(DEPTH, 6 * D_MODEL // nb),
        in_specs=[pl.BlockSpec((r, D_MODEL), lambda l, n: (0, 0)),
                  pl.BlockSpec((None, D_MODEL, nb), lambda l, n: (l, 0, n)),
                  pl.BlockSpec((None, 1, nb), lambda l, n: (l, 0, n))],
        out_specs=pl.BlockSpec((None, r, nb), lambda l, n: (l, 0, n)),
        out_shape=jax.ShapeDtypeStruct((DEPTH, r, 6 * D_MODEL), F32),
        compiler_params=_cparams(("arbitrary", "arbitrary")),
        name="ada_mod",
    )(c_rows, ada_w, ada_b.reshape(DEPTH, 1, 6 * D_MODEL))
    return out.reshape(DEPTH, r, 6, D_MODEL)


def _t5_bucket(rel):
    nb = REL_BUCKETS // 2
    max_exact = nb // 2
    ret = jnp.where(rel > 0, nb, 0)
    n = jnp.abs(rel)
    nf = jnp.maximum(n, 1).astype(jnp.float32)
    large = max_exact + (jnp.log(nf / max_exact) / math.log(REL_MAX_DIST / max_exact) * (nb - max_exact)).astype(jnp.int32)
    large = jnp.minimum(large, nb - 1)
    return ret + jnp.where(n < max_exact, n, large)


def _bias_kernel(tq, rb_ref, bk_ref, o_ref):
    h = pl.program_id(0)
    bk = bk_ref[...]
    row = jnp.zeros(bk.shape, F32)
    bmax = rb_ref[0, h]
    for b in range(REL_BUCKETS):
        row = row + jnp.where(bk == b, rb_ref[b, h], 0.0)
        bmax = jnp.maximum(bmax, rb_ref[b, h])
    row = (row - bmax) * LOG2E
    table = jnp.broadcast_to(row, (tq, 2 * tq))
    o_ref[...] = pltpu.roll(table, tq + 1, axis=1, stride=1, stride_axis=0)[:, :tq]


def _bias_call(rel_bias, tq):
    rel = (jnp.arange(-1, 2, dtype=jnp.int32) * tq)[:, None, None] + (jnp.arange(2 * tq, dtype=jnp.int32) - (tq - 1))
    buckets = _t5_bucket(rel)
    return pl.pallas_call(
        functools.partial(_bias_kernel, tq),
        grid=(A_HEADS, 3),
        in_specs=[pl.BlockSpec(memory_space=pltpu.SMEM),
                  pl.BlockSpec((None, 1, 2 * tq), lambda h, o: (o, 0, 0))],
        out_specs=pl.BlockSpec((None, None, tq, tq), lambda h, o: (h, o, 0, 0)),
        out_shape=jax.ShapeDtypeStruct((A_HEADS, 3, tq, tq), F32),
        compiler_params=_cparams(("arbitrary", "arbitrary")),
        name="rel_bias_tiles",
    )(rel_bias, buckets)


def _rope_tables(s):
    d = C_QK_DIM
    inv = ROPE_BASE ** (-jnp.arange(0, d, 2, dtype=jnp.float32) / d)
    ang = jnp.arange(s, dtype=jnp.float32)[:, None] * inv[None, :]
    return jnp.cos(ang), jnp.sin(ang)


def _half_rms(z, g):
    lo_lane = lax.broadcasted_iota(jnp.int32, (1, LANES), 1) < A_QK_DIM
    z2 = z * z
    tot = jnp.sum(z2, axis=-1, keepdims=True)
    lo = jnp.sum(jnp.where(lo_lane, z2, 0.0), axis=-1, keepdims=True)
    ms = jnp.where(lo_lane, lo, tot - lo) * (1.0 / A_QK_DIM)
    return z * lax.rsqrt(ms + EPS) * g


def _inproj_ab_kernel(x_ref, mod_ref, g_ref, w_ref, qg_ref, kg_ref, gb_ref,
                      qa_ref, ka_ref, va_ref, qk_ref, vb_ref, ob_ref, gt_ref):
    x = x_ref[...]
    h = _rms(x, g_ref[...]) * (1.0 + mod_ref[1:2, :]) + mod_ref[0:1, :]
    hb = h.astype(MXU_DT)
    qscale = (A_QK_DIM ** -0.5) * LOG2E
    for hd in range(A_HEADS):
        c0 = hd * LANES
        q = jnp.dot(hb, w_ref[:, c0:c0 + LANES], preferred_element_type=F32)
        qa_ref[:, c0:c0 + LANES] = (_half_rms(q, qg_ref[...]) * qscale).astype(qa_ref.dtype)
        k = jnp.dot(hb, w_ref[:, A_QK_WIDTH + c0:A_QK_WIDTH + c0 + LANES], preferred_element_type=F32)
        ka_ref[:, c0:c0 + LANES] = _half_rms(k, kg_ref[...]).astype(ka_ref.dtype)
    o = 2 * A_QK_WIDTH
    va_ref[...] = jnp.dot(hb, w_ref[:, o:o + A_WIDTH], preferred_element_type=F32).astype(va_ref.dtype)
    o += A_WIDTH
    qk_ref[...] = jnp.dot(hb, w_ref[:, o:o + 2 * B_WIDTH], preferred_element_type=F32)
    o += 2 * B_WIDTH
    vb_ref[...] = jnp.dot(hb, w_ref[:, o:o + B_WIDTH], preferred_element_type=F32).astype(vb_ref.dtype)
    o += B_WIDTH
    ob_ref[...] = jnp.dot(hb, w_ref[:, o:o + B_WIDTH], preferred_element_type=F32)
    o += B_WIDTH
    gt_ref[...] = jnp.dot(hb, w_ref[:, o:o + LANES], preferred_element_type=F32) + gb_ref[...]


def _inproj_ab_call(geo, tm, x, mod_l, g, w_pad, qg, kg, gate_b):
    t = geo.t
    per_seg = geo.seg // tm
    row = lambda i: (i, 0)
    const = lambda i: (0, 0)
    widths = (A_QK_WIDTH, A_QK_WIDTH, A_WIDTH, 2 * B_WIDTH, B_WIDTH, B_WIDTH, LANES)
    dtypes = (ACT_DT, ACT_DT, ACT_DT, F32, ACT_DT, F32, F32)
    return pl.pallas_call(
        _inproj_ab_kernel,
        grid=(t // tm,),
        in_specs=[pl.BlockSpec((tm, D_MODEL), row),
                  pl.BlockSpec((None, 6, D_MODEL), lambda i: (i // per_seg, 0, 0)),
                  pl.BlockSpec((1, D_MODEL), const),
                  pl.BlockSpec((D_MODEL, AB_IN_PAD), const),
                  pl.BlockSpec((1, LANES), const),
                  pl.BlockSpec((1, LANES), const),
                  pl.BlockSpec((1, LANES), const)],
        out_specs=[pl.BlockSpec((tm, w), row) for w in widths],
        out_shape=[jax.ShapeDtypeStruct((t, w), d) for w, d in zip(widths, dtypes)],
        compiler_params=_cparams(("parallel",)),
        name="inproj_ab",
    )(x, mod_l, g, w_pad, qg, kg, gate_b)


def _conv_kernel(geo, tc, x_ref, prev_ref, next_ref, w_ref, b_ref, q_ref, k_ref):
    i = pl.program_id(0)
    row0 = i * tc
    x = x_ref[...]
    first = geo.seq_start(row0)
    last = geo.seq_start(row0 + tc) | (row0 + tc == geo.t)
    prev_row = jnp.where(first, 0.0, prev_ref[7:8, :])
    next_row = jnp.where(last, 0.0, next_ref[0:1, :])
    ridx = lax.broadcasted_iota(jnp.int32, (tc, 1), 0)
    x_prev = jnp.where(ridx == 0, prev_row, pltpu.roll(x, 1, axis=0))
    x_next = jnp.where(ridx == tc - 1, next_row, pltpu.roll(x, tc - 1, axis=0))
    y = x_prev * w_ref[0:1, :] + x * w_ref[1:2, :] + x_next * w_ref[2:3, :] + b_ref[...]
    y = _silu(y)
    q_ref[...] = y[:, :B_WIDTH].astype(q_ref.dtype)
    k_ref[...] = (y[:, B_WIDTH:] * (B_DIM ** -0.5)).astype(k_ref.dtype)


def _conv_call(geo, tc, qk, w, b):
    t = geo.t
    r8 = tc // 8
    nb8 = t // 8
    return pl.pallas_call(
        functools.partial(_conv_kernel, geo, tc),
        grid=(t // tc,),
        in_specs=[pl.BlockSpec((tc, 2 * B_WIDTH), lambda i: (i, 0)),
                  pl.BlockSpec((8, 2 * B_WIDTH), lambda i: (jnp.maximum(i * r8 - 1, 0), 0)),
                  pl.BlockSpec((8, 2 * B_WIDTH), lambda i: (jnp.minimum((i + 1) * r8, nb8 - 1), 0)),
                  pl.BlockSpec((3, 2 * B_WIDTH), lambda i: (0, 0)),
                  pl.BlockSpec((1, 2 * B_WIDTH), lambda i: (0, 0))],
        out_specs=[pl.BlockSpec((tc, B_WIDTH), lambda i: (i, 0))] * 2,
        out_shape=[jax.ShapeDtypeStruct((t, B_WIDTH), ACT_DT)] * 2,
        compiler_params=_cparams(("parallel",)),
        name="mlstm_conv",
    )(qk, qk, qk, w, b)


def _attn_finish(lam_init, acc0, l0, acc1, l1, dl_ref, ng_ref, o_ref):
    dl = dl_ref[...]
    lam = (jnp.exp(jnp.sum(dl[0:1] * dl[1:2], axis=-1, keepdims=True))
           - jnp.exp(jnp.sum(dl[2:3] * dl[3:4], axis=-1, keepdims=True)) + lam_init)
    out = acc0 / l0 - lam * (acc1 / l1)
    o_ref[...] = (_rms(out, ng_ref[...]) * (1.0 - lam_init)).astype(o_ref.dtype)


def _attn_kernel(lam_init, nk, q_ref, k_ref, v_ref, bias_ref, sc_ref, dl_ref, ng_ref, o_ref,
                 m_sc, l_sc, acc_sc):
    h = pl.program_id(1)
    i = pl.program_id(2)
    j = pl.program_id(3)

    @pl.when(j == 0)
    def _():
        m_sc[...] = jnp.full(m_sc.shape, NEG_BIG, F32)
        l_sc[...] = jnp.zeros(l_sc.shape, F32)
        acc_sc[...] = jnp.zeros(acc_sc.shape, F32)

    q = q_ref[...]
    lo_lane = lax.broadcasted_iota(jnp.int32, (1, LANES), 1) < A_QK_DIM
    qsub = (jnp.where(lo_lane, q, jnp.zeros_like(q)), jnp.where(lo_lane, jnp.zeros_like(q), q))
    k = k_ref[...]
    v = v_ref[...]

    def step(near):
        if near:
            shift = 0.0
        else:
            shift = jnp.where(j < i, sc_ref[h, 0], sc_ref[h, 1])
        for sub in range(2):
            s = _mm_nt(qsub[sub], k)
            if near:
                s = s + bias_ref[...]
            m_old = m_sc[sub]
            m_new = jnp.maximum(m_old, jnp.max(s, axis=-1, keepdims=True) + shift)
            p = jnp.exp2(s - (m_new - shift))
            alpha = jnp.exp2(m_old - m_new)
            l_sc[sub] = alpha * l_sc[sub] + jnp.sum(p, axis=-1, keepdims=True)
            acc_sc[sub] = alpha * acc_sc[sub] + _mm(p, v)
            m_sc[sub] = m_new

    near = jnp.abs(j - i) <= 1
    pl.when(near)(lambda: step(True))
    pl.when(jnp.logical_not(near))(lambda: step(False))

    @pl.when(j == nk - 1)
    def _():
        _attn_finish(lam_init, acc_sc[0], l_sc[0], acc_sc[1], l_sc[1], dl_ref, ng_ref, o_ref)


def _attn_bounded_kernel(lam_init, nk, q_ref, k_ref, v_ref, bias_ref, sc_ref, dl_ref, ng_ref, o_ref, acc_sc):
    h = pl.program_id(1)
    i = pl.program_id(2)
    j = pl.program_id(3)

    @pl.when(j == 0)
    def _():
        acc_sc[...] = jnp.zeros(acc_sc.shape, F32)

    @pl.when(j == jnp.maximum(i - 1, 0))
    def _():
        acc_sc[...] = acc_sc[...] * sc_ref[h, 2]

    @pl.when(j == i + 2)
    def _():
        acc_sc[...] = acc_sc[...] * sc_ref[h, 3]

    q = q_ref[...]
    k = k_ref[...]
    v = v_ref[...]
    lo = (lax.broadcasted_iota(jnp.int32, q.shape, 1) < A_QK_DIM).astype(F32).astype(q.dtype)
    qsub = (q * lo, q * (1 - lo))
    ones_col = (lax.broadcasted_iota(jnp.int32, v.shape, 1) == 0).astype(v.dtype)
    v_aug = jnp.concatenate([v, ones_col], axis=1)

    def step(near):
        for sub in range(2):
            s = _mm_nt(qsub[sub], k)
            if near:
                s = s + bias_ref[...]
            acc_sc[sub] += _mm(jnp.exp2(s), v_aug)

    near = jnp.abs(j - i) <= 1
    pl.when(near)(lambda: step(True))
    pl.when(jnp.logical_not(near))(lambda: step(False))

    @pl.when(j == nk - 1)
    def _():
        a0 = acc_sc[0]
        a1 = acc_sc[1]
        _attn_finish(lam_init, a0[:, :A_V_DIM], a0[:, A_V_DIM:A_V_DIM + 1], a1[:, :A_V_DIM],
                     a1[:, A_V_DIM:A_V_DIM + 1], dl_ref, ng_ref, o_ref)


def _attn_scalars(bias, tq, q_gain, k_gain):
    far_l = bias[:, 0, tq - 1, 0]
    far_r = bias[:, 2, 0, tq - 1]
    sc = jnp.stack([far_l, far_r, jnp.exp2(far_l), jnp.exp2(-far_r)], axis=-1)
    bound = A_QK_DIM * jnp.max(jnp.abs(q_gain)) * jnp.max(jnp.abs(k_gain)) * (A_QK_DIM ** -0.5) * LOG2E * 1.02
    spread = -jnp.min(bias)
    ok = bound + 2.0 * spread <= 80.0
    return sc, ok


def _attn_call(geo, tq, lam_init, bounded, qa, ka, va, bias, sc, dl, ng):
    outs = []
    for (nb, s, row_off) in ((geo.bp, geo.sp, 0), (geo.bs, geo.ss, geo.tp)):
        nq = s // tq
        off = row_off // tq
        qmap = lambda b, h, i, j, off=off, nq=nq: (off + b * nq + i, h)
        kmap = lambda b, h, i, j, off=off, nq=nq: (off + b * nq + j, h)
        omap = lambda b, h, i, j, nq=nq: (b * nq + i, h)
        if bounded:
            body = functools.partial(_attn_bounded_kernel, lam_init, nq)
            scratch = [pltpu.VMEM((2, tq, 2 * LANES), F32)]
        else:
            body = functools.partial(_attn_kernel, lam_init, nq)
            scratch = [pltpu.VMEM((2, tq, 1), F32), pltpu.VMEM((2, tq, 1), F32), pltpu.VMEM((2, tq, LANES), F32)]
        outs.append(pl.pallas_call(
            body,
            grid=(nb, A_HEADS, nq, nq),
            in_specs=[pl.BlockSpec((tq, LANES), qmap),
                      pl.BlockSpec((tq, LANES), kmap),
                      pl.BlockSpec((tq, LANES), kmap),
                      pl.BlockSpec((None, None, tq, tq), lambda b, h, i, j: (h, jnp.clip(j - i + 1, 0, 2), 0, 0)),
                      pl.BlockSpec(memory_space=pltpu.SMEM),
                      pl.BlockSpec((4, A_QK_DIM), lambda b, h, i, j: (0, 0)),
                      pl.BlockSpec((1, LANES), lambda b, h, i, j: (0, 0))],
            out_specs=pl.BlockSpec((tq, LANES), omap),
            out_shape=jax.ShapeDtypeStruct((nb * s, A_WIDTH), ACT_DT),
            scratch_shapes=scratch,
            compiler_params=_cparams(("parallel", "parallel", "parallel", "arbitrary")),
            name="diff_attn_bounded" if bounded else "diff_attn",
        )(qa, ka, va, bias, sc, dl, ng))
    return jnp.concatenate(outs, axis=0)


def _tri(lower):
    r = lax.broadcasted_iota(jnp.int32, (CHUNK, CHUNK), 0)
    c = lax.broadcasted_iota(jnp.int32, (CHUNK, CHUNK), 1)
    return (c <= r) if lower else (c >= r)


def _mlstm_kernel(geo, tb, reverse, *refs):
    if reverse:
        q_ref, k_ref, v_ref, g_ref, hf_ref, ob_ref, ng_ref, o_ref, c_sc, m_sc = refs
    else:
        q_ref, k_ref, v_ref, g_ref, o_ref, c_sc, m_sc = refs
    step = pl.program_id(0)
    nblk = geo.t // tb
    blk = (nblk - 1 - step) if reverse else step
    row0 = blk * tb
    if reverse:
        fresh = geo.seq_start(row0 + tb) | (row0 + tb == geo.t)
    else:
        fresh = geo.seq_start(row0)

    @pl.when(fresh)
    def _():
        c_sc[...] = jnp.zeros(c_sc.shape, F32)
        m_sc[...] = jnp.zeros(m_sc.shape, F32)

    mask = _tri(not reverse)
    cum_l = mask.astype(F32)
    cum_r = _tri(reverse).astype(F32)
    ones_col = (lax.broadcasted_iota(jnp.int32, (CHUNK, LANES), 1) == 0).astype(MXU_DT)
    nch = tb // CHUNK

    def chunk(ci, carry):
        c_idx = (nch - 1 - ci) if reverse else ci
        r0 = pl.multiple_of(c_idx * CHUNK, CHUNK)
        g = g_ref[pl.ds(r0, CHUNK), :]
        g_t = g.T
        b_col = _mm_hi(cum_l, _log_sigmoid(g))
        b_row = _mm_hi(_log_sigmoid(g_t), cum_r)
        for hd in range(B_HEADS):
            ci_col = hd * N_GATES + (2 if reverse else 0)
            cf_col = ci_col + 1
            lanes = slice(hd * B_DIM, (hd + 1) * B_DIM)
            q = q_ref[pl.ds(r0, CHUNK), lanes]
            k = k_ref[pl.ds(r0, CHUNK), lanes]
            v = v_ref[pl.ds(r0, CHUNK), lanes]
            bc = b_col[:, cf_col:cf_col + 1]
            br = b_row[cf_col:cf_col + 1, :]
            ic = g[:, ci_col:ci_col + 1]
            ir = g_t[ci_col:ci_col + 1, :]
            m_prev = m_sc[hd]
            log_d = jnp.where(mask, bc - br + ir, NEG_BIG)
            m_inter = bc + m_prev
            m_t = jnp.maximum(jnp.max(log_d, axis=-1, keepdims=True), m_inter)
            s = _mm_nt(q, k) * jnp.exp(log_d - m_t)
            inter = jnp.exp(m_inter - m_t)
            c_aug = c_sc[hd]
            qc = _mm(q, c_aug)
            num = _mm(s, v) + inter * qc[:, :B_DIM]
            den = jnp.sum(s, axis=-1, keepdims=True) + inter * qc[:, B_DIM:B_DIM + 1]
            hout = num / jnp.maximum(jnp.abs(den), jnp.exp(-m_t))
            b_last = bc[0:1, :] if reverse else bc[CHUNK - 1:CHUNK, :]
            log_w = b_last - bc + ic
            m_new = jnp.maximum(b_last + m_prev, jnp.max(log_w, axis=0, keepdims=True))
            w = jnp.exp(log_w - m_new)
            decay = jnp.exp(b_last + m_prev - m_new)
            v_aug = jnp.concatenate([v, ones_col], axis=1)
            c_sc[hd] = decay * c_aug + _mm_tn(k.astype(F32) * w, v_aug)
            m_sc[hd] = m_new
            if reverse:
                hsum = hf_ref[pl.ds(r0, CHUNK), lanes] + hout
                y = _rms(hsum, ng_ref[...]) * _sigmoid(ob_ref[pl.ds(r0, CHUNK), lanes])
                o_ref[pl.ds(r0, CHUNK), lanes] = y.astype(o_ref.dtype)
            else:
                o_ref[pl.ds(r0, CHUNK), lanes] = hout
        return carry

    lax.fori_loop(0, nch, chunk, 0)


def _mlstm_call(geo, tb, qb, kb, vb, gates, ob, ng):
    t = geo.t
    nblk = t // tb
    fmap = lambda s: (s, 0)
    rmap = lambda s: (nblk - 1 - s, 0)
    scratch = [pltpu.VMEM((B_HEADS, B_DIM, 2 * B_DIM), F32), pltpu.VMEM((B_HEADS, 1, 1), F32)]
    wide = lambda m: pl.BlockSpec((tb, B_WIDTH), m)
    hf = pl.pallas_call(
        functools.partial(_mlstm_kernel, geo, tb, False),
        grid=(nblk,),
        in_specs=[wide(fmap), wide(fmap), wide(fmap), pl.BlockSpec((tb, LANES), fmap)],
        out_specs=wide(fmap),
        out_shape=jax.ShapeDtypeStruct((t, B_WIDTH), F32),
        scratch_shapes=scratch,
        compiler_params=_cparams(("arbitrary",)),
        name="mlstm_fwd",
    )(qb, kb, vb, gates)
    return pl.pallas_call(
        functools.partial(_mlstm_kernel, geo, tb, True),
        grid=(nblk,),
        in_specs=[wide(rmap), wide(rmap), wide(rmap), pl.BlockSpec((tb, LANES), rmap), wide(rmap), wide(rmap),
                  pl.BlockSpec((1, B_DIM), lambda s: (0, 0))],
        out_specs=wide(rmap),
        out_shape=jax.ShapeDtypeStruct((t, B_WIDTH), ACT_DT),
        scratch_shapes=scratch,
        compiler_params=_cparams(("arbitrary",)),
        name="mlstm_bwd",
    )(qb, kb, vb, gates, hf, ob, ng)


def _inproj_c_kernel(x_ref, mod_ref, g_ref, w_ref, cos_ref, sin_ref, q_ref, k_ref, v_ref, gt_ref, h_sc):
    j = pl.program_id(1)
    nsub = C_HEADS
    sub = w_ref.shape[1] // nsub

    def proj(c):
        return jnp.dot(h_sc[...], w_ref[:, c * sub:(c + 1) * sub], preferred_element_type=F32)

    def rope(o_ref, scale):
        cos = cos_ref[...] * scale
        sin = sin_ref[...] * scale
        half = C_QK_DIM // 2
        for hd in range(C_HEADS):
            y = proj(hd)
            x1, x2 = y[:, :half], y[:, half:]
            o_ref[:, hd * C_QK_DIM:hd * C_QK_DIM + half] = (x1 * cos - x2 * sin).astype(o_ref.dtype)
            o_ref[:, hd * C_QK_DIM + half:(hd + 1) * C_QK_DIM] = (x1 * sin + x2 * cos).astype(o_ref.dtype)

    @pl.when(j == 0)
    def _():
        x = x_ref[...]
        h = _rms(x, g_ref[...]) * (1.0 + mod_ref[1:2, :]) + mod_ref[0:1, :]
        h_sc[...] = h.astype(h_sc.dtype)
        rope(q_ref, 1.0)

    @pl.when(j == 1)
    def _():
        rope(k_ref, C_QK_DIM ** -0.5)

    @pl.when((j == 2) | (j == 3))
    def _():
        for c in range(nsub):
            v_ref[:, c * sub:(c + 1) * sub] = proj(c).astype(v_ref.dtype)

    @pl.when(j >= 4)
    def _():
        for c in range(nsub):
            gt_ref[:, c * sub:(c + 1) * sub] = _silu(proj(c)).astype(gt_ref.dtype)


def _inproj_c_call(geo, tm, x, mod_l, g, w, cos, sin):
    t = geo.t
    per_seg = geo.seg // tm
    nw = D_MODEL
    return pl.pallas_call(
        _inproj_c_kernel,
        grid=(t // tm, C_IN // nw),
        in_specs=[pl.BlockSpec((tm, D_MODEL), lambda i, j: (i, 0)),
                  pl.BlockSpec((None, 6, D_MODEL), lambda i, j: (i // per_seg, 0, 0)),
                  pl.BlockSpec((1, D_MODEL), lambda i, j: (0, 0)),
                  pl.BlockSpec((D_MODEL, nw), lambda i, j: (0, j)),
                  pl.BlockSpec((tm, C_QK_DIM // 2), lambda i, j: (geo.pos_block(i, tm), 0)),
                  pl.BlockSpec((tm, C_QK_DIM // 2), lambda i, j: (geo.pos_block(i, tm), 0))],
        out_specs=[pl.BlockSpec((tm, nw), lambda i, j: (i, 0)),
                   pl.BlockSpec((tm, nw), lambda i, j: (i, 0)),
                   pl.BlockSpec((tm, nw), lambda i, j: (i, jnp.clip(j - 2, 0, 1))),
                   pl.BlockSpec((tm, nw), lambda i, j: (i, jnp.clip(j - 4, 0, 1)))],
        out_shape=[jax.ShapeDtypeStruct((t, C_QK_WIDTH), ACT_DT),
                   jax.ShapeDtypeStruct((t, C_QK_WIDTH), ACT_DT),
                   jax.ShapeDtypeStruct((t, C_V_WIDTH), ACT_DT),
                   jax.ShapeDtypeStruct((t, C_V_WIDTH), ACT_DT)],
        scratch_shapes=[pltpu.VMEM((tm, D_MODEL), MXU_DT)],
        compiler_params=_cparams(("parallel", "arbitrary")),
        name="inproj_c",
    )(x, mod_l, g, w, cos, sin)


def _ret_kernel(geo, tb, reverse, *refs):
    if reverse:
        q_ref, k_ref, v_ref, dlg_ref, yf_ref, gt_ref, ng_ref, o_ref, r_sc, intra_sc, vec_sc = refs
    else:
        q_ref, k_ref, v_ref, dlg_ref, o_ref, r_sc, intra_sc, vec_sc = refs
    step = pl.program_id(0)
    nblk = geo.t // tb
    blk = (nblk - 1 - step) if reverse else step
    row0 = blk * tb
    if reverse:
        fresh = geo.seq_start(row0 + tb) | (row0 + tb == geo.t)
    else:
        fresh = geo.seq_start(row0)

    @pl.when(fresh)
    def _():
        r_sc[...] = jnp.zeros(r_sc.shape, F32)

    @pl.when(step == 0)
    def _():
        lg_all = _log_sigmoid(dlg_ref[...])
        ti = lax.broadcasted_iota(jnp.int32, (RET_CHUNK, RET_CHUNK), 0)
        si = lax.broadcasted_iota(jnp.int32, (RET_CHUNK, RET_CHUNK), 1)
        dist = ((si - ti) if reverse else (ti - si)).astype(F32)
        pos = lax.broadcasted_iota(jnp.int32, (RET_CHUNK, LANES), 0).astype(F32)
        upos = (RET_CHUNK - 1.0 - pos) if reverse else pos
        lane = lax.broadcasted_iota(jnp.int32, (RET_CHUNK, LANES), 1)
        d = 1 if reverse else 0
        for hd in range(C_HEADS):
            lg = lg_all[d:d + 1, hd:hd + 1]
            intra_sc[hd] = jnp.where(dist >= 0, jnp.exp(jnp.maximum(dist, 0.0) * lg), 0.0)
            vec_sc[hd] = jnp.where(lane == 0, jnp.exp((upos + 1.0) * lg),
                                   jnp.where(lane == 1, jnp.exp((RET_CHUNK - 1.0 - upos) * lg),
                                             jnp.exp(RET_CHUNK * lg)))

    nch = tb // RET_CHUNK

    def chunk(ci, carry):
        c_idx = (nch - 1 - ci) if reverse else ci
        r0 = pl.multiple_of(c_idx * RET_CHUNK, RET_CHUNK)
        for hd in range(C_HEADS):
            vec = vec_sc[hd]
            q_scale, k_scale, c_decay = vec[:, 0:1], vec[:, 1:2], vec[0:1, 2:3]
            ql = slice(hd * C_QK_DIM, (hd + 1) * C_QK_DIM)
            vl = slice(hd * C_V_DIM, (hd + 1) * C_V_DIM)
            q = q_ref[pl.ds(r0, RET_CHUNK), ql]
            k = k_ref[pl.ds(r0, RET_CHUNK), ql]
            v = v_ref[pl.ds(r0, RET_CHUNK), vl]
            r_old = r_sc[hd]
            s = _mm_nt(q, k) * intra_sc[hd]
            y = _mm(s, v) + q_scale * _mm(q, r_old)
            r_sc[hd] = c_decay * r_old + _mm_tn(k.astype(F32) * k_scale, v)
            if reverse:
                ysum = yf_ref[pl.ds(r0, RET_CHUNK), vl] + y
                out = _rms(ysum, ng_ref[...]) * gt_ref[pl.ds(r0, RET_CHUNK), vl].astype(F32)
                o_ref[pl.ds(r0, RET_CHUNK), vl] = out.astype(o_ref.dtype)
            else:
                o_ref[pl.ds(r0, RET_CHUNK), vl] = y
        return carry

    lax.fori_loop(0, nch, chunk, 0)


def _ret_call(geo, tb, q, k, v, decay_logit, gt, ng):
    t = geo.t
    nblk = t // tb
    fmap = lambda s: (s, 0)
    rmap = lambda s: (nblk - 1 - s, 0)
    scratch = [pltpu.VMEM((C_HEADS, C_QK_DIM, C_V_DIM), F32), pltpu.VMEM((C_HEADS, RET_CHUNK, RET_CHUNK), F32),
               pltpu.VMEM((C_HEADS, RET_CHUNK, LANES), F32)]
    qk = lambda m: pl.BlockSpec((tb, C_QK_WIDTH), m)
    vv = lambda m: pl.BlockSpec((tb, C_V_WIDTH), m)
    dspec = pl.BlockSpec((2, C_HEADS), lambda s: (0, 0))
    yf = pl.pallas_call(
        functools.partial(_ret_kernel, geo, tb, False),
        grid=(nblk,),
        in_specs=[qk(fmap), qk(fmap), vv(fmap), dspec],
        out_specs=vv(fmap),
        out_shape=jax.ShapeDtypeStruct((t, C_V_WIDTH), F32),
        scratch_shapes=scratch,
        compiler_params=_cparams(("arbitrary",)),
        name="ret_fwd",
    )(q, k, v, decay_logit)
    return pl.pallas_call(
        functools.partial(_ret_kernel, geo, tb, True),
        grid=(nblk,),
        in_specs=[qk(rmap), qk(rmap), vv(rmap), dspec, vv(rmap), vv(rmap),
                  pl.BlockSpec((1, C_V_DIM), lambda s: (0, 0))],
        out_specs=vv(rmap),
        out_shape=jax.ShapeDtypeStruct((t, C_V_WIDTH), ACT_DT),
        scratch_shapes=scratch,
        compiler_params=_cparams(("arbitrary",)),
        name="ret_bwd",
    )(q, k, v, decay_logit, yf, gt, ng)


def _route(probs):
    p = [probs[e:e + 1, :] for e in range(N_EXPERTS)]
    scores = []
    for g in range(N_GROUPS):
        a, b, c, d = p[EPG * g:EPG * g + EPG]
        hi1, lo1 = jnp.maximum(a, b), jnp.minimum(a, b)
        hi2, lo2 = jnp.maximum(c, d), jnp.minimum(c, d)
        scores.append(jnp.maximum(hi1, hi2) + jnp.maximum(jnp.minimum(hi1, hi2), jnp.maximum(lo1, lo2)))
    g_sel = jnp.zeros(scores[0].shape, jnp.int32)
    best = scores[0]
    for g in range(1, N_GROUPS):
        better = scores[g] > best
        g_sel = jnp.where(better, g, g_sel)
        best = jnp.where(better, scores[g], best)
    vals = []
    for kk in range(EPG):
        v = p[kk]
        for g in range(1, N_GROUPS):
            v = jnp.where(g_sel == g, p[EPG * g + kk], v)
        vals.append(v)

    def argmax4(xs):
        idx = jnp.zeros(xs[0].shape, jnp.int32)
        top = xs[0]
        for kk in range(1, EPG):
            better = xs[kk] > top
            idx = jnp.where(better, kk, idx)
            top = jnp.where(better, xs[kk], top)
        return idx, top

    i1, v1 = argmax4(vals)
    i2, v2 = argmax4([jnp.where(i1 == kk, -1.0, vals[kk]) for kk in range(EPG)])
    tot = v1 + v2
    w1, w2 = v1 / tot, v2 / tot
    e1 = g_sel * EPG + i1
    e2 = g_sel * EPG + i2
    eidx = lax.broadcasted_iota(jnp.int32, probs.shape, 0)
    lo, hi = jnp.minimum(i1, i2), jnp.maximum(i1, i2)
    pair = jnp.where(lo == 0, hi - 1, jnp.where(lo == 1, hi + 1, N_PAIRS - 1))
    return jnp.where(eidx == e1, w1, 0.0) + jnp.where(eidx == e2, w2, 0.0), g_sel * N_PAIRS + pair


def _outproj_kernel(nparts, *refs):
    y_refs = refs[:nparts]
    w_refs = refs[nparts:2 * nparts]
    x_ref, mod_ref, g_ref, rw_ref, rb_ref, xo_ref, hx_ref, bkt_ref = refs[2 * nparts:]
    m = jnp.dot(y_refs[0][...], w_refs[0][...], preferred_element_type=F32)
    for p in range(1, nparts):
        m = m + jnp.dot(y_refs[p][...], w_refs[p][...], preferred_element_type=F32)
    x = x_ref[...] + mod_ref[2:3, :] * m
    xo_ref[...] = x
    h = _rms(x, g_ref[...]) * (1.0 + mod_ref[4:5, :]) + mod_ref[3:4, :]
    hx_ref[:, :D_MODEL] = h
    logits = lax.dot_general(rw_ref[...], h, (((1,), (1,)), ((), ())), precision=HI,
                             preferred_element_type=F32) + rb_ref[...]
    z = jnp.exp(logits - jnp.max(logits, axis=0, keepdims=True))
    probs = z / jnp.sum(z, axis=0, keepdims=True)
    cmb, bucket = _route(probs)
    pad = jnp.zeros((LANES - N_EXPERTS, cmb.shape[1]), F32)
    hx_ref[:, D_MODEL:] = jnp.concatenate([cmb, pad], axis=0).T
    bkt_ref[...] = bucket


def _outproj_call(geo, tm, ys, ws, x, mod_l, g, rw_t, rb):
    t = geo.t
    per_seg = geo.seg // tm
    row = lambda i: (i, 0)
    const = lambda i: (0, 0)
    n = len(ys)
    return pl.pallas_call(
        functools.partial(_outproj_kernel, n),
        grid=(t // tm,),
        in_specs=([pl.BlockSpec((tm, y.shape[1]), row) for y in ys]
                  + [pl.BlockSpec(w.shape, const) for w in ws]
                  + [pl.BlockSpec((tm, D_MODEL), row),
                     pl.BlockSpec((None, 6, D_MODEL), lambda i: (i // per_seg, 0, 0)),
                     pl.BlockSpec((1, D_MODEL), const),
                     pl.BlockSpec((N_EXPERTS, D_MODEL), const),
                     pl.BlockSpec((N_EXPERTS, 1), const)]),
        out_specs=[pl.BlockSpec((tm, D_MODEL), row), pl.BlockSpec((tm, HX_W), row),
                   pl.BlockSpec((None, 1, tm), lambda i: (i, 0, 0))],
        out_shape=[jax.ShapeDtypeStruct((t, D_MODEL), F32), jax.ShapeDtypeStruct((t, HX_W), F32),
                   jax.ShapeDtypeStruct((t // tm, 1, tm), jnp.int32)],
        compiler_params=_cparams(("parallel",)),
        name="outproj_router",
    )(*ys, *ws, x, mod_l, g, rw_t, rb)


def _plan_kernel(tm, bkt_ref, pos_ref, te_ref, nv_ref):
    nblk, _, blk = bkt_ref.shape
    nrow = 32
    bid = lax.broadcasted_iota(jnp.int32, (nrow, blk), 0)
    r = lax.broadcasted_iota(jnp.int32, (blk, blk), 0)
    c = lax.broadcasted_iota(jnp.int32, (blk, blk), 1)
    prefix = (r <= c).astype(MXU_DT)

    def count(b, acc):
        return acc + jnp.sum((bkt_ref[b] == bid).astype(F32), axis=1, keepdims=True)

    counts = lax.fori_loop(0, nblk, count, jnp.zeros((nrow, 1), F32))
    padded = jnp.floor((counts + (tm - 1.0)) / tm) * tm
    rows = lax.broadcasted_iota(jnp.int32, (nrow, 1), 0)
    offs = jnp.zeros((nrow, 1), F32)
    ends = []
    run = jnp.zeros((1, 1), F32)
    for b in range(N_BUCKETS):
        offs = jnp.where(rows == b, run, offs)
        run = run + padded[b:b + 1, :]
        ends.append(run)

    def place(b, carry):
        oh = (bkt_ref[b] == bid).astype(F32)
        pre = jnp.dot(oh.astype(MXU_DT), prefix, preferred_element_type=F32)
        pos = jnp.sum(oh * (offs + carry + pre - 1.0), axis=0, keepdims=True)
        pos_ref[b] = pos.astype(jnp.int32)
        return carry + pre[:, blk - 1:blk]

    lax.fori_loop(0, nblk, place, jnp.zeros((nrow, 1), F32))
    start = lax.broadcasted_iota(jnp.int32, (1, te_ref.shape[1]), 1).astype(F32) * tm
    tb = jnp.zeros(start.shape, F32)
    for b in range(N_BUCKETS - 1):
        tb = tb + (ends[b] <= start).astype(F32)
    grp = jnp.floor((tb + 0.5) / N_PAIRS)
    pair = tb - grp * N_PAIRS
    lo = (pair >= 3).astype(F32) + (pair >= 5).astype(F32)
    hi = jnp.where(pair == 0, 1.0, jnp.where((pair == 1) | (pair == 3), 2.0, 3.0))
    te_ref[0:1, :] = (grp * EPG + lo).astype(jnp.int32)
    te_ref[1:2, :] = (grp * EPG + hi).astype(jnp.int32)
    nv_ref[...] = jnp.broadcast_to(run / tm, nv_ref.shape).astype(jnp.int32)


def _plan_call(tm, bkt):
    nblk, _, blk = bkt.shape
    nt = nblk * blk // tm + N_BUCKETS
    ntp = -(-nt // LANES) * LANES
    pos, te, nv = pl.pallas_call(
        functools.partial(_plan_kernel, tm),
        out_shape=[jax.ShapeDtypeStruct(bkt.shape, jnp.int32), jax.ShapeDtypeStruct((2, ntp), jnp.int32),
                   jax.ShapeDtypeStruct((1, LANES), jnp.int32)],
        compiler_params=pltpu.CompilerParams(vmem_limit_bytes=VMEM_LIMIT),
        name="moe_plan",
    )(bkt)
    return pos, te[0, :nt], te[1, :nt], nv[0, :1]


def _dispatch_kernel(tb, pos_ref, hx_ref, init_ref, xs_ref, sem):
    del init_ref

    def issue(g, carry):
        base = pl.multiple_of(g * SUBLANES, SUBLANES)
        rows = hx_ref.at[pl.ds(base, SUBLANES)]
        for r in range(SUBLANES):
            pltpu.make_async_copy(rows.at[pl.ds(r, 1)], xs_ref.at[pl.ds(pos_ref[0, base + r], 1)],
                                  sem).start(priority=r % 2)
        return carry

    lax.fori_loop(0, tb // SUBLANES, issue, 0)
    pltpu.make_async_copy(hx_ref, xs_ref.at[pl.ds(0, tb)], sem).wait()


def _dispatch_call(tb, pos, hx, xs_init):
    nsteps = hx.shape[0] // tb
    return pl.pallas_call(
        functools.partial(_dispatch_kernel, tb),
        grid=(nsteps,),
        in_specs=[pl.BlockSpec((None, 1, tb), lambda i: (i, 0, 0), memory_space=pltpu.SMEM),
                  pl.BlockSpec((tb, HX_W), lambda i: (i, 0)),
                  pl.BlockSpec(memory_space=pl.ANY)],
        out_specs=pl.BlockSpec(memory_space=pl.ANY),
        out_shape=jax.ShapeDtypeStruct(xs_init.shape, xs_init.dtype),
        scratch_shapes=[pltpu.SemaphoreType.DMA(())],
        input_output_aliases={2: 0},
        compiler_params=pltpu.CompilerParams(dimension_semantics=("arbitrary",), disable_bounds_checks=True,
                                             has_side_effects=True),
        name="moe_dispatch",
    )(pos, hx, xs_init)


def _moe_kernel(te0_ref, te1_ref, nv_ref, xs_ref, w1a_ref, w3a_ref, w2a_ref, w1b_ref, w3b_ref, w2b_ref, o_ref):
    n = pl.program_id(0)

    @pl.when(n < nv_ref[0])
    def _():
        h = xs_ref[:, :D_MODEL].astype(MXU_DT)
        cmb = xs_ref[:, D_MODEL:]
        lane = lax.broadcasted_iota(jnp.int32, cmb.shape, 1)

        def expert(e_id, w1_ref, w3_ref, w2_ref):
            a = jnp.dot(h, w1_ref[...], preferred_element_type=F32)
            b = jnp.dot(h, w3_ref[...], preferred_element_type=F32)
            c = jnp.sum(jnp.where(lane == e_id, cmb, 0.0), axis=-1, keepdims=True)
            return c * _mm(_silu(a) * b, w2_ref[...])

        o_ref[...] = (expert(te0_ref[n], w1a_ref, w3a_ref, w2a_ref)
                      + expert(te1_ref[n], w1b_ref, w3b_ref, w2b_ref))

    @pl.when(n >= nv_ref[0])
    def _():
        o_ref[...] = jnp.zeros(o_ref.shape, F32)


def _moe_call(tm, layer, te0, te1, nv, xs, w1, w3, w2):
    rows = xs.shape[0]
    nt = rows // tm
    tile = lambda n, te0, te1, nv: (jnp.minimum(n, nv[0] - 1), 0)
    wa = lambda n, te0, te1, nv: (layer, te0[jnp.minimum(n, nv[0] - 1)], 0, 0)
    wb = lambda n, te0, te1, nv: (layer, te1[jnp.minimum(n, nv[0] - 1)], 0, 0)
    up = lambda m: pl.BlockSpec((None, None, D_MODEL, D_FF), m)
    down = lambda m: pl.BlockSpec((None, None, D_FF, D_MODEL), m)
    return pl.pallas_call(
        _moe_kernel,
        grid_spec=pltpu.PrefetchScalarGridSpec(
            num_scalar_prefetch=3,
            grid=(nt,),
            in_specs=[pl.BlockSpec((tm, HX_W), tile), up(wa), up(wa), down(wa), up(wb), up(wb), down(wb)],
            out_specs=pl.BlockSpec((tm, D_MODEL), lambda n, te0, te1, nv: (n, 0))),
        out_shape=jax.ShapeDtypeStruct((rows, D_MODEL), F32),
        compiler_params=_cparams(("arbitrary",)),
        name="moe",
    )(te0, te1, nv, xs, w1, w3, w2, w1, w3, w2)


def _collect_kernel(tb, nsteps, pos_ref, posn_ref, ys_ref, x_ref, mod_ref, o_ref, buf, sem):
    i = pl.program_id(0)
    slot = i % 2

    def issue(p_ref, s):
        def body(g, carry):
            base = pl.multiple_of(g * SUBLANES, SUBLANES)
            rows = buf.at[s, pl.ds(base, SUBLANES)]
            for r in range(SUBLANES):
                pltpu.make_async_copy(ys_ref.at[pl.ds(p_ref[0, base + r], 1)], rows.at[pl.ds(r, 1)],
                                      sem.at[s]).start(priority=r % 2)
            return carry
        lax.fori_loop(0, tb // SUBLANES, body, 0)

    pl.when(i == 0)(lambda: issue(pos_ref, 0))
    pl.when(i + 1 < nsteps)(lambda: issue(posn_ref, 1 - slot))
    pltpu.make_async_copy(ys_ref.at[pl.ds(0, tb)], buf.at[slot], sem.at[slot]).wait()
    o_ref[...] = x_ref[...] + mod_ref[5:6, :] * buf[slot]


def _collect_call(geo, tb, pos, ys, x, mod_l):
    t = geo.t
    nsteps = t // tb
    per_seg = geo.seg // tb
    return pl.pallas_call(
        functools.partial(_collect_kernel, tb, nsteps),
        grid=(nsteps,),
        in_specs=[pl.BlockSpec((None, 1, tb), lambda i: (i, 0, 0), memory_space=pltpu.SMEM),
                  pl.BlockSpec((None, 1, tb), lambda i: (jnp.minimum(i + 1, nsteps - 1), 0, 0),
                               memory_space=pltpu.SMEM),
                  pl.BlockSpec(memory_space=pl.ANY),
                  pl.BlockSpec((tb, D_MODEL), lambda i: (i, 0)),
                  pl.BlockSpec((None, 6, D_MODEL), lambda i: (i // per_seg, 0, 0))],
        out_specs=pl.BlockSpec((tb, D_MODEL), lambda i: (i, 0)),
        out_shape=jax.ShapeDtypeStruct((t, D_MODEL), F32),
        scratch_shapes=[pltpu.VMEM((2, tb, D_MODEL), F32), pltpu.SemaphoreType.DMA((2,))],
        compiler_params=pltpu.CompilerParams(dimension_semantics=("arbitrary",), vmem_limit_bytes=VMEM_LIMIT,
                                             disable_bounds_checks=True),
        name="moe_collect",
    )(pos, pos, ys, x, mod_l)


def _tiles(geo):
    seg = geo.seg
    return dict(tm=min(512, seg), tmc=min(1024, seg), tq=min(1024, seg), tscan=min(512, seg), tmoe=min(512, seg))


def _forward(geo, tiles, x_prompt, x_sample, c_prompt, c_sample, rel_bias, router_w, router_b, ada_w, ada_b,
             norm_mix_g, norm_ffn_g, w_in_ab, w_out_ab, q_norm_g, k_norm_g, diff_lambda, diff_norm_g,
             mlstm_conv_w, mlstm_conv_b, mlstm_gate_b, mlstm_norm_g, w_in_c, w_out_c, ret_decay_logit,
             ret_norm_g, moe_w1, moe_w3, moe_w2):
    tm, tmc, tq, tscan, tmoe = tiles["tm"], tiles["tmc"], tiles["tq"], tiles["tscan"], tiles["tmoe"]
    x = jnp.concatenate([x_prompt.reshape(geo.tp, D_MODEL), x_sample.reshape(geo.t - geo.tp, D_MODEL)], axis=0)
    c_rows = jnp.concatenate([jnp.repeat(c_prompt, geo.sp // geo.seg, axis=0),
                              jnp.repeat(c_sample, geo.ss // geo.seg, axis=0)], axis=0)
    mod = _ada_call(c_rows, ada_w, ada_b)

    bias = _bias_call(rel_bias, tq)
    cos, sin = _rope_tables(max(geo.sp, geo.ss))
    rw_t = router_w.T
    rb = router_b.reshape(N_EXPERTS, 1)
    w1 = moe_w1.astype(MXU_DT)
    w3 = moe_w3.astype(MXU_DT)
    w2 = moe_w2.astype(MXU_DT)
    xs = jnp.zeros((geo.t + N_BUCKETS * tmoe, HX_W), F32)

    for l in range(DEPTH):
        j = l // 2
        mod_l = mod[l]
        g_mix = norm_mix_g[l].reshape(1, D_MODEL)
        g_ffn = norm_ffn_g[l].reshape(1, D_MODEL)
        if l % 2 == 0:
            lam_init = 0.8 - 0.6 * math.exp(-0.3 * l)
            w_pad = jnp.pad(w_in_ab[j], ((0, 0), (0, AB_IN_PAD - AB_IN))).astype(MXU_DT)
            qg = jnp.tile(q_norm_g[j], 2).reshape(1, LANES)
            kg = jnp.tile(k_norm_g[j], 2).reshape(1, LANES)
            gate_b = jnp.pad(mlstm_gate_b[j].reshape(1, B_HEADS * N_GATES), ((0, 0), (0, LANES - B_HEADS * N_GATES)))
            qa, ka, va, qk, vb, ob, gates = _inproj_ab_call(geo, tm, x, mod_l, g_mix, w_pad, qg, kg, gate_b)
            sc, bounded_ok = _attn_scalars(bias, tq, q_norm_g[j], k_norm_g[j])
            ya = lax.cond(bounded_ok,
                          functools.partial(_attn_call, geo, tq, lam_init, True),
                          functools.partial(_attn_call, geo, tq, lam_init, False),
                          qa, ka, va, bias, sc, diff_lambda[j], diff_norm_g[j].reshape(1, A_V_DIM))
            qb, kb = _conv_call(geo, tscan, qk, mlstm_conv_w[j], mlstm_conv_b[j].reshape(1, 2 * B_WIDTH))
            yb = _mlstm_call(geo, tscan, qb, kb, vb, gates, ob, mlstm_norm_g[j].reshape(1, B_DIM))
            w_o = w_out_ab[j].astype(MXU_DT)
            ys, ws = [ya, yb], [w_o[:A_WIDTH], w_o[A_WIDTH:]]
        else:
            q, k, v, gt = _inproj_c_call(geo, tmc, x, mod_l, g_mix, w_in_c[j].astype(MXU_DT), cos, sin)
            y = _ret_call(geo, tscan, q, k, v, ret_decay_logit[j], gt, ret_norm_g[j].reshape(1, C_V_DIM))
            ys, ws = [y], [w_out_c[j].astype(MXU_DT)]
        x, hx, bucket = _outproj_call(geo, tm, ys, ws, x, mod_l, g_ffn, rw_t, rb)
        pos, te0, te1, n_tiles = _plan_call(tmoe, bucket)
        xs = _dispatch_call(tm, pos, hx, xs)
        ysort = _moe_call(tmoe, l, te0, te1, n_tiles, xs, w1, w3, w2)
        x = _collect_call(geo, tm, pos, ysort, x, mod_l)

    y_prompt = x[:geo.tp].reshape(x_prompt.shape)
    y_sample = x[geo.tp:].reshape(x_sample.shape)
    return (y_prompt, y_sample)


def kernel(x_prompt, x_sample, c_prompt, c_sample, rel_bias, router_w, router_b, ada_w, ada_b, norm_mix_g, norm_ffn_g, w_in_ab, w_out_ab, q_norm_g, k_norm_g, diff_lambda, diff_norm_g, mlstm_conv_w, mlstm_conv_b, mlstm_gate_b, mlstm_norm_g, w_in_c, w_out_c, ret_decay_logit, ret_norm_g, moe_w1, moe_w3, moe_w2):
    geo = Geo(x_prompt.shape[0], x_prompt.shape[1], x_sample.shape[0], x_sample.shape[1])
    return _forward(geo, _tiles(geo), x_prompt, x_sample, c_prompt, c_sample, rel_bias, router_w, router_b,
                    ada_w, ada_b, norm_mix_g, norm_ffn_g, w_in_ab, w_out_ab, q_norm_g, k_norm_g, diff_lambda,
                    diff_norm_g, mlstm_conv_w, mlstm_conv_b, mlstm_gate_b, mlstm_norm_g, w_in_c, w_out_c,
                    ret_decay_logit, ret_norm_g, moe_w1, moe_w3, moe_w2)
```

```python
import functools
import math

import jax
import jax.numpy as jnp
import numpy as np
from jax import lax
from jax.experimental import pallas as pl
from jax.experimental.pallas import tpu as pltpu

F32 = jnp.float32
MXU_DT = jnp.bfloat16
ACT_DT = jnp.bfloat16
HI = lax.Precision.HIGHEST

D_MODEL = 1024
DEPTH = 4
A_HEADS = 4
A_QK_DIM = 64
A_V_DIM = 128
A_QK_WIDTH = 512
A_WIDTH = 512
B_HEADS = 4
B_DIM = 128
B_WIDTH = 512
N_GATES = 4
AB_IN = 3600
AB_IN_PAD = 3712
C_HEADS = 4
C_QK_DIM = 256
C_V_DIM = 512
C_QK_WIDTH = 1024
C_V_WIDTH = 2048
C_IN = 6144
CHUNK = 128
KB_PER_STEP = 2
RET_CHUNK = 256
REL_BUCKETS = 32
REL_MAX_DIST = 128
N_EXPERTS = 16
N_GROUPS = 4
EPG = 4
N_PAIRS = 6
N_BUCKETS = N_GROUPS * N_PAIRS
D_FF = 512
ROPE_BASE = 10000.0
EPS = 1e-6
LANES = 128
SUBLANES = 8
NEG_BIG = -1e30
LOG2E = math.log2(math.e)
HX_W = D_MODEL + LANES
VMEM_LIMIT = 56 * 1024 * 1024


def _cparams(sem):
    return pltpu.CompilerParams(dimension_semantics=sem, vmem_limit_bytes=VMEM_LIMIT)


def _mm(a, b):
    return jnp.dot(a.astype(MXU_DT), b.astype(MXU_DT), preferred_element_type=F32)


def _mm_nt(a, b):
    return lax.dot_general(a.astype(MXU_DT), b.astype(MXU_DT), (((1,), (1,)), ((), ())),
                           preferred_element_type=F32)


def _mm_tn(a, b):
    return lax.dot_general(a.astype(MXU_DT), b.astype(MXU_DT), (((0,), (0,)), ((), ())),
                           preferred_element_type=F32)


def _mm_hi(a, b):
    return jnp.dot(a, b, precision=HI, preferred_element_type=F32)


def _silu(x):
    return x * (1.0 / (1.0 + jnp.exp(-x)))


def _sigmoid(x):
    return 1.0 / (1.0 + jnp.exp(-x))


def _log_sigmoid(x):
    return jnp.minimum(x, 0.0) - jnp.log1p(jnp.exp(-jnp.abs(x)))


def _rms(x, g):
    return x * lax.rsqrt(jnp.mean(x * x, axis=-1, keepdims=True) + EPS) * g


class Geo:
    def __init__(self, bp, sp, bs, ss):
        self.bp, self.sp, self.bs, self.ss = bp, sp, bs, ss
        self.tp = bp * sp
        self.t = bp * sp + bs * ss
        self.seg = math.gcd(sp, ss)

    def seq_start(self, row0):
        return jnp.where(row0 < self.tp, row0 % self.sp == 0, (row0 - self.tp) % self.ss == 0)

    def pos_block(self, blk, rows):
        nbp = self.tp // rows
        return jnp.where(blk < nbp, blk % (self.sp // rows), (blk - nbp) % (self.ss // rows))


def _ada_kernel(c_ref, w_ref, b_ref, o_ref):
    o_ref[...] = _mm_hi(_silu(c_ref[...]), w_ref[...]) + b_ref[...]


def _ada_call(c_rows, ada_w, ada_b):
    r = c_rows.shape[0]
    nb = 1536
    out = pl.pallas_call(
        _ada_kernel,
        grid=(DEPTH, 6 * D_MODEL // nb),
        in_specs=[pl.BlockSpec((r, D_MODEL), lambda l, n: (0, 0)),
                  pl.BlockSpec((None, D_MODEL, nb), lambda l, n: (l, 0, n)),
                  pl.BlockSpec((None, 1, nb), lambda l, n: (l, 0, n))],
        out_specs=pl.BlockSpec((None, r, nb), lambda l, n: (l, 0, n)),
        out_shape=jax.ShapeDtypeStruct((DEPTH, r, 6 * D_MODEL), F32),
        compiler_params=_cparams(("arbitrary", "arbitrary")),
        name="ada_mod",
    )(c_rows, ada_w, ada_b.reshape(DEPTH, 1, 6 * D_MODEL))
    return out.reshape(DEPTH, r, 6, D_MODEL)


def _t5_bucket(rel):
    nb = REL_BUCKETS // 2
    max_exact = nb // 2
    ret = jnp.where(rel > 0, nb, 0)
    n = jnp.abs(rel)
    nf = jnp.maximum(n, 1).astype(jnp.float32)
    large = max_exact + (jnp.log(nf / max_exact) / math.log(REL_MAX_DIST / max_exact) * (nb - max_exact)).astype(jnp.int32)
    large = jnp.minimum(large, nb - 1)
    return ret + jnp.where(n < max_exact, n, large)


def _bias_kernel(tq, rb_ref, bk_ref, o_ref):
    h = pl.program_id(0)
    bk = bk_ref[...]
    row = jnp.zeros(bk.shape, F32)
    bmax = rb_ref[0, h]
    for b in range(REL_BUCKETS):
        row = row + jnp.where(bk == b, rb_ref[b, h], 0.0)
        bmax = jnp.maximum(bmax, rb_ref[b, h])
    row = (row - bmax) * LOG2E
    table = jnp.broadcast_to(row, (tq, 2 * tq))
    o_ref[...] = pltpu.roll(table, tq + 1, axis=1, stride=1, stride_axis=0)[:, :tq]


def _bias_call(rel_bias, tq):
    rel = (jnp.arange(-1, 2, dtype=jnp.int32) * tq)[:, None, None] + (jnp.arange(2 * tq, dtype=jnp.int32) - (tq - 1))
    buckets = _t5_bucket(rel)
    return pl.pallas_call(
        functools.partial(_bias_kernel, tq),
        grid=(A_HEADS, 3),
        in_specs=[pl.BlockSpec(memory_space=pltpu.SMEM),
                  pl.BlockSpec((None, 1, 2 * tq), lambda h, o: (o, 0, 0))],
        out_specs=pl.BlockSpec((None, None, tq, tq), lambda h, o: (h, o, 0, 0)),
        out_shape=jax.ShapeDtypeStruct((A_HEADS, 3, tq, tq), F32),
        compiler_params=_cparams(("arbitrary", "arbitrary")),
        name="rel_bias_tiles",
    )(rel_bias, buckets)


def _rope_tables(s):
    d = C_QK_DIM
    inv = ROPE_BASE ** (-jnp.arange(0, d, 2, dtype=jnp.float32) / d)
    ang = jnp.arange(s, dtype=jnp.float32)[:, None] * inv[None, :]
    return jnp.cos(ang), jnp.sin(ang)


def _half_rms(z, g):
    lo_lane = lax.broadcasted_iota(jnp.int32, (1, LANES), 1) < A_QK_DIM
    z2 = z * z
    tot = jnp.sum(z2, axis=-1, keepdims=True)
    lo = jnp.sum(jnp.where(lo_lane, z2, 0.0), axis=-1, keepdims=True)
    ms = jnp.where(lo_lane, lo, tot - lo) * (1.0 / A_QK_DIM)
    return z * lax.rsqrt(ms + EPS) * g


def _inproj_ab_kernel(x_ref, mod_ref, g_ref, w_ref, qg_ref, kg_ref, gb_ref,
                      qa_ref, ka_ref, va_ref, qk_ref, vb_ref, ob_ref, gt_ref):
    x = x_ref[...]
    h = _rms(x, g_ref[...]) * (1.0 + mod_ref[1:2, :]) + mod_ref[0:1, :]
    hb = h.astype(MXU_DT)
    qscale = (A_QK_DIM ** -0.5) * LOG2E
    for hd in range(A_HEADS):
        c0 = hd * LANES
        q = jnp.dot(hb, w_ref[:, c0:c0 + LANES], preferred_element_type=F32)
        qa_ref[:, c0:c0 + LANES] = (_half_rms(q, qg_ref[...]) * qscale).astype(qa_ref.dtype)
        k = jnp.dot(hb, w_ref[:, A_QK_WIDTH + c0:A_QK_WIDTH + c0 + LANES], preferred_element_type=F32)
        ka_ref[:, c0:c0 + LANES] = _half_rms(k, kg_ref[...]).astype(ka_ref.dtype)
    o = 2 * A_QK_WIDTH
    va_ref[...] = jnp.dot(hb, w_ref[:, o:o + A_WIDTH], preferred_element_type=F32).astype(va_ref.dtype)
    o += A_WIDTH
    qk_ref[...] = jnp.dot(hb, w_ref[:, o:o + 2 * B_WIDTH], preferred_element_type=F32)
    o += 2 * B_WIDTH
    vb_ref[...] = jnp.dot(hb, w_ref[:, o:o + B_WIDTH], preferred_element_type=F32).astype(vb_ref.dtype)
    o += B_WIDTH
    ob_ref[...] = jnp.dot(hb, w_ref[:, o:o + B_WIDTH], preferred_element_type=F32)
    o += B_WIDTH
    gt_ref[...] = jnp.dot(hb, w_ref[:, o:o + LANES], preferred_element_type=F32) + gb_ref[...]


def _inproj_ab_call(geo, tm, x, mod_l, g, w_pad, qg, kg, gate_b):
    t = geo.t
    per_seg = geo.seg // tm
    row = lambda i: (i, 0)
    const = lambda i: (0, 0)
    widths = (A_QK_WIDTH, A_QK_WIDTH, A_WIDTH, 2 * B_WIDTH, B_WIDTH, B_WIDTH, LANES)
    dtypes = (ACT_DT, ACT_DT, ACT_DT, F32, ACT_DT, F32, F32)
    return pl.pallas_call(
        _inproj_ab_kernel,
        grid=(t // tm,),
        in_specs=[pl.BlockSpec((tm, D_MODEL), row),
                  pl.BlockSpec((None, 6, D_MODEL), lambda i: (i // per_seg, 0, 0)),
                  pl.BlockSpec((1, D_MODEL), const),
                  pl.BlockSpec((D_MODEL, AB_IN_PAD), const),
                  pl.BlockSpec((1, LANES), const),
                  pl.BlockSpec((1, LANES), const),
                  pl.BlockSpec((1, LANES), const)],
        out_specs=[pl.BlockSpec((tm, w), row) for w in widths],
        out_shape=[jax.ShapeDtypeStruct((t, w), d) for w, d in zip(widths, dtypes)],
        compiler_params=_cparams(("parallel",)),
        name="inproj_ab",
    )(x, mod_l, g, w_pad, qg, kg, gate_b)


def _conv_kernel(geo, tc, x_ref, prev_ref, next_ref, w_ref, b_ref, q_ref, k_ref):
    i = pl.program_id(0)
    row0 = i * tc
    x = x_ref[...]
    first = geo.seq_start(row0)
    last = geo.seq_start(row0 + tc) | (row0 + tc == geo.t)
    prev_row = jnp.where(first, 0.0, prev_ref[7:8, :])
    next_row = jnp.where(last, 0.0, next_ref[0:1, :])
    ridx = lax.broadcasted_iota(jnp.int32, (tc, 1), 0)
    x_prev = jnp.where(ridx == 0, prev_row, pltpu.roll(x, 1, axis=0))
    x_next = jnp.where(ridx == tc - 1, next_row, pltpu.roll(x, tc - 1, axis=0))
    y = x_prev * w_ref[0:1, :] + x * w_ref[1:2, :] + x_next * w_ref[2:3, :] + b_ref[...]
    y = _silu(y)
    q_ref[...] = y[:, :B_WIDTH].astype(q_ref.dtype)
    k_ref[...] = (y[:, B_WIDTH:] * (B_DIM ** -0.5)).astype(k_ref.dtype)


def _conv_call(geo, tc, qk, w, b):
    t = geo.t
    r8 = tc // 8
    nb8 = t // 8
    return pl.pallas_call(
        functools.partial(_conv_kernel, geo, tc),
        grid=(t // tc,),
        in_specs=[pl.BlockSpec((tc, 2 * B_WIDTH), lambda i: (i, 0)),
                  pl.BlockSpec((8, 2 * B_WIDTH), lambda i: (jnp.maximum(i * r8 - 1, 0), 0)),
                  pl.BlockSpec((8, 2 * B_WIDTH), lambda i: (jnp.minimum((i + 1) * r8, nb8 - 1), 0)),
                  pl.BlockSpec((3, 2 * B_WIDTH), lambda i: (0, 0)),
                  pl.BlockSpec((1, 2 * B_WIDTH), lambda i: (0, 0))],
        out_specs=[pl.BlockSpec((tc, B_WIDTH), lambda i: (i, 0))] * 2,
        out_shape=[jax.ShapeDtypeStruct((t, B_WIDTH), ACT_DT)] * 2,
        compiler_params=_cparams(("parallel",)),
        name="mlstm_conv",
    )(qk, qk, qk, w, b)


def _attn_finish(lam_init, acc0, l0, acc1, l1, dl_ref, ng_ref, o_ref):
    dl = dl_ref[...]
    lam = (jnp.exp(jnp.sum(dl[0:1] * dl[1:2], axis=-1, keepdims=True))
           - jnp.exp(jnp.sum(dl[2:3] * dl[3:4], axis=-1, keepdims=True)) + lam_init)
    out = acc0 / l0 - lam * (acc1 / l1)
    o_ref[...] = (_rms(out, ng_ref[...]) * (1.0 - lam_init)).astype(o_ref.dtype)


def _attn_kernel(lam_init, nk, q_ref, k_ref, v_ref, bias_ref, sc_ref, dl_ref, ng_ref, o_ref,
                 m_sc, l_sc, acc_sc):
    h = pl.program_id(1)
    i = pl.program_id(2)
    j = pl.program_id(3)

    @pl.when(j == 0)
    def _():
        m_sc[...] = jnp.full(m_sc.shape, NEG_BIG, F32)
        l_sc[...] = jnp.zeros(l_sc.shape, F32)
        acc_sc[...] = jnp.zeros(acc_sc.shape, F32)

    q = q_ref[...]
    lo_lane = lax.broadcasted_iota(jnp.int32, (1, LANES), 1) < A_QK_DIM
    qsub = (jnp.where(lo_lane, q, jnp.zeros_like(q)), jnp.where(lo_lane, jnp.zeros_like(q), q))
    k = k_ref[...]
    v = v_ref[...]

    def step(near):
        if near:
            shift = 0.0
        else:
            shift = jnp.where(j < i, sc_ref[h, 0], sc_ref[h, 1])
        for sub in range(2):
            s = _mm_nt(qsub[sub], k)
            if near:
                s = s + bias_ref[...]
            m_old = m_sc[sub]
            m_new = jnp.maximum(m_old, jnp.max(s, axis=-1, keepdims=True) + shift)
            p = jnp.exp2(s - (m_new - shift))
            alpha = jnp.exp2(m_old - m_new)
            l_sc[sub] = alpha * l_sc[sub] + jnp.sum(p, axis=-1, keepdims=True)
            acc_sc[sub] = alpha * acc_sc[sub] + _mm(p, v)
            m_sc[sub] = m_new

    near = jnp.abs(j - i) <= 1
    pl.when(near)(lambda: step(True))
    pl.when(jnp.logical_not(near))(lambda: step(False))

    @pl.when(j == nk - 1)
    def _():
        _attn_finish(lam_init, acc_sc[0], l_sc[0], acc_sc[1], l_sc[1], dl_ref, ng_ref, o_ref)


def _attn_bounded_kernel(lam_init, nk, q_ref, k_ref, v_ref, bias_a_ref, bias_b_ref, sc_ref, dl_ref, ng_ref, o_ref,
                         acc_sc):
    h = pl.program_id(1)
    i = pl.program_id(2)
    j = pl.program_id(3)
    tq = q_ref.shape[0]

    @pl.when(j == 0)
    def _():
        acc_sc[...] = jnp.zeros(acc_sc.shape, F32)

    q = q_ref[...]
    lo = (lax.broadcasted_iota(jnp.int32, q.shape, 1) < A_QK_DIM).astype(F32).astype(q.dtype)
    qsub = (q * lo, q * (1 - lo))
    ones_col = (lax.broadcasted_iota(jnp.int32, (tq, LANES), 1) == 0).astype(v_ref.dtype)

    for kb, bias_ref in enumerate((bias_a_ref, bias_b_ref)):
        jb = j * KB_PER_STEP + kb

        @pl.when(jb == jnp.maximum(i - 1, 0))
        def _():
            acc_sc[...] = acc_sc[...] * sc_ref[h, 2]

        @pl.when(jb == i + 2)
        def _():
            acc_sc[...] = acc_sc[...] * sc_ref[h, 3]

        def step(near, kb=kb, bias_ref=bias_ref):
            k = k_ref[kb * tq:(kb + 1) * tq, :]
            v_aug = jnp.concatenate([v_ref[kb * tq:(kb + 1) * tq, :], ones_col], axis=1)
            for sub in range(2):
                s = _mm_nt(qsub[sub], k)
                if near:
                    s = s + bias_ref[...]
                acc_sc[sub] += _mm(jnp.exp2(s), v_aug)

        near = jnp.abs(jb - i) <= 1
        pl.when(near)(functools.partial(step, True))
        pl.when(jnp.logical_not(near))(functools.partial(step, False))

    @pl.when(j == nk - 1)
    def _():
        a0 = acc_sc[0]
        a1 = acc_sc[1]
        _attn_finish(lam_init, a0[:, :A_V_DIM], a0[:, A_V_DIM:A_V_DIM + 1], a1[:, :A_V_DIM],
                     a1[:, A_V_DIM:A_V_DIM + 1], dl_ref, ng_ref, o_ref)


def _attn_scalars(bias, tq, q_gain, k_gain):
    far_l = bias[:, 0, tq - 1, 0]
    far_r = bias[:, 2, 0, tq - 1]
    sc = jnp.stack([far_l, far_r, jnp.exp2(far_l), jnp.exp2(-far_r)], axis=-1)
    bound = A_QK_DIM * jnp.max(jnp.abs(q_gain)) * jnp.max(jnp.abs(k_gain)) * (A_QK_DIM ** -0.5) * LOG2E * 1.02
    spread = -jnp.min(bias)
    ok = bound + 2.0 * spread <= 80.0
    return sc, ok


def _attn_call(geo, tq, lam_init, bounded, qa, ka, va, bias, sc, dl, ng):
    outs = []
    kb = KB_PER_STEP if bounded else 1
    for (nb, s, row_off) in ((geo.bp, geo.sp, 0), (geo.bs, geo.ss, geo.tp)):
        nq = s // tq
        nk = nq // kb
        off = row_off // tq
        qmap = lambda b, h, i, j, off=off, nq=nq: (off + b * nq + i, h)
        kmap = lambda b, h, i, j, off=off, nk=nk: (off // kb + b * nk + j, h)
        omap = lambda b, h, i, j, nq=nq: (b * nq + i, h)
        tile = lambda b, h, i, j, k=0: (h, jnp.clip(j * kb + k - i + 1, 0, 2), 0, 0)
        bias_specs = [pl.BlockSpec((None, None, tq, tq), functools.partial(tile, k=k)) for k in range(kb)]
        if bounded:
            body = functools.partial(_attn_bounded_kernel, lam_init, nk)
            scratch = [pltpu.VMEM((2, tq, 2 * LANES), F32)]
        else:
            body = functools.partial(_attn_kernel, lam_init, nk)
            scratch = [pltpu.VMEM((2, tq, 1), F32), pltpu.VMEM((2, tq, 1), F32), pltpu.VMEM((2, tq, LANES), F32)]
        outs.append(pl.pallas_call(
            body,
            grid=(nb, A_HEADS, nq, nk),
            in_specs=[pl.BlockSpec((tq, LANES), qmap),
                      pl.BlockSpec((kb * tq, LANES), kmap),
                      pl.BlockSpec((kb * tq, LANES), kmap)]
                     + bias_specs
                     + [pl.BlockSpec(memory_space=pltpu.SMEM),
                        pl.BlockSpec((4, A_QK_DIM), lambda b, h, i, j: (0, 0)),
                        pl.BlockSpec((1, LANES), lambda b, h, i, j: (0, 0))],
            out_specs=pl.BlockSpec((tq, LANES), omap),
            out_shape=jax.ShapeDtypeStruct((nb * s, A_WIDTH), ACT_DT),
            scratch_shapes=scratch,
            compiler_params=_cparams(("parallel", "parallel", "parallel", "arbitrary")),
            name="diff_attn_bounded" if bounded else "diff_attn",
        )(qa, ka, va, *([bias] * kb), sc, dl, ng))
    return jnp.concatenate(outs, axis=0)


def _tri(lower):
    r = lax.broadcasted_iota(jnp.int32, (CHUNK, CHUNK), 0)
    c = lax.broadcasted_iota(jnp.int32, (CHUNK, CHUNK), 1)
    return (c <= r) if lower else (c >= r)


def _mlstm_kernel(geo, tb, reverse, *refs):
    if reverse:
        q_ref, k_ref, v_ref, g_ref, hf_ref, ob_ref, ng_ref, o_ref, c_sc, m_sc = refs
    else:
        q_ref, k_ref, v_ref, g_ref, o_ref, c_sc, m_sc = refs
    step = pl.program_id(0)
    nblk = geo.t // tb
    blk = (nblk - 1 - step) if reverse else step
    row0 = blk * tb
    if reverse:
        fresh = geo.seq_start(row0 + tb) | (row0 + tb == geo.t)
    else:
        fresh = geo.seq_start(row0)

    @pl.when(fresh)
    def _():
        c_sc[...] = jnp.zeros(c_sc.shape, F32)
        m_sc[...] = jnp.zeros(m_sc.shape, F32)

    mask = _tri(not reverse)
    cum_l = mask.astype(F32)
    cum_r = _tri(reverse).astype(F32)
    ones_col = (lax.broadcasted_iota(jnp.int32, (CHUNK, LANES), 1) == 0).astype(MXU_DT)
    nch = tb // CHUNK

    def chunk(ci, carry):
        c_idx = (nch - 1 - ci) if reverse else ci
        r0 = pl.multiple_of(c_idx * CHUNK, CHUNK)
        g = g_ref[pl.ds(r0, CHUNK), :]
        g_t = g.T
        b_col = _mm_hi(cum_l, _log_sigmoid(g))
        b_row = _mm_hi(_log_sigmoid(g_t), cum_r)
        for hd in range(B_HEADS):
            ci_col = hd * N_GATES + (2 if reverse else 0)
            cf_col = ci_col + 1
            lanes = slice(hd * B_DIM, (hd + 1) * B_DIM)
            q = q_ref[pl.ds(r0, CHUNK), lanes]
            k = k_ref[pl.ds(r0, CHUNK), lanes]
            v = v_ref[pl.ds(r0, CHUNK), lanes]
            bc = b_col[:, cf_col:cf_col + 1]
            br = b_row[cf_col:cf_col + 1, :]
            ic = g[:, ci_col:ci_col + 1]
            ir = g_t[ci_col:ci_col + 1, :]
            m_prev = m_sc[hd]
            log_d = jnp.where(mask, bc - br + ir, NEG_BIG)
            m_inter = bc + m_prev
            m_t = jnp.maximum(jnp.max(log_d, axis=-1, keepdims=True), m_inter)
            s = _mm_nt(q, k) * jnp.exp(log_d - m_t)
            inter = jnp.exp(m_inter - m_t)
            c_aug = c_sc[hd]
            qc = _mm(q, c_aug)
            num = _mm(s, v) + inter * qc[:, :B_DIM]
            den = jnp.sum(s, axis=-1, keepdims=True) + inter * qc[:, B_DIM:B_DIM + 1]
            hout = num / jnp.maximum(jnp.abs(den), jnp.exp(-m_t))
            b_last = bc[0:1, :] if reverse else bc[CHUNK - 1:CHUNK, :]
            log_w = b_last - bc + ic
            m_new = jnp.maximum(b_last + m_prev, jnp.max(log_w, axis=0, keepdims=True))
            w = jnp.exp(log_w - m_new)
            decay = jnp.exp(b_last + m_prev - m_new)
            v_aug = jnp.concatenate([v, ones_col], axis=1)
            c_sc[hd] = decay * c_aug + _mm_tn(k.astype(F32) * w, v_aug)
            m_sc[hd] = m_new
            if reverse:
                hsum = hf_ref[pl.ds(r0, CHUNK), lanes] + hout
                y = _rms(hsum, ng_ref[...]) * _sigmoid(ob_ref[pl.ds(r0, CHUNK), lanes])
                o_ref[pl.ds(r0, CHUNK), lanes] = y.astype(o_ref.dtype)
            else:
                o_ref[pl.ds(r0, CHUNK), lanes] = hout
        return carry

    lax.fori_loop(0, nch, chunk, 0)


def _mlstm_call(geo, tb, qb, kb, vb, gates, ob, ng):
    t = geo.t
    nblk = t // tb
    fmap = lambda s: (s, 0)
    rmap = lambda s: (nblk - 1 - s, 0)
    scratch = [pltpu.VMEM((B_HEADS, B_DIM, 2 * B_DIM), F32), pltpu.VMEM((B_HEADS, 1, 1), F32)]
    wide = lambda m: pl.BlockSpec((tb, B_WIDTH), m)
    hf = pl.pallas_call(
        functools.partial(_mlstm_kernel, geo, tb, False),
        grid=(nblk,),
        in_specs=[wide(fmap), wide(fmap), wide(fmap), pl.BlockSpec((tb, LANES), fmap)],
        out_specs=wide(fmap),
        out_shape=jax.ShapeDtypeStruct((t, B_WIDTH), F32),
        scratch_shapes=scratch,
        compiler_params=_cparams(("arbitrary",)),
        name="mlstm_fwd",
    )(qb, kb, vb, gates)
    return pl.pallas_call(
        functools.partial(_mlstm_kernel, geo, tb, True),
        grid=(nblk,),
        in_specs=[wide(rmap), wide(rmap), wide(rmap), pl.BlockSpec((tb, LANES), rmap), wide(rmap), wide(rmap),
                  pl.BlockSpec((1, B_DIM), lambda s: (0, 0))],
        out_specs=wide(rmap),
        out_shape=jax.ShapeDtypeStruct((t, B_WIDTH), ACT_DT),
        scratch_shapes=scratch,
        compiler_params=_cparams(("arbitrary",)),
        name="mlstm_bwd",
    )(qb, kb, vb, gates, hf, ob, ng)


def _inproj_c_kernel(x_ref, mod_ref, g_ref, w_ref, cos_ref, sin_ref, q_ref, k_ref, v_ref, gt_ref, h_sc):
    j = pl.program_id(1)
    nsub = C_HEADS
    sub = w_ref.shape[1] // nsub

    def proj(c):
        return jnp.dot(h_sc[...], w_ref[:, c * sub:(c + 1) * sub], preferred_element_type=F32)

    def rope(o_ref, scale):
        cos = cos_ref[...] * scale
        sin = sin_ref[...] * scale
        half = C_QK_DIM // 2
        for hd in range(C_HEADS):
            y = proj(hd)
            x1, x2 = y[:, :half], y[:, half:]
            o_ref[:, hd * C_QK_DIM:hd * C_QK_DIM + half] = (x1 * cos - x2 * sin).astype(o_ref.dtype)
            o_ref[:, hd * C_QK_DIM + half:(hd + 1) * C_QK_DIM] = (x1 * sin + x2 * cos).astype(o_ref.dtype)

    @pl.when(j == 0)
    def _():
        x = x_ref[...]
        h = _rms(x, g_ref[...]) * (1.0 + mod_ref[1:2, :]) + mod_ref[0:1, :]
        h_sc[...] = h.astype(h_sc.dtype)
        rope(q_ref, 1.0)

    @pl.when(j == 1)
    def _():
        rope(k_ref, C_QK_DIM ** -0.5)

    @pl.when((j == 2) | (j == 3))
    def _():
        for c in range(nsub):
            v_ref[:, c * sub:(c + 1) * sub] = proj(c).astype(v_ref.dtype)

    @pl.when(j >= 4)
    def _():
        for c in range(nsub):
            gt_ref[:, c * sub:(c + 1) * sub] = _silu(proj(c)).astype(gt_ref.dtype)


def _inproj_c_call(geo, tm, x, mod_l, g, w, cos, sin):
    t = geo.t
    per_seg = geo.seg // tm
    nw = D_MODEL
    return pl.pallas_call(
        _inproj_c_kernel,
        grid=(t // tm, C_IN // nw),
        in_specs=[pl.BlockSpec((tm, D_MODEL), lambda i, j: (i, 0)),
                  pl.BlockSpec((None, 6, D_MODEL), lambda i, j: (i // per_seg, 0, 0)),
                  pl.BlockSpec((1, D_MODEL), lambda i, j: (0, 0)),
                  pl.BlockSpec((D_MODEL, nw), lambda i, j: (0, j)),
                  pl.BlockSpec((tm, C_QK_DIM // 2), lambda i, j: (geo.pos_block(i, tm), 0)),
                  pl.BlockSpec((tm, C_QK_DIM // 2), lambda i, j: (geo.pos_block(i, tm), 0))],
        out_specs=[pl.BlockSpec((tm, nw), lambda i, j: (i, 0)),
                   pl.BlockSpec((tm, nw), lambda i, j: (i, 0)),
                   pl.BlockSpec((tm, nw), lambda i, j: (i, jnp.clip(j - 2, 0, 1))),
                   pl.BlockSpec((tm, nw), lambda i, j: (i, jnp.clip(j - 4, 0, 1)))],
        out_shape=[jax.ShapeDtypeStruct((t, C_QK_WIDTH), ACT_DT),
                   jax.ShapeDtypeStruct((t, C_QK_WIDTH), ACT_DT),
                   jax.ShapeDtypeStruct((t, C_V_WIDTH), ACT_DT),
                   jax.ShapeDtypeStruct((t, C_V_WIDTH), ACT_DT)],
        scratch_shapes=[pltpu.VMEM((tm, D_MODEL), MXU_DT)],
        compiler_params=_cparams(("parallel", "arbitrary")),
        name="inproj_c",
    )(x, mod_l, g, w, cos, sin)


def _ret_kernel(geo, tb, reverse, *refs):
    if reverse:
        q_ref, k_ref, v_ref, dlg_ref, yf_ref, gt_ref, ng_ref, o_ref, r_sc, intra_sc, vec_sc = refs
    else:
        q_ref, k_ref, v_ref, dlg_ref, o_ref, r_sc, intra_sc, vec_sc = refs
    step = pl.program_id(0)
    nblk = geo.t // tb
    blk = (nblk - 1 - step) if reverse else step
    row0 = blk * tb
    if reverse:
        fresh = geo.seq_start(row0 + tb) | (row0 + tb == geo.t)
    else:
        fresh = geo.seq_start(row0)

    @pl.when(fresh)
    def _():
        r_sc[...] = jnp.zeros(r_sc.shape, F32)

    @pl.when(step == 0)
    def _():
        lg_all = _log_sigmoid(dlg_ref[...])
        ti = lax.broadcasted_iota(jnp.int32, (RET_CHUNK, RET_CHUNK), 0)
        si = lax.broadcasted_iota(jnp.int32, (RET_CHUNK, RET_CHUNK), 1)
        dist = ((si - ti) if reverse else (ti - si)).astype(F32)
        pos = lax.broadcasted_iota(jnp.int32, (RET_CHUNK, LANES), 0).astype(F32)
        upos = (RET_CHUNK - 1.0 - pos) if reverse else pos
        lane = lax.broadcasted_iota(jnp.int32, (RET_CHUNK, LANES), 1)
        d = 1 if reverse else 0
        for hd in range(C_HEADS):
            lg = lg_all[d:d + 1, hd:hd + 1]
            intra_sc[hd] = jnp.where(dist >= 0, jnp.exp(jnp.maximum(dist, 0.0) * lg), 0.0)
            vec_sc[hd] = jnp.where(lane == 0, jnp.exp((upos + 1.0) * lg),
                                   jnp.where(lane == 1, jnp.exp((RET_CHUNK - 1.0 - upos) * lg),
                                             jnp.exp(RET_CHUNK * lg)))

    nch = tb // RET_CHUNK

    def chunk(ci, carry):
        c_idx = (nch - 1 - ci) if reverse else ci
        r0 = pl.multiple_of(c_idx * RET_CHUNK, RET_CHUNK)
        for hd in range(C_HEADS):
            vec = vec_sc[hd]
            q_scale, k_scale, c_decay = vec[:, 0:1], vec[:, 1:2], vec[0:1, 2:3]
            ql = slice(hd * C_QK_DIM, (hd + 1) * C_QK_DIM)
            vl = slice(hd * C_V_DIM, (hd + 1) * C_V_DIM)
            q = q_ref[pl.ds(r0, RET_CHUNK), ql]
            k = k_ref[pl.ds(r0, RET_CHUNK), ql]
            v = v_ref[pl.ds(r0, RET_CHUNK), vl]
            r_old = r_sc[hd]
            s = _mm_nt(q, k) * intra_sc[hd]
            y = _mm(s, v) + q_scale * _mm(q, r_old)
            r_sc[hd] = c_decay * r_old + _mm_tn(k.astype(F32) * k_scale, v)
            if reverse:
                ysum = yf_ref[pl.ds(r0, RET_CHUNK), vl].astype(F32) + y
                out = _rms(ysum, ng_ref[...]) * gt_ref[pl.ds(r0, RET_CHUNK), vl].astype(F32)
                o_ref[pl.ds(r0, RET_CHUNK), vl] = out.astype(o_ref.dtype)
            else:
                o_ref[pl.ds(r0, RET_CHUNK), vl] = y.astype(o_ref.dtype)
        return carry

    lax.fori_loop(0, nch, chunk, 0)


def _ret_call(geo, tb, q, k, v, decay_logit, gt, ng):
    t = geo.t
    nblk = t // tb
    fmap = lambda s: (s, 0)
    rmap = lambda s: (nblk - 1 - s, 0)
    scratch = [pltpu.VMEM((C_HEADS, C_QK_DIM, C_V_DIM), F32), pltpu.VMEM((C_HEADS, RET_CHUNK, RET_CHUNK), F32),
               pltpu.VMEM((C_HEADS, RET_CHUNK, LANES), F32)]
    qk = lambda m: pl.BlockSpec((tb, C_QK_WIDTH), m)
    vv = lambda m: pl.BlockSpec((tb, C_V_WIDTH), m)
    dspec = pl.BlockSpec((2, C_HEADS), lambda s: (0, 0))
    yf = pl.pallas_call(
        functools.partial(_ret_kernel, geo, tb, False),
        grid=(nblk,),
        in_specs=[qk(fmap), qk(fmap), vv(fmap), dspec],
        out_specs=vv(fmap),
        out_shape=jax.ShapeDtypeStruct((t, C_V_WIDTH), ACT_DT),
        scratch_shapes=scratch,
        compiler_params=_cparams(("arbitrary",)),
        name="ret_fwd",
    )(q, k, v, decay_logit)
    return pl.pallas_call(
        functools.partial(_ret_kernel, geo, tb, True),
        grid=(nblk,),
        in_specs=[qk(rmap), qk(rmap), vv(rmap), dspec, vv(rmap), vv(rmap),
                  pl.BlockSpec((1, C_V_DIM), lambda s: (0, 0))],
        out_specs=vv(rmap),
        out_shape=jax.ShapeDtypeStruct((t, C_V_WIDTH), ACT_DT),
        scratch_shapes=scratch,
        compiler_params=_cparams(("arbitrary",)),
        name="ret_bwd",
    )(q, k, v, decay_logit, yf, gt, ng)


def _route(probs):
    p = [probs[e:e + 1, :] for e in range(N_EXPERTS)]
    scores = []
    for g in range(N_GROUPS):
        a, b, c, d = p[EPG * g:EPG * g + EPG]
        hi1, lo1 = jnp.maximum(a, b), jnp.minimum(a, b)
        hi2, lo2 = jnp.maximum(c, d), jnp.minimum(c, d)
        scores.append(jnp.maximum(hi1, hi2) + jnp.maximum(jnp.minimum(hi1, hi2), jnp.maximum(lo1, lo2)))
    g_sel = jnp.zeros(scores[0].shape, jnp.int32)
    best = scores[0]
    for g in range(1, N_GROUPS):
        better = scores[g] > best
        g_sel = jnp.where(better, g, g_sel)
        best = jnp.where(better, scores[g], best)
    vals = []
    for kk in range(EPG):
        v = p[kk]
        for g in range(1, N_GROUPS):
            v = jnp.where(g_sel == g, p[EPG * g + kk], v)
        vals.append(v)

    def argmax4(xs):
        idx = jnp.zeros(xs[0].shape, jnp.int32)
        top = xs[0]
        for kk in range(1, EPG):
            better = xs[kk] > top
            idx = jnp.where(better, kk, idx)
            top = jnp.where(better, xs[kk], top)
        return idx, top

    i1, v1 = argmax4(vals)
    i2, v2 = argmax4([jnp.where(i1 == kk, -1.0, vals[kk]) for kk in range(EPG)])
    tot = v1 + v2
    w1, w2 = v1 / tot, v2 / tot
    e1 = g_sel * EPG + i1
    e2 = g_sel * EPG + i2
    eidx = lax.broadcasted_iota(jnp.int32, probs.shape, 0)
    lo, hi = jnp.minimum(i1, i2), jnp.maximum(i1, i2)
    pair = jnp.where(lo == 0, hi - 1, jnp.where(lo == 1, hi + 1, N_PAIRS - 1))
    return jnp.where(eidx == e1, w1, 0.0) + jnp.where(eidx == e2, w2, 0.0), g_sel * N_PAIRS + pair


def _outproj_kernel(nparts, *refs):
    y_refs = refs[:nparts]
    w_refs = refs[nparts:2 * nparts]
    x_ref, mod_ref, g_ref, rw_ref, rb_ref, xo_ref, hx_ref, bkt_ref = refs[2 * nparts:]
    m = jnp.dot(y_refs[0][...], w_refs[0][...], preferred_element_type=F32)
    for p in range(1, nparts):
        m = m + jnp.dot(y_refs[p][...], w_refs[p][...], preferred_element_type=F32)
    x = x_ref[...] + mod_ref[2:3, :] * m
    xo_ref[...] = x
    h = _rms(x, g_ref[...]) * (1.0 + mod_ref[4:5, :]) + mod_ref[3:4, :]
    hx_ref[:, :D_MODEL] = h
    logits = lax.dot_general(rw_ref[...], h, (((1,), (1,)), ((), ())), precision=HI,
                             preferred_element_type=F32) + rb_ref[...]
    z = jnp.exp(logits - jnp.max(logits, axis=0, keepdims=True))
    probs = z / jnp.sum(z, axis=0, keepdims=True)
    cmb, bucket = _route(probs)
    pad = jnp.zeros((LANES - N_EXPERTS, cmb.shape[1]), F32)
    hx_ref[:, D_MODEL:] = jnp.concatenate([cmb, pad], axis=0).T
    bkt_ref[...] = bucket


def _outproj_call(geo, tm, ys, ws, x, mod_l, g, rw_t, rb):
    t = geo.t
    per_seg = geo.seg // tm
    row = lambda i: (i, 0)
    const = lambda i: (0, 0)
    n = len(ys)
    return pl.pallas_call(
        functools.partial(_outproj_kernel, n),
        grid=(t // tm,),
        in_specs=([pl.BlockSpec((tm, y.shape[1]), row) for y in ys]
                  + [pl.BlockSpec(w.shape, const) for w in ws]
                  + [pl.BlockSpec((tm, D_MODEL), row),
                     pl.BlockSpec((None, 6, D_MODEL), lambda i: (i // per_seg, 0, 0)),
                     pl.BlockSpec((1, D_MODEL), const),
                     pl.BlockSpec((N_EXPERTS, D_MODEL), const),
                     pl.BlockSpec((N_EXPERTS, 1), const)]),
        out_specs=[pl.BlockSpec((tm, D_MODEL), row), pl.BlockSpec((tm, HX_W), row),
                   pl.BlockSpec((None, 1, tm), lambda i: (i, 0, 0))],
        out_shape=[jax.ShapeDtypeStruct((t, D_MODEL), F32), jax.ShapeDtypeStruct((t, HX_W), F32),
                   jax.ShapeDtypeStruct((t // tm, 1, tm), jnp.int32)],
        compiler_params=_cparams(("parallel",)),
        name="outproj_router",
    )(*ys, *ws, x, mod_l, g, rw_t, rb)


def _plan_kernel(tm, bkt_ref, pos_ref, te_ref, nv_ref):
    nblk, _, blk = bkt_ref.shape
    nrow = 32
    bid = lax.broadcasted_iota(jnp.int32, (nrow, blk), 0)
    r = lax.broadcasted_iota(jnp.int32, (blk, blk), 0)
    c = lax.broadcasted_iota(jnp.int32, (blk, blk), 1)
    prefix = (r <= c).astype(MXU_DT)

    def count(b, acc):
        return acc + jnp.sum((bkt_ref[b] == bid).astype(F32), axis=1, keepdims=True)

    counts = lax.fori_loop(0, nblk, count, jnp.zeros((nrow, 1), F32))
    padded = jnp.floor((counts + (tm - 1.0)) / tm) * tm
    rows = lax.broadcasted_iota(jnp.int32, (nrow, 1), 0)
    offs = jnp.zeros((nrow, 1), F32)
    ends = []
    run = jnp.zeros((1, 1), F32)
    for b in range(N_BUCKETS):
        offs = jnp.where(rows == b, run, offs)
        run = run + padded[b:b + 1, :]
        ends.append(run)

    def place(b, carry):
        oh = (bkt_ref[b] == bid).astype(F32)
        pre = jnp.dot(oh.astype(MXU_DT), prefix, preferred_element_type=F32)
        pos = jnp.sum(oh * (offs + carry + pre - 1.0), axis=0, keepdims=True)
        pos_ref[b] = pos.astype(jnp.int32)
        return carry + pre[:, blk - 1:blk]

    lax.fori_loop(0, nblk, place, jnp.zeros((nrow, 1), F32))
    start = lax.broadcasted_iota(jnp.int32, (1, te_ref.shape[1]), 1).astype(F32) * tm
    tb = jnp.zeros(start.shape, F32)
    for b in range(N_BUCKETS - 1):
        tb = tb + (ends[b] <= start).astype(F32)
    grp = jnp.floor((tb + 0.5) / N_PAIRS)
    pair = tb - grp * N_PAIRS
    lo = (pair >= 3).astype(F32) + (pair >= 5).astype(F32)
    hi = jnp.where(pair == 0, 1.0, jnp.where((pair == 1) | (pair == 3), 2.0, 3.0))
    te_ref[0:1, :] = (grp * EPG + lo).astype(jnp.int32)
    te_ref[1:2, :] = (grp * EPG + hi).astype(jnp.int32)
    nv_ref[...] = jnp.broadcast_to(run / tm, nv_ref.shape).astype(jnp.int32)


def _plan_call(tm, bkt):
    nblk, _, blk = bkt.shape
    nt = nblk * blk // tm + N_BUCKETS
    ntp = -(-nt // LANES) * LANES
    pos, te, nv = pl.pallas_call(
        functools.partial(_plan_kernel, tm),
        out_shape=[jax.ShapeDtypeStruct(bkt.shape, jnp.int32), jax.ShapeDtypeStruct((2, ntp), jnp.int32),
                   jax.ShapeDtypeStruct((1, LANES), jnp.int32)],
        compiler_params=pltpu.CompilerParams(vmem_limit_bytes=VMEM_LIMIT),
        name="moe_plan",
    )(bkt)
    return pos, te[0, :nt], te[1, :nt], nv[0, :1]


def _dispatch_kernel(tb, pos_ref, hx_ref, init_ref, xs_ref, sem):
    del init_ref

    def issue(g, carry):
        base = pl.multiple_of(g * SUBLANES, SUBLANES)
        rows = hx_ref.at[pl.ds(base, SUBLANES)]
        for r in range(SUBLANES):
            pltpu.make_async_copy(rows.at[pl.ds(r, 1)], xs_ref.at[pl.ds(pos_ref[0, base + r], 1)],
                                  sem).start(priority=r % 2)
        return carry

    lax.fori_loop(0, tb // SUBLANES, issue, 0)
    pltpu.make_async_copy(hx_ref, xs_ref.at[pl.ds(0, tb)], sem).wait()


def _dispatch_call(tb, pos, hx, xs_init):
    nsteps = hx.shape[0] // tb
    return pl.pallas_call(
        functools.partial(_dispatch_kernel, tb),
        grid=(nsteps,),
        in_specs=[pl.BlockSpec((None, 1, tb), lambda i: (i, 0, 0), memory_space=pltpu.SMEM),
                  pl.BlockSpec((tb, HX_W), lambda i: (i, 0)),
                  pl.BlockSpec(memory_space=pl.ANY)],
        out_specs=pl.BlockSpec(memory_space=pl.ANY),
        out_shape=jax.ShapeDtypeStruct(xs_init.shape, xs_init.dtype),
        scratch_shapes=[pltpu.SemaphoreType.DMA(())],
        input_output_aliases={2: 0},
        compiler_params=pltpu.CompilerParams(dimension_semantics=("arbitrary",), disable_bounds_checks=True,
                                             has_side_effects=True),
        name="moe_dispatch",
    )(pos, hx, xs_init)


def _moe_kernel(te0_ref, te1_ref, nv_ref, xs_ref, w1a_ref, w3a_ref, w2a_ref, w1b_ref, w3b_ref, w2b_ref, o_ref):
    n = pl.program_id(0)

    @pl.when(n < nv_ref[0])
    def _():
        h = xs_ref[:, :D_MODEL].astype(MXU_DT)
        cmb = xs_ref[:, D_MODEL:]
        lane = lax.broadcasted_iota(jnp.int32, cmb.shape, 1)

        def expert(e_id, w1_ref, w3_ref, w2_ref):
            a = jnp.dot(h, w1_ref[...], preferred_element_type=F32)
            b = jnp.dot(h, w3_ref[...], preferred_element_type=F32)
            c = jnp.sum(jnp.where(lane == e_id, cmb, 0.0), axis=-1, keepdims=True)
            return c * _mm(_silu(a) * b, w2_ref[...])

        o_ref[...] = (expert(te0_ref[n], w1a_ref, w3a_ref, w2a_ref)
                      + expert(te1_ref[n], w1b_ref, w3b_ref, w2b_ref))

    @pl.when(n >= nv_ref[0])
    def _():
        o_ref[...] = jnp.zeros(o_ref.shape, F32)


def _moe_call(tm, layer, te0, te1, nv, xs, w1, w3, w2):
    rows = xs.shape[0]
    nt = rows // tm
    tile = lambda n, te0, te1, nv: (jnp.minimum(n, nv[0] - 1), 0)
    wa = lambda n, te0, te1, nv: (layer, te0[jnp.minimum(n, nv[0] - 1)], 0, 0)
    wb = lambda n, te0, te1, nv: (layer, te1[jnp.minimum(n, nv[0] - 1)], 0, 0)
    up = lambda m: pl.BlockSpec((None, None, D_MODEL, D_FF), m)
    down = lambda m: pl.BlockSpec((None, None, D_FF, D_MODEL), m)
    return pl.pallas_call(
        _moe_kernel,
        grid_spec=pltpu.PrefetchScalarGridSpec(
            num_scalar_prefetch=3,
            grid=(nt,),
            in_specs=[pl.BlockSpec((tm, HX_W), tile), up(wa), up(wa), down(wa), up(wb), up(wb), down(wb)],
            out_specs=pl.BlockSpec((tm, D_MODEL), lambda n, te0, te1, nv: (n, 0))),
        out_shape=jax.ShapeDtypeStruct((rows, D_MODEL), F32),
        compiler_params=_cparams(("arbitrary",)),
        name="moe",
    )(te0, te1, nv, xs, w1, w3, w2, w1, w3, w2)


def _collect_kernel(tb, nsteps, pos_ref, posn_ref, ys_ref, x_ref, mod_ref, o_ref, buf, sem):
    i = pl.program_id(0)
    slot = i % 2

    def issue(p_ref, s):
        def body(g, carry):
            base = pl.multiple_of(g * SUBLANES, SUBLANES)
            rows = buf.at[s, pl.ds(base, SUBLANES)]
            for r in range(SUBLANES):
                pltpu.make_async_copy(ys_ref.at[pl.ds(p_ref[0, base + r], 1)], rows.at[pl.ds(r, 1)],
                                      sem.at[s]).start(priority=r % 2)
            return carry
        lax.fori_loop(0, tb // SUBLANES, body, 0)

    pl.when(i == 0)(lambda: issue(pos_ref, 0))
    pl.when(i + 1 < nsteps)(lambda: issue(posn_ref, 1 - slot))
    pltpu.make_async_copy(ys_ref.at[pl.ds(0, tb)], buf.at[slot], sem.at[slot]).wait()
    o_ref[...] = x_ref[...] + mod_ref[5:6, :] * buf[slot]


def _collect_call(geo, tb, pos, ys, x, mod_l):
    t = geo.t
    nsteps = t // tb
    per_seg = geo.seg // tb
    return pl.pallas_call(
        functools.partial(_collect_kernel, tb, nsteps),
        grid=(nsteps,),
        in_specs=[pl.BlockSpec((None, 1, tb), lambda i: (i, 0, 0), memory_space=pltpu.SMEM),
                  pl.BlockSpec((None, 1, tb), lambda i: (jnp.minimum(i + 1, nsteps - 1), 0, 0),
                               memory_space=pltpu.SMEM),
                  pl.BlockSpec(memory_space=pl.ANY),
                  pl.BlockSpec((tb, D_MODEL), lambda i: (i, 0)),
                  pl.BlockSpec((None, 6, D_MODEL), lambda i: (i // per_seg, 0, 0))],
        out_specs=pl.BlockSpec((tb, D_MODEL), lambda i: (i, 0)),
        out_shape=jax.ShapeDtypeStruct((t, D_MODEL), F32),
        scratch_shapes=[pltpu.VMEM((2, tb, D_MODEL), F32), pltpu.SemaphoreType.DMA((2,))],
        compiler_params=pltpu.CompilerParams(dimension_semantics=("arbitrary",), vmem_limit_bytes=VMEM_LIMIT,
                                             disable_bounds_checks=True),
        name="moe_collect",
    )(pos, pos, ys, x, mod_l)


def _tiles(geo):
    seg = geo.seg
    return dict(tm=min(512, seg), tmc=min(1024, seg), tq=min(1024, seg), tscan=min(512, seg), tmoe=min(512, seg))


def _forward(geo, tiles, x_prompt, x_sample, c_prompt, c_sample, rel_bias, router_w, router_b, ada_w, ada_b,
             norm_mix_g, norm_ffn_g, w_in_ab, w_out_ab, q_norm_g, k_norm_g, diff_lambda, diff_norm_g,
             mlstm_conv_w, mlstm_conv_b, mlstm_gate_b, mlstm_norm_g, w_in_c, w_out_c, ret_decay_logit,
             ret_norm_g, moe_w1, moe_w3, moe_w2):
    tm, tmc, tq, tscan, tmoe = tiles["tm"], tiles["tmc"], tiles["tq"], tiles["tscan"], tiles["tmoe"]
    x = jnp.concatenate([x_prompt.reshape(geo.tp, D_MODEL), x_sample.reshape(geo.t - geo.tp, D_MODEL)], axis=0)
    c_rows = jnp.concatenate([jnp.repeat(c_prompt, geo.sp // geo.seg, axis=0),
                              jnp.repeat(c_sample, geo.ss // geo.seg, axis=0)], axis=0)
    mod = _ada_call(c_rows, ada_w, ada_b)

    bias = _bias_call(rel_bias, tq)
    cos, sin = _rope_tables(max(geo.sp, geo.ss))
    rw_t = router_w.T
    rb = router_b.reshape(N_EXPERTS, 1)
    w1 = moe_w1.astype(MXU_DT)
    w3 = moe_w3.astype(MXU_DT)
    w2 = moe_w2.astype(MXU_DT)
    xs = jnp.zeros((geo.t + N_BUCKETS * tmoe, HX_W), F32)

    for l in range(DEPTH):
        j = l // 2
        mod_l = mod[l]
        g_mix = norm_mix_g[l].reshape(1, D_MODEL)
        g_ffn = norm_ffn_g[l].reshape(1, D_MODEL)
        if l % 2 == 0:
            lam_init = 0.8 - 0.6 * math.exp(-0.3 * l)
            w_pad = jnp.pad(w_in_ab[j], ((0, 0), (0, AB_IN_PAD - AB_IN))).astype(MXU_DT)
            qg = jnp.tile(q_norm_g[j], 2).reshape(1, LANES)
            kg = jnp.tile(k_norm_g[j], 2).reshape(1, LANES)
            gate_b = jnp.pad(mlstm_gate_b[j].reshape(1, B_HEADS * N_GATES), ((0, 0), (0, LANES - B_HEADS * N_GATES)))
            qa, ka, va, qk, vb, ob, gates = _inproj_ab_call(geo, tm, x, mod_l, g_mix, w_pad, qg, kg, gate_b)
            sc, bounded_ok = _attn_scalars(bias, tq, q_norm_g[j], k_norm_g[j])
            ya = lax.cond(bounded_ok,
                          functools.partial(_attn_call, geo, tq, lam_init, True),
                          functools.partial(_attn_call, geo, tq, lam_init, False),
                          qa, ka, va, bias, sc, diff_lambda[j], diff_norm_g[j].reshape(1, A_V_DIM))
            qb, kb = _conv_call(geo, tscan, qk, mlstm_conv_w[j], mlstm_conv_b[j].reshape(1, 2 * B_WIDTH))
            yb = _mlstm_call(geo, tscan, qb, kb, vb, gates, ob, mlstm_norm_g[j].reshape(1, B_DIM))
            w_o = w_out_ab[j].astype(MXU_DT)
            ys, ws = [ya, yb], [w_o[:A_WIDTH], w_o[A_WIDTH:]]
        else:
            q, k, v, gt = _inproj_c_call(geo, tmc, x, mod_l, g_mix, w_in_c[j].astype(MXU_DT), cos, sin)
            y = _ret_call(geo, tscan, q, k, v, ret_decay_logit[j], gt, ret_norm_g[j].reshape(1, C_V_DIM))
            ys, ws = [y], [w_out_c[j].astype(MXU_DT)]
        x, hx, bucket = _outproj_call(geo, tm, ys, ws, x, mod_l, g_ffn, rw_t, rb)
        pos, te0, te1, n_tiles = _plan_call(tmoe, bucket)
        xs = _dispatch_call(tm, pos, hx, xs)
        ysort = _moe_call(tmoe, l, te0, te1, n_tiles, xs, w1, w3, w2)
        x = _collect_call(geo, tm, pos, ysort, x, mod_l)

    y_prompt = x[:geo.tp].reshape(x_prompt.shape)
    y_sample = x[geo.tp:].reshape(x_sample.shape)
    return (y_prompt, y_sample)


def kernel(x_prompt, x_sample, c_prompt, c_sample, rel_bias, router_w, router_b, ada_w, ada_b, norm_mix_g, norm_ffn_g, w_in_ab, w_out_ab, q_norm_g, k_norm_g, diff_lambda, diff_norm_g, mlstm_conv_w, mlstm_conv_b, mlstm_gate_b, mlstm_norm_g, w_in_c, w_out_c, ret_decay_logit, ret_norm_g, moe_w1, moe_w3, moe_w2):
    geo = Geo(x_prompt.shape[0], x_prompt.shape[1], x_sample.shape[0], x_sample.shape[1])
    return _forward(geo, _tiles(geo), x_prompt, x_sample, c_prompt, c_sample, rel_bias, router_w, router_b,
                    ada_w, ada_b, norm_mix_g, norm_ffn_g, w_in_ab, w_out_ab, q_norm_g, k_norm_g, diff_lambda,
                    diff_norm_g, mlstm_conv_w, mlstm_conv_b, mlstm_gate_b, mlstm_norm_g, w_in_c, w_out_c,
                    ret_decay_logit, ret_norm_g, moe_w1, moe_w3, moe_w2)
```

```python
import functools
import math

import jax
import jax.numpy as jnp
import numpy as np
from jax import lax
from jax.experimental import pallas as pl
from jax.experimental.pallas import tpu as pltpu

F32 = jnp.float32
MXU_DT = jnp.bfloat16
ACT_DT = jnp.bfloat16
HI = lax.Precision.HIGHEST

D_MODEL = 1024
DEPTH = 4
A_HEADS = 4
A_QK_DIM = 64
A_V_DIM = 128
A_QK_WIDTH = 512
A_WIDTH = 512
B_HEADS = 4
B_DIM = 128
B_WIDTH = 512
N_GATES = 4
AB_IN = 3600
AB_IN_PAD = 3712
C_HEADS = 4
C_QK_DIM = 256
C_V_DIM = 512
C_QK_WIDTH = 1024
C_V_WIDTH = 2048
C_IN = 6144
CHUNK = 128
KB_PER_STEP = 4
RET_CHUNK = 256
REL_BUCKETS = 32
REL_MAX_DIST = 128
N_EXPERTS = 16
N_GROUPS = 4
EPG = 4
N_PAIRS = 6
N_BUCKETS = N_GROUPS * N_PAIRS
D_FF = 512
ROPE_BASE = 10000.0
EPS = 1e-6
LANES = 128
SUBLANES = 8
NEG_BIG = -1e30
LOG2E = math.log2(math.e)
HX_H = D_MODEL // 2
HX_W = HX_H + LANES
VMEM_LIMIT = 56 * 1024 * 1024


def _cparams(sem):
    return pltpu.CompilerParams(dimension_semantics=sem, vmem_limit_bytes=VMEM_LIMIT)


def _mm(a, b):
    return jnp.dot(a.astype(MXU_DT), b.astype(MXU_DT), preferred_element_type=F32)


def _mm_nt(a, b):
    return lax.dot_general(a.astype(MXU_DT), b.astype(MXU_DT), (((1,), (1,)), ((), ())),
                           preferred_element_type=F32)


def _mm_tn(a, b):
    return lax.dot_general(a.astype(MXU_DT), b.astype(MXU_DT), (((0,), (0,)), ((), ())),
                           preferred_element_type=F32)


def _mm_hi(a, b):
    return jnp.dot(a, b, precision=HI, preferred_element_type=F32)


def _silu(x):
    return x * (1.0 / (1.0 + jnp.exp(-x)))


def _sigmoid(x):
    return 1.0 / (1.0 + jnp.exp(-x))


def _log_sigmoid(x):
    return jnp.minimum(x, 0.0) - jnp.log1p(jnp.exp(-jnp.abs(x)))


def _rms(x, g):
    return x * lax.rsqrt(jnp.mean(x * x, axis=-1, keepdims=True) + EPS) * g


class Geo:
    def __init__(self, bp, sp, bs, ss):
        self.bp, self.sp, self.bs, self.ss = bp, sp, bs, ss
        self.tp = bp * sp
        self.t = bp * sp + bs * ss
        self.seg = math.gcd(sp, ss)

    def seq_start(self, row0):
        return jnp.where(row0 < self.tp, row0 % self.sp == 0, (row0 - self.tp) % self.ss == 0)

    def pos_block(self, blk, rows):
        nbp = self.tp // rows
        return jnp.where(blk < nbp, blk % (self.sp // rows), (blk - nbp) % (self.ss // rows))


def _ada_kernel(c_ref, w_ref, b_ref, o_ref):
    o_ref[...] = _mm_hi(_silu(c_ref[...]), w_ref[...]) + b_ref[...]


def _ada_call(c_rows, ada_w, ada_b):
    r = c_rows.shape[0]
    nb = 1536
    out = pl.pallas_call(
        _ada_kernel,
        grid=(DEPTH, 6 * D_MODEL // nb),
        in_specs=[pl.BlockSpec((r, D_MODEL), lambda l, n: (0, 0)),
                  pl.BlockSpec((None, D_MODEL, nb), lambda l, n: (l, 0, n)),
                  pl.BlockSpec((None, 1, nb), lambda l, n: (l, 0, n))],
        out_specs=pl.BlockSpec((None, r, nb), lambda l, n: (l, 0, n)),
        out_shape=jax.ShapeDtypeStruct((DEPTH, r, 6 * D_MODEL), F32),
        compiler_params=_cparams(("arbitrary", "arbitrary")),
        name="ada_mod",
    )(c_rows, ada_w, ada_b.reshape(DEPTH, 1, 6 * D_MODEL))
    return out.reshape(DEPTH, r, 6, D_MODEL)


def _t5_bucket(rel):
    nb = REL_BUCKETS // 2
    max_exact = nb // 2
    ret = jnp.where(rel > 0, nb, 0)
    n = jnp.abs(rel)
    nf = jnp.maximum(n, 1).astype(jnp.float32)
    large = max_exact + (jnp.log(nf / max_exact) / math.log(REL_MAX_DIST / max_exact) * (nb - max_exact)).astype(jnp.int32)
    large = jnp.minimum(large, nb - 1)
    return ret + jnp.where(n < max_exact, n, large)


def _bias_kernel(tq, rb_ref, bk_ref, o_ref):
    h = pl.program_id(0)
    bk = bk_ref[...]
    row = jnp.zeros(bk.shape, F32)
    bmax = rb_ref[0, h]
    for b in range(REL_BUCKETS):
        row = row + jnp.where(bk == b, rb_ref[b, h], 0.0)
        bmax = jnp.maximum(bmax, rb_ref[b, h])
    row = (row - bmax) * LOG2E
    table = jnp.broadcast_to(row, (tq, 2 * tq))
    o_ref[...] = pltpu.roll(table, tq + 1, axis=1, stride=1, stride_axis=0)[:, :tq]


def _bias_call(rel_bias, tq):
    rel = (jnp.arange(-1, 2, dtype=jnp.int32) * tq)[:, None, None] + (jnp.arange(2 * tq, dtype=jnp.int32) - (tq - 1))
    buckets = _t5_bucket(rel)
    return pl.pallas_call(
        functools.partial(_bias_kernel, tq),
        grid=(A_HEADS, 3),
        in_specs=[pl.BlockSpec(memory_space=pltpu.SMEM),
                  pl.BlockSpec((None, 1, 2 * tq), lambda h, o: (o, 0, 0))],
        out_specs=pl.BlockSpec((None, None, tq, tq), lambda h, o: (h, o, 0, 0)),
        out_shape=jax.ShapeDtypeStruct((A_HEADS, 3, tq, tq), F32),
        compiler_params=_cparams(("arbitrary", "arbitrary")),
        name="rel_bias_tiles",
    )(rel_bias, buckets)


def _rope_tables(s):
    d = C_QK_DIM
    inv = ROPE_BASE ** (-jnp.arange(0, d, 2, dtype=jnp.float32) / d)
    ang = jnp.arange(s, dtype=jnp.float32)[:, None] * inv[None, :]
    return jnp.cos(ang), jnp.sin(ang)


def _half_rms(z, g):
    lo_lane = lax.broadcasted_iota(jnp.int32, (1, LANES), 1) < A_QK_DIM
    z2 = z * z
    tot = jnp.sum(z2, axis=-1, keepdims=True)
    lo = jnp.sum(jnp.where(lo_lane, z2, 0.0), axis=-1, keepdims=True)
    ms = jnp.where(lo_lane, lo, tot - lo) * (1.0 / A_QK_DIM)
    return z * lax.rsqrt(ms + EPS) * g


def _inproj_ab_kernel(x_ref, mod_ref, g_ref, w_ref, qg_ref, kg_ref, gb_ref,
                      qa_ref, ka_ref, va_ref, qk_ref, vb_ref, ob_ref, gt_ref):
    x = x_ref[...]
    h = _rms(x, g_ref[...]) * (1.0 + mod_ref[1:2, :]) + mod_ref[0:1, :]
    hb = h.astype(MXU_DT)
    qscale = (A_QK_DIM ** -0.5) * LOG2E
    for hd in range(A_HEADS):
        c0 = hd * LANES
        q = jnp.dot(hb, w_ref[:, c0:c0 + LANES], preferred_element_type=F32)
        qa_ref[:, c0:c0 + LANES] = (_half_rms(q, qg_ref[...]) * qscale).astype(qa_ref.dtype)
        k = jnp.dot(hb, w_ref[:, A_QK_WIDTH + c0:A_QK_WIDTH + c0 + LANES], preferred_element_type=F32)
        ka_ref[:, c0:c0 + LANES] = _half_rms(k, kg_ref[...]).astype(ka_ref.dtype)
    o = 2 * A_QK_WIDTH
    va_ref[...] = jnp.dot(hb, w_ref[:, o:o + A_WIDTH], preferred_element_type=F32).astype(va_ref.dtype)
    o += A_WIDTH
    qk_ref[...] = jnp.dot(hb, w_ref[:, o:o + 2 * B_WIDTH], preferred_element_type=F32)
    o += 2 * B_WIDTH
    vb_ref[...] = jnp.dot(hb, w_ref[:, o:o + B_WIDTH], preferred_element_type=F32).astype(vb_ref.dtype)
    o += B_WIDTH
    ob_ref[...] = jnp.dot(hb, w_ref[:, o:o + B_WIDTH], preferred_element_type=F32)
    o += B_WIDTH
    gt_ref[...] = jnp.dot(hb, w_ref[:, o:o + LANES], preferred_element_type=F32) + gb_ref[...]


def _inproj_ab_call(geo, tm, x, mod_l, g, w_pad, qg, kg, gate_b):
    t = geo.t
    per_seg = geo.seg // tm
    row = lambda i: (i, 0)
    const = lambda i: (0, 0)
    widths = (A_QK_WIDTH, A_QK_WIDTH, A_WIDTH, 2 * B_WIDTH, B_WIDTH, B_WIDTH, LANES)
    dtypes = (ACT_DT, ACT_DT, ACT_DT, F32, ACT_DT, F32, F32)
    return pl.pallas_call(
        _inproj_ab_kernel,
        grid=(t // tm,),
        in_specs=[pl.BlockSpec((tm, D_MODEL), row),
                  pl.BlockSpec((None, 6, D_MODEL), lambda i: (i // per_seg, 0, 0)),
                  pl.BlockSpec((1, D_MODEL), const),
                  pl.BlockSpec((D_MODEL, AB_IN_PAD), const),
                  pl.BlockSpec((1, LANES), const),
                  pl.BlockSpec((1, LANES), const),
                  pl.BlockSpec((1, LANES), const)],
        out_specs=[pl.BlockSpec((tm, w), row) for w in widths],
        out_shape=[jax.ShapeDtypeStruct((t, w), d) for w, d in zip(widths, dtypes)],
        compiler_params=_cparams(("parallel",)),
        name="inproj_ab",
    )(x, mod_l, g, w_pad, qg, kg, gate_b)


def _conv_kernel(geo, tc, x_ref, prev_ref, next_ref, w_ref, b_ref, q_ref, k_ref):
    i = pl.program_id(0)
    row0 = i * tc
    x = x_ref[...]
    first = geo.seq_start(row0)
    last = geo.seq_start(row0 + tc) | (row0 + tc == geo.t)
    prev_row = jnp.where(first, 0.0, prev_ref[7:8, :])
    next_row = jnp.where(last, 0.0, next_ref[0:1, :])
    ridx = lax.broadcasted_iota(jnp.int32, (tc, 1), 0)
    x_prev = jnp.where(ridx == 0, prev_row, pltpu.roll(x, 1, axis=0))
    x_next = jnp.where(ridx == tc - 1, next_row, pltpu.roll(x, tc - 1, axis=0))
    y = x_prev * w_ref[0:1, :] + x * w_ref[1:2, :] + x_next * w_ref[2:3, :] + b_ref[...]
    y = _silu(y)
    q_ref[...] = y[:, :B_WIDTH].astype(q_ref.dtype)
    k_ref[...] = (y[:, B_WIDTH:] * (B_DIM ** -0.5)).astype(k_ref.dtype)


def _conv_call(geo, tc, qk, w, b):
    t = geo.t
    r8 = tc // 8
    nb8 = t // 8
    return pl.pallas_call(
        functools.partial(_conv_kernel, geo, tc),
        grid=(t // tc,),
        in_specs=[pl.BlockSpec((tc, 2 * B_WIDTH), lambda i: (i, 0)),
                  pl.BlockSpec((8, 2 * B_WIDTH), lambda i: (jnp.maximum(i * r8 - 1, 0), 0)),
                  pl.BlockSpec((8, 2 * B_WIDTH), lambda i: (jnp.minimum((i + 1) * r8, nb8 - 1), 0)),
                  pl.BlockSpec((3, 2 * B_WIDTH), lambda i: (0, 0)),
                  pl.BlockSpec((1, 2 * B_WIDTH), lambda i: (0, 0))],
        out_specs=[pl.BlockSpec((tc, B_WIDTH), lambda i: (i, 0))] * 2,
        out_shape=[jax.ShapeDtypeStruct((t, B_WIDTH), ACT_DT)] * 2,
        compiler_params=_cparams(("parallel",)),
        name="mlstm_conv",
    )(qk, qk, qk, w, b)


def _attn_finish(lam_init, acc0, l0, acc1, l1, dl_ref, ng_ref, o_ref):
    dl = dl_ref[...]
    lam = (jnp.exp(jnp.sum(dl[0:1] * dl[1:2], axis=-1, keepdims=True))
           - jnp.exp(jnp.sum(dl[2:3] * dl[3:4], axis=-1, keepdims=True)) + lam_init)
    out = acc0 / l0 - lam * (acc1 / l1)
    o_ref[...] = (_rms(out, ng_ref[...]) * (1.0 - lam_init)).astype(o_ref.dtype)


def _attn_kernel(lam_init, nk, q_ref, k_ref, v_ref, bias_ref, sc_ref, dl_ref, ng_ref, o_ref,
                 m_sc, l_sc, acc_sc):
    h = pl.program_id(1)
    i = pl.program_id(2)
    j = pl.program_id(3)

    @pl.when(j == 0)
    def _():
        m_sc[...] = jnp.full(m_sc.shape, NEG_BIG, F32)
        l_sc[...] = jnp.zeros(l_sc.shape, F32)
        acc_sc[...] = jnp.zeros(acc_sc.shape, F32)

    q = q_ref[...]
    lo_lane = lax.broadcasted_iota(jnp.int32, (1, LANES), 1) < A_QK_DIM
    qsub = (jnp.where(lo_lane, q, jnp.zeros_like(q)), jnp.where(lo_lane, jnp.zeros_like(q), q))
    k = k_ref[...]
    v = v_ref[...]

    def step(near):
        if near:
            shift = 0.0
        else:
            shift = jnp.where(j < i, sc_ref[h, 0], sc_ref[h, 1])
        for sub in range(2):
            s = _mm_nt(qsub[sub], k)
            if near:
                s = s + bias_ref[...]
            m_old = m_sc[sub]
            m_new = jnp.maximum(m_old, jnp.max(s, axis=-1, keepdims=True) + shift)
            p = jnp.exp2(s - (m_new - shift))
            alpha = jnp.exp2(m_old - m_new)
            l_sc[sub] = alpha * l_sc[sub] + jnp.sum(p, axis=-1, keepdims=True)
            acc_sc[sub] = alpha * acc_sc[sub] + _mm(p, v)
            m_sc[sub] = m_new

    near = jnp.abs(j - i) <= 1
    pl.when(near)(lambda: step(True))
    pl.when(jnp.logical_not(near))(lambda: step(False))

    @pl.when(j == nk - 1)
    def _():
        _attn_finish(lam_init, acc_sc[0], l_sc[0], acc_sc[1], l_sc[1], dl_ref, ng_ref, o_ref)


def _attn_bounded_kernel(lam_init, nk, q_ref, k_ref, v_ref, bias_ref, sc_ref, dl_ref, ng_ref, o_ref, acc_sc):
    h = pl.program_id(1)
    i = pl.program_id(2)
    j = pl.program_id(3)
    tq = q_ref.shape[0]

    @pl.when(j == 0)
    def _():
        acc_sc[...] = jnp.zeros(acc_sc.shape, F32)

    q = q_ref[...]
    lo = (lax.broadcasted_iota(jnp.int32, q.shape, 1) < A_QK_DIM).astype(F32).astype(q.dtype)
    qsub = (q * lo, q * (1 - lo))
    ones_col = (lax.broadcasted_iota(jnp.int32, (tq, LANES), 1) == 0).astype(v_ref.dtype)

    for kb in range(KB_PER_STEP):
        jb = j * KB_PER_STEP + kb

        @pl.when(jb == jnp.maximum(i - 1, 0))
        def _():
            acc_sc[...] = acc_sc[...] * sc_ref[h, 2]

        @pl.when(jb == i + 2)
        def _():
            acc_sc[...] = acc_sc[...] * sc_ref[h, 3]

        def step(near, kb=kb, jb=jb):
            k = k_ref[kb * tq:(kb + 1) * tq, :]
            v_aug = jnp.concatenate([v_ref[kb * tq:(kb + 1) * tq, :], ones_col], axis=1)
            for sub in range(2):
                s = _mm_nt(qsub[sub], k)
                if near:
                    s = s + bias_ref[jb - i + 1]
                acc_sc[sub] += _mm(jnp.exp2(s), v_aug)

        near = jnp.abs(jb - i) <= 1
        pl.when(near)(functools.partial(step, True))
        pl.when(jnp.logical_not(near))(functools.partial(step, False))

    @pl.when(j == nk - 1)
    def _():
        a0 = acc_sc[0]
        a1 = acc_sc[1]
        _attn_finish(lam_init, a0[:, :A_V_DIM], a0[:, A_V_DIM:A_V_DIM + 1], a1[:, :A_V_DIM],
                     a1[:, A_V_DIM:A_V_DIM + 1], dl_ref, ng_ref, o_ref)


def _attn_scalars(bias, tq, q_gain, k_gain):
    far_l = bias[:, 0, tq - 1, 0]
    far_r = bias[:, 2, 0, tq - 1]
    sc = jnp.stack([far_l, far_r, jnp.exp2(far_l), jnp.exp2(-far_r)], axis=-1)
    bound = A_QK_DIM * jnp.max(jnp.abs(q_gain)) * jnp.max(jnp.abs(k_gain)) * (A_QK_DIM ** -0.5) * LOG2E * 1.02
    spread = -jnp.min(bias)
    ok = bound + 2.0 * spread <= 80.0
    return sc, ok


def _attn_call(geo, tq, lam_init, bounded, qa, ka, va, bias, sc, dl, ng):
    outs = []
    kb = KB_PER_STEP if bounded else 1
    for (nb, s, row_off) in ((geo.bp, geo.sp, 0), (geo.bs, geo.ss, geo.tp)):
        nq = s // tq
        nk = nq // kb
        off = row_off // tq
        qmap = lambda b, h, i, j, off=off, nq=nq: (off + b * nq + i, h)
        kmap = lambda b, h, i, j, off=off, nk=nk: (off // kb + b * nk + j, h)
        omap = lambda b, h, i, j, nq=nq: (b * nq + i, h)
        if bounded:
            body = functools.partial(_attn_bounded_kernel, lam_init, nk)
            scratch = [pltpu.VMEM((2, tq, 2 * LANES), F32)]
            bias_spec = pl.BlockSpec((None, 3, tq, tq), lambda b, h, i, j: (h, 0, 0, 0))
        else:
            body = functools.partial(_attn_kernel, lam_init, nk)
            scratch = [pltpu.VMEM((2, tq, 1), F32), pltpu.VMEM((2, tq, 1), F32), pltpu.VMEM((2, tq, LANES), F32)]
            bias_spec = pl.BlockSpec((None, None, tq, tq), lambda b, h, i, j: (h, jnp.clip(j - i + 1, 0, 2), 0, 0))
        outs.append(pl.pallas_call(
            body,
            grid=(nb, A_HEADS, nq, nk),
            in_specs=[pl.BlockSpec((tq, LANES), qmap),
                      pl.BlockSpec((kb * tq, LANES), kmap),
                      pl.BlockSpec((kb * tq, LANES), kmap),
                      bias_spec,
                      pl.BlockSpec(memory_space=pltpu.SMEM),
                      pl.BlockSpec((4, A_QK_DIM), lambda b, h, i, j: (0, 0)),
                      pl.BlockSpec((1, LANES), lambda b, h, i, j: (0, 0))],
            out_specs=pl.BlockSpec((tq, LANES), omap),
            out_shape=jax.ShapeDtypeStruct((nb * s, A_WIDTH), ACT_DT),
            scratch_shapes=scratch,
            compiler_params=_cparams(("parallel", "parallel", "parallel", "arbitrary")),
            name="diff_attn_bounded" if bounded else "diff_attn",
        )(qa, ka, va, bias, sc, dl, ng))
    return jnp.concatenate(outs, axis=0)


def _tri(lower):
    r = lax.broadcasted_iota(jnp.int32, (CHUNK, CHUNK), 0)
    c = lax.broadcasted_iota(jnp.int32, (CHUNK, CHUNK), 1)
    return (c <= r) if lower else (c >= r)


def _mlstm_kernel(geo, tb, reverse, *refs):
    if reverse:
        q_ref, k_ref, v_ref, g_ref, hf_ref, ob_ref, ng_ref, o_ref, c_sc, m_sc = refs
    else:
        q_ref, k_ref, v_ref, g_ref, o_ref, c_sc, m_sc = refs
    step = pl.program_id(0)
    nblk = geo.t // tb
    blk = (nblk - 1 - step) if reverse else step
    row0 = blk * tb
    if reverse:
        fresh = geo.seq_start(row0 + tb) | (row0 + tb == geo.t)
    else:
        fresh = geo.seq_start(row0)

    @pl.when(fresh)
    def _():
        c_sc[...] = jnp.zeros(c_sc.shape, F32)
        m_sc[...] = jnp.zeros(m_sc.shape, F32)

    mask = _tri(not reverse)
    cum_l = mask.astype(F32)
    cum_r = _tri(reverse).astype(F32)
    ones_col = (lax.broadcasted_iota(jnp.int32, (CHUNK, LANES), 1) == 0).astype(MXU_DT)
    nch = tb // CHUNK

    def chunk(ci, carry):
        c_idx = (nch - 1 - ci) if reverse else ci
        r0 = pl.multiple_of(c_idx * CHUNK, CHUNK)
        g = g_ref[pl.ds(r0, CHUNK), :]
        g_t = g.T
        b_col = _mm_hi(cum_l, _log_sigmoid(g))
        b_row = _mm_hi(_log_sigmoid(g_t), cum_r)
        for hd in range(B_HEADS):
            ci_col = hd * N_GATES + (2 if reverse else 0)
            cf_col = ci_col + 1
            lanes = slice(hd * B_DIM, (hd + 1) * B_DIM)
            q = q_ref[pl.ds(r0, CHUNK), lanes]
            k = k_ref[pl.ds(r0, CHUNK), lanes]
            v = v_ref[pl.ds(r0, CHUNK), lanes]
            bc = b_col[:, cf_col:cf_col + 1]
            br = b_row[cf_col:cf_col + 1, :]
            ic = g[:, ci_col:ci_col + 1]
            ir = g_t[ci_col:ci_col + 1, :]
            m_prev = m_sc[hd]
            log_d = jnp.where(mask, bc - br + ir, NEG_BIG)
            m_inter = bc + m_prev
            m_t = jnp.maximum(jnp.max(log_d, axis=-1, keepdims=True), m_inter)
            s = _mm_nt(q, k) * jnp.exp(log_d - m_t)
            inter = jnp.exp(m_inter - m_t)
            c_aug = c_sc[hd]
            qc = _mm(q, c_aug)
            num = _mm(s, v) + inter * qc[:, :B_DIM]
            den = jnp.sum(s, axis=-1, keepdims=True) + inter * qc[:, B_DIM:B_DIM + 1]
            hout = num / jnp.maximum(jnp.abs(den), jnp.exp(-m_t))
            b_last = bc[0:1, :] if reverse else bc[CHUNK - 1:CHUNK, :]
            log_w = b_last - bc + ic
            m_new = jnp.maximum(b_last + m_prev, jnp.max(log_w, axis=0, keepdims=True))
            w = jnp.exp(log_w - m_new)
            decay = jnp.exp(b_last + m_prev - m_new)
            v_aug = jnp.concatenate([v, ones_col], axis=1)
            c_sc[hd] = decay * c_aug + _mm_tn(k.astype(F32) * w, v_aug)
            m_sc[hd] = m_new
            if reverse:
                hsum = hf_ref[pl.ds(r0, CHUNK), lanes].astype(F32) + hout
                y = _rms(hsum, ng_ref[...]) * _sigmoid(ob_ref[pl.ds(r0, CHUNK), lanes])
                o_ref[pl.ds(r0, CHUNK), lanes] = y.astype(o_ref.dtype)
            else:
                o_ref[pl.ds(r0, CHUNK), lanes] = hout.astype(o_ref.dtype)
        return carry

    lax.fori_loop(0, nch, chunk, 0)


def _mlstm_call(geo, tb, qb, kb, vb, gates, ob, ng):
    t = geo.t
    nblk = t // tb
    fmap = lambda s: (s, 0)
    rmap = lambda s: (nblk - 1 - s, 0)
    scratch = [pltpu.VMEM((B_HEADS, B_DIM, 2 * B_DIM), F32), pltpu.VMEM((B_HEADS, 1, 1), F32)]
    wide = lambda m: pl.BlockSpec((tb, B_WIDTH), m)
    hf = pl.pallas_call(
        functools.partial(_mlstm_kernel, geo, tb, False),
        grid=(nblk,),
        in_specs=[wide(fmap), wide(fmap), wide(fmap), pl.BlockSpec((tb, LANES), fmap)],
        out_specs=wide(fmap),
        out_shape=jax.ShapeDtypeStruct((t, B_WIDTH), ACT_DT),
        scratch_shapes=scratch,
        compiler_params=_cparams(("arbitrary",)),
        name="mlstm_fwd",
    )(qb, kb, vb, gates)
    return pl.pallas_call(
        functools.partial(_mlstm_kernel, geo, tb, True),
        grid=(nblk,),
        in_specs=[wide(rmap), wide(rmap), wide(rmap), pl.BlockSpec((tb, LANES), rmap), wide(rmap), wide(rmap),
                  pl.BlockSpec((1, B_DIM), lambda s: (0, 0))],
        out_specs=wide(rmap),
        out_shape=jax.ShapeDtypeStruct((t, B_WIDTH), ACT_DT),
        scratch_shapes=scratch,
        compiler_params=_cparams(("arbitrary",)),
        name="mlstm_bwd",
    )(qb, kb, vb, gates, hf, ob, ng)


def _inproj_c_kernel(x_ref, mod_ref, g_ref, w_ref, cos_ref, sin_ref, q_ref, k_ref, v_ref, gt_ref, h_sc):
    j = pl.program_id(1)
    nsub = C_HEADS
    sub = w_ref.shape[1] // nsub

    def proj(c):
        return jnp.dot(h_sc[...], w_ref[:, c * sub:(c + 1) * sub], preferred_element_type=F32)

    def rope(o_ref, scale):
        cos = cos_ref[...] * scale
        sin = sin_ref[...] * scale
        half = C_QK_DIM // 2
        for hd in range(C_HEADS):
            y = proj(hd)
            x1, x2 = y[:, :half], y[:, half:]
            o_ref[:, hd * C_QK_DIM:hd * C_QK_DIM + half] = (x1 * cos - x2 * sin).astype(o_ref.dtype)
            o_ref[:, hd * C_QK_DIM + half:(hd + 1) * C_QK_DIM] = (x1 * sin + x2 * cos).astype(o_ref.dtype)

    @pl.when(j == 0)
    def _():
        x = x_ref[...]
        h = _rms(x, g_ref[...]) * (1.0 + mod_ref[1:2, :]) + mod_ref[0:1, :]
        h_sc[...] = h.astype(h_sc.dtype)
        rope(q_ref, 1.0)

    @pl.when(j == 1)
    def _():
        rope(k_ref, C_QK_DIM ** -0.5)

    @pl.when((j == 2) | (j == 3))
    def _():
        for c in range(nsub):
            v_ref[:, c * sub:(c + 1) * sub] = proj(c).astype(v_ref.dtype)

    @pl.when(j >= 4)
    def _():
        for c in range(nsub):
            gt_ref[:, c * sub:(c + 1) * sub] = _silu(proj(c)).astype(gt_ref.dtype)


def _inproj_c_call(geo, tm, x, mod_l, g, w, cos, sin):
    t = geo.t
    per_seg = geo.seg // tm
    nw = D_MODEL
    return pl.pallas_call(
        _inproj_c_kernel,
        grid=(t // tm, C_IN // nw),
        in_specs=[pl.BlockSpec((tm, D_MODEL), lambda i, j: (i, 0)),
                  pl.BlockSpec((None, 6, D_MODEL), lambda i, j: (i // per_seg, 0, 0)),
                  pl.BlockSpec((1, D_MODEL), lambda i, j: (0, 0)),
                  pl.BlockSpec((D_MODEL, nw), lambda i, j: (0, j)),
                  pl.BlockSpec((tm, C_QK_DIM // 2), lambda i, j: (geo.pos_block(i, tm), 0)),
                  pl.BlockSpec((tm, C_QK_DIM // 2), lambda i, j: (geo.pos_block(i, tm), 0))],
        out_specs=[pl.BlockSpec((tm, nw), lambda i, j: (i, 0)),
                   pl.BlockSpec((tm, nw), lambda i, j: (i, 0)),
                   pl.BlockSpec((tm, nw), lambda i, j: (i, jnp.clip(j - 2, 0, 1))),
                   pl.BlockSpec((tm, nw), lambda i, j: (i, jnp.clip(j - 4, 0, 1)))],
        out_shape=[jax.ShapeDtypeStruct((t, C_QK_WIDTH), ACT_DT),
                   jax.ShapeDtypeStruct((t, C_QK_WIDTH), ACT_DT),
                   jax.ShapeDtypeStruct((t, C_V_WIDTH), ACT_DT),
                   jax.ShapeDtypeStruct((t, C_V_WIDTH), ACT_DT)],
        scratch_shapes=[pltpu.VMEM((tm, D_MODEL), MXU_DT)],
        compiler_params=_cparams(("parallel", "arbitrary")),
        name="inproj_c",
    )(x, mod_l, g, w, cos, sin)


def _ret_kernel(geo, tb, reverse, *refs):
    if reverse:
        q_ref, k_ref, v_ref, dlg_ref, yf_ref, gt_ref, ng_ref, o_ref, r_sc, intra_sc, vec_sc = refs
    else:
        q_ref, k_ref, v_ref, dlg_ref, o_ref, r_sc, intra_sc, vec_sc = refs
    step = pl.program_id(0)
    nblk = geo.t // tb
    blk = (nblk - 1 - step) if reverse else step
    row0 = blk * tb
    if reverse:
        fresh = geo.seq_start(row0 + tb) | (row0 + tb == geo.t)
    else:
        fresh = geo.seq_start(row0)

    @pl.when(fresh)
    def _():
        r_sc[...] = jnp.zeros(r_sc.shape, F32)

    @pl.when(step == 0)
    def _():
        lg_all = _log_sigmoid(dlg_ref[...])
        ti = lax.broadcasted_iota(jnp.int32, (RET_CHUNK, RET_CHUNK), 0)
        si = lax.broadcasted_iota(jnp.int32, (RET_CHUNK, RET_CHUNK), 1)
        dist = ((si - ti) if reverse else (ti - si)).astype(F32)
        pos = lax.broadcasted_iota(jnp.int32, (RET_CHUNK, LANES), 0).astype(F32)
        upos = (RET_CHUNK - 1.0 - pos) if reverse else pos
        lane = lax.broadcasted_iota(jnp.int32, (RET_CHUNK, LANES), 1)
        d = 1 if reverse else 0
        for hd in range(C_HEADS):
            lg = lg_all[d:d + 1, hd:hd + 1]
            intra_sc[hd] = jnp.where(dist >= 0, jnp.exp(jnp.maximum(dist, 0.0) * lg), 0.0)
            vec_sc[hd] = jnp.where(lane == 0, jnp.exp((upos + 1.0) * lg),
                                   jnp.where(lane == 1, jnp.exp((RET_CHUNK - 1.0 - upos) * lg),
                                             jnp.exp(RET_CHUNK * lg)))

    nch = tb // RET_CHUNK

    def chunk(ci, carry):
        c_idx = (nch - 1 - ci) if reverse else ci
        r0 = pl.multiple_of(c_idx * RET_CHUNK, RET_CHUNK)
        for hd in range(C_HEADS):
            vec = vec_sc[hd]
            q_scale, k_scale, c_decay = vec[:, 0:1], vec[:, 1:2], vec[0:1, 2:3]
            ql = slice(hd * C_QK_DIM, (hd + 1) * C_QK_DIM)
            vl = slice(hd * C_V_DIM, (hd + 1) * C_V_DIM)
            q = q_ref[pl.ds(r0, RET_CHUNK), ql]
            k = k_ref[pl.ds(r0, RET_CHUNK), ql]
            v = v_ref[pl.ds(r0, RET_CHUNK), vl]
            r_old = r_sc[hd]
            s = _mm_nt(q, k) * intra_sc[hd]
            y = _mm(s, v) + q_scale * _mm(q, r_old)
            r_sc[hd] = c_decay * r_old + _mm_tn(k.astype(F32) * k_scale, v)
            if reverse:
                ysum = yf_ref[pl.ds(r0, RET_CHUNK), vl].astype(F32) + y
                out = _rms(ysum, ng_ref[...]) * gt_ref[pl.ds(r0, RET_CHUNK), vl].astype(F32)
                o_ref[pl.ds(r0, RET_CHUNK), vl] = out.astype(o_ref.dtype)
            else:
                o_ref[pl.ds(r0, RET_CHUNK), vl] = y.astype(o_ref.dtype)
        return carry

    lax.fori_loop(0, nch, chunk, 0)


def _ret_call(geo, tb, q, k, v, decay_logit, gt, ng):
    t = geo.t
    nblk = t // tb
    fmap = lambda s: (s, 0)
    rmap = lambda s: (nblk - 1 - s, 0)
    scratch = [pltpu.VMEM((C_HEADS, C_QK_DIM, C_V_DIM), F32), pltpu.VMEM((C_HEADS, RET_CHUNK, RET_CHUNK), F32),
               pltpu.VMEM((C_HEADS, RET_CHUNK, LANES), F32)]
    qk = lambda m: pl.BlockSpec((tb, C_QK_WIDTH), m)
    vv = lambda m: pl.BlockSpec((tb, C_V_WIDTH), m)
    dspec = pl.BlockSpec((2, C_HEADS), lambda s: (0, 0))
    yf = pl.pallas_call(
        functools.partial(_ret_kernel, geo, tb, False),
        grid=(nblk,),
        in_specs=[qk(fmap), qk(fmap), vv(fmap), dspec],
        out_specs=vv(fmap),
        out_shape=jax.ShapeDtypeStruct((t, C_V_WIDTH), ACT_DT),
        scratch_shapes=scratch,
        compiler_params=_cparams(("arbitrary",)),
        name="ret_fwd",
    )(q, k, v, decay_logit)
    return pl.pallas_call(
        functools.partial(_ret_kernel, geo, tb, True),
        grid=(nblk,),
        in_specs=[qk(rmap), qk(rmap), vv(rmap), dspec, vv(rmap), vv(rmap),
                  pl.BlockSpec((1, C_V_DIM), lambda s: (0, 0))],
        out_specs=vv(rmap),
        out_shape=jax.ShapeDtypeStruct((t, C_V_WIDTH), ACT_DT),
        scratch_shapes=scratch,
        compiler_params=_cparams(("arbitrary",)),
        name="ret_bwd",
    )(q, k, v, decay_logit, yf, gt, ng)


def _route(probs):
    p = [probs[e:e + 1, :] for e in range(N_EXPERTS)]
    scores = []
    for g in range(N_GROUPS):
        a, b, c, d = p[EPG * g:EPG * g + EPG]
        hi1, lo1 = jnp.maximum(a, b), jnp.minimum(a, b)
        hi2, lo2 = jnp.maximum(c, d), jnp.minimum(c, d)
        scores.append(jnp.maximum(hi1, hi2) + jnp.maximum(jnp.minimum(hi1, hi2), jnp.maximum(lo1, lo2)))
    g_sel = jnp.zeros(scores[0].shape, jnp.int32)
    best = scores[0]
    for g in range(1, N_GROUPS):
        better = scores[g] > best
        g_sel = jnp.where(better, g, g_sel)
        best = jnp.where(better, scores[g], best)
    vals = []
    for kk in range(EPG):
        v = p[kk]
        for g in range(1, N_GROUPS):
            v = jnp.where(g_sel == g, p[EPG * g + kk], v)
        vals.append(v)

    def argmax4(xs):
        idx = jnp.zeros(xs[0].shape, jnp.int32)
        top = xs[0]
        for kk in range(1, EPG):
            better = xs[kk] > top
            idx = jnp.where(better, kk, idx)
            top = jnp.where(better, xs[kk], top)
        return idx, top

    i1, v1 = argmax4(vals)
    i2, v2 = argmax4([jnp.where(i1 == kk, -1.0, vals[kk]) for kk in range(EPG)])
    tot = v1 + v2
    w1, w2 = v1 / tot, v2 / tot
    e1 = g_sel * EPG + i1
    e2 = g_sel * EPG + i2
    eidx = lax.broadcasted_iota(jnp.int32, probs.shape, 0)
    lo, hi = jnp.minimum(i1, i2), jnp.maximum(i1, i2)
    pair = jnp.where(lo == 0, hi - 1, jnp.where(lo == 1, hi + 1, N_PAIRS - 1))
    return jnp.where(eidx == e1, w1, 0.0) + jnp.where(eidx == e2, w2, 0.0), g_sel * N_PAIRS + pair


def _pack_pairs(h):
    n = h.shape[1] // 2
    bits = lax.bitcast_convert_type(h.astype(MXU_DT).astype(F32), jnp.uint32)
    return (bits[:, :n] & jnp.uint32(0xFFFF0000)) | (bits[:, n:] >> 16)


def _unpack_pairs(u):
    hi = lax.bitcast_convert_type(u & jnp.uint32(0xFFFF0000), F32)
    lo = lax.bitcast_convert_type(u << 16, F32)
    return jnp.concatenate([hi, lo], axis=1).astype(MXU_DT)


def _outproj_kernel(nparts, *refs):
    y_refs = refs[:nparts]
    w_refs = refs[nparts:2 * nparts]
    x_ref, mod_ref, g_ref, rw_ref, rb_ref, xo_ref, hx_ref, bkt_ref = refs[2 * nparts:]
    m = jnp.dot(y_refs[0][...], w_refs[0][...], preferred_element_type=F32)
    for p in range(1, nparts):
        m = m + jnp.dot(y_refs[p][...], w_refs[p][...], preferred_element_type=F32)
    x = x_ref[...] + mod_ref[2:3, :] * m
    xo_ref[...] = x
    h = _rms(x, g_ref[...]) * (1.0 + mod_ref[4:5, :]) + mod_ref[3:4, :]
    hx_ref[:, :HX_H] = _pack_pairs(h)
    logits = lax.dot_general(rw_ref[...], h, (((1,), (1,)), ((), ())), precision=HI,
                             preferred_element_type=F32) + rb_ref[...]
    z = jnp.exp(logits - jnp.max(logits, axis=0, keepdims=True))
    probs = z / jnp.sum(z, axis=0, keepdims=True)
    cmb, bucket = _route(probs)
    pad = jnp.zeros((LANES - N_EXPERTS, cmb.shape[1]), F32)
    hx_ref[:, HX_H:] = lax.bitcast_convert_type(jnp.concatenate([cmb, pad], axis=0).T, jnp.uint32)
    bkt_ref[...] = bucket


def _outproj_call(geo, tm, ys, ws, x, mod_l, g, rw_t, rb):
    t = geo.t
    per_seg = geo.seg // tm
    row = lambda i: (i, 0)
    const = lambda i: (0, 0)
    n = len(ys)
    return pl.pallas_call(
        functools.partial(_outproj_kernel, n),
        grid=(t // tm,),
        in_specs=([pl.BlockSpec((tm, y.shape[1]), row) for y in ys]
                  + [pl.BlockSpec(w.shape, const) for w in ws]
                  + [pl.BlockSpec((tm, D_MODEL), row),
                     pl.BlockSpec((None, 6, D_MODEL), lambda i: (i // per_seg, 0, 0)),
                     pl.BlockSpec((1, D_MODEL), const),
                     pl.BlockSpec((N_EXPERTS, D_MODEL), const),
                     pl.BlockSpec((N_EXPERTS, 1), const)]),
        out_specs=[pl.BlockSpec((tm, D_MODEL), row), pl.BlockSpec((tm, HX_W), row),
                   pl.BlockSpec((None, 1, tm), lambda i: (i, 0, 0))],
        out_shape=[jax.ShapeDtypeStruct((t, D_MODEL), F32), jax.ShapeDtypeStruct((t, HX_W), jnp.uint32),
                   jax.ShapeDtypeStruct((t // tm, 1, tm), jnp.int32)],
        compiler_params=_cparams(("parallel",)),
        name="outproj_router",
    )(*ys, *ws, x, mod_l, g, rw_t, rb)


def _plan_kernel(tm, bkt_ref, pos_ref, te_ref, nv_ref):
    nblk, _, blk = bkt_ref.shape
    nrow = 32
    bid = lax.broadcasted_iota(jnp.int32, (nrow, blk), 0)
    r = lax.broadcasted_iota(jnp.int32, (blk, blk), 0)
    c = lax.broadcasted_iota(jnp.int32, (blk, blk), 1)
    prefix = (r <= c).astype(MXU_DT)

    def count(b, acc):
        return acc + jnp.sum((bkt_ref[b] == bid).astype(F32), axis=1, keepdims=True)

    counts = lax.fori_loop(0, nblk, count, jnp.zeros((nrow, 1), F32))
    padded = jnp.floor((counts + (tm - 1.0)) / tm) * tm
    rows = lax.broadcasted_iota(jnp.int32, (nrow, 1), 0)
    offs = jnp.zeros((nrow, 1), F32)
    ends = []
    run = jnp.zeros((1, 1), F32)
    for b in range(N_BUCKETS):
        offs = jnp.where(rows == b, run, offs)
        run = run + padded[b:b + 1, :]
        ends.append(run)

    def place(b, carry):
        oh = (bkt_ref[b] == bid).astype(F32)
        pre = jnp.dot(oh.astype(MXU_DT), prefix, preferred_element_type=F32)
        pos = jnp.sum(oh * (offs + carry + pre - 1.0), axis=0, keepdims=True)
        pos_ref[b] = pos.astype(jnp.int32)
        return carry + pre[:, blk - 1:blk]

    lax.fori_loop(0, nblk, place, jnp.zeros((nrow, 1), F32))
    start = lax.broadcasted_iota(jnp.int32, (1, te_ref.shape[1]), 1).astype(F32) * tm
    tb = jnp.zeros(start.shape, F32)
    for b in range(N_BUCKETS - 1):
        tb = tb + (ends[b] <= start).astype(F32)
    grp = jnp.floor((tb + 0.5) / N_PAIRS)
    pair = tb - grp * N_PAIRS
    lo = (pair >= 3).astype(F32) + (pair >= 5).astype(F32)
    hi = jnp.where(pair == 0, 1.0, jnp.where((pair == 1) | (pair == 3), 2.0, 3.0))
    te_ref[0:1, :] = (grp * EPG + lo).astype(jnp.int32)
    te_ref[1:2, :] = (grp * EPG + hi).astype(jnp.int32)
    nv_ref[...] = jnp.broadcast_to(run / tm, nv_ref.shape).astype(jnp.int32)


def _plan_call(tm, bkt):
    nblk, _, blk = bkt.shape
    nt = nblk * blk // tm + N_BUCKETS
    ntp = -(-nt // LANES) * LANES
    pos, te, nv = pl.pallas_call(
        functools.partial(_plan_kernel, tm),
        out_shape=[jax.ShapeDtypeStruct(bkt.shape, jnp.int32), jax.ShapeDtypeStruct((2, ntp), jnp.int32),
                   jax.ShapeDtypeStruct((1, LANES), jnp.int32)],
        compiler_params=pltpu.CompilerParams(vmem_limit_bytes=VMEM_LIMIT),
        name="moe_plan",
    )(bkt)
    return pos, te[0, :nt], te[1, :nt], nv[0, :1]


def _dispatch_kernel(tb, pos_ref, hx_ref, init_ref, xs_ref, sem):
    del init_ref

    def issue(g, carry):
        base = pl.multiple_of(g * SUBLANES, SUBLANES)
        rows = hx_ref.at[pl.ds(base, SUBLANES)]
        for r in range(SUBLANES):
            pltpu.make_async_copy(rows.at[pl.ds(r, 1)], xs_ref.at[pl.ds(pos_ref[0, base + r], 1)],
                                  sem).start(priority=r % 2)
        return carry

    lax.fori_loop(0, tb // SUBLANES, issue, 0)
    pltpu.make_async_copy(hx_ref, xs_ref.at[pl.ds(0, tb)], sem).wait()


def _dispatch_call(tb, pos, hx, xs_init):
    nsteps = hx.shape[0] // tb
    return pl.pallas_call(
        functools.partial(_dispatch_kernel, tb),
        grid=(nsteps,),
        in_specs=[pl.BlockSpec((None, 1, tb), lambda i: (i, 0, 0), memory_space=pltpu.SMEM),
                  pl.BlockSpec((tb, HX_W), lambda i: (i, 0)),
                  pl.BlockSpec(memory_space=pl.ANY)],
        out_specs=pl.BlockSpec(memory_space=pl.ANY),
        out_shape=jax.ShapeDtypeStruct(xs_init.shape, xs_init.dtype),
        scratch_shapes=[pltpu.SemaphoreType.DMA(())],
        input_output_aliases={2: 0},
        compiler_params=pltpu.CompilerParams(dimension_semantics=("arbitrary",), disable_bounds_checks=True,
                                             has_side_effects=True),
        name="moe_dispatch",
    )(pos, hx, xs_init)


def _moe_kernel(te0_ref, te1_ref, nv_ref, xs_ref, w1a_ref, w3a_ref, w2a_ref, w1b_ref, w3b_ref, w2b_ref, o_ref):
    n = pl.program_id(0)

    @pl.when(n < nv_ref[0])
    def _():
        h = _unpack_pairs(xs_ref[:, :HX_H])
        cmb = lax.bitcast_convert_type(xs_ref[:, HX_H:], F32)
        lane = lax.broadcasted_iota(jnp.int32, cmb.shape, 1)

        def expert(e_id, w1_ref, w3_ref, w2_ref):
            a = jnp.dot(h, w1_ref[...], preferred_element_type=F32)
            b = jnp.dot(h, w3_ref[...], preferred_element_type=F32)
            c = jnp.sum(jnp.where(lane == e_id, cmb, 0.0), axis=-1, keepdims=True)
            return c * _mm(_silu(a) * b, w2_ref[...])

        o_ref[...] = (expert(te0_ref[n], w1a_ref, w3a_ref, w2a_ref)
                      + expert(te1_ref[n], w1b_ref, w3b_ref, w2b_ref))

    @pl.when(n >= nv_ref[0])
    def _():
        o_ref[...] = jnp.zeros(o_ref.shape, F32)


def _moe_call(tm, layer, te0, te1, nv, xs, w1, w3, w2):
    rows = xs.shape[0]
    nt = rows // tm
    tile = lambda n, te0, te1, nv: (jnp.minimum(n, nv[0] - 1), 0)
    wa = lambda n, te0, te1, nv: (layer, te0[jnp.minimum(n, nv[0] - 1)], 0, 0)
    wb = lambda n, te0, te1, nv: (layer, te1[jnp.minimum(n, nv[0] - 1)], 0, 0)
    up = lambda m: pl.BlockSpec((None, None, D_MODEL, D_FF), m)
    down = lambda m: pl.BlockSpec((None, None, D_FF, D_MODEL), m)
    return pl.pallas_call(
        _moe_kernel,
        grid_spec=pltpu.PrefetchScalarGridSpec(
            num_scalar_prefetch=3,
            grid=(nt,),
            in_specs=[pl.BlockSpec((tm, HX_W), tile), up(wa), up(wa), down(wa), up(wb), up(wb), down(wb)],
            out_specs=pl.BlockSpec((tm, D_MODEL), lambda n, te0, te1, nv: (n, 0))),
        out_shape=jax.ShapeDtypeStruct((rows, D_MODEL), F32),
        compiler_params=_cparams(("arbitrary",)),
        name="moe",
    )(te0, te1, nv, xs, w1, w3, w2, w1, w3, w2)


def _collect_kernel(tb, nsteps, pos_ref, posn_ref, ys_ref, x_ref, mod_ref, o_ref, buf, sem):
    i = pl.program_id(0)
    slot = i % 2

    def issue(p_ref, s):
        def body(g, carry):
            base = pl.multiple_of(g * SUBLANES, SUBLANES)
            rows = buf.at[s, pl.ds(base, SUBLANES)]
            for r in range(SUBLANES):
                pltpu.make_async_copy(ys_ref.at[pl.ds(p_ref[0, base + r], 1)], rows.at[pl.ds(r, 1)],
                                      sem.at[s]).start(priority=r % 2)
            return carry
        lax.fori_loop(0, tb // SUBLANES, body, 0)

    pl.when(i == 0)(lambda: issue(pos_ref, 0))
    pl.when(i + 1 < nsteps)(lambda: issue(posn_ref, 1 - slot))
    pltpu.make_async_copy(ys_ref.at[pl.ds(0, tb)], buf.at[slot], sem.at[slot]).wait()
    o_ref[...] = x_ref[...] + mod_ref[5:6, :] * buf[slot]


def _collect_call(geo, tb, pos, ys, x, mod_l):
    t = geo.t
    nsteps = t // tb
    per_seg = geo.seg // tb
    return pl.pallas_call(
        functools.partial(_collect_kernel, tb, nsteps),
        grid=(nsteps,),
        in_specs=[pl.BlockSpec((None, 1, tb), lambda i: (i, 0, 0), memory_space=pltpu.SMEM),
                  pl.BlockSpec((None, 1, tb), lambda i: (jnp.minimum(i + 1, nsteps - 1), 0, 0),
                               memory_space=pltpu.SMEM),
                  pl.BlockSpec(memory_space=pl.ANY),
                  pl.BlockSpec((tb, D_MODEL), lambda i: (i, 0)),
                  pl.BlockSpec((None, 6, D_MODEL), lambda i: (i // per_seg, 0, 0))],
        out_specs=pl.BlockSpec((tb, D_MODEL), lambda i: (i, 0)),
        out_shape=jax.ShapeDtypeStruct((t, D_MODEL), F32),
        scratch_shapes=[pltpu.VMEM((2, tb, D_MODEL), F32), pltpu.SemaphoreType.DMA((2,))],
        compiler_params=pltpu.CompilerParams(dimension_semantics=("arbitrary",), vmem_limit_bytes=VMEM_LIMIT,
                                             disable_bounds_checks=True),
        name="moe_collect",
    )(pos, pos, ys, x, mod_l)


def _tiles(geo):
    seg = geo.seg
    return dict(tm=min(512, seg), tmc=min(1024, seg), tq=min(1024, seg), tscan=min(512, seg), tmoe=min(512, seg))


def _forward(geo, tiles, x_prompt, x_sample, c_prompt, c_sample, rel_bias, router_w, router_b, ada_w, ada_b,
             norm_mix_g, norm_ffn_g, w_in_ab, w_out_ab, q_norm_g, k_norm_g, diff_lambda, diff_norm_g,
             mlstm_conv_w, mlstm_conv_b, mlstm_gate_b, mlstm_norm_g, w_in_c, w_out_c, ret_decay_logit,
             ret_norm_g, moe_w1, moe_w3, moe_w2):
    tm, tmc, tq, tscan, tmoe = tiles["tm"], tiles["tmc"], tiles["tq"], tiles["tscan"], tiles["tmoe"]
    x = jnp.concatenate([x_prompt.reshape(geo.tp, D_MODEL), x_sample.reshape(geo.t - geo.tp, D_MODEL)], axis=0)
    c_rows = jnp.concatenate([jnp.repeat(c_prompt, geo.sp // geo.seg, axis=0),
                              jnp.repeat(c_sample, geo.ss // geo.seg, axis=0)], axis=0)
    mod = _ada_call(c_rows, ada_w, ada_b)

    bias = _bias_call(rel_bias, tq)
    cos, sin = _rope_tables(max(geo.sp, geo.ss))
    rw_t = router_w.T
    rb = router_b.reshape(N_EXPERTS, 1)
    w1 = moe_w1.astype(MXU_DT)
    w3 = moe_w3.astype(MXU_DT)
    w2 = moe_w2.astype(MXU_DT)
    xs = jnp.zeros((geo.t + N_BUCKETS * tmoe, HX_W), jnp.uint32)

    for l in range(DEPTH):
        j = l // 2
        mod_l = mod[l]
        g_mix = norm_mix_g[l].reshape(1, D_MODEL)
        g_ffn = norm_ffn_g[l].reshape(1, D_MODEL)
        if l % 2 == 0:
            lam_init = 0.8 - 0.6 * math.exp(-0.3 * l)
            w_pad = jnp.pad(w_in_ab[j], ((0, 0), (0, AB_IN_PAD - AB_IN))).astype(MXU_DT)
            qg = jnp.tile(q_norm_g[j], 2).reshape(1, LANES)
            kg = jnp.tile(k_norm_g[j], 2).reshape(1, LANES)
            gate_b = jnp.pad(mlstm_gate_b[j].reshape(1, B_HEADS * N_GATES), ((0, 0), (0, LANES - B_HEADS * N_GATES)))
            qa, ka, va, qk, vb, ob, gates = _inproj_ab_call(geo, tm, x, mod_l, g_mix, w_pad, qg, kg, gate_b)
            sc, bounded_ok = _attn_scalars(bias, tq, q_norm_g[j], k_norm_g[j])
            ya = lax.cond(bounded_ok,
                          functools.partial(_attn_call, geo, tq, lam_init, True),
                          functools.partial(_attn_call, geo, tq, lam_init, False),
                          qa, ka, va, bias, sc, diff_lambda[j], diff_norm_g[j].reshape(1, A_V_DIM))
            qb, kb = _conv_call(geo, tscan, qk, mlstm_conv_w[j], mlstm_conv_b[j].reshape(1, 2 * B_WIDTH))
            yb = _mlstm_call(geo, tscan, qb, kb, vb, gates, ob, mlstm_norm_g[j].reshape(1, B_DIM))
            w_o = w_out_ab[j].astype(MXU_DT)
            ys, ws = [ya, yb], [w_o[:A_WIDTH], w_o[A_WIDTH:]]
        else:
            q, k, v, gt = _inproj_c_call(geo, tmc, x, mod_l, g_mix, w_in_c[j].astype(MXU_DT), cos, sin)
            y = _ret_call(geo, tscan, q, k, v, ret_decay_logit[j], gt, ret_norm_g[j].reshape(1, C_V_DIM))
            ys, ws = [y], [w_out_c[j].astype(MXU_DT)]
        x, hx, bucket = _outproj_call(geo, tm, ys, ws, x, mod_l, g_ffn, rw_t, rb)
        pos, te0, te1, n_tiles = _plan_call(tmoe, bucket)
        xs = _dispatch_call(tm, pos, hx, xs)
        ysort = _moe_call(tmoe, l, te0, te1, n_tiles, xs, w1, w3, w2)
        x = _collect_call(geo, tm, pos, ysort, x, mod_l)

    y_prompt = x[:geo.tp].reshape(x_prompt.shape)
    y_sample = x[geo.tp:].reshape(x_sample.shape)
    return (y_prompt, y_sample)


def kernel(x_prompt, x_sample, c_prompt, c_sample, rel_bias, router_w, router_b, ada_w, ada_b, norm_mix_g, norm_ffn_g, w_in_ab, w_out_ab, q_norm_g, k_norm_g, diff_lambda, diff_norm_g, mlstm_conv_w, mlstm_conv_b, mlstm_gate_b, mlstm_norm_g, w_in_c, w_out_c, ret_decay_logit, ret_norm_g, moe_w1, moe_w3, moe_w2):
    geo = Geo(x_prompt.shape[0], x_prompt.shape[1], x_sample.shape[0], x_sample.shape[1])
    return _forward(geo, _tiles(geo), x_prompt, x_sample, c_prompt, c_sample, rel_bias, router_w, router_b,
                    ada_w, ada_b, norm_mix_g, norm_ffn_g, w_in_ab, w_out_ab, q_norm_g, k_norm_g, diff_lambda,
                    diff_norm_g, mlstm_conv_w, mlstm_conv_b, mlstm_gate_b, mlstm_norm_g, w_in_c, w_out_c,
                    ret_decay_logit, ret_norm_g, moe_w1, moe_w3, moe_w2)
```

```python
import functools
import math

import jax
import jax.numpy as jnp
import numpy as np
from jax import lax
from jax.experimental import pallas as pl
from jax.experimental.pallas import tpu as pltpu

F32 = jnp.float32
MXU_DT = jnp.bfloat16
ACT_DT = jnp.bfloat16
HI = lax.Precision.HIGHEST

D_MODEL = 1024
DEPTH = 4
A_HEADS = 4
A_QK_DIM = 64
A_V_DIM = 128
A_QK_WIDTH = 512
A_WIDTH = 512
B_HEADS = 4
B_DIM = 128
B_WIDTH = 512
N_GATES = 4
AB_IN = 3600
AB_IN_PAD = 3712
C_HEADS = 4
C_QK_DIM = 256
C_V_DIM = 512
C_QK_WIDTH = 1024
C_V_WIDTH = 2048
C_IN = 6144
CHUNK = 128
KB_PER_STEP = 8
RET_CHUNK = 256
REL_BUCKETS = 32
REL_MAX_DIST = 128
N_EXPERTS = 16
N_GROUPS = 4
EPG = 4
N_PAIRS = 6
N_BUCKETS = N_GROUPS * N_PAIRS
D_FF = 512
ROPE_BASE = 10000.0
EPS = 1e-6
LANES = 128
SUBLANES = 8
NEG_BIG = -1e30
LOG2E = math.log2(math.e)
HX_H = D_MODEL // 2
HX_W = HX_H + LANES
VMEM_LIMIT = 56 * 1024 * 1024


def _cparams(sem):
    return pltpu.CompilerParams(dimension_semantics=sem, vmem_limit_bytes=VMEM_LIMIT)


def _mm(a, b):
    return jnp.dot(a.astype(MXU_DT), b.astype(MXU_DT), preferred_element_type=F32)


def _mm_nt(a, b):
    return lax.dot_general(a.astype(MXU_DT), b.astype(MXU_DT), (((1,), (1,)), ((), ())),
                           preferred_element_type=F32)


def _mm_tn(a, b):
    return lax.dot_general(a.astype(MXU_DT), b.astype(MXU_DT), (((0,), (0,)), ((), ())),
                           preferred_element_type=F32)


def _mm_hi(a, b):
    return jnp.dot(a, b, precision=HI, preferred_element_type=F32)


def _silu(x):
    return x * (1.0 / (1.0 + jnp.exp(-x)))


def _sigmoid(x):
    return 1.0 / (1.0 + jnp.exp(-x))


def _log_sigmoid(x):
    return jnp.minimum(x, 0.0) - jnp.log1p(jnp.exp(-jnp.abs(x)))


def _rms(x, g):
    return x * lax.rsqrt(jnp.mean(x * x, axis=-1, keepdims=True) + EPS) * g


class Geo:
    def __init__(self, bp, sp, bs, ss):
        self.bp, self.sp, self.bs, self.ss = bp, sp, bs, ss
        self.tp = bp * sp
        self.t = bp * sp + bs * ss
        self.seg = math.gcd(sp, ss)

    def seq_start(self, row0):
        return jnp.where(row0 < self.tp, row0 % self.sp == 0, (row0 - self.tp) % self.ss == 0)

    def pos_block(self, blk, rows):
        nbp = self.tp // rows
        return jnp.where(blk < nbp, blk % (self.sp // rows), (blk - nbp) % (self.ss // rows))


def _ada_kernel(c_ref, w_ref, b_ref, o_ref):
    o_ref[...] = _mm_hi(_silu(c_ref[...]), w_ref[...]) + b_ref[...]


def _ada_call(c_rows, ada_w, ada_b):
    r = c_rows.shape[0]
    nb = 1536
    out = pl.pallas_call(
        _ada_kernel,
        grid=(DEPTH, 6 * D_MODEL // nb),
        in_specs=[pl.BlockSpec((r, D_MODEL), lambda l, n: (0, 0)),
                  pl.BlockSpec((None, D_MODEL, nb), lambda l, n: (l, 0, n)),
                  pl.BlockSpec((None, 1, nb), lambda l, n: (l, 0, n))],
        out_specs=pl.BlockSpec((None, r, nb), lambda l, n: (l, 0, n)),
        out_shape=jax.ShapeDtypeStruct((DEPTH, r, 6 * D_MODEL), F32),
        compiler_params=_cparams(("arbitrary", "arbitrary")),
        name="ada_mod",
    )(c_rows, ada_w, ada_b.reshape(DEPTH, 1, 6 * D_MODEL))
    return out.reshape(DEPTH, r, 6, D_MODEL)


def _t5_bucket(rel):
    nb = REL_BUCKETS // 2
    max_exact = nb // 2
    ret = jnp.where(rel > 0, nb, 0)
    n = jnp.abs(rel)
    nf = jnp.maximum(n, 1).astype(jnp.float32)
    large = max_exact + (jnp.log(nf / max_exact) / math.log(REL_MAX_DIST / max_exact) * (nb - max_exact)).astype(jnp.int32)
    large = jnp.minimum(large, nb - 1)
    return ret + jnp.where(n < max_exact, n, large)


def _bias_kernel(tq, rb_ref, bk_ref, o_ref):
    h = pl.program_id(0)
    bk = bk_ref[...]
    row = jnp.zeros(bk.shape, F32)
    bmax = rb_ref[0, h]
    for b in range(REL_BUCKETS):
        row = row + jnp.where(bk == b, rb_ref[b, h], 0.0)
        bmax = jnp.maximum(bmax, rb_ref[b, h])
    row = (row - bmax) * LOG2E
    table = jnp.broadcast_to(row, (tq, 2 * tq))
    o_ref[...] = pltpu.roll(table, tq + 1, axis=1, stride=1, stride_axis=0)[:, :tq]


def _bias_call(rel_bias, tq):
    rel = (jnp.arange(-1, 2, dtype=jnp.int32) * tq)[:, None, None] + (jnp.arange(2 * tq, dtype=jnp.int32) - (tq - 1))
    buckets = _t5_bucket(rel)
    return pl.pallas_call(
        functools.partial(_bias_kernel, tq),
        grid=(A_HEADS, 3),
        in_specs=[pl.BlockSpec(memory_space=pltpu.SMEM),
                  pl.BlockSpec((None, 1, 2 * tq), lambda h, o: (o, 0, 0))],
        out_specs=pl.BlockSpec((None, None, tq, tq), lambda h, o: (h, o, 0, 0)),
        out_shape=jax.ShapeDtypeStruct((A_HEADS, 3, tq, tq), F32),
        compiler_params=_cparams(("arbitrary", "arbitrary")),
        name="rel_bias_tiles",
    )(rel_bias, buckets)


def _rope_tables(s):
    d = C_QK_DIM
    inv = ROPE_BASE ** (-jnp.arange(0, d, 2, dtype=jnp.float32) / d)
    ang = jnp.arange(s, dtype=jnp.float32)[:, None] * inv[None, :]
    return jnp.cos(ang), jnp.sin(ang)


def _half_rms(z, g):
    lo_lane = lax.broadcasted_iota(jnp.int32, (1, LANES), 1) < A_QK_DIM
    z2 = z * z
    tot = jnp.sum(z2, axis=-1, keepdims=True)
    lo = jnp.sum(jnp.where(lo_lane, z2, 0.0), axis=-1, keepdims=True)
    ms = jnp.where(lo_lane, lo, tot - lo) * (1.0 / A_QK_DIM)
    return z * lax.rsqrt(ms + EPS) * g


def _inproj_ab_kernel(x_ref, mod_ref, g_ref, w_ref, qg_ref, kg_ref, gb_ref,
                      qa_ref, ka_ref, va_ref, qk_ref, vb_ref, ob_ref, gt_ref):
    x = x_ref[...]
    h = _rms(x, g_ref[...]) * (1.0 + mod_ref[1:2, :]) + mod_ref[0:1, :]
    hb = h.astype(MXU_DT)
    qscale = (A_QK_DIM ** -0.5) * LOG2E
    for hd in range(A_HEADS):
        c0 = hd * LANES
        q = jnp.dot(hb, w_ref[:, c0:c0 + LANES], preferred_element_type=F32)
        qa_ref[:, c0:c0 + LANES] = (_half_rms(q, qg_ref[...]) * qscale).astype(qa_ref.dtype)
        k = jnp.dot(hb, w_ref[:, A_QK_WIDTH + c0:A_QK_WIDTH + c0 + LANES], preferred_element_type=F32)
        ka_ref[:, c0:c0 + LANES] = _half_rms(k, kg_ref[...]).astype(ka_ref.dtype)
    o = 2 * A_QK_WIDTH
    va_ref[...] = jnp.dot(hb, w_ref[:, o:o + A_WIDTH], preferred_element_type=F32).astype(va_ref.dtype)
    o += A_WIDTH
    qk_ref[...] = jnp.dot(hb, w_ref[:, o:o + 2 * B_WIDTH], preferred_element_type=F32)
    o += 2 * B_WIDTH
    vb_ref[...] = jnp.dot(hb, w_ref[:, o:o + B_WIDTH], preferred_element_type=F32).astype(vb_ref.dtype)
    o += B_WIDTH
    ob_ref[...] = jnp.dot(hb, w_ref[:, o:o + B_WIDTH], preferred_element_type=F32)
    o += B_WIDTH
    gt_ref[...] = jnp.dot(hb, w_ref[:, o:o + LANES], preferred_element_type=F32) + gb_ref[...]


def _inproj_ab_call(geo, tm, x, mod_l, g, w_pad, qg, kg, gate_b):
    t = geo.t
    per_seg = geo.seg // tm
    row = lambda i: (i, 0)
    const = lambda i: (0, 0)
    widths = (A_QK_WIDTH, A_QK_WIDTH, A_WIDTH, 2 * B_WIDTH, B_WIDTH, B_WIDTH, LANES)
    dtypes = (ACT_DT, ACT_DT, ACT_DT, F32, ACT_DT, F32, F32)
    return pl.pallas_call(
        _inproj_ab_kernel,
        grid=(t // tm,),
        in_specs=[pl.BlockSpec((tm, D_MODEL), row),
                  pl.BlockSpec((None, 6, D_MODEL), lambda i: (i // per_seg, 0, 0)),
                  pl.BlockSpec((1, D_MODEL), const),
                  pl.BlockSpec((D_MODEL, AB_IN_PAD), const),
                  pl.BlockSpec((1, LANES), const),
                  pl.BlockSpec((1, LANES), const),
                  pl.BlockSpec((1, LANES), const)],
        out_specs=[pl.BlockSpec((tm, w), row) for w in widths],
        out_shape=[jax.ShapeDtypeStruct((t, w), d) for w, d in zip(widths, dtypes)],
        compiler_params=_cparams(("parallel",)),
        name="inproj_ab",
    )(x, mod_l, g, w_pad, qg, kg, gate_b)


def _conv_kernel(geo, tc, x_ref, prev_ref, next_ref, w_ref, b_ref, q_ref, k_ref):
    i = pl.program_id(0)
    row0 = i * tc
    x = x_ref[...]
    first = geo.seq_start(row0)
    last = geo.seq_start(row0 + tc) | (row0 + tc == geo.t)
    prev_row = jnp.where(first, 0.0, prev_ref[7:8, :])
    next_row = jnp.where(last, 0.0, next_ref[0:1, :])
    ridx = lax.broadcasted_iota(jnp.int32, (tc, 1), 0)
    x_prev = jnp.where(ridx == 0, prev_row, pltpu.roll(x, 1, axis=0))
    x_next = jnp.where(ridx == tc - 1, next_row, pltpu.roll(x, tc - 1, axis=0))
    y = x_prev * w_ref[0:1, :] + x * w_ref[1:2, :] + x_next * w_ref[2:3, :] + b_ref[...]
    y = _silu(y)
    q_ref[...] = y[:, :B_WIDTH].astype(q_ref.dtype)
    k_ref[...] = (y[:, B_WIDTH:] * (B_DIM ** -0.5)).astype(k_ref.dtype)


def _conv_call(geo, tc, qk, w, b):
    t = geo.t
    r8 = tc // 8
    nb8 = t // 8
    return pl.pallas_call(
        functools.partial(_conv_kernel, geo, tc),
        grid=(t // tc,),
        in_specs=[pl.BlockSpec((tc, 2 * B_WIDTH), lambda i: (i, 0)),
                  pl.BlockSpec((8, 2 * B_WIDTH), lambda i: (jnp.maximum(i * r8 - 1, 0), 0)),
                  pl.BlockSpec((8, 2 * B_WIDTH), lambda i: (jnp.minimum((i + 1) * r8, nb8 - 1), 0)),
                  pl.BlockSpec((3, 2 * B_WIDTH), lambda i: (0, 0)),
                  pl.BlockSpec((1, 2 * B_WIDTH), lambda i: (0, 0))],
        out_specs=[pl.BlockSpec((tc, B_WIDTH), lambda i: (i, 0))] * 2,
        out_shape=[jax.ShapeDtypeStruct((t, B_WIDTH), ACT_DT)] * 2,
        compiler_params=_cparams(("parallel",)),
        name="mlstm_conv",
    )(qk, qk, qk, w, b)


def _attn_finish(lam_init, acc0, l0, acc1, l1, dl_ref, ng_ref, o_ref):
    dl = dl_ref[...]
    lam = (jnp.exp(jnp.sum(dl[0:1] * dl[1:2], axis=-1, keepdims=True))
           - jnp.exp(jnp.sum(dl[2:3] * dl[3:4], axis=-1, keepdims=True)) + lam_init)
    out = acc0 / l0 - lam * (acc1 / l1)
    o_ref[...] = (_rms(out, ng_ref[...]) * (1.0 - lam_init)).astype(o_ref.dtype)


def _attn_kernel(lam_init, nk, q_ref, k_ref, v_ref, bias_ref, sc_ref, dl_ref, ng_ref, o_ref,
                 m_sc, l_sc, acc_sc):
    h = pl.program_id(1)
    i = pl.program_id(2)
    j = pl.program_id(3)

    @pl.when(j == 0)
    def _():
        m_sc[...] = jnp.full(m_sc.shape, NEG_BIG, F32)
        l_sc[...] = jnp.zeros(l_sc.shape, F32)
        acc_sc[...] = jnp.zeros(acc_sc.shape, F32)

    q = q_ref[...]
    lo_lane = lax.broadcasted_iota(jnp.int32, (1, LANES), 1) < A_QK_DIM
    qsub = (jnp.where(lo_lane, q, jnp.zeros_like(q)), jnp.where(lo_lane, jnp.zeros_like(q), q))
    k = k_ref[...]
    v = v_ref[...]

    def step(near):
        if near:
            shift = 0.0
        else:
            shift = jnp.where(j < i, sc_ref[h, 0], sc_ref[h, 1])
        for sub in range(2):
            s = _mm_nt(qsub[sub], k)
            if near:
                s = s + bias_ref[...]
            m_old = m_sc[sub]
            m_new = jnp.maximum(m_old, jnp.max(s, axis=-1, keepdims=True) + shift)
            p = jnp.exp2(s - (m_new - shift))
            alpha = jnp.exp2(m_old - m_new)
            l_sc[sub] = alpha * l_sc[sub] + jnp.sum(p, axis=-1, keepdims=True)
            acc_sc[sub] = alpha * acc_sc[sub] + _mm(p, v)
            m_sc[sub] = m_new

    near = jnp.abs(j - i) <= 1
    pl.when(near)(lambda: step(True))
    pl.when(jnp.logical_not(near))(lambda: step(False))

    @pl.when(j == nk - 1)
    def _():
        _attn_finish(lam_init, acc_sc[0], l_sc[0], acc_sc[1], l_sc[1], dl_ref, ng_ref, o_ref)


def _attn_bounded_kernel(lam_init, nk, q_ref, k_ref, v_ref, bias_ref, sc_ref, dl_ref, ng_ref, o_ref, acc_sc):
    h = pl.program_id(1)
    i = pl.program_id(2)
    j = pl.program_id(3)
    tq = q_ref.shape[0]

    @pl.when(j == 0)
    def _():
        acc_sc[...] = jnp.zeros(acc_sc.shape, F32)

    q = q_ref[...]
    lo = (lax.broadcasted_iota(jnp.int32, q.shape, 1) < A_QK_DIM).astype(F32).astype(q.dtype)
    qsub = (q * lo, q * (1 - lo))
    ones_col = (lax.broadcasted_iota(jnp.int32, (tq, LANES), 1) == 0).astype(v_ref.dtype)

    nkb = k_ref.shape[0] // tq
    for kb in range(nkb):
        jb = j * nkb + kb

        @pl.when(jb == jnp.maximum(i - 1, 0))
        def _():
            acc_sc[...] = acc_sc[...] * sc_ref[h, 2]

        @pl.when(jb == i + 2)
        def _():
            acc_sc[...] = acc_sc[...] * sc_ref[h, 3]

        def step(near, kb=kb, jb=jb):
            k = k_ref[kb * tq:(kb + 1) * tq, :]
            v_aug = jnp.concatenate([v_ref[kb * tq:(kb + 1) * tq, :], ones_col], axis=1)
            for sub in range(2):
                s = _mm_nt(qsub[sub], k)
                if near:
                    s = s + bias_ref[jb - i + 1]
                acc_sc[sub] += _mm(jnp.exp2(s), v_aug)

        near = jnp.abs(jb - i) <= 1
        pl.when(near)(functools.partial(step, True))
        pl.when(jnp.logical_not(near))(functools.partial(step, False))

    @pl.when(j == nk - 1)
    def _():
        a0 = acc_sc[0]
        a1 = acc_sc[1]
        _attn_finish(lam_init, a0[:, :A_V_DIM], a0[:, A_V_DIM:A_V_DIM + 1], a1[:, :A_V_DIM],
                     a1[:, A_V_DIM:A_V_DIM + 1], dl_ref, ng_ref, o_ref)


def _attn_scalars(bias, tq, q_gain, k_gain):
    far_l = bias[:, 0, tq - 1, 0]
    far_r = bias[:, 2, 0, tq - 1]
    sc = jnp.stack([far_l, far_r, jnp.exp2(far_l), jnp.exp2(-far_r)], axis=-1)
    bound = A_QK_DIM * jnp.max(jnp.abs(q_gain)) * jnp.max(jnp.abs(k_gain)) * (A_QK_DIM ** -0.5) * LOG2E * 1.02
    spread = -jnp.min(bias)
    ok = bound + 2.0 * spread <= 80.0
    return sc, ok


def _attn_call(geo, tq, lam_init, bounded, qa, ka, va, bias, sc, dl, ng):
    outs = []
    for (nb, s, row_off) in ((geo.bp, geo.sp, 0), (geo.bs, geo.ss, geo.tp)):
        nq = s // tq
        kb = math.gcd(KB_PER_STEP, nq) if bounded else 1
        nk = nq // kb
        off = row_off // tq
        qmap = lambda b, h, i, j, off=off, nq=nq: (off + b * nq + i, h)
        assert off % kb == 0, "the group's first key block must be aligned to the key blocks of one grid step"
        kmap = lambda b, h, i, j, koff=off // kb, nk=nk: (koff + b * nk + j, h)
        omap = lambda b, h, i, j, nq=nq: (b * nq + i, h)
        if bounded:
            body = functools.partial(_attn_bounded_kernel, lam_init, nk)
            scratch = [pltpu.VMEM((2, tq, 2 * LANES), F32)]
            bias_spec = pl.BlockSpec((None, 3, tq, tq), lambda b, h, i, j: (h, 0, 0, 0))
        else:
            body = functools.partial(_attn_kernel, lam_init, nk)
            scratch = [pltpu.VMEM((2, tq, 1), F32), pltpu.VMEM((2, tq, 1), F32), pltpu.VMEM((2, tq, LANES), F32)]
            bias_spec = pl.BlockSpec((None, None, tq, tq), lambda b, h, i, j: (h, jnp.clip(j - i + 1, 0, 2), 0, 0))
        outs.append(pl.pallas_call(
            body,
            grid=(nb, A_HEADS, nq, nk),
            in_specs=[pl.BlockSpec((tq, LANES), qmap),
                      pl.BlockSpec((kb * tq, LANES), kmap),
                      pl.BlockSpec((kb * tq, LANES), kmap),
                      bias_spec,
                      pl.BlockSpec(memory_space=pltpu.SMEM),
                      pl.BlockSpec((4, A_QK_DIM), lambda b, h, i, j: (0, 0)),
                      pl.BlockSpec((1, LANES), lambda b, h, i, j: (0, 0))],
            out_specs=pl.BlockSpec((tq, LANES), omap),
            out_shape=jax.ShapeDtypeStruct((nb * s, A_WIDTH), ACT_DT),
            scratch_shapes=scratch,
            compiler_params=_cparams(("parallel", "parallel", "parallel", "arbitrary")),
            name="diff_attn_bounded" if bounded else "diff_attn",
        )(qa, ka, va, bias, sc, dl, ng))
    return jnp.concatenate(outs, axis=0)


def _tri(lower):
    r = lax.broadcasted_iota(jnp.int32, (CHUNK, CHUNK), 0)
    c = lax.broadcasted_iota(jnp.int32, (CHUNK, CHUNK), 1)
    return (c <= r) if lower else (c >= r)


def _mlstm_kernel(geo, tb, reverse, *refs):
    if reverse:
        q_ref, k_ref, v_ref, g_ref, hf_ref, ob_ref, ng_ref, o_ref, c_sc, m_sc = refs
    else:
        q_ref, k_ref, v_ref, g_ref, o_ref, c_sc, m_sc = refs
    step = pl.program_id(0)
    nblk = geo.t // tb
    blk = (nblk - 1 - step) if reverse else step
    row0 = blk * tb
    if reverse:
        fresh = geo.seq_start(row0 + tb) | (row0 + tb == geo.t)
    else:
        fresh = geo.seq_start(row0)

    @pl.when(fresh)
    def _():
        c_sc[...] = jnp.zeros(c_sc.shape, F32)
        m_sc[...] = jnp.zeros(m_sc.shape, F32)

    mask = _tri(not reverse)
    cum_l = mask.astype(F32)
    cum_r = _tri(reverse).astype(F32)
    ones_col = (lax.broadcasted_iota(jnp.int32, (CHUNK, LANES), 1) == 0).astype(MXU_DT)
    nch = tb // CHUNK

    def chunk(ci, carry):
        c_idx = (nch - 1 - ci) if reverse else ci
        r0 = pl.multiple_of(c_idx * CHUNK, CHUNK)
        g = g_ref[pl.ds(r0, CHUNK), :]
        g_t = g.T
        b_col = _mm_hi(cum_l, _log_sigmoid(g))
        b_row = _mm_hi(_log_sigmoid(g_t), cum_r)
        for hd in range(B_HEADS):
            ci_col = hd * N_GATES + (2 if reverse else 0)
            cf_col = ci_col + 1
            lanes = slice(hd * B_DIM, (hd + 1) * B_DIM)
            q = q_ref[pl.ds(r0, CHUNK), lanes]
            k = k_ref[pl.ds(r0, CHUNK), lanes]
            v = v_ref[pl.ds(r0, CHUNK), lanes]
            bc = b_col[:, cf_col:cf_col + 1]
            br = b_row[cf_col:cf_col + 1, :]
            ic = g[:, ci_col:ci_col + 1]
            ir = g_t[ci_col:ci_col + 1, :]
            m_prev = m_sc[hd]
            log_d = jnp.where(mask, bc - br + ir, NEG_BIG)
            m_inter = bc + m_prev
            m_t = jnp.maximum(jnp.max(log_d, axis=-1, keepdims=True), m_inter)
            s = _mm_nt(q, k) * jnp.exp(log_d - m_t)
            inter = jnp.exp(m_inter - m_t)
            c_aug = c_sc[hd]
            qc = _mm(q, c_aug)
            num = _mm(s, v) + inter * qc[:, :B_DIM]
            den = jnp.sum(s, axis=-1, keepdims=True) + inter * qc[:, B_DIM:B_DIM + 1]
            hout = num / jnp.maximum(jnp.abs(den), jnp.exp(-m_t))
            b_last = bc[0:1, :] if reverse else bc[CHUNK - 1:CHUNK, :]
            log_w = b_last - bc + ic
            m_new = jnp.maximum(b_last + m_prev, jnp.max(log_w, axis=0, keepdims=True))
            w = jnp.exp(log_w - m_new)
            decay = jnp.exp(b_last + m_prev - m_new)
            v_aug = jnp.concatenate([v, ones_col], axis=1)
            c_sc[hd] = decay * c_aug + _mm_tn(k.astype(F32) * w, v_aug)
            m_sc[hd] = m_new
            if reverse:
                hsum = hf_ref[pl.ds(r0, CHUNK), lanes].astype(F32) + hout
                y = _rms(hsum, ng_ref[...]) * _sigmoid(ob_ref[pl.ds(r0, CHUNK), lanes])
                o_ref[pl.ds(r0, CHUNK), lanes] = y.astype(o_ref.dtype)
            else:
                o_ref[pl.ds(r0, CHUNK), lanes] = hout.astype(o_ref.dtype)
        return carry

    lax.fori_loop(0, nch, chunk, 0)


def _mlstm_call(geo, tb, qb, kb, vb, gates, ob, ng):
    t = geo.t
    nblk = t // tb
    fmap = lambda s: (s, 0)
    rmap = lambda s: (nblk - 1 - s, 0)
    scratch = [pltpu.VMEM((B_HEADS, B_DIM, 2 * B_DIM), F32), pltpu.VMEM((B_HEADS, 1, 1), F32)]
    wide = lambda m: pl.BlockSpec((tb, B_WIDTH), m)
    hf = pl.pallas_call(
        functools.partial(_mlstm_kernel, geo, tb, False),
        grid=(nblk,),
        in_specs=[wide(fmap), wide(fmap), wide(fmap), pl.BlockSpec((tb, LANES), fmap)],
        out_specs=wide(fmap),
        out_shape=jax.ShapeDtypeStruct((t, B_WIDTH), ACT_DT),
        scratch_shapes=scratch,
        compiler_params=_cparams(("arbitrary",)),
        name="mlstm_fwd",
    )(qb, kb, vb, gates)
    return pl.pallas_call(
        functools.partial(_mlstm_kernel, geo, tb, True),
        grid=(nblk,),
        in_specs=[wide(rmap), wide(rmap), wide(rmap), pl.BlockSpec((tb, LANES), rmap), wide(rmap), wide(rmap),
                  pl.BlockSpec((1, B_DIM), lambda s: (0, 0))],
        out_specs=wide(rmap),
        out_shape=jax.ShapeDtypeStruct((t, B_WIDTH), ACT_DT),
        scratch_shapes=scratch,
        compiler_params=_cparams(("arbitrary",)),
        name="mlstm_bwd",
    )(qb, kb, vb, gates, hf, ob, ng)


def _inproj_c_kernel(x_ref, mod_ref, g_ref, w_ref, cos_ref, sin_ref, q_ref, k_ref, v_ref, gt_ref, h_sc):
    j = pl.program_id(1)
    nsub = C_HEADS
    sub = w_ref.shape[1] // nsub

    def proj(c):
        return jnp.dot(h_sc[...], w_ref[:, c * sub:(c + 1) * sub], preferred_element_type=F32)

    def rope(o_ref, scale):
        cos = cos_ref[...] * scale
        sin = sin_ref[...] * scale
        half = C_QK_DIM // 2
        for hd in range(C_HEADS):
            y = proj(hd)
            x1, x2 = y[:, :half], y[:, half:]
            o_ref[:, hd * C_QK_DIM:hd * C_QK_DIM + half] = (x1 * cos - x2 * sin).astype(o_ref.dtype)
            o_ref[:, hd * C_QK_DIM + half:(hd + 1) * C_QK_DIM] = (x1 * sin + x2 * cos).astype(o_ref.dtype)

    @pl.when(j == 0)
    def _():
        x = x_ref[...]
        h = _rms(x, g_ref[...]) * (1.0 + mod_ref[1:2, :]) + mod_ref[0:1, :]
        h_sc[...] = h.astype(h_sc.dtype)
        rope(q_ref, 1.0)

    @pl.when(j == 1)
    def _():
        rope(k_ref, C_QK_DIM ** -0.5)

    @pl.when((j == 2) | (j == 3))
    def _():
        for c in range(nsub):
            v_ref[:, c * sub:(c + 1) * sub] = proj(c).astype(v_ref.dtype)

    @pl.when(j >= 4)
    def _():
        for c in range(nsub):
            gt_ref[:, c * sub:(c + 1) * sub] = _silu(proj(c)).astype(gt_ref.dtype)


def _inproj_c_call(geo, tm, x, mod_l, g, w, cos, sin):
    t = geo.t
    per_seg = geo.seg // tm
    nw = D_MODEL
    return pl.pallas_call(
        _inproj_c_kernel,
        grid=(t // tm, C_IN // nw),
        in_specs=[pl.BlockSpec((tm, D_MODEL), lambda i, j: (i, 0)),
                  pl.BlockSpec((None, 6, D_MODEL), lambda i, j: (i // per_seg, 0, 0)),
                  pl.BlockSpec((1, D_MODEL), lambda i, j: (0, 0)),
                  pl.BlockSpec((D_MODEL, nw), lambda i, j: (0, j)),
                  pl.BlockSpec((tm, C_QK_DIM // 2), lambda i, j: (geo.pos_block(i, tm), 0)),
                  pl.BlockSpec((tm, C_QK_DIM // 2), lambda i, j: (geo.pos_block(i, tm), 0))],
        out_specs=[pl.BlockSpec((tm, nw), lambda i, j: (i, 0)),
                   pl.BlockSpec((tm, nw), lambda i, j: (i, 0)),
                   pl.BlockSpec((tm, nw), lambda i, j: (i, jnp.clip(j - 2, 0, 1))),
                   pl.BlockSpec((tm, nw), lambda i, j: (i, jnp.clip(j - 4, 0, 1)))],
        out_shape=[jax.ShapeDtypeStruct((t, C_QK_WIDTH), ACT_DT),
                   jax.ShapeDtypeStruct((t, C_QK_WIDTH), ACT_DT),
                   jax.ShapeDtypeStruct((t, C_V_WIDTH), ACT_DT),
                   jax.ShapeDtypeStruct((t, C_V_WIDTH), ACT_DT)],
        scratch_shapes=[pltpu.VMEM((tm, D_MODEL), MXU_DT)],
        compiler_params=_cparams(("parallel", "arbitrary")),
        name="inproj_c",
    )(x, mod_l, g, w, cos, sin)


def _ret_kernel(geo, tb, reverse, *refs):
    if reverse:
        q_ref, k_ref, v_ref, dlg_ref, yf_ref, gt_ref, ng_ref, o_ref, r_sc, intra_sc, vec_sc = refs
    else:
        q_ref, k_ref, v_ref, dlg_ref, o_ref, r_sc, intra_sc, vec_sc = refs
    step = pl.program_id(0)
    nblk = geo.t // tb
    blk = (nblk - 1 - step) if reverse else step
    row0 = blk * tb
    if reverse:
        fresh = geo.seq_start(row0 + tb) | (row0 + tb == geo.t)
    else:
        fresh = geo.seq_start(row0)

    @pl.when(fresh)
    def _():
        r_sc[...] = jnp.zeros(r_sc.shape, F32)

    @pl.when(step == 0)
    def _():
        lg_all = _log_sigmoid(dlg_ref[...])
        ti = lax.broadcasted_iota(jnp.int32, (RET_CHUNK, RET_CHUNK), 0)
        si = lax.broadcasted_iota(jnp.int32, (RET_CHUNK, RET_CHUNK), 1)
        dist = ((si - ti) if reverse else (ti - si)).astype(F32)
        pos = lax.broadcasted_iota(jnp.int32, (RET_CHUNK, LANES), 0).astype(F32)
        upos = (RET_CHUNK - 1.0 - pos) if reverse else pos
        lane = lax.broadcasted_iota(jnp.int32, (RET_CHUNK, LANES), 1)
        d = 1 if reverse else 0
        for hd in range(C_HEADS):
            lg = lg_all[d:d + 1, hd:hd + 1]
            intra_sc[hd] = jnp.where(dist >= 0, jnp.exp(jnp.maximum(dist, 0.0) * lg), 0.0)
            vec_sc[hd] = jnp.where(lane == 0, jnp.exp((upos + 1.0) * lg),
                                   jnp.where(lane == 1, jnp.exp((RET_CHUNK - 1.0 - upos) * lg),
                                             jnp.exp(RET_CHUNK * lg)))

    nch = tb // RET_CHUNK

    def chunk(ci, carry):
        c_idx = (nch - 1 - ci) if reverse else ci
        r0 = pl.multiple_of(c_idx * RET_CHUNK, RET_CHUNK)
        for hd in range(C_HEADS):
            vec = vec_sc[hd]
            q_scale, k_scale, c_decay = vec[:, 0:1], vec[:, 1:2], vec[0:1, 2:3]
            ql = slice(hd * C_QK_DIM, (hd + 1) * C_QK_DIM)
            vl = slice(hd * C_V_DIM, (hd + 1) * C_V_DIM)
            q = q_ref[pl.ds(r0, RET_CHUNK), ql]
            k = k_ref[pl.ds(r0, RET_CHUNK), ql]
            v = v_ref[pl.ds(r0, RET_CHUNK), vl]
            r_old = r_sc[hd]
            s = _mm_nt(q, k) * intra_sc[hd]
            y = _mm(s, v) + q_scale * _mm(q, r_old)
            r_sc[hd] = c_decay * r_old + _mm_tn(k.astype(F32) * k_scale, v)
            if reverse:
                ysum = yf_ref[pl.ds(r0, RET_CHUNK), vl].astype(F32) + y
                out = _rms(ysum, ng_ref[...]) * gt_ref[pl.ds(r0, RET_CHUNK), vl].astype(F32)
                o_ref[pl.ds(r0, RET_CHUNK), vl] = out.astype(o_ref.dtype)
            else:
                o_ref[pl.ds(r0, RET_CHUNK), vl] = y.astype(o_ref.dtype)
        return carry

    lax.fori_loop(0, nch, chunk, 0)


def _ret_call(geo, tb, q, k, v, decay_logit, gt, ng):
    t = geo.t
    nblk = t // tb
    fmap = lambda s: (s, 0)
    rmap = lambda s: (nblk - 1 - s, 0)
    scratch = [pltpu.VMEM((C_HEADS, C_QK_DIM, C_V_DIM), F32), pltpu.VMEM((C_HEADS, RET_CHUNK, RET_CHUNK), F32),
               pltpu.VMEM((C_HEADS, RET_CHUNK, LANES), F32)]
    qk = lambda m: pl.BlockSpec((tb, C_QK_WIDTH), m)
    vv = lambda m: pl.BlockSpec((tb, C_V_WIDTH), m)
    dspec = pl.BlockSpec((2, C_HEADS), lambda s: (0, 0))
    yf = pl.pallas_call(
        functools.partial(_ret_kernel, geo, tb, False),
        grid=(nblk,),
        in_specs=[qk(fmap), qk(fmap), vv(fmap), dspec],
        out_specs=vv(fmap),
        out_shape=jax.ShapeDtypeStruct((t, C_V_WIDTH), ACT_DT),
        scratch_shapes=scratch,
        compiler_params=_cparams(("arbitrary",)),
        name="ret_fwd",
    )(q, k, v, decay_logit)
    return pl.pallas_call(
        functools.partial(_ret_kernel, geo, tb, True),
        grid=(nblk,),
        in_specs=[qk(rmap), qk(rmap), vv(rmap), dspec, vv(rmap), vv(rmap),
                  pl.BlockSpec((1, C_V_DIM), lambda s: (0, 0))],
        out_specs=vv(rmap),
        out_shape=jax.ShapeDtypeStruct((t, C_V_WIDTH), ACT_DT),
        scratch_shapes=scratch,
        compiler_params=_cparams(("arbitrary",)),
        name="ret_bwd",
    )(q, k, v, decay_logit, yf, gt, ng)


def _route(probs):
    p = [probs[e:e + 1, :] for e in range(N_EXPERTS)]
    scores = []
    for g in range(N_GROUPS):
        a, b, c, d = p[EPG * g:EPG * g + EPG]
        hi1, lo1 = jnp.maximum(a, b), jnp.minimum(a, b)
        hi2, lo2 = jnp.maximum(c, d), jnp.minimum(c, d)
        scores.append(jnp.maximum(hi1, hi2) + jnp.maximum(jnp.minimum(hi1, hi2), jnp.maximum(lo1, lo2)))
    g_sel = jnp.zeros(scores[0].shape, jnp.int32)
    best = scores[0]
    for g in range(1, N_GROUPS):
        better = scores[g] > best
        g_sel = jnp.where(better, g, g_sel)
        best = jnp.where(better, scores[g], best)
    vals = []
    for kk in range(EPG):
        v = p[kk]
        for g in range(1, N_GROUPS):
            v = jnp.where(g_sel == g, p[EPG * g + kk], v)
        vals.append(v)

    def argmax4(xs):
        idx = jnp.zeros(xs[0].shape, jnp.int32)
        top = xs[0]
        for kk in range(1, EPG):
            better = xs[kk] > top
            idx = jnp.where(better, kk, idx)
            top = jnp.where(better, xs[kk], top)
        return idx, top

    i1, v1 = argmax4(vals)
    i2, v2 = argmax4([jnp.where(i1 == kk, -1.0, vals[kk]) for kk in range(EPG)])
    tot = v1 + v2
    w1, w2 = v1 / tot, v2 / tot
    e1 = g_sel * EPG + i1
    e2 = g_sel * EPG + i2
    eidx = lax.broadcasted_iota(jnp.int32, probs.shape, 0)
    lo, hi = jnp.minimum(i1, i2), jnp.maximum(i1, i2)
    pair = jnp.where(lo == 0, hi - 1, jnp.where(lo == 1, hi + 1, N_PAIRS - 1))
    return jnp.where(eidx == e1, w1, 0.0) + jnp.where(eidx == e2, w2, 0.0), g_sel * N_PAIRS + pair


def _pack_pairs(h):
    n = h.shape[1] // 2
    bits = lax.bitcast_convert_type(h.astype(MXU_DT).astype(F32), jnp.uint32)
    return (bits[:, :n] & jnp.uint32(0xFFFF0000)) | (bits[:, n:] >> 16)


def _unpack_pairs(u):
    hi = lax.bitcast_convert_type(u & jnp.uint32(0xFFFF0000), F32)
    lo = lax.bitcast_convert_type(u << 16, F32)
    return jnp.concatenate([hi, lo], axis=1).astype(MXU_DT)


def _outproj_kernel(nparts, *refs):
    y_refs = refs[:nparts]
    w_refs = refs[nparts:2 * nparts]
    x_ref, mod_ref, g_ref, rw_ref, rb_ref, xo_ref, hx_ref, bkt_ref = refs[2 * nparts:]
    m = jnp.dot(y_refs[0][...], w_refs[0][...], preferred_element_type=F32)
    for p in range(1, nparts):
        m = m + jnp.dot(y_refs[p][...], w_refs[p][...], preferred_element_type=F32)
    x = x_ref[...] + mod_ref[2:3, :] * m
    xo_ref[...] = x
    h = _rms(x, g_ref[...]) * (1.0 + mod_ref[4:5, :]) + mod_ref[3:4, :]
    hx_ref[:, :HX_H] = _pack_pairs(h)
    logits = lax.dot_general(rw_ref[...], h, (((1,), (1,)), ((), ())), precision=HI,
                             preferred_element_type=F32) + rb_ref[...]
    z = jnp.exp(logits - jnp.max(logits, axis=0, keepdims=True))
    probs = z / jnp.sum(z, axis=0, keepdims=True)
    cmb, bucket = _route(probs)
    pad = jnp.zeros((LANES - N_EXPERTS, cmb.shape[1]), F32)
    hx_ref[:, HX_H:] = lax.bitcast_convert_type(jnp.concatenate([cmb, pad], axis=0).T, jnp.uint32)
    bkt_ref[...] = bucket


def _outproj_call(geo, tm, ys, ws, x, mod_l, g, rw_t, rb):
    t = geo.t
    per_seg = geo.seg // tm
    row = lambda i: (i, 0)
    const = lambda i: (0, 0)
    n = len(ys)
    return pl.pallas_call(
        functools.partial(_outproj_kernel, n),
        grid=(t // tm,),
        in_specs=([pl.BlockSpec((tm, y.shape[1]), row) for y in ys]
                  + [pl.BlockSpec(w.shape, const) for w in ws]
                  + [pl.BlockSpec((tm, D_MODEL), row),
                     pl.BlockSpec((None, 6, D_MODEL), lambda i: (i // per_seg, 0, 0)),
                     pl.BlockSpec((1, D_MODEL), const),
                     pl.BlockSpec((N_EXPERTS, D_MODEL), const),
                     pl.BlockSpec((N_EXPERTS, 1), const)]),
        out_specs=[pl.BlockSpec((tm, D_MODEL), row), pl.BlockSpec((tm, HX_W), row),
                   pl.BlockSpec((None, 1, tm), lambda i: (i, 0, 0))],
        out_shape=[jax.ShapeDtypeStruct((t, D_MODEL), F32), jax.ShapeDtypeStruct((t, HX_W), jnp.uint32),
                   jax.ShapeDtypeStruct((t // tm, 1, tm), jnp.int32)],
        compiler_params=_cparams(("parallel",)),
        name="outproj_router",
    )(*ys, *ws, x, mod_l, g, rw_t, rb)


def _plan_kernel(tm, bkt_ref, pos_ref, te_ref, nv_ref):
    nblk, _, blk = bkt_ref.shape
    nrow = 32
    bid = lax.broadcasted_iota(jnp.int32, (nrow, blk), 0)
    r = lax.broadcasted_iota(jnp.int32, (blk, blk), 0)
    c = lax.broadcasted_iota(jnp.int32, (blk, blk), 1)
    prefix = (r <= c).astype(MXU_DT)

    def count(b, acc):
        return acc + jnp.sum((bkt_ref[b] == bid).astype(F32), axis=1, keepdims=True)

    counts = lax.fori_loop(0, nblk, count, jnp.zeros((nrow, 1), F32))
    padded = jnp.floor((counts + (tm - 1.0)) / tm) * tm
    rows = lax.broadcasted_iota(jnp.int32, (nrow, 1), 0)
    offs = jnp.zeros((nrow, 1), F32)
    ends = []
    run = jnp.zeros((1, 1), F32)
    for b in range(N_BUCKETS):
        offs = jnp.where(rows == b, run, offs)
        run = run + padded[b:b + 1, :]
        ends.append(run)

    def place(b, carry):
        oh = (bkt_ref[b] == bid).astype(F32)
        pre = jnp.dot(oh.astype(MXU_DT), prefix, preferred_element_type=F32)
        pos = jnp.sum(oh * (offs + carry + pre - 1.0), axis=0, keepdims=True)
        pos_ref[b] = pos.astype(jnp.int32)
        return carry + pre[:, blk - 1:blk]

    lax.fori_loop(0, nblk, place, jnp.zeros((nrow, 1), F32))
    start = lax.broadcasted_iota(jnp.int32, (1, te_ref.shape[1]), 1).astype(F32) * tm
    tb = jnp.zeros(start.shape, F32)
    for b in range(N_BUCKETS - 1):
        tb = tb + (ends[b] <= start).astype(F32)
    grp = jnp.floor((tb + 0.5) / N_PAIRS)
    pair = tb - grp * N_PAIRS
    lo = (pair >= 3).astype(F32) + (pair >= 5).astype(F32)
    hi = jnp.where(pair == 0, 1.0, jnp.where((pair == 1) | (pair == 3), 2.0, 3.0))
    te_ref[0:1, :] = (grp * EPG + lo).astype(jnp.int32)
    te_ref[1:2, :] = (grp * EPG + hi).astype(jnp.int32)
    nv_ref[...] = jnp.broadcast_to(run / tm, nv_ref.shape).astype(jnp.int32)


def _plan_call(tm, bkt):
    nblk, _, blk = bkt.shape
    nt = nblk * blk // tm + N_BUCKETS
    ntp = -(-nt // LANES) * LANES
    pos, te, nv = pl.pallas_call(
        functools.partial(_plan_kernel, tm),
        out_shape=[jax.ShapeDtypeStruct(bkt.shape, jnp.int32), jax.ShapeDtypeStruct((2, ntp), jnp.int32),
                   jax.ShapeDtypeStruct((1, LANES), jnp.int32)],
        compiler_params=pltpu.CompilerParams(vmem_limit_bytes=VMEM_LIMIT),
        name="moe_plan",
    )(bkt)
    return pos, te[0, :nt], te[1, :nt], nv[0, :1]


def _dispatch_kernel(tb, pos_ref, hx_ref, init_ref, xs_ref, sem):
    del init_ref

    def issue(g, carry):
        base = pl.multiple_of(g * SUBLANES, SUBLANES)
        rows = hx_ref.at[pl.ds(base, SUBLANES)]
        for r in range(SUBLANES):
            pltpu.make_async_copy(rows.at[pl.ds(r, 1)], xs_ref.at[pl.ds(pos_ref[0, base + r], 1)],
                                  sem).start(priority=r % 2)
        return carry

    lax.fori_loop(0, tb // SUBLANES, issue, 0)
    pltpu.make_async_copy(hx_ref, xs_ref.at[pl.ds(0, tb)], sem).wait()


def _dispatch_call(tb, pos, hx, xs_init):
    nsteps = hx.shape[0] // tb
    return pl.pallas_call(
        functools.partial(_dispatch_kernel, tb),
        grid=(nsteps,),
        in_specs=[pl.BlockSpec((None, 1, tb), lambda i: (i, 0, 0), memory_space=pltpu.SMEM),
                  pl.BlockSpec((tb, HX_W), lambda i: (i, 0)),
                  pl.BlockSpec(memory_space=pl.ANY)],
        out_specs=pl.BlockSpec(memory_space=pl.ANY),
        out_shape=jax.ShapeDtypeStruct(xs_init.shape, xs_init.dtype),
        scratch_shapes=[pltpu.SemaphoreType.DMA(())],
        input_output_aliases={2: 0},
        compiler_params=pltpu.CompilerParams(dimension_semantics=("arbitrary",), disable_bounds_checks=True,
                                             has_side_effects=True),
        name="moe_dispatch",
    )(pos, hx, xs_init)


def _moe_kernel(te0_ref, te1_ref, nv_ref, xs_ref, w1a_ref, w3a_ref, w2a_ref, w1b_ref, w3b_ref, w2b_ref, o_ref):
    n = pl.program_id(0)

    @pl.when(n < nv_ref[0])
    def _():
        h = _unpack_pairs(xs_ref[:, :HX_H])
        cmb = lax.bitcast_convert_type(xs_ref[:, HX_H:], F32)
        lane = lax.broadcasted_iota(jnp.int32, cmb.shape, 1)

        def expert(e_id, w1_ref, w3_ref, w2_ref):
            a = jnp.dot(h, w1_ref[...], preferred_element_type=F32)
            b = jnp.dot(h, w3_ref[...], preferred_element_type=F32)
            c = jnp.sum(jnp.where(lane == e_id, cmb, 0.0), axis=-1, keepdims=True)
            return c * _mm(_silu(a) * b, w2_ref[...])

        o_ref[...] = (expert(te0_ref[n], w1a_ref, w3a_ref, w2a_ref)
                      + expert(te1_ref[n], w1b_ref, w3b_ref, w2b_ref))

    @pl.when(n >= nv_ref[0])
    def _():
        o_ref[...] = jnp.zeros(o_ref.shape, F32)


def _moe_call(tm, layer, te0, te1, nv, xs, w1, w3, w2):
    rows = xs.shape[0]
    nt = rows // tm
    tile = lambda n, te0, te1, nv: (jnp.minimum(n, nv[0] - 1), 0)
    wa = lambda n, te0, te1, nv: (layer, te0[jnp.minimum(n, nv[0] - 1)], 0, 0)
    wb = lambda n, te0, te1, nv: (layer, te1[jnp.minimum(n, nv[0] - 1)], 0, 0)
    up = lambda m: pl.BlockSpec((None, None, D_MODEL, D_FF), m)
    down = lambda m: pl.BlockSpec((None, None, D_FF, D_MODEL), m)
    return pl.pallas_call(
        _moe_kernel,
        grid_spec=pltpu.PrefetchScalarGridSpec(
            num_scalar_prefetch=3,
            grid=(nt,),
            in_specs=[pl.BlockSpec((tm, HX_W), tile), up(wa), up(wa), down(wa), up(wb), up(wb), down(wb)],
            out_specs=pl.BlockSpec((tm, D_MODEL), lambda n, te0, te1, nv: (n, 0))),
        out_shape=jax.ShapeDtypeStruct((rows, D_MODEL), F32),
        compiler_params=_cparams(("arbitrary",)),
        name="moe",
    )(te0, te1, nv, xs, w1, w3, w2, w1, w3, w2)


def _collect_kernel(tb, nsteps, pos_ref, posn_ref, ys_ref, x_ref, mod_ref, o_ref, buf, sem):
    i = pl.program_id(0)
    slot = i % 2

    def issue(p_ref, s):
        def body(g, carry):
            base = pl.multiple_of(g * SUBLANES, SUBLANES)
            rows = buf.at[s, pl.ds(base, SUBLANES)]
            for r in range(SUBLANES):
                pltpu.make_async_copy(ys_ref.at[pl.ds(p_ref[0, base + r], 1)], rows.at[pl.ds(r, 1)],
                                      sem.at[s]).start(priority=r % 2)
            return carry
        lax.fori_loop(0, tb // SUBLANES, body, 0)

    pl.when(i == 0)(lambda: issue(pos_ref, 0))
    pl.when(i + 1 < nsteps)(lambda: issue(posn_ref, 1 - slot))
    pltpu.make_async_copy(ys_ref.at[pl.ds(0, tb)], buf.at[slot], sem.at[slot]).wait()
    o_ref[...] = x_ref[...] + mod_ref[5:6, :] * buf[slot]


def _collect_call(geo, tb, pos, ys, x, mod_l):
    t = geo.t
    nsteps = t // tb
    per_seg = geo.seg // tb
    return pl.pallas_call(
        functools.partial(_collect_kernel, tb, nsteps),
        grid=(nsteps,),
        in_specs=[pl.BlockSpec((None, 1, tb), lambda i: (i, 0, 0), memory_space=pltpu.SMEM),
                  pl.BlockSpec((None, 1, tb), lambda i: (jnp.minimum(i + 1, nsteps - 1), 0, 0),
                               memory_space=pltpu.SMEM),
                  pl.BlockSpec(memory_space=pl.ANY),
                  pl.BlockSpec((tb, D_MODEL), lambda i: (i, 0)),
                  pl.BlockSpec((None, 6, D_MODEL), lambda i: (i // per_seg, 0, 0))],
        out_specs=pl.BlockSpec((tb, D_MODEL), lambda i: (i, 0)),
        out_shape=jax.ShapeDtypeStruct((t, D_MODEL), F32),
        scratch_shapes=[pltpu.VMEM((2, tb, D_MODEL), F32), pltpu.SemaphoreType.DMA((2,))],
        compiler_params=pltpu.CompilerParams(dimension_semantics=("arbitrary",), vmem_limit_bytes=VMEM_LIMIT,
                                             disable_bounds_checks=True),
        name="moe_collect",
    )(pos, pos, ys, x, mod_l)


def _tiles(geo):
    seg = geo.seg
    return dict(tm=min(512, seg), tmc=min(1024, seg), tq=min(1024, seg), tscan=min(512, seg), tmoe=min(512, seg))


def _forward(geo, tiles, x_prompt, x_sample, c_prompt, c_sample, rel_bias, router_w, router_b, ada_w, ada_b,
             norm_mix_g, norm_ffn_g, w_in_ab, w_out_ab, q_norm_g, k_norm_g, diff_lambda, diff_norm_g,
             mlstm_conv_w, mlstm_conv_b, mlstm_gate_b, mlstm_norm_g, w_in_c, w_out_c, ret_decay_logit,
             ret_norm_g, moe_w1, moe_w3, moe_w2):
    tm, tmc, tq, tscan, tmoe = tiles["tm"], tiles["tmc"], tiles["tq"], tiles["tscan"], tiles["tmoe"]
    x = jnp.concatenate([x_prompt.reshape(geo.tp, D_MODEL), x_sample.reshape(geo.t - geo.tp, D_MODEL)], axis=0)
    c_rows = jnp.concatenate([jnp.repeat(c_prompt, geo.sp // geo.seg, axis=0),
                              jnp.repeat(c_sample, geo.ss // geo.seg, axis=0)], axis=0)
    mod = _ada_call(c_rows, ada_w, ada_b)

    bias = _bias_call(rel_bias, tq)
    cos, sin = _rope_tables(max(geo.sp, geo.ss))
    rw_t = router_w.T
    rb = router_b.reshape(N_EXPERTS, 1)
    w1 = moe_w1.astype(MXU_DT)
    w3 = moe_w3.astype(MXU_DT)
    w2 = moe_w2.astype(MXU_DT)
    xs = jnp.zeros((geo.t + N_BUCKETS * tmoe, HX_W), jnp.uint32)

    for l in range(DEPTH):
        j = l // 2
        mod_l = mod[l]
        g_mix = norm_mix_g[l].reshape(1, D_MODEL)
        g_ffn = norm_ffn_g[l].reshape(1, D_MODEL)
        if l % 2 == 0:
            lam_init = 0.8 - 0.6 * math.exp(-0.3 * l)
            w_pad = jnp.pad(w_in_ab[j], ((0, 0), (0, AB_IN_PAD - AB_IN))).astype(MXU_DT)
            qg = jnp.tile(q_norm_g[j], 2).reshape(1, LANES)
            kg = jnp.tile(k_norm_g[j], 2).reshape(1, LANES)
            gate_b = jnp.pad(mlstm_gate_b[j].reshape(1, B_HEADS * N_GATES), ((0, 0), (0, LANES - B_HEADS * N_GATES)))
            qa, ka, va, qk, vb, ob, gates = _inproj_ab_call(geo, tm, x, mod_l, g_mix, w_pad, qg, kg, gate_b)
            sc, bounded_ok = _attn_scalars(bias, tq, q_norm_g[j], k_norm_g[j])
            ya = lax.cond(bounded_ok,
                          functools.partial(_attn_call, geo, tq, lam_init, True),
                          functools.partial(_attn_call, geo, tq, lam_init, False),
                          qa, ka, va, bias, sc, diff_lambda[j], diff_norm_g[j].reshape(1, A_V_DIM))
            qb, kb = _conv_call(geo, tscan, qk, mlstm_conv_w[j], mlstm_conv_b[j].reshape(1, 2 * B_WIDTH))
            yb = _mlstm_call(geo, tscan, qb, kb, vb, gates, ob, mlstm_norm_g[j].reshape(1, B_DIM))
            w_o = w_out_ab[j].astype(MXU_DT)
            ys, ws = [ya, yb], [w_o[:A_WIDTH], w_o[A_WIDTH:]]
        else:
            q, k, v, gt = _inproj_c_call(geo, tmc, x, mod_l, g_mix, w_in_c[j].astype(MXU_DT), cos, sin)
            y = _ret_call(geo, tscan, q, k, v, ret_decay_logit[j], gt, ret_norm_g[j].reshape(1, C_V_DIM))
            ys, ws = [y], [w_out_c[j].astype(MXU_DT)]
        x, hx, bucket = _outproj_call(geo, tm, ys, ws, x, mod_l, g_ffn, rw_t, rb)
        pos, te0, te1, n_tiles = _plan_call(tmoe, bucket)
        xs = _dispatch_call(tm, pos, hx, xs)
        ysort = _moe_call(tmoe, l, te0, te1, n_tiles, xs, w1, w3, w2)
        x = _collect_call(geo, tm, pos, ysort, x, mod_l)

    y_prompt = x[:geo.tp].reshape(x_prompt.shape)
    y_sample = x[geo.tp:].reshape(x_sample.shape)
    return (y_prompt, y_sample)


def kernel(x_prompt, x_sample, c_prompt, c_sample, rel_bias, router_w, router_b, ada_w, ada_b, norm_mix_g, norm_ffn_g, w_in_ab, w_out_ab, q_norm_g, k_norm_g, diff_lambda, diff_norm_g, mlstm_conv_w, mlstm_conv_b, mlstm_gate_b, mlstm_norm_g, w_in_c, w_out_c, ret_decay_logit, ret_norm_g, moe_w1, moe_w3, moe_w2):
    geo = Geo(x_prompt.shape[0], x_prompt.shape[1], x_sample.shape[0], x_sample.shape[1])
    return _forward(geo, _tiles(geo), x_prompt, x_sample, c_prompt, c_sample, rel_bias, router_w, router_b,
                    ada_w, ada_b, norm_mix_g, norm_ffn_g, w_in_ab, w_out_ab, q_norm_g, k_norm_g, diff_lambda,
                    diff_norm_g, mlstm_conv_w, mlstm_conv_b, mlstm_gate_b, mlstm_norm_g, w_in_c, w_out_c,
                    ret_decay_logit, ret_norm_g, moe_w1, moe_w3, moe_w2)
```

```python
import functools
import math

import jax
import jax.numpy as jnp
import numpy as np
from jax import lax
from jax.experimental import pallas as pl
from jax.experimental.pallas import tpu as pltpu

F32 = jnp.float32
MXU_DT = jnp.bfloat16
ACT_DT = jnp.bfloat16
HI = lax.Precision.HIGHEST

D_MODEL = 1024
DEPTH = 4
A_HEADS = 4
A_QK_DIM = 64
A_V_DIM = 128
A_QK_WIDTH = 512
A_WIDTH = 512
B_HEADS = 4
B_DIM = 128
B_WIDTH = 512
N_GATES = 4
B_CONV = 3
AB_IN = 3600
AB_IN_PAD = 3712
C_HEADS = 4
C_QK_DIM = 256
C_V_DIM = 512
C_QK_WIDTH = 1024
C_V_WIDTH = 2048
C_IN = 6144
CHUNK = 128
KB_PER_STEP = 4
RET_CHUNK = 256
REL_BUCKETS = 32
REL_MAX_DIST = 128
N_EXPERTS = 16
N_GROUPS = 4
EPG = 4
N_PAIRS = 6
N_BUCKETS = N_GROUPS * N_PAIRS
D_FF = 512
ROPE_BASE = 10000.0
EPS = 1e-6
LANES = 128
SUBLANES = 8
NEG_BIG = -1e30
LOG2E = math.log2(math.e)
HX_H = D_MODEL // 2
HX_W = HX_H + LANES
VMEM_LIMIT = 56 * 1024 * 1024


def _cparams(sem):
    return pltpu.CompilerParams(dimension_semantics=sem, vmem_limit_bytes=VMEM_LIMIT)


def _mm(a, b):
    return jnp.dot(a.astype(MXU_DT), b.astype(MXU_DT), preferred_element_type=F32)


def _mm_nt(a, b):
    return lax.dot_general(a.astype(MXU_DT), b.astype(MXU_DT), (((1,), (1,)), ((), ())),
                           preferred_element_type=F32)


def _mm_tn(a, b):
    return lax.dot_general(a.astype(MXU_DT), b.astype(MXU_DT), (((0,), (0,)), ((), ())),
                           preferred_element_type=F32)


def _mm_hi(a, b):
    return jnp.dot(a, b, precision=HI, preferred_element_type=F32)


def _silu(x):
    return x * (1.0 / (1.0 + jnp.exp(-x)))


def _sigmoid(x):
    return 1.0 / (1.0 + jnp.exp(-x))


def _log_sigmoid(x):
    return jnp.minimum(x, 0.0) - jnp.log1p(jnp.exp(-jnp.abs(x)))


def _rms(x, g):
    return x * lax.rsqrt(jnp.mean(x * x, axis=-1, keepdims=True) + EPS) * g


class Geo:
    def __init__(self, bp, sp, bs, ss):
        self.bp, self.sp, self.bs, self.ss = bp, sp, bs, ss
        self.tp = bp * sp
        self.t = bp * sp + bs * ss
        self.seg = math.gcd(sp, ss)

    def seq_start(self, row0):
        return jnp.where(row0 < self.tp, row0 % self.sp == 0, (row0 - self.tp) % self.ss == 0)

    def pos_block(self, blk, rows):
        nbp = self.tp // rows
        return jnp.where(blk < nbp, blk % (self.sp // rows), (blk - nbp) % (self.ss // rows))


def _ada_kernel(c_ref, w_ref, b_ref, o_ref):
    o_ref[...] = _mm_hi(_silu(c_ref[...]), w_ref[...]) + b_ref[...]


def _ada_call(c_rows, ada_w, ada_b):
    r = c_rows.shape[0]
    nb = 1536
    out = pl.pallas_call(
        _ada_kernel,
        grid=(DEPTH, 6 * D_MODEL // nb),
        in_specs=[pl.BlockSpec((r, D_MODEL), lambda l, n: (0, 0)),
                  pl.BlockSpec((None, D_MODEL, nb), lambda l, n: (l, 0, n)),
                  pl.BlockSpec((None, 1, nb), lambda l, n: (l, 0, n))],
        out_specs=pl.BlockSpec((None, r, nb), lambda l, n: (l, 0, n)),
        out_shape=jax.ShapeDtypeStruct((DEPTH, r, 6 * D_MODEL), F32),
        compiler_params=_cparams(("arbitrary", "arbitrary")),
        name="ada_mod",
    )(c_rows, ada_w, ada_b.reshape(DEPTH, 1, 6 * D_MODEL))
    return out.reshape(DEPTH, r, 6, D_MODEL)


def _t5_bucket(rel):
    nb = REL_BUCKETS // 2
    max_exact = nb // 2
    ret = jnp.where(rel > 0, nb, 0)
    n = jnp.abs(rel)
    nf = jnp.maximum(n, 1).astype(jnp.float32)
    large = max_exact + (jnp.log(nf / max_exact) / math.log(REL_MAX_DIST / max_exact) * (nb - max_exact)).astype(jnp.int32)
    large = jnp.minimum(large, nb - 1)
    return ret + jnp.where(n < max_exact, n, large)


def _bias_kernel(tq, rb_ref, bk_ref, o_ref):
    h = pl.program_id(0)
    bk = bk_ref[...]
    row = jnp.zeros(bk.shape, F32)
    bmax = rb_ref[0, h]
    for b in range(REL_BUCKETS):
        row = row + jnp.where(bk == b, rb_ref[b, h], 0.0)
        bmax = jnp.maximum(bmax, rb_ref[b, h])
    row = (row - bmax) * LOG2E
    table = jnp.broadcast_to(row, (tq, 2 * tq))
    o_ref[...] = pltpu.roll(table, tq + 1, axis=1, stride=1, stride_axis=0)[:, :tq]


def _bias_call(rel_bias, tq):
    rel = (jnp.arange(-1, 2, dtype=jnp.int32) * tq)[:, None, None] + (jnp.arange(2 * tq, dtype=jnp.int32) - (tq - 1))
    buckets = _t5_bucket(rel)
    return pl.pallas_call(
        functools.partial(_bias_kernel, tq),
        grid=(A_HEADS, 3),
        in_specs=[pl.BlockSpec(memory_space=pltpu.SMEM),
                  pl.BlockSpec((None, 1, 2 * tq), lambda h, o: (o, 0, 0))],
        out_specs=pl.BlockSpec((None, None, tq, tq), lambda h, o: (h, o, 0, 0)),
        out_shape=jax.ShapeDtypeStruct((A_HEADS, 3, tq, tq), F32),
        compiler_params=_cparams(("arbitrary", "arbitrary")),
        name="rel_bias_tiles",
    )(rel_bias, buckets)


def _rope_tables(s):
    d = C_QK_DIM
    inv = ROPE_BASE ** (-jnp.arange(0, d, 2, dtype=jnp.float32) / d)
    ang = jnp.arange(s, dtype=jnp.float32)[:, None] * inv[None, :]
    return jnp.cos(ang), jnp.sin(ang)


def _half_rms(z, g):
    lo_lane = lax.broadcasted_iota(jnp.int32, (1, LANES), 1) < A_QK_DIM
    z2 = z * z
    tot = jnp.sum(z2, axis=-1, keepdims=True)
    lo = jnp.sum(jnp.where(lo_lane, z2, 0.0), axis=-1, keepdims=True)
    ms = jnp.where(lo_lane, lo, tot - lo) * (1.0 / A_QK_DIM)
    return z * lax.rsqrt(ms + EPS) * g


def _inproj_ab_kernel(geo, tm, x_ref, xp_ref, xn_ref, mod_ref, g_ref, w_ref, qg_ref, kg_ref, gb_ref, cw_ref, cb_ref,
                      qa_ref, ka_ref, va_ref, qb_ref, kb_ref, vb_ref, ob_ref, gt_ref):
    norm = lambda x: _rms(x, g_ref[...]) * (1.0 + mod_ref[1:2, :]) + mod_ref[0:1, :]
    hb = norm(x_ref[...]).astype(MXU_DT)
    qscale = (A_QK_DIM ** -0.5) * LOG2E
    for hd in range(A_HEADS):
        c0 = hd * LANES
        q = jnp.dot(hb, w_ref[:, c0:c0 + LANES], preferred_element_type=F32)
        qa_ref[:, c0:c0 + LANES] = (_half_rms(q, qg_ref[...]) * qscale).astype(qa_ref.dtype)
        k = jnp.dot(hb, w_ref[:, A_QK_WIDTH + c0:A_QK_WIDTH + c0 + LANES], preferred_element_type=F32)
        ka_ref[:, c0:c0 + LANES] = _half_rms(k, kg_ref[...]).astype(ka_ref.dtype)
    o = 2 * A_QK_WIDTH
    va_ref[...] = jnp.dot(hb, w_ref[:, o:o + A_WIDTH], preferred_element_type=F32).astype(va_ref.dtype)
    o += A_WIDTH
    halo = jnp.concatenate([norm(xp_ref[...]), norm(xn_ref[...])], axis=0).astype(MXU_DT)
    qk = jnp.dot(jnp.concatenate([hb, halo], axis=0), w_ref[:, o:o + 2 * B_WIDTH], preferred_element_type=F32)
    row0 = pl.program_id(0) * tm
    first = geo.seq_start(row0)
    last = geo.seq_start(row0 + tm) | (row0 + tm == geo.t)
    prev_row = jnp.where(first, 0.0, qk[tm + SUBLANES - 1:tm + SUBLANES, :])
    next_row = jnp.where(last, 0.0, qk[tm + SUBLANES:tm + SUBLANES + 1, :])
    cur = qk[:tm]
    ridx = lax.broadcasted_iota(jnp.int32, (tm, 1), 0)
    x_prev = jnp.where(ridx == 0, prev_row, pltpu.roll(cur, 1, axis=0))
    x_next = jnp.where(ridx == tm - 1, next_row, pltpu.roll(cur, tm - 1, axis=0))
    y = _silu(x_prev * cw_ref[0:1, :] + cur * cw_ref[1:2, :] + x_next * cw_ref[2:3, :] + cb_ref[...])
    qb_ref[...] = y[:, :B_WIDTH].astype(qb_ref.dtype)
    kb_ref[...] = (y[:, B_WIDTH:] * (B_DIM ** -0.5)).astype(kb_ref.dtype)
    o += 2 * B_WIDTH
    vb_ref[...] = jnp.dot(hb, w_ref[:, o:o + B_WIDTH], preferred_element_type=F32).astype(vb_ref.dtype)
    o += B_WIDTH
    ob_ref[...] = jnp.dot(hb, w_ref[:, o:o + B_WIDTH], preferred_element_type=F32)
    o += B_WIDTH
    gt_ref[...] = jnp.dot(hb, w_ref[:, o:o + LANES], preferred_element_type=F32) + gb_ref[...]


def _inproj_ab_call(geo, tm, x, mod_l, g, w_pad, qg, kg, gate_b, conv_w, conv_b):
    t = geo.t
    per_seg = geo.seg // tm
    r8 = tm // SUBLANES
    nb8 = t // SUBLANES
    row = lambda i: (i, 0)
    const = lambda i: (0, 0)
    widths = (A_QK_WIDTH, A_QK_WIDTH, A_WIDTH, B_WIDTH, B_WIDTH, B_WIDTH, B_WIDTH, LANES)
    dtypes = (ACT_DT, ACT_DT, ACT_DT, ACT_DT, ACT_DT, ACT_DT, F32, F32)
    return pl.pallas_call(
        functools.partial(_inproj_ab_kernel, geo, tm),
        grid=(t // tm,),
        in_specs=[pl.BlockSpec((tm, D_MODEL), row),
                  pl.BlockSpec((SUBLANES, D_MODEL), lambda i: (jnp.maximum(i * r8 - 1, 0), 0)),
                  pl.BlockSpec((SUBLANES, D_MODEL), lambda i: (jnp.minimum((i + 1) * r8, nb8 - 1), 0)),
                  pl.BlockSpec((None, 6, D_MODEL), lambda i: (i // per_seg, 0, 0)),
                  pl.BlockSpec((1, D_MODEL), const),
                  pl.BlockSpec((D_MODEL, AB_IN_PAD), const),
                  pl.BlockSpec((1, LANES), const),
                  pl.BlockSpec((1, LANES), const),
                  pl.BlockSpec((1, LANES), const),
                  pl.BlockSpec((B_CONV, 2 * B_WIDTH), const),
                  pl.BlockSpec((1, 2 * B_WIDTH), const)],
        out_specs=[pl.BlockSpec((tm, w), row) for w in widths],
        out_shape=[jax.ShapeDtypeStruct((t, w), d) for w, d in zip(widths, dtypes)],
        compiler_params=_cparams(("parallel",)),
        name="inproj_ab",
    )(x, x, x, mod_l, g, w_pad, qg, kg, gate_b, conv_w, conv_b)


def _attn_finish(lam_init, acc0, l0, acc1, l1, dl_ref, ng_ref, o_ref):
    dl = dl_ref[...]
    lam = (jnp.exp(jnp.sum(dl[0:1] * dl[1:2], axis=-1, keepdims=True))
           - jnp.exp(jnp.sum(dl[2:3] * dl[3:4], axis=-1, keepdims=True)) + lam_init)
    out = acc0 / l0 - lam * (acc1 / l1)
    o_ref[...] = (_rms(out, ng_ref[...]) * (1.0 - lam_init)).astype(o_ref.dtype)


def _attn_kernel(lam_init, nk, q_ref, k_ref, v_ref, bias_ref, sc_ref, dl_ref, ng_ref, o_ref,
                 m_sc, l_sc, acc_sc):
    h = pl.program_id(1)
    i = pl.program_id(2)
    j = pl.program_id(3)

    @pl.when(j == 0)
    def _():
        m_sc[...] = jnp.full(m_sc.shape, NEG_BIG, F32)
        l_sc[...] = jnp.zeros(l_sc.shape, F32)
        acc_sc[...] = jnp.zeros(acc_sc.shape, F32)

    q = q_ref[...]
    lo_lane = lax.broadcasted_iota(jnp.int32, (1, LANES), 1) < A_QK_DIM
    qsub = (jnp.where(lo_lane, q, jnp.zeros_like(q)), jnp.where(lo_lane, jnp.zeros_like(q), q))
    k = k_ref[...]
    v = v_ref[...]

    def step(near):
        if near:
            shift = 0.0
        else:
            shift = jnp.where(j < i, sc_ref[h, 0], sc_ref[h, 1])
        for sub in range(2):
            s = _mm_nt(qsub[sub], k)
            if near:
                s = s + bias_ref[...]
            m_old = m_sc[sub]
            m_new = jnp.maximum(m_old, jnp.max(s, axis=-1, keepdims=True) + shift)
            p = jnp.exp2(s - (m_new - shift))
            alpha = jnp.exp2(m_old - m_new)
            l_sc[sub] = alpha * l_sc[sub] + jnp.sum(p, axis=-1, keepdims=True)
            acc_sc[sub] = alpha * acc_sc[sub] + _mm(p, v)
            m_sc[sub] = m_new

    near = jnp.abs(j - i) <= 1
    pl.when(near)(lambda: step(True))
    pl.when(jnp.logical_not(near))(lambda: step(False))

    @pl.when(j == nk - 1)
    def _():
        _attn_finish(lam_init, acc_sc[0], l_sc[0], acc_sc[1], l_sc[1], dl_ref, ng_ref, o_ref)


def _attn_bounded_kernel(lam_init, nk, q_ref, k_ref, v_ref, bias_ref, sc_ref, dl_ref, ng_ref, o_ref, acc_sc):
    h = pl.program_id(1)
    i = pl.program_id(2)
    j = pl.program_id(3)
    tq = q_ref.shape[0]

    @pl.when(j == 0)
    def _():
        acc_sc[...] = jnp.zeros(acc_sc.shape, F32)

    q = q_ref[...]
    lo = (lax.broadcasted_iota(jnp.int32, q.shape, 1) < A_QK_DIM).astype(F32).astype(q.dtype)
    qsub = (q * lo, q * (1 - lo))
    ones_col = (lax.broadcasted_iota(jnp.int32, (tq, LANES), 1) == 0).astype(v_ref.dtype)

    for kb in range(KB_PER_STEP):
        jb = j * KB_PER_STEP + kb

        @pl.when(jb == jnp.maximum(i - 1, 0))
        def _():
            acc_sc[...] = acc_sc[...] * sc_ref[h, 2]

        @pl.when(jb == i + 2)
        def _():
            acc_sc[...] = acc_sc[...] * sc_ref[h, 3]

        def step(near, kb=kb, jb=jb):
            k = k_ref[kb * tq:(kb + 1) * tq, :]
            v_aug = jnp.concatenate([v_ref[kb * tq:(kb + 1) * tq, :], ones_col], axis=1)
            for sub in range(2):
                s = _mm_nt(qsub[sub], k)
                if near:
                    s = s + bias_ref[jb - i + 1]
                acc_sc[sub] += _mm(jnp.exp2(s), v_aug)

        near = jnp.abs(jb - i) <= 1
        pl.when(near)(functools.partial(step, True))
        pl.when(jnp.logical_not(near))(functools.partial(step, False))

    @pl.when(j == nk - 1)
    def _():
        a0 = acc_sc[0]
        a1 = acc_sc[1]
        _attn_finish(lam_init, a0[:, :A_V_DIM], a0[:, A_V_DIM:A_V_DIM + 1], a1[:, :A_V_DIM],
                     a1[:, A_V_DIM:A_V_DIM + 1], dl_ref, ng_ref, o_ref)


def _attn_scalars(bias, tq, q_gain, k_gain):
    far_l = bias[:, 0, tq - 1, 0]
    far_r = bias[:, 2, 0, tq - 1]
    sc = jnp.stack([far_l, far_r, jnp.exp2(far_l), jnp.exp2(-far_r)], axis=-1)
    bound = A_QK_DIM * jnp.max(jnp.abs(q_gain)) * jnp.max(jnp.abs(k_gain)) * (A_QK_DIM ** -0.5) * LOG2E * 1.02
    spread = -jnp.min(bias)
    ok = bound + 2.0 * spread <= 80.0
    return sc, ok


def _attn_call(geo, tq, lam_init, bounded, qa, ka, va, bias, sc, dl, ng):
    outs = []
    kb = KB_PER_STEP if bounded else 1
    for (nb, s, row_off) in ((geo.bp, geo.sp, 0), (geo.bs, geo.ss, geo.tp)):
        nq = s // tq
        nk = nq // kb
        off = row_off // tq
        qmap = lambda b, h, i, j, off=off, nq=nq: (off + b * nq + i, h)
        kmap = lambda b, h, i, j, off=off, nk=nk: (off // kb + b * nk + j, h)
        omap = lambda b, h, i, j, nq=nq: (b * nq + i, h)
        if bounded:
            body = functools.partial(_attn_bounded_kernel, lam_init, nk)
            scratch = [pltpu.VMEM((2, tq, 2 * LANES), F32)]
            bias_spec = pl.BlockSpec((None, 3, tq, tq), lambda b, h, i, j: (h, 0, 0, 0))
        else:
            body = functools.partial(_attn_kernel, lam_init, nk)
            scratch = [pltpu.VMEM((2, tq, 1), F32), pltpu.VMEM((2, tq, 1), F32), pltpu.VMEM((2, tq, LANES), F32)]
            bias_spec = pl.BlockSpec((None, None, tq, tq), lambda b, h, i, j: (h, jnp.clip(j - i + 1, 0, 2), 0, 0))
        outs.append(pl.pallas_call(
            body,
            grid=(nb, A_HEADS, nq, nk),
            in_specs=[pl.BlockSpec((tq, LANES), qmap),
                      pl.BlockSpec((kb * tq, LANES), kmap),
                      pl.BlockSpec((kb * tq, LANES), kmap),
                      bias_spec,
                      pl.BlockSpec(memory_space=pltpu.SMEM),
                      pl.BlockSpec((4, A_QK_DIM), lambda b, h, i, j: (0, 0)),
                      pl.BlockSpec((1, LANES), lambda b, h, i, j: (0, 0))],
            out_specs=pl.BlockSpec((tq, LANES), omap),
            out_shape=jax.ShapeDtypeStruct((nb * s, A_WIDTH), ACT_DT),
            scratch_shapes=scratch,
            compiler_params=_cparams(("parallel", "parallel", "parallel", "arbitrary")),
            name="diff_attn_bounded" if bounded else "diff_attn",
        )(qa, ka, va, bias, sc, dl, ng))
    return jnp.concatenate(outs, axis=0)


def _tri(lower):
    r = lax.broadcasted_iota(jnp.int32, (CHUNK, CHUNK), 0)
    c = lax.broadcasted_iota(jnp.int32, (CHUNK, CHUNK), 1)
    return (c <= r) if lower else (c >= r)


def _mlstm_kernel(geo, tb, reverse, *refs):
    if reverse:
        q_ref, k_ref, v_ref, g_ref, hf_ref, ob_ref, ng_ref, o_ref, c_sc, m_sc = refs
    else:
        q_ref, k_ref, v_ref, g_ref, o_ref, c_sc, m_sc = refs
    step = pl.program_id(0)
    nblk = geo.t // tb
    blk = (nblk - 1 - step) if reverse else step
    row0 = blk * tb
    if reverse:
        fresh = geo.seq_start(row0 + tb) | (row0 + tb == geo.t)
    else:
        fresh = geo.seq_start(row0)

    @pl.when(fresh)
    def _():
        c_sc[...] = jnp.zeros(c_sc.shape, F32)
        m_sc[...] = jnp.zeros(m_sc.shape, F32)

    mask = _tri(not reverse)
    cum_l = mask.astype(F32)
    cum_r = _tri(reverse).astype(F32)
    ones_col = (lax.broadcasted_iota(jnp.int32, (CHUNK, LANES), 1) == 0).astype(MXU_DT)
    nch = tb // CHUNK

    def chunk(ci, carry):
        c_idx = (nch - 1 - ci) if reverse else ci
        r0 = pl.multiple_of(c_idx * CHUNK, CHUNK)
        g = g_ref[pl.ds(r0, CHUNK), :]
        g_t = g.T
        b_col = _mm_hi(cum_l, _log_sigmoid(g))
        b_row = _mm_hi(_log_sigmoid(g_t), cum_r)
        for hd in range(B_HEADS):
            ci_col = hd * N_GATES + (2 if reverse else 0)
            cf_col = ci_col + 1
            lanes = slice(hd * B_DIM, (hd + 1) * B_DIM)
            q = q_ref[pl.ds(r0, CHUNK), lanes]
            k = k_ref[pl.ds(r0, CHUNK), lanes]
            v = v_ref[pl.ds(r0, CHUNK), lanes]
            bc = b_col[:, cf_col:cf_col + 1]
            br = b_row[cf_col:cf_col + 1, :]
            ic = g[:, ci_col:ci_col + 1]
            ir = g_t[ci_col:ci_col + 1, :]
            m_prev = m_sc[hd]
            log_d = jnp.where(mask, bc - br + ir, NEG_BIG)
            m_inter = bc + m_prev
            m_t = jnp.maximum(jnp.max(log_d, axis=-1, keepdims=True), m_inter)
            s = _mm_nt(q, k) * jnp.exp(log_d - m_t)
            inter = jnp.exp(m_inter - m_t)
            c_aug = c_sc[hd]
            qc = _mm(q, c_aug)
            num = _mm(s, v) + inter * qc[:, :B_DIM]
            den = jnp.sum(s, axis=-1, keepdims=True) + inter * qc[:, B_DIM:B_DIM + 1]
            hout = num / jnp.maximum(jnp.abs(den), jnp.exp(-m_t))
            b_last = bc[0:1, :] if reverse else bc[CHUNK - 1:CHUNK, :]
            log_w = b_last - bc + ic
            m_new = jnp.maximum(b_last + m_prev, jnp.max(log_w, axis=0, keepdims=True))
            w = jnp.exp(log_w - m_new)
            decay = jnp.exp(b_last + m_prev - m_new)
            v_aug = jnp.concatenate([v, ones_col], axis=1)
            c_sc[hd] = decay * c_aug + _mm_tn(k.astype(F32) * w, v_aug)
            m_sc[hd] = m_new
            if reverse:
                hsum = hf_ref[pl.ds(r0, CHUNK), lanes].astype(F32) + hout
                y = _rms(hsum, ng_ref[...]) * _sigmoid(ob_ref[pl.ds(r0, CHUNK), lanes])
                o_ref[pl.ds(r0, CHUNK), lanes] = y.astype(o_ref.dtype)
            else:
                o_ref[pl.ds(r0, CHUNK), lanes] = hout.astype(o_ref.dtype)
        return carry

    lax.fori_loop(0, nch, chunk, 0)


def _mlstm_call(geo, tb, qb, kb, vb, gates, ob, ng):
    t = geo.t
    nblk = t // tb
    fmap = lambda s: (s, 0)
    rmap = lambda s: (nblk - 1 - s, 0)
    scratch = [pltpu.VMEM((B_HEADS, B_DIM, 2 * B_DIM), F32), pltpu.VMEM((B_HEADS, 1, 1), F32)]
    wide = lambda m: pl.BlockSpec((tb, B_WIDTH), m)
    hf = pl.pallas_call(
        functools.partial(_mlstm_kernel, geo, tb, False),
        grid=(nblk,),
        in_specs=[wide(fmap), wide(fmap), wide(fmap), pl.BlockSpec((tb, LANES), fmap)],
        out_specs=wide(fmap),
        out_shape=jax.ShapeDtypeStruct((t, B_WIDTH), ACT_DT),
        scratch_shapes=scratch,
        compiler_params=_cparams(("arbitrary",)),
        name="mlstm_fwd",
    )(qb, kb, vb, gates)
    return pl.pallas_call(
        functools.partial(_mlstm_kernel, geo, tb, True),
        grid=(nblk,),
        in_specs=[wide(rmap), wide(rmap), wide(rmap), pl.BlockSpec((tb, LANES), rmap), wide(rmap), wide(rmap),
                  pl.BlockSpec((1, B_DIM), lambda s: (0, 0))],
        out_specs=wide(rmap),
        out_shape=jax.ShapeDtypeStruct((t, B_WIDTH), ACT_DT),
        scratch_shapes=scratch,
        compiler_params=_cparams(("arbitrary",)),
        name="mlstm_bwd",
    )(qb, kb, vb, gates, hf, ob, ng)


def _inproj_c_kernel(x_ref, mod_ref, g_ref, w_ref, cos_ref, sin_ref, q_ref, k_ref, v_ref, gt_ref, h_sc):
    j = pl.program_id(1)
    nsub = C_HEADS
    sub = w_ref.shape[1] // nsub

    def proj(c):
        return jnp.dot(h_sc[...], w_ref[:, c * sub:(c + 1) * sub], preferred_element_type=F32)

    def rope(o_ref, scale):
        cos = cos_ref[...] * scale
        sin = sin_ref[...] * scale
        half = C_QK_DIM // 2
        for hd in range(C_HEADS):
            y = proj(hd)
            x1, x2 = y[:, :half], y[:, half:]
            o_ref[:, hd * C_QK_DIM:hd * C_QK_DIM + half] = (x1 * cos - x2 * sin).astype(o_ref.dtype)
            o_ref[:, hd * C_QK_DIM + half:(hd + 1) * C_QK_DIM] = (x1 * sin + x2 * cos).astype(o_ref.dtype)

    @pl.when(j == 0)
    def _():
        x = x_ref[...]
        h = _rms(x, g_ref[...]) * (1.0 + mod_ref[1:2, :]) + mod_ref[0:1, :]
        h_sc[...] = h.astype(h_sc.dtype)
        rope(q_ref, 1.0)

    @pl.when(j == 1)
    def _():
        rope(k_ref, C_QK_DIM ** -0.5)

    @pl.when((j == 2) | (j == 3))
    def _():
        for c in range(nsub):
            v_ref[:, c * sub:(c + 1) * sub] = proj(c).astype(v_ref.dtype)

    @pl.when(j >= 4)
    def _():
        for c in range(nsub):
            gt_ref[:, c * sub:(c + 1) * sub] = _silu(proj(c)).astype(gt_ref.dtype)


def _inproj_c_call(geo, tm, x, mod_l, g, w, cos, sin):
    t = geo.t
    per_seg = geo.seg // tm
    nw = D_MODEL
    return pl.pallas_call(
        _inproj_c_kernel,
        grid=(t // tm, C_IN // nw),
        in_specs=[pl.BlockSpec((tm, D_MODEL), lambda i, j: (i, 0)),
                  pl.BlockSpec((None, 6, D_MODEL), lambda i, j: (i // per_seg, 0, 0)),
                  pl.BlockSpec((1, D_MODEL), lambda i, j: (0, 0)),
                  pl.BlockSpec((D_MODEL, nw), lambda i, j: (0, j)),
                  pl.BlockSpec((tm, C_QK_DIM // 2), lambda i, j: (geo.pos_block(i, tm), 0)),
                  pl.BlockSpec((tm, C_QK_DIM // 2), lambda i, j: (geo.pos_block(i, tm), 0))],
        out_specs=[pl.BlockSpec((tm, nw), lambda i, j: (i, 0)),
                   pl.BlockSpec((tm, nw), lambda i, j: (i, 0)),
                   pl.BlockSpec((tm, nw), lambda i, j: (i, jnp.clip(j - 2, 0, 1))),
                   pl.BlockSpec((tm, nw), lambda i, j: (i, jnp.clip(j - 4, 0, 1)))],
        out_shape=[jax.ShapeDtypeStruct((t, C_QK_WIDTH), ACT_DT),
                   jax.ShapeDtypeStruct((t, C_QK_WIDTH), ACT_DT),
                   jax.ShapeDtypeStruct((t, C_V_WIDTH), ACT_DT),
                   jax.ShapeDtypeStruct((t, C_V_WIDTH), ACT_DT)],
        scratch_shapes=[pltpu.VMEM((tm, D_MODEL), MXU_DT)],
        compiler_params=_cparams(("parallel", "arbitrary")),
        name="inproj_c",
    )(x, mod_l, g, w, cos, sin)


def _ret_kernel(geo, tb, reverse, *refs):
    if reverse:
        q_ref, k_ref, v_ref, dlg_ref, yf_ref, gt_ref, ng_ref, o_ref, r_sc, intra_sc, vec_sc = refs
    else:
        q_ref, k_ref, v_ref, dlg_ref, o_ref, r_sc, intra_sc, vec_sc = refs
    step = pl.program_id(0)
    nblk = geo.t // tb
    blk = (nblk - 1 - step) if reverse else step
    row0 = blk * tb
    if reverse:
        fresh = geo.seq_start(row0 + tb) | (row0 + tb == geo.t)
    else:
        fresh = geo.seq_start(row0)

    @pl.when(fresh)
    def _():
        r_sc[...] = jnp.zeros(r_sc.shape, F32)

    @pl.when(step == 0)
    def _():
        lg_all = _log_sigmoid(dlg_ref[...])
        ti = lax.broadcasted_iota(jnp.int32, (RET_CHUNK, RET_CHUNK), 0)
        si = lax.broadcasted_iota(jnp.int32, (RET_CHUNK, RET_CHUNK), 1)
        dist = ((si - ti) if reverse else (ti - si)).astype(F32)
        pos = lax.broadcasted_iota(jnp.int32, (RET_CHUNK, LANES), 0).astype(F32)
        upos = (RET_CHUNK - 1.0 - pos) if reverse else pos
        lane = lax.broadcasted_iota(jnp.int32, (RET_CHUNK, LANES), 1)
        d = 1 if reverse else 0
        for hd in range(C_HEADS):
            lg = lg_all[d:d + 1, hd:hd + 1]
            intra_sc[hd] = jnp.where(dist >= 0, jnp.exp(jnp.maximum(dist, 0.0) * lg), 0.0)
            vec_sc[hd] = jnp.where(lane == 0, jnp.exp((upos + 1.0) * lg),
                                   jnp.where(lane == 1, jnp.exp((RET_CHUNK - 1.0 - upos) * lg),
                                             jnp.exp(RET_CHUNK * lg)))

    nch = tb // RET_CHUNK

    def chunk(ci, carry):
        c_idx = (nch - 1 - ci) if reverse else ci
        r0 = pl.multiple_of(c_idx * RET_CHUNK, RET_CHUNK)
        for hd in range(C_HEADS):
            vec = vec_sc[hd]
            q_scale, k_scale, c_decay = vec[:, 0:1], vec[:, 1:2], vec[0:1, 2:3]
            ql = slice(hd * C_QK_DIM, (hd + 1) * C_QK_DIM)
            vl = slice(hd * C_V_DIM, (hd + 1) * C_V_DIM)
            q = q_ref[pl.ds(r0, RET_CHUNK), ql]
            k = k_ref[pl.ds(r0, RET_CHUNK), ql]
            v = v_ref[pl.ds(r0, RET_CHUNK), vl]
            r_old = r_sc[hd]
            s = _mm_nt(q, k) * intra_sc[hd]
            y = _mm(s, v) + q_scale * _mm(q, r_old)
            r_sc[hd] = c_decay * r_old + _mm_tn(k.astype(F32) * k_scale, v)
            if reverse:
                ysum = yf_ref[pl.ds(r0, RET_CHUNK), vl].astype(F32) + y
                out = _rms(ysum, ng_ref[...]) * gt_ref[pl.ds(r0, RET_CHUNK), vl].astype(F32)
                o_ref[pl.ds(r0, RET_CHUNK), vl] = out.astype(o_ref.dtype)
            else:
                o_ref[pl.ds(r0, RET_CHUNK), vl] = y.astype(o_ref.dtype)
        return carry

    lax.fori_loop(0, nch, chunk, 0)


def _ret_call(geo, tb, q, k, v, decay_logit, gt, ng):
    t = geo.t
    nblk = t // tb
    fmap = lambda s: (s, 0)
    rmap = lambda s: (nblk - 1 - s, 0)
    scratch = [pltpu.VMEM((C_HEADS, C_QK_DIM, C_V_DIM), F32), pltpu.VMEM((C_HEADS, RET_CHUNK, RET_CHUNK), F32),
               pltpu.VMEM((C_HEADS, RET_CHUNK, LANES), F32)]
    qk = lambda m: pl.BlockSpec((tb, C_QK_WIDTH), m)
    vv = lambda m: pl.BlockSpec((tb, C_V_WIDTH), m)
    dspec = pl.BlockSpec((2, C_HEADS), lambda s: (0, 0))
    yf = pl.pallas_call(
        functools.partial(_ret_kernel, geo, tb, False),
        grid=(nblk,),
        in_specs=[qk(fmap), qk(fmap), vv(fmap), dspec],
        out_specs=vv(fmap),
        out_shape=jax.ShapeDtypeStruct((t, C_V_WIDTH), ACT_DT),
        scratch_shapes=scratch,
        compiler_params=_cparams(("arbitrary",)),
        name="ret_fwd",
    )(q, k, v, decay_logit)
    return pl.pallas_call(
        functools.partial(_ret_kernel, geo, tb, True),
        grid=(nblk,),
        in_specs=[qk(rmap), qk(rmap), vv(rmap), dspec, vv(rmap), vv(rmap),
                  pl.BlockSpec((1, C_V_DIM), lambda s: (0, 0))],
        out_specs=vv(rmap),
        out_shape=jax.ShapeDtypeStruct((t, C_V_WIDTH), ACT_DT),
        scratch_shapes=scratch,
        compiler_params=_cparams(("arbitrary",)),
        name="ret_bwd",
    )(q, k, v, decay_logit, yf, gt, ng)


def _route(probs):
    p = [probs[e:e + 1, :] for e in range(N_EXPERTS)]
    scores = []
    for g in range(N_GROUPS):
        a, b, c, d = p[EPG * g:EPG * g + EPG]
        hi1, lo1 = jnp.maximum(a, b), jnp.minimum(a, b)
        hi2, lo2 = jnp.maximum(c, d), jnp.minimum(c, d)
        scores.append(jnp.maximum(hi1, hi2) + jnp.maximum(jnp.minimum(hi1, hi2), jnp.maximum(lo1, lo2)))
    g_sel = jnp.zeros(scores[0].shape, jnp.int32)
    best = scores[0]
    for g in range(1, N_GROUPS):
        better = scores[g] > best
        g_sel = jnp.where(better, g, g_sel)
        best = jnp.where(better, scores[g], best)
    vals = []
    for kk in range(EPG):
        v = p[kk]
        for g in range(1, N_GROUPS):
            v = jnp.where(g_sel == g, p[EPG * g + kk], v)
        vals.append(v)

    def argmax4(xs):
        idx = jnp.zeros(xs[0].shape, jnp.int32)
        top = xs[0]
        for kk in range(1, EPG):
            better = xs[kk] > top
            idx = jnp.where(better, kk, idx)
            top = jnp.where(better, xs[kk], top)
        return idx, top

    i1, v1 = argmax4(vals)
    i2, v2 = argmax4([jnp.where(i1 == kk, -1.0, vals[kk]) for kk in range(EPG)])
    tot = v1 + v2
    w1, w2 = v1 / tot, v2 / tot
    e1 = g_sel * EPG + i1
    e2 = g_sel * EPG + i2
    eidx = lax.broadcasted_iota(jnp.int32, probs.shape, 0)
    lo, hi = jnp.minimum(i1, i2), jnp.maximum(i1, i2)
    pair = jnp.where(lo == 0, hi - 1, jnp.where(lo == 1, hi + 1, N_PAIRS - 1))
    return jnp.where(eidx == e1, w1, 0.0) + jnp.where(eidx == e2, w2, 0.0), g_sel * N_PAIRS + pair


def _pack_pairs(h):
    n = h.shape[1] // 2
    bits = lax.bitcast_convert_type(h.astype(MXU_DT).astype(F32), jnp.uint32)
    return (bits[:, :n] & jnp.uint32(0xFFFF0000)) | (bits[:, n:] >> 16)


def _unpack_pairs(u):
    hi = lax.bitcast_convert_type(u & jnp.uint32(0xFFFF0000), F32)
    lo = lax.bitcast_convert_type(u << 16, F32)
    return jnp.concatenate([hi, lo], axis=1).astype(MXU_DT)


def _outproj_kernel(nparts, *refs):
    y_refs = refs[:nparts]
    w_refs = refs[nparts:2 * nparts]
    x_ref, mod_ref, g_ref, rw_ref, rb_ref, xo_ref, hx_ref, bkt_ref = refs[2 * nparts:]
    m = jnp.dot(y_refs[0][...], w_refs[0][...], preferred_element_type=F32)
    for p in range(1, nparts):
        m = m + jnp.dot(y_refs[p][...], w_refs[p][...], preferred_element_type=F32)
    x = x_ref[...] + mod_ref[2:3, :] * m
    xo_ref[...] = x
    h = _rms(x, g_ref[...]) * (1.0 + mod_ref[4:5, :]) + mod_ref[3:4, :]
    hx_ref[:, :HX_H] = _pack_pairs(h)
    logits = lax.dot_general(rw_ref[...], h, (((1,), (1,)), ((), ())), precision=HI,
                             preferred_element_type=F32) + rb_ref[...]
    z = jnp.exp(logits - jnp.max(logits, axis=0, keepdims=True))
    probs = z / jnp.sum(z, axis=0, keepdims=True)
    cmb, bucket = _route(probs)
    pad = jnp.zeros((LANES - N_EXPERTS, cmb.shape[1]), F32)
    hx_ref[:, HX_H:] = lax.bitcast_convert_type(jnp.concatenate([cmb, pad], axis=0).T, jnp.uint32)
    bkt_ref[...] = bucket


def _outproj_call(geo, tm, ys, ws, x, mod_l, g, rw_t, rb):
    t = geo.t
    per_seg = geo.seg // tm
    row = lambda i: (i, 0)
    const = lambda i: (0, 0)
    n = len(ys)
    return pl.pallas_call(
        functools.partial(_outproj_kernel, n),
        grid=(t // tm,),
        in_specs=([pl.BlockSpec((tm, y.shape[1]), row) for y in ys]
                  + [pl.BlockSpec(w.shape, const) for w in ws]
                  + [pl.BlockSpec((tm, D_MODEL), row),
                     pl.BlockSpec((None, 6, D_MODEL), lambda i: (i // per_seg, 0, 0)),
                     pl.BlockSpec((1, D_MODEL), const),
                     pl.BlockSpec((N_EXPERTS, D_MODEL), const),
                     pl.BlockSpec((N_EXPERTS, 1), const)]),
        out_specs=[pl.BlockSpec((tm, D_MODEL), row), pl.BlockSpec((tm, HX_W), row),
                   pl.BlockSpec((None, 1, tm), lambda i: (i, 0, 0))],
        out_shape=[jax.ShapeDtypeStruct((t, D_MODEL), F32), jax.ShapeDtypeStruct((t, HX_W), jnp.uint32),
                   jax.ShapeDtypeStruct((t // tm, 1, tm), jnp.int32)],
        compiler_params=_cparams(("parallel",)),
        name="outproj_router",
    )(*ys, *ws, x, mod_l, g, rw_t, rb)


def _plan_kernel(tm, bkt_ref, pos_ref, te_ref, nv_ref):
    nblk, _, blk = bkt_ref.shape
    nrow = 32
    bid = lax.broadcasted_iota(jnp.int32, (nrow, blk), 0)
    r = lax.broadcasted_iota(jnp.int32, (blk, blk), 0)
    c = lax.broadcasted_iota(jnp.int32, (blk, blk), 1)
    prefix = (r <= c).astype(MXU_DT)

    def count(b, acc):
        return acc + jnp.sum((bkt_ref[b] == bid).astype(F32), axis=1, keepdims=True)

    counts = lax.fori_loop(0, nblk, count, jnp.zeros((nrow, 1), F32))
    padded = jnp.floor((counts + (tm - 1.0)) / tm) * tm
    rows = lax.broadcasted_iota(jnp.int32, (nrow, 1), 0)
    offs = jnp.zeros((nrow, 1), F32)
    ends = []
    run = jnp.zeros((1, 1), F32)
    for b in range(N_BUCKETS):
        offs = jnp.where(rows == b, run, offs)
        run = run + padded[b:b + 1, :]
        ends.append(run)

    def place(b, carry):
        oh = (bkt_ref[b] == bid).astype(F32)
        pre = jnp.dot(oh.astype(MXU_DT), prefix, preferred_element_type=F32)
        pos = jnp.sum(oh * (offs + carry + pre - 1.0), axis=0, keepdims=True)
        pos_ref[b] = pos.astype(jnp.int32)
        return carry + pre[:, blk - 1:blk]

    lax.fori_loop(0, nblk, place, jnp.zeros((nrow, 1), F32))
    start = lax.broadcasted_iota(jnp.int32, (1, te_ref.shape[1]), 1).astype(F32) * tm
    tb = jnp.zeros(start.shape, F32)
    for b in range(N_BUCKETS - 1):
        tb = tb + (ends[b] <= start).astype(F32)
    grp = jnp.floor((tb + 0.5) / N_PAIRS)
    pair = tb - grp * N_PAIRS
    lo = (pair >= 3).astype(F32) + (pair >= 5).astype(F32)
    hi = jnp.where(pair == 0, 1.0, jnp.where((pair == 1) | (pair == 3), 2.0, 3.0))
    te_ref[0:1, :] = (grp * EPG + lo).astype(jnp.int32)
    te_ref[1:2, :] = (grp * EPG + hi).astype(jnp.int32)
    nv_ref[...] = jnp.broadcast_to(run / tm, nv_ref.shape).astype(jnp.int32)


def _plan_call(tm, bkt):
    nblk, _, blk = bkt.shape
    nt = nblk * blk // tm + N_BUCKETS
    ntp = -(-nt // LANES) * LANES
    pos, te, nv = pl.pallas_call(
        functools.partial(_plan_kernel, tm),
        out_shape=[jax.ShapeDtypeStruct(bkt.shape, jnp.int32), jax.ShapeDtypeStruct((2, ntp), jnp.int32),
                   jax.ShapeDtypeStruct((1, LANES), jnp.int32)],
        compiler_params=pltpu.CompilerParams(vmem_limit_bytes=VMEM_LIMIT),
        name="moe_plan",
    )(bkt)
    return pos, te[0, :nt], te[1, :nt], nv[0, :1]


def _dispatch_kernel(tb, pos_ref, hx_ref, init_ref, xs_ref, sem):
    del init_ref

    def issue(g, carry):
        base = pl.multiple_of(g * SUBLANES, SUBLANES)
        rows = hx_ref.at[pl.ds(base, SUBLANES)]
        for r in range(SUBLANES):
            pltpu.make_async_copy(rows.at[pl.ds(r, 1)], xs_ref.at[pl.ds(pos_ref[0, base + r], 1)],
                                  sem).start(priority=r % 2)
        return carry

    lax.fori_loop(0, tb // SUBLANES, issue, 0)
    pltpu.make_async_copy(hx_ref, xs_ref.at[pl.ds(0, tb)], sem).wait()


def _dispatch_call(tb, pos, hx, xs_init):
    nsteps = hx.shape[0] // tb
    return pl.pallas_call(
        functools.partial(_dispatch_kernel, tb),
        grid=(nsteps,),
        in_specs=[pl.BlockSpec((None, 1, tb), lambda i: (i, 0, 0), memory_space=pltpu.SMEM),
                  pl.BlockSpec((tb, HX_W), lambda i: (i, 0)),
                  pl.BlockSpec(memory_space=pl.ANY)],
        out_specs=pl.BlockSpec(memory_space=pl.ANY),
        out_shape=jax.ShapeDtypeStruct(xs_init.shape, xs_init.dtype),
        scratch_shapes=[pltpu.SemaphoreType.DMA(())],
        input_output_aliases={2: 0},
        compiler_params=pltpu.CompilerParams(dimension_semantics=("arbitrary",), disable_bounds_checks=True,
                                             has_side_effects=True),
        name="moe_dispatch",
    )(pos, hx, xs_init)


def _moe_kernel(te0_ref, te1_ref, nv_ref, xs_ref, w1a_ref, w3a_ref, w2a_ref, w1b_ref, w3b_ref, w2b_ref, o_ref):
    n = pl.program_id(0)

    @pl.when(n < nv_ref[0])
    def _():
        h = _unpack_pairs(xs_ref[:, :HX_H])
        cmb = lax.bitcast_convert_type(xs_ref[:, HX_H:], F32)
        lane = lax.broadcasted_iota(jnp.int32, cmb.shape, 1)

        def expert(e_id, w1_ref, w3_ref, w2_ref):
            a = jnp.dot(h, w1_ref[...], preferred_element_type=F32)
            b = jnp.dot(h, w3_ref[...], preferred_element_type=F32)
            c = jnp.sum(jnp.where(lane == e_id, cmb, 0.0), axis=-1, keepdims=True)
            return c * _mm(_silu(a) * b, w2_ref[...])

        o_ref[...] = (expert(te0_ref[n], w1a_ref, w3a_ref, w2a_ref)
                      + expert(te1_ref[n], w1b_ref, w3b_ref, w2b_ref))

    @pl.when(n >= nv_ref[0])
    def _():
        o_ref[...] = jnp.zeros(o_ref.shape, F32)


def _moe_call(tm, layer, te0, te1, nv, xs, w1, w3, w2):
    rows = xs.shape[0]
    nt = rows // tm
    tile = lambda n, te0, te1, nv: (jnp.minimum(n, nv[0] - 1), 0)
    wa = lambda n, te0, te1, nv: (layer, te0[jnp.minimum(n, nv[0] - 1)], 0, 0)
    wb = lambda n, te0, te1, nv: (layer, te1[jnp.minimum(n, nv[0] - 1)], 0, 0)
    up = lambda m: pl.BlockSpec((None, None, D_MODEL, D_FF), m)
    down = lambda m: pl.BlockSpec((None, None, D_FF, D_MODEL), m)
    return pl.pallas_call(
        _moe_kernel,
        grid_spec=pltpu.PrefetchScalarGridSpec(
            num_scalar_prefetch=3,
            grid=(nt,),
            in_specs=[pl.BlockSpec((tm, HX_W), tile), up(wa), up(wa), down(wa), up(wb), up(wb), down(wb)],
            out_specs=pl.BlockSpec((tm, D_MODEL), lambda n, te0, te1, nv: (n, 0))),
        out_shape=jax.ShapeDtypeStruct((rows, D_MODEL), F32),
        compiler_params=_cparams(("arbitrary",)),
        name="moe",
    )(te0, te1, nv, xs, w1, w3, w2, w1, w3, w2)


def _collect_kernel(tb, nsteps, pos_ref, posn_ref, ys_ref, x_ref, mod_ref, o_ref, buf, sem):
    i = pl.program_id(0)
    slot = i % 2

    def issue(p_ref, s):
        def body(g, carry):
            base = pl.multiple_of(g * SUBLANES, SUBLANES)
            rows = buf.at[s, pl.ds(base, SUBLANES)]
            for r in range(SUBLANES):
                pltpu.make_async_copy(ys_ref.at[pl.ds(p_ref[0, base + r], 1)], rows.at[pl.ds(r, 1)],
                                      sem.at[s]).start(priority=r % 2)
            return carry
        lax.fori_loop(0, tb // SUBLANES, body, 0)

    pl.when(i == 0)(lambda: issue(pos_ref, 0))
    pl.when(i + 1 < nsteps)(lambda: issue(posn_ref, 1 - slot))
    pltpu.make_async_copy(ys_ref.at[pl.ds(0, tb)], buf.at[slot], sem.at[slot]).wait()
    o_ref[...] = x_ref[...] + mod_ref[5:6, :] * buf[slot]


def _collect_call(geo, tb, pos, ys, x, mod_l):
    t = geo.t
    nsteps = t // tb
    per_seg = geo.seg // tb
    return pl.pallas_call(
        functools.partial(_collect_kernel, tb, nsteps),
        grid=(nsteps,),
        in_specs=[pl.BlockSpec((None, 1, tb), lambda i: (i, 0, 0), memory_space=pltpu.SMEM),
                  pl.BlockSpec((None, 1, tb), lambda i: (jnp.minimum(i + 1, nsteps - 1), 0, 0),
                               memory_space=pltpu.SMEM),
                  pl.BlockSpec(memory_space=pl.ANY),
                  pl.BlockSpec((tb, D_MODEL), lambda i: (i, 0)),
                  pl.BlockSpec((None, 6, D_MODEL), lambda i: (i // per_seg, 0, 0))],
        out_specs=pl.BlockSpec((tb, D_MODEL), lambda i: (i, 0)),
        out_shape=jax.ShapeDtypeStruct((t, D_MODEL), F32),
        scratch_shapes=[pltpu.VMEM((2, tb, D_MODEL), F32), pltpu.SemaphoreType.DMA((2,))],
        compiler_params=pltpu.CompilerParams(dimension_semantics=("arbitrary",), vmem_limit_bytes=VMEM_LIMIT,
                                             disable_bounds_checks=True),
        name="moe_collect",
    )(pos, pos, ys, x, mod_l)


def _tiles(geo):
    seg = geo.seg
    return dict(tm=min(512, seg), tmc=min(1024, seg), tq=min(1024, seg), tscan=min(512, seg), tmoe=min(512, seg))


def _forward(geo, tiles, x_prompt, x_sample, c_prompt, c_sample, rel_bias, router_w, router_b, ada_w, ada_b,
             norm_mix_g, norm_ffn_g, w_in_ab, w_out_ab, q_norm_g, k_norm_g, diff_lambda, diff_norm_g,
             mlstm_conv_w, mlstm_conv_b, mlstm_gate_b, mlstm_norm_g, w_in_c, w_out_c, ret_decay_logit,
             ret_norm_g, moe_w1, moe_w3, moe_w2):
    tm, tmc, tq, tscan, tmoe = tiles["tm"], tiles["tmc"], tiles["tq"], tiles["tscan"], tiles["tmoe"]
    x = jnp.concatenate([x_prompt.reshape(geo.tp, D_MODEL), x_sample.reshape(geo.t - geo.tp, D_MODEL)], axis=0)
    c_rows = jnp.concatenate([jnp.repeat(c_prompt, geo.sp // geo.seg, axis=0),
                              jnp.repeat(c_sample, geo.ss // geo.seg, axis=0)], axis=0)
    mod = _ada_call(c_rows, ada_w, ada_b)

    bias = _bias_call(rel_bias, tq)
    cos, sin = _rope_tables(max(geo.sp, geo.ss))
    rw_t = router_w.T
    rb = router_b.reshape(N_EXPERTS, 1)
    w1 = moe_w1.astype(MXU_DT)
    w3 = moe_w3.astype(MXU_DT)
    w2 = moe_w2.astype(MXU_DT)
    xs = jnp.zeros((geo.t + N_BUCKETS * tmoe, HX_W), jnp.uint32)

    for l in range(DEPTH):
        j = l // 2
        mod_l = mod[l]
        g_mix = norm_mix_g[l].reshape(1, D_MODEL)
        g_ffn = norm_ffn_g[l].reshape(1, D_MODEL)
        if l % 2 == 0:
            lam_init = 0.8 - 0.6 * math.exp(-0.3 * l)
            w_pad = jnp.pad(w_in_ab[j], ((0, 0), (0, AB_IN_PAD - AB_IN))).astype(MXU_DT)
            qg = jnp.tile(q_norm_g[j], 2).reshape(1, LANES)
            kg = jnp.tile(k_norm_g[j], 2).reshape(1, LANES)
            gate_b = jnp.pad(mlstm_gate_b[j].reshape(1, B_HEADS * N_GATES), ((0, 0), (0, LANES - B_HEADS * N_GATES)))
            qa, ka, va, qb, kb, vb, ob, gates = _inproj_ab_call(
                geo, tm, x, mod_l, g_mix, w_pad, qg, kg, gate_b, mlstm_conv_w[j],
                mlstm_conv_b[j].reshape(1, 2 * B_WIDTH))
            sc, bounded_ok = _attn_scalars(bias, tq, q_norm_g[j], k_norm_g[j])
            ya = lax.cond(bounded_ok,
                          functools.partial(_attn_call, geo, tq, lam_init, True),
                          functools.partial(_attn_call, geo, tq, lam_init, False),
                          qa, ka, va, bias, sc, diff_lambda[j], diff_norm_g[j].reshape(1, A_V_DIM))
            yb = _mlstm_call(geo, tscan, qb, kb, vb, gates, ob, mlstm_norm_g[j].reshape(1, B_DIM))
            w_o = w_out_ab[j].astype(MXU_DT)
            ys, ws = [ya, yb], [w_o[:A_WIDTH], w_o[A_WIDTH:]]
        else:
            q, k, v, gt = _inproj_c_call(geo, tmc, x, mod_l, g_mix, w_in_c[j].astype(MXU_DT), cos, sin)
            y = _ret_call(geo, tscan, q, k, v, ret_decay_logit[j], gt, ret_norm_g[j].reshape(1, C_V_DIM))
            ys, ws = [y], [w_out_c[j].astype(MXU_DT)]
        x, hx, bucket = _outproj_call(geo, tm, ys, ws, x, mod_l, g_ffn, rw_t, rb)
        pos, te0, te1, n_tiles = _plan_call(tmoe, bucket)
        xs = _dispatch_call(tm, pos, hx, xs)
        ysort = _moe_call(tmoe, l, te0, te1, n_tiles, xs, w1, w3, w2)
        x = _collect_call(geo, tm, pos, ysort, x, mod_l)

    y_prompt = x[:geo.tp].reshape(x_prompt.shape)
    y_sample = x[geo.tp:].reshape(x_sample.shape)
    return (y_prompt, y_sample)


def kernel(x_prompt, x_sample, c_prompt, c_sample, rel_bias, router_w, router_b, ada_w, ada_b, norm_mix_g, norm_ffn_g, w_in_ab, w_out_ab, q_norm_g, k_norm_g, diff_lambda, diff_norm_g, mlstm_conv_w, mlstm_conv_b, mlstm_gate_b, mlstm_norm_g, w_in_c, w_out_c, ret_decay_logit, ret_norm_g, moe_w1, moe_w3, moe_w2):
    geo = Geo(x_prompt.shape[0], x_prompt.shape[1], x_sample.shape[0], x_sample.shape[1])
    return _forward(geo, _tiles(geo), x_prompt, x_sample, c_prompt, c_sample, rel_bias, router_w, router_b,
                    ada_w, ada_b, norm_mix_g, norm_ffn_g, w_in_ab, w_out_ab, q_norm_g, k_norm_g, diff_lambda,
                    diff_norm_g, mlstm_conv_w, mlstm_conv_b, mlstm_gate_b, mlstm_norm_g, w_in_c, w_out_c,
                    ret_decay_logit, ret_norm_g, moe_w1, moe_w3, moe_w2)
```

```python
import functools
import math

import jax
import jax.numpy as jnp
import numpy as np
from jax import lax
from jax.experimental import pallas as pl
from jax.experimental.pallas import tpu as pltpu

F32 = jnp.float32
MXU_DT = jnp.bfloat16
ACT_DT = jnp.bfloat16
HI = lax.Precision.HIGHEST

D_MODEL = 1024
DEPTH = 4
A_HEADS = 4
A_QK_DIM = 64
A_V_DIM = 128
A_QK_WIDTH = 512
A_WIDTH = 512
B_HEADS = 4
B_DIM = 128
B_WIDTH = 512
N_GATES = 4
B_CONV = 3
AB_IN = 3600
AB_IN_PAD = 3712
C_HEADS = 4
C_QK_DIM = 256
C_V_DIM = 512
C_QK_WIDTH = 1024
C_V_WIDTH = 2048
C_IN = 6144
CHUNK = 128
KB_PER_STEP = 8
RET_CHUNK = 256
REL_BUCKETS = 32
REL_MAX_DIST = 128
N_EXPERTS = 16
N_GROUPS = 4
EPG = 4
N_PAIRS = 6
N_BUCKETS = N_GROUPS * N_PAIRS
D_FF = 512
ROPE_BASE = 10000.0
EPS = 1e-6
LANES = 128
SUBLANES = 8
NEG_BIG = -1e30
LOG2E = math.log2(math.e)
HX_H = D_MODEL // 2
HX_W = HX_H + LANES
VMEM_LIMIT = 56 * 1024 * 1024


def _cparams(sem):
    return pltpu.CompilerParams(dimension_semantics=sem, vmem_limit_bytes=VMEM_LIMIT)


def _mm(a, b):
    return jnp.dot(a.astype(MXU_DT), b.astype(MXU_DT), preferred_element_type=F32)


def _mm_nt(a, b):
    return lax.dot_general(a.astype(MXU_DT), b.astype(MXU_DT), (((1,), (1,)), ((), ())),
                           preferred_element_type=F32)


def _mm_tn(a, b):
    return lax.dot_general(a.astype(MXU_DT), b.astype(MXU_DT), (((0,), (0,)), ((), ())),
                           preferred_element_type=F32)


def _mm_hi(a, b):
    return jnp.dot(a, b, precision=HI, preferred_element_type=F32)


def _silu(x):
    return x * (1.0 / (1.0 + jnp.exp(-x)))


def _sigmoid(x):
    return 1.0 / (1.0 + jnp.exp(-x))


def _log_sigmoid(x):
    return jnp.minimum(x, 0.0) - jnp.log1p(jnp.exp(-jnp.abs(x)))


def _rms(x, g):
    return x * lax.rsqrt(jnp.mean(x * x, axis=-1, keepdims=True) + EPS) * g


class Geo:
    def __init__(self, bp, sp, bs, ss):
        self.bp, self.sp, self.bs, self.ss = bp, sp, bs, ss
        self.tp = bp * sp
        self.t = bp * sp + bs * ss
        self.seg = math.gcd(sp, ss)

    def seq_start(self, row0):
        return jnp.where(row0 < self.tp, row0 % self.sp == 0, (row0 - self.tp) % self.ss == 0)

    def pos_block(self, blk, rows):
        nbp = self.tp // rows
        return jnp.where(blk < nbp, blk % (self.sp // rows), (blk - nbp) % (self.ss // rows))


def _ada_kernel(c_ref, w_ref, b_ref, o_ref):
    o_ref[...] = _mm_hi(_silu(c_ref[...]), w_ref[...]) + b_ref[...]


def _ada_call(c_rows, ada_w, ada_b):
    r = c_rows.shape[0]
    nb = 1536
    out = pl.pallas_call(
        _ada_kernel,
        grid=(DEPTH, 6 * D_MODEL // nb),
        in_specs=[pl.BlockSpec((r, D_MODEL), lambda l, n: (0, 0)),
                  pl.BlockSpec((None, D_MODEL, nb), lambda l, n: (l, 0, n)),
                  pl.BlockSpec((None, 1, nb), lambda l, n: (l, 0, n))],
        out_specs=pl.BlockSpec((None, r, nb), lambda l, n: (l, 0, n)),
        out_shape=jax.ShapeDtypeStruct((DEPTH, r, 6 * D_MODEL), F32),
        compiler_params=_cparams(("arbitrary", "arbitrary")),
        name="ada_mod",
    )(c_rows, ada_w, ada_b.reshape(DEPTH, 1, 6 * D_MODEL))
    return out.reshape(DEPTH, r, 6, D_MODEL)


def _t5_bucket(rel):
    nb = REL_BUCKETS // 2
    max_exact = nb // 2
    ret = jnp.where(rel > 0, nb, 0)
    n = jnp.abs(rel)
    nf = jnp.maximum(n, 1).astype(jnp.float32)
    large = max_exact + (jnp.log(nf / max_exact) / math.log(REL_MAX_DIST / max_exact) * (nb - max_exact)).astype(jnp.int32)
    large = jnp.minimum(large, nb - 1)
    return ret + jnp.where(n < max_exact, n, large)


def _bias_kernel(tq, rb_ref, bk_ref, o_ref):
    h = pl.program_id(0)
    bk = bk_ref[...]
    row = jnp.zeros(bk.shape, F32)
    bmax = rb_ref[0, h]
    for b in range(REL_BUCKETS):
        row = row + jnp.where(bk == b, rb_ref[b, h], 0.0)
        bmax = jnp.maximum(bmax, rb_ref[b, h])
    row = (row - bmax) * LOG2E
    table = jnp.broadcast_to(row, (tq, 2 * tq))
    o_ref[...] = pltpu.roll(table, tq + 1, axis=1, stride=1, stride_axis=0)[:, :tq]


def _bias_call(rel_bias, tq):
    rel = (jnp.arange(-1, 2, dtype=jnp.int32) * tq)[:, None, None] + (jnp.arange(2 * tq, dtype=jnp.int32) - (tq - 1))
    buckets = _t5_bucket(rel)
    return pl.pallas_call(
        functools.partial(_bias_kernel, tq),
        grid=(A_HEADS, 3),
        in_specs=[pl.BlockSpec(memory_space=pltpu.SMEM),
                  pl.BlockSpec((None, 1, 2 * tq), lambda h, o: (o, 0, 0))],
        out_specs=pl.BlockSpec((None, None, tq, tq), lambda h, o: (h, o, 0, 0)),
        out_shape=jax.ShapeDtypeStruct((A_HEADS, 3, tq, tq), F32),
        compiler_params=_cparams(("arbitrary", "arbitrary")),
        name="rel_bias_tiles",
    )(rel_bias, buckets)


def _rope_tables(s):
    d = C_QK_DIM
    inv = ROPE_BASE ** (-jnp.arange(0, d, 2, dtype=jnp.float32) / d)
    ang = jnp.arange(s, dtype=jnp.float32)[:, None] * inv[None, :]
    return jnp.cos(ang), jnp.sin(ang)


def _half_rms(z, g):
    lo_lane = lax.broadcasted_iota(jnp.int32, (1, LANES), 1) < A_QK_DIM
    z2 = z * z
    tot = jnp.sum(z2, axis=-1, keepdims=True)
    lo = jnp.sum(jnp.where(lo_lane, z2, 0.0), axis=-1, keepdims=True)
    ms = jnp.where(lo_lane, lo, tot - lo) * (1.0 / A_QK_DIM)
    return z * lax.rsqrt(ms + EPS) * g


def _inproj_ab_kernel(geo, tm, x_ref, xp_ref, xn_ref, mod_ref, g_ref, w_ref, qg_ref, kg_ref, gb_ref, cw_ref, cb_ref,
                      qa_ref, ka_ref, va_ref, qb_ref, kb_ref, vb_ref, ob_ref, gt_ref):
    norm = lambda x: _rms(x, g_ref[...]) * (1.0 + mod_ref[1:2, :]) + mod_ref[0:1, :]
    hb = norm(x_ref[...]).astype(MXU_DT)
    qscale = (A_QK_DIM ** -0.5) * LOG2E
    for hd in range(A_HEADS):
        c0 = hd * LANES
        q = jnp.dot(hb, w_ref[:, c0:c0 + LANES], preferred_element_type=F32)
        qa_ref[:, c0:c0 + LANES] = (_half_rms(q, qg_ref[...]) * qscale).astype(qa_ref.dtype)
        k = jnp.dot(hb, w_ref[:, A_QK_WIDTH + c0:A_QK_WIDTH + c0 + LANES], preferred_element_type=F32)
        ka_ref[:, c0:c0 + LANES] = _half_rms(k, kg_ref[...]).astype(ka_ref.dtype)
    o = 2 * A_QK_WIDTH
    va_ref[...] = jnp.dot(hb, w_ref[:, o:o + A_WIDTH], preferred_element_type=F32).astype(va_ref.dtype)
    o += A_WIDTH
    halo = jnp.concatenate([norm(xp_ref[...]), norm(xn_ref[...])], axis=0).astype(MXU_DT)
    qk = jnp.dot(jnp.concatenate([hb, halo], axis=0), w_ref[:, o:o + 2 * B_WIDTH], preferred_element_type=F32)
    row0 = pl.program_id(0) * tm
    first = geo.seq_start(row0)
    last = geo.seq_start(row0 + tm) | (row0 + tm == geo.t)
    prev_row = jnp.where(first, 0.0, qk[tm + SUBLANES - 1:tm + SUBLANES, :])
    next_row = jnp.where(last, 0.0, qk[tm + SUBLANES:tm + SUBLANES + 1, :])
    cur = qk[:tm]
    ridx = lax.broadcasted_iota(jnp.int32, (tm, 1), 0)
    x_prev = jnp.where(ridx == 0, prev_row, pltpu.roll(cur, 1, axis=0))
    x_next = jnp.where(ridx == tm - 1, next_row, pltpu.roll(cur, tm - 1, axis=0))
    y = _silu(x_prev * cw_ref[0:1, :] + cur * cw_ref[1:2, :] + x_next * cw_ref[2:3, :] + cb_ref[...])
    qb_ref[...] = y[:, :B_WIDTH].astype(qb_ref.dtype)
    kb_ref[...] = (y[:, B_WIDTH:] * (B_DIM ** -0.5)).astype(kb_ref.dtype)
    o += 2 * B_WIDTH
    vb_ref[...] = jnp.dot(hb, w_ref[:, o:o + B_WIDTH], preferred_element_type=F32).astype(vb_ref.dtype)
    o += B_WIDTH
    ob_ref[...] = jnp.dot(hb, w_ref[:, o:o + B_WIDTH], preferred_element_type=F32)
    o += B_WIDTH
    gt_ref[...] = jnp.dot(hb, w_ref[:, o:o + LANES], preferred_element_type=F32) + gb_ref[...]


def _inproj_ab_call(geo, tm, x, mod_l, g, w_pad, qg, kg, gate_b, conv_w, conv_b):
    t = geo.t
    per_seg = geo.seg // tm
    r8 = tm // SUBLANES
    nb8 = t // SUBLANES
    row = lambda i: (i, 0)
    const = lambda i: (0, 0)
    widths = (A_QK_WIDTH, A_QK_WIDTH, A_WIDTH, B_WIDTH, B_WIDTH, B_WIDTH, B_WIDTH, LANES)
    dtypes = (ACT_DT, ACT_DT, ACT_DT, ACT_DT, ACT_DT, ACT_DT, F32, F32)
    return pl.pallas_call(
        functools.partial(_inproj_ab_kernel, geo, tm),
        grid=(t // tm,),
        in_specs=[pl.BlockSpec((tm, D_MODEL), row),
                  pl.BlockSpec((SUBLANES, D_MODEL), lambda i: (jnp.maximum(i * r8 - 1, 0), 0)),
                  pl.BlockSpec((SUBLANES, D_MODEL), lambda i: (jnp.minimum((i + 1) * r8, nb8 - 1), 0)),
                  pl.BlockSpec((None, 6, D_MODEL), lambda i: (i // per_seg, 0, 0)),
                  pl.BlockSpec((1, D_MODEL), const),
                  pl.BlockSpec((D_MODEL, AB_IN_PAD), const),
                  pl.BlockSpec((1, LANES), const),
                  pl.BlockSpec((1, LANES), const),
                  pl.BlockSpec((1, LANES), const),
                  pl.BlockSpec((B_CONV, 2 * B_WIDTH), const),
                  pl.BlockSpec((1, 2 * B_WIDTH), const)],
        out_specs=[pl.BlockSpec((tm, w), row) for w in widths],
        out_shape=[jax.ShapeDtypeStruct((t, w), d) for w, d in zip(widths, dtypes)],
        compiler_params=_cparams(("parallel",)),
        name="inproj_ab",
    )(x, x, x, mod_l, g, w_pad, qg, kg, gate_b, conv_w, conv_b)


def _attn_finish(lam_init, acc0, l0, acc1, l1, dl_ref, ng_ref, o_ref):
    dl = dl_ref[...]
    lam = (jnp.exp(jnp.sum(dl[0:1] * dl[1:2], axis=-1, keepdims=True))
           - jnp.exp(jnp.sum(dl[2:3] * dl[3:4], axis=-1, keepdims=True)) + lam_init)
    out = acc0 / l0 - lam * (acc1 / l1)
    o_ref[...] = (_rms(out, ng_ref[...]) * (1.0 - lam_init)).astype(o_ref.dtype)


def _attn_kernel(lam_init, nk, q_ref, k_ref, v_ref, bias_ref, sc_ref, dl_ref, ng_ref, o_ref,
                 m_sc, l_sc, acc_sc):
    h = pl.program_id(1)
    i = pl.program_id(2)
    j = pl.program_id(3)

    @pl.when(j == 0)
    def _():
        m_sc[...] = jnp.full(m_sc.shape, NEG_BIG, F32)
        l_sc[...] = jnp.zeros(l_sc.shape, F32)
        acc_sc[...] = jnp.zeros(acc_sc.shape, F32)

    q = q_ref[...]
    lo_lane = lax.broadcasted_iota(jnp.int32, (1, LANES), 1) < A_QK_DIM
    qsub = (jnp.where(lo_lane, q, jnp.zeros_like(q)), jnp.where(lo_lane, jnp.zeros_like(q), q))
    k = k_ref[...]
    v = v_ref[...]

    def step(near):
        if near:
            shift = 0.0
        else:
            shift = jnp.where(j < i, sc_ref[h, 0], sc_ref[h, 1])
        for sub in range(2):
            s = _mm_nt(qsub[sub], k)
            if near:
                s = s + bias_ref[...]
            m_old = m_sc[sub]
            m_new = jnp.maximum(m_old, jnp.max(s, axis=-1, keepdims=True) + shift)
            p = jnp.exp2(s - (m_new - shift))
            alpha = jnp.exp2(m_old - m_new)
            l_sc[sub] = alpha * l_sc[sub] + jnp.sum(p, axis=-1, keepdims=True)
            acc_sc[sub] = alpha * acc_sc[sub] + _mm(p, v)
            m_sc[sub] = m_new

    near = jnp.abs(j - i) <= 1
    pl.when(near)(lambda: step(True))
    pl.when(jnp.logical_not(near))(lambda: step(False))

    @pl.when(j == nk - 1)
    def _():
        _attn_finish(lam_init, acc_sc[0], l_sc[0], acc_sc[1], l_sc[1], dl_ref, ng_ref, o_ref)


def _attn_bounded_kernel(lam_init, nk, q_ref, k_ref, v_ref, bias_ref, sc_ref, dl_ref, ng_ref, o_ref, acc_sc):
    h = pl.program_id(1)
    i = pl.program_id(2)
    j = pl.program_id(3)
    tq = q_ref.shape[0]

    @pl.when(j == 0)
    def _():
        acc_sc[...] = jnp.zeros(acc_sc.shape, F32)

    q = q_ref[...]
    lo = (lax.broadcasted_iota(jnp.int32, q.shape, 1) < A_QK_DIM).astype(F32).astype(q.dtype)
    qsub = (q * lo, q * (1 - lo))
    ones_col = (lax.broadcasted_iota(jnp.int32, (tq, LANES), 1) == 0).astype(v_ref.dtype)

    nkb = k_ref.shape[0] // tq

    def block(kb, carry):
        jb = j * nkb + kb
        r0 = pl.multiple_of(kb * tq, tq)

        @pl.when(jb == jnp.maximum(i - 1, 0))
        def _():
            acc_sc[...] = acc_sc[...] * sc_ref[h, 2]

        @pl.when(jb == i + 2)
        def _():
            acc_sc[...] = acc_sc[...] * sc_ref[h, 3]

        def step(near):
            k = k_ref[pl.ds(r0, tq), :]
            v_aug = jnp.concatenate([v_ref[pl.ds(r0, tq), :], ones_col], axis=1)
            for sub in range(2):
                s = _mm_nt(qsub[sub], k)
                if near:
                    s = s + bias_ref[jb - i + 1]
                acc_sc[sub] += _mm(jnp.exp2(s), v_aug)

        near = jnp.abs(jb - i) <= 1
        pl.when(near)(functools.partial(step, True))
        pl.when(jnp.logical_not(near))(functools.partial(step, False))
        return carry

    lax.fori_loop(0, nkb, block, 0)

    @pl.when(j == nk - 1)
    def _():
        a0 = acc_sc[0]
        a1 = acc_sc[1]
        _attn_finish(lam_init, a0[:, :A_V_DIM], a0[:, A_V_DIM:A_V_DIM + 1], a1[:, :A_V_DIM],
                     a1[:, A_V_DIM:A_V_DIM + 1], dl_ref, ng_ref, o_ref)


def _attn_scalars(bias, tq, q_gain, k_gain):
    far_l = bias[:, 0, tq - 1, 0]
    far_r = bias[:, 2, 0, tq - 1]
    sc = jnp.stack([far_l, far_r, jnp.exp2(far_l), jnp.exp2(-far_r)], axis=-1)
    bound = A_QK_DIM * jnp.max(jnp.abs(q_gain)) * jnp.max(jnp.abs(k_gain)) * (A_QK_DIM ** -0.5) * LOG2E * 1.02
    spread = -jnp.min(bias)
    ok = bound + 2.0 * spread <= 80.0
    return sc, ok


def _attn_call(geo, tq, lam_init, bounded, qa, ka, va, bias, sc, dl, ng):
    outs = []
    for (nb, s, row_off) in ((geo.bp, geo.sp, 0), (geo.bs, geo.ss, geo.tp)):
        nq = s // tq
        kb = math.gcd(KB_PER_STEP, nq) if bounded else 1
        nk = nq // kb
        off = row_off // tq
        qmap = lambda b, h, i, j, off=off, nq=nq: (off + b * nq + i, h)
        assert off % kb == 0, "the group's first key block must be aligned to the key blocks of one grid step"
        kmap = lambda b, h, i, j, koff=off // kb, nk=nk: (koff + b * nk + j, h)
        omap = lambda b, h, i, j, nq=nq: (b * nq + i, h)
        if bounded:
            body = functools.partial(_attn_bounded_kernel, lam_init, nk)
            scratch = [pltpu.VMEM((2, tq, 2 * LANES), F32)]
            bias_spec = pl.BlockSpec((None, 3, tq, tq), lambda b, h, i, j: (h, 0, 0, 0))
        else:
            body = functools.partial(_attn_kernel, lam_init, nk)
            scratch = [pltpu.VMEM((2, tq, 1), F32), pltpu.VMEM((2, tq, 1), F32), pltpu.VMEM((2, tq, LANES), F32)]
            bias_spec = pl.BlockSpec((None, None, tq, tq), lambda b, h, i, j: (h, jnp.clip(j - i + 1, 0, 2), 0, 0))
        outs.append(pl.pallas_call(
            body,
            grid=(nb, A_HEADS, nq, nk),
            in_specs=[pl.BlockSpec((tq, LANES), qmap),
                      pl.BlockSpec((kb * tq, LANES), kmap),
                      pl.BlockSpec((kb * tq, LANES), kmap),
                      bias_spec,
                      pl.BlockSpec(memory_space=pltpu.SMEM),
                      pl.BlockSpec((4, A_QK_DIM), lambda b, h, i, j: (0, 0)),
                      pl.BlockSpec((1, LANES), lambda b, h, i, j: (0, 0))],
            out_specs=pl.BlockSpec((tq, LANES), omap),
            out_shape=jax.ShapeDtypeStruct((nb * s, A_WIDTH), ACT_DT),
            scratch_shapes=scratch,
            compiler_params=_cparams(("parallel", "parallel", "parallel", "arbitrary")),
            name="diff_attn_bounded" if bounded else "diff_attn",
        )(qa, ka, va, bias, sc, dl, ng))
    return jnp.concatenate(outs, axis=0)


def _tri(lower):
    r = lax.broadcasted_iota(jnp.int32, (CHUNK, CHUNK), 0)
    c = lax.broadcasted_iota(jnp.int32, (CHUNK, CHUNK), 1)
    return (c <= r) if lower else (c >= r)


def _mlstm_kernel(geo, tb, reverse, *refs):
    if reverse:
        q_ref, k_ref, v_ref, g_ref, hf_ref, ob_ref, ng_ref, o_ref, c_sc, m_sc = refs
    else:
        q_ref, k_ref, v_ref, g_ref, o_ref, c_sc, m_sc = refs
    step = pl.program_id(0)
    nblk = geo.t // tb
    blk = (nblk - 1 - step) if reverse else step
    row0 = blk * tb
    if reverse:
        fresh = geo.seq_start(row0 + tb) | (row0 + tb == geo.t)
    else:
        fresh = geo.seq_start(row0)

    @pl.when(fresh)
    def _():
        c_sc[...] = jnp.zeros(c_sc.shape, F32)
        m_sc[...] = jnp.zeros(m_sc.shape, F32)

    mask = _tri(not reverse)
    cum_l = mask.astype(F32)
    cum_r = _tri(reverse).astype(F32)
    ones_col = (lax.broadcasted_iota(jnp.int32, (CHUNK, LANES), 1) == 0).astype(MXU_DT)
    nch = tb // CHUNK

    def chunk(ci, carry):
        c_idx = (nch - 1 - ci) if reverse else ci
        r0 = pl.multiple_of(c_idx * CHUNK, CHUNK)
        g = g_ref[pl.ds(r0, CHUNK), :]
        g_t = g.T
        b_col = _mm_hi(cum_l, _log_sigmoid(g))
        b_row = _mm_hi(_log_sigmoid(g_t), cum_r)
        for hd in range(B_HEADS):
            ci_col = hd * N_GATES + (2 if reverse else 0)
            cf_col = ci_col + 1
            lanes = slice(hd * B_DIM, (hd + 1) * B_DIM)
            q = q_ref[pl.ds(r0, CHUNK), lanes]
            k = k_ref[pl.ds(r0, CHUNK), lanes]
            v = v_ref[pl.ds(r0, CHUNK), lanes]
            bc = b_col[:, cf_col:cf_col + 1]
            br = b_row[cf_col:cf_col + 1, :]
            ic = g[:, ci_col:ci_col + 1]
            ir = g_t[ci_col:ci_col + 1, :]
            m_prev = m_sc[hd]
            log_d = jnp.where(mask, bc - br + ir, NEG_BIG)
            m_inter = bc + m_prev
            m_t = jnp.maximum(jnp.max(log_d, axis=-1, keepdims=True), m_inter)
            s = _mm_nt(q, k) * jnp.exp(log_d - m_t)
            inter = jnp.exp(m_inter - m_t)
            c_aug = c_sc[hd]
            qc = _mm(q, c_aug)
            num = _mm(s, v) + inter * qc[:, :B_DIM]
            den = jnp.sum(s, axis=-1, keepdims=True) + inter * qc[:, B_DIM:B_DIM + 1]
            hout = num / jnp.maximum(jnp.abs(den), jnp.exp(-m_t))
            b_last = bc[0:1, :] if reverse else bc[CHUNK - 1:CHUNK, :]
            log_w = b_last - bc + ic
            m_new = jnp.maximum(b_last + m_prev, jnp.max(log_w, axis=0, keepdims=True))
            w = jnp.exp(log_w - m_new)
            decay = jnp.exp(b_last + m_prev - m_new)
            v_aug = jnp.concatenate([v, ones_col], axis=1)
            c_sc[hd] = decay * c_aug + _mm_tn(k.astype(F32) * w, v_aug)
            m_sc[hd] = m_new
            if reverse:
                hsum = hf_ref[pl.ds(r0, CHUNK), lanes].astype(F32) + hout
                y = _rms(hsum, ng_ref[...]) * _sigmoid(ob_ref[pl.ds(r0, CHUNK), lanes])
                o_ref[pl.ds(r0, CHUNK), lanes] = y.astype(o_ref.dtype)
            else:
                o_ref[pl.ds(r0, CHUNK), lanes] = hout.astype(o_ref.dtype)
        return carry

    lax.fori_loop(0, nch, chunk, 0)


def _mlstm_call(geo, tb, qb, kb, vb, gates, ob, ng):
    t = geo.t
    nblk = t // tb
    fmap = lambda s: (s, 0)
    rmap = lambda s: (nblk - 1 - s, 0)
    scratch = [pltpu.VMEM((B_HEADS, B_DIM, 2 * B_DIM), F32), pltpu.VMEM((B_HEADS, 1, 1), F32)]
    wide = lambda m: pl.BlockSpec((tb, B_WIDTH), m)
    hf = pl.pallas_call(
        functools.partial(_mlstm_kernel, geo, tb, False),
        grid=(nblk,),
        in_specs=[wide(fmap), wide(fmap), wide(fmap), pl.BlockSpec((tb, LANES), fmap)],
        out_specs=wide(fmap),
        out_shape=jax.ShapeDtypeStruct((t, B_WIDTH), ACT_DT),
        scratch_shapes=scratch,
        compiler_params=_cparams(("arbitrary",)),
        name="mlstm_fwd",
    )(qb, kb, vb, gates)
    return pl.pallas_call(
        functools.partial(_mlstm_kernel, geo, tb, True),
        grid=(nblk,),
        in_specs=[wide(rmap), wide(rmap), wide(rmap), pl.BlockSpec((tb, LANES), rmap), wide(rmap), wide(rmap),
                  pl.BlockSpec((1, B_DIM), lambda s: (0, 0))],
        out_specs=wide(rmap),
        out_shape=jax.ShapeDtypeStruct((t, B_WIDTH), ACT_DT),
        scratch_shapes=scratch,
        compiler_params=_cparams(("arbitrary",)),
        name="mlstm_bwd",
    )(qb, kb, vb, gates, hf, ob, ng)


def _inproj_c_kernel(x_ref, mod_ref, g_ref, w_ref, cos_ref, sin_ref, q_ref, k_ref, v_ref, gt_ref, h_sc):
    j = pl.program_id(1)
    nsub = C_HEADS
    sub = w_ref.shape[1] // nsub

    def proj(c):
        return jnp.dot(h_sc[...], w_ref[:, c * sub:(c + 1) * sub], preferred_element_type=F32)

    def rope(o_ref, scale):
        cos = cos_ref[...] * scale
        sin = sin_ref[...] * scale
        half = C_QK_DIM // 2
        for hd in range(C_HEADS):
            y = proj(hd)
            x1, x2 = y[:, :half], y[:, half:]
            o_ref[:, hd * C_QK_DIM:hd * C_QK_DIM + half] = (x1 * cos - x2 * sin).astype(o_ref.dtype)
            o_ref[:, hd * C_QK_DIM + half:(hd + 1) * C_QK_DIM] = (x1 * sin + x2 * cos).astype(o_ref.dtype)

    @pl.when(j == 0)
    def _():
        x = x_ref[...]
        h = _rms(x, g_ref[...]) * (1.0 + mod_ref[1:2, :]) + mod_ref[0:1, :]
        h_sc[...] = h.astype(h_sc.dtype)
        rope(q_ref, 1.0)

    @pl.when(j == 1)
    def _():
        rope(k_ref, C_QK_DIM ** -0.5)

    @pl.when((j == 2) | (j == 3))
    def _():
        for c in range(nsub):
            v_ref[:, c * sub:(c + 1) * sub] = proj(c).astype(v_ref.dtype)

    @pl.when(j >= 4)
    def _():
        for c in range(nsub):
            gt_ref[:, c * sub:(c + 1) * sub] = _silu(proj(c)).astype(gt_ref.dtype)


def _inproj_c_call(geo, tm, x, mod_l, g, w, cos, sin):
    t = geo.t
    per_seg = geo.seg // tm
    nw = D_MODEL
    return pl.pallas_call(
        _inproj_c_kernel,
        grid=(t // tm, C_IN // nw),
        in_specs=[pl.BlockSpec((tm, D_MODEL), lambda i, j: (i, 0)),
                  pl.BlockSpec((None, 6, D_MODEL), lambda i, j: (i // per_seg, 0, 0)),
                  pl.BlockSpec((1, D_MODEL), lambda i, j: (0, 0)),
                  pl.BlockSpec((D_MODEL, nw), lambda i, j: (0, j)),
                  pl.BlockSpec((tm, C_QK_DIM // 2), lambda i, j: (geo.pos_block(i, tm), 0)),
                  pl.BlockSpec((tm, C_QK_DIM // 2), lambda i, j: (geo.pos_block(i, tm), 0))],
        out_specs=[pl.BlockSpec((tm, nw), lambda i, j: (i, 0)),
                   pl.BlockSpec((tm, nw), lambda i, j: (i, 0)),
                   pl.BlockSpec((tm, nw), lambda i, j: (i, jnp.clip(j - 2, 0, 1))),
                   pl.BlockSpec((tm, nw), lambda i, j: (i, jnp.clip(j - 4, 0, 1)))],
        out_shape=[jax.ShapeDtypeStruct((t, C_QK_WIDTH), ACT_DT),
                   jax.ShapeDtypeStruct((t, C_QK_WIDTH), ACT_DT),
                   jax.ShapeDtypeStruct((t, C_V_WIDTH), ACT_DT),
                   jax.ShapeDtypeStruct((t, C_V_WIDTH), ACT_DT)],
        scratch_shapes=[pltpu.VMEM((tm, D_MODEL), MXU_DT)],
        compiler_params=_cparams(("parallel", "arbitrary")),
        name="inproj_c",
    )(x, mod_l, g, w, cos, sin)


def _ret_kernel(geo, tb, reverse, *refs):
    if reverse:
        q_ref, k_ref, v_ref, dlg_ref, yf_ref, gt_ref, ng_ref, o_ref, r_sc, intra_sc, vec_sc = refs
    else:
        q_ref, k_ref, v_ref, dlg_ref, o_ref, r_sc, intra_sc, vec_sc = refs
    step = pl.program_id(0)
    nblk = geo.t // tb
    blk = (nblk - 1 - step) if reverse else step
    row0 = blk * tb
    if reverse:
        fresh = geo.seq_start(row0 + tb) | (row0 + tb == geo.t)
    else:
        fresh = geo.seq_start(row0)

    @pl.when(fresh)
    def _():
        r_sc[...] = jnp.zeros(r_sc.shape, F32)

    @pl.when(step == 0)
    def _():
        lg_all = _log_sigmoid(dlg_ref[...])
        ti = lax.broadcasted_iota(jnp.int32, (RET_CHUNK, RET_CHUNK), 0)
        si = lax.broadcasted_iota(jnp.int32, (RET_CHUNK, RET_CHUNK), 1)
        dist = ((si - ti) if reverse else (ti - si)).astype(F32)
        pos = lax.broadcasted_iota(jnp.int32, (RET_CHUNK, LANES), 0).astype(F32)
        upos = (RET_CHUNK - 1.0 - pos) if reverse else pos
        lane = lax.broadcasted_iota(jnp.int32, (RET_CHUNK, LANES), 1)
        d = 1 if reverse else 0
        for hd in range(C_HEADS):
            lg = lg_all[d:d + 1, hd:hd + 1]
            intra_sc[hd] = jnp.where(dist >= 0, jnp.exp(jnp.maximum(dist, 0.0) * lg), 0.0)
            vec_sc[hd] = jnp.where(lane == 0, jnp.exp((upos + 1.0) * lg),
                                   jnp.where(lane == 1, jnp.exp((RET_CHUNK - 1.0 - upos) * lg),
                                             jnp.exp(RET_CHUNK * lg)))

    nch = tb // RET_CHUNK

    def chunk(ci, carry):
        c_idx = (nch - 1 - ci) if reverse else ci
        r0 = pl.multiple_of(c_idx * RET_CHUNK, RET_CHUNK)
        for hd in range(C_HEADS):
            vec = vec_sc[hd]
            q_scale, k_scale, c_decay = vec[:, 0:1], vec[:, 1:2], vec[0:1, 2:3]
            ql = slice(hd * C_QK_DIM, (hd + 1) * C_QK_DIM)
            vl = slice(hd * C_V_DIM, (hd + 1) * C_V_DIM)
            q = q_ref[pl.ds(r0, RET_CHUNK), ql]
            k = k_ref[pl.ds(r0, RET_CHUNK), ql]
            v = v_ref[pl.ds(r0, RET_CHUNK), vl]
            r_old = r_sc[hd]
            s = _mm_nt(q, k) * intra_sc[hd]
            y = _mm(s, v) + q_scale * _mm(q, r_old)
            r_sc[hd] = c_decay * r_old + _mm_tn(k.astype(F32) * k_scale, v)
            if reverse:
                ysum = yf_ref[pl.ds(r0, RET_CHUNK), vl].astype(F32) + y
                out = _rms(ysum, ng_ref[...]) * gt_ref[pl.ds(r0, RET_CHUNK), vl].astype(F32)
                o_ref[pl.ds(r0, RET_CHUNK), vl] = out.astype(o_ref.dtype)
            else:
                o_ref[pl.ds(r0, RET_CHUNK), vl] = y.astype(o_ref.dtype)
        return carry

    lax.fori_loop(0, nch, chunk, 0)


def _ret_call(geo, tb, q, k, v, decay_logit, gt, ng):
    t = geo.t
    nblk = t // tb
    fmap = lambda s: (s, 0)
    rmap = lambda s: (nblk - 1 - s, 0)
    scratch = [pltpu.VMEM((C_HEADS, C_QK_DIM, C_V_DIM), F32), pltpu.VMEM((C_HEADS, RET_CHUNK, RET_CHUNK), F32),
               pltpu.VMEM((C_HEADS, RET_CHUNK, LANES), F32)]
    qk = lambda m: pl.BlockSpec((tb, C_QK_WIDTH), m)
    vv = lambda m: pl.BlockSpec((tb, C_V_WIDTH), m)
    dspec = pl.BlockSpec((2, C_HEADS), lambda s: (0, 0))
    yf = pl.pallas_call(
        functools.partial(_ret_kernel, geo, tb, False),
        grid=(nblk,),
        in_specs=[qk(fmap), qk(fmap), vv(fmap), dspec],
        out_specs=vv(fmap),
        out_shape=jax.ShapeDtypeStruct((t, C_V_WIDTH), ACT_DT),
        scratch_shapes=scratch,
        compiler_params=_cparams(("arbitrary",)),
        name="ret_fwd",
    )(q, k, v, decay_logit)
    return pl.pallas_call(
        functools.partial(_ret_kernel, geo, tb, True),
        grid=(nblk,),
        in_specs=[qk(rmap), qk(rmap), vv(rmap), dspec, vv(rmap), vv(rmap),
                  pl.BlockSpec((1, C_V_DIM), lambda s: (0, 0))],
        out_specs=vv(rmap),
        out_shape=jax.ShapeDtypeStruct((t, C_V_WIDTH), ACT_DT),
        scratch_shapes=scratch,
        compiler_params=_cparams(("arbitrary",)),
        name="ret_bwd",
    )(q, k, v, decay_logit, yf, gt, ng)


def _route(probs):
    p = [probs[e:e + 1, :] for e in range(N_EXPERTS)]
    scores = []
    for g in range(N_GROUPS):
        a, b, c, d = p[EPG * g:EPG * g + EPG]
        hi1, lo1 = jnp.maximum(a, b), jnp.minimum(a, b)
        hi2, lo2 = jnp.maximum(c, d), jnp.minimum(c, d)
        scores.append(jnp.maximum(hi1, hi2) + jnp.maximum(jnp.minimum(hi1, hi2), jnp.maximum(lo1, lo2)))
    g_sel = jnp.zeros(scores[0].shape, jnp.int32)
    best = scores[0]
    for g in range(1, N_GROUPS):
        better = scores[g] > best
        g_sel = jnp.where(better, g, g_sel)
        best = jnp.where(better, scores[g], best)
    vals = []
    for kk in range(EPG):
        v = p[kk]
        for g in range(1, N_GROUPS):
            v = jnp.where(g_sel == g, p[EPG * g + kk], v)
        vals.append(v)

    def argmax4(xs):
        idx = jnp.zeros(xs[0].shape, jnp.int32)
        top = xs[0]
        for kk in range(1, EPG):
            better = xs[kk] > top
            idx = jnp.where(better, kk, idx)
            top = jnp.where(better, xs[kk], top)
        return idx, top

    i1, v1 = argmax4(vals)
    i2, v2 = argmax4([jnp.where(i1 == kk, -1.0, vals[kk]) for kk in range(EPG)])
    tot = v1 + v2
    w1, w2 = v1 / tot, v2 / tot
    e1 = g_sel * EPG + i1
    e2 = g_sel * EPG + i2
    eidx = lax.broadcasted_iota(jnp.int32, probs.shape, 0)
    lo, hi = jnp.minimum(i1, i2), jnp.maximum(i1, i2)
    pair = jnp.where(lo == 0, hi - 1, jnp.where(lo == 1, hi + 1, N_PAIRS - 1))
    return jnp.where(eidx == e1, w1, 0.0) + jnp.where(eidx == e2, w2, 0.0), g_sel * N_PAIRS + pair


def _pack_pairs(h):
    n = h.shape[1] // 2
    bits = lax.bitcast_convert_type(h.astype(MXU_DT).astype(F32), jnp.uint32)
    return (bits[:, :n] & jnp.uint32(0xFFFF0000)) | (bits[:, n:] >> 16)


def _unpack_pairs(u):
    hi = lax.bitcast_convert_type(u & jnp.uint32(0xFFFF0000), F32)
    lo = lax.bitcast_convert_type(u << 16, F32)
    return jnp.concatenate([hi, lo], axis=1).astype(MXU_DT)


def _outproj_kernel(nparts, *refs):
    y_refs = refs[:nparts]
    w_refs = refs[nparts:2 * nparts]
    x_ref, mod_ref, g_ref, rw_ref, rb_ref, xo_ref, hx_ref, bkt_ref = refs[2 * nparts:]
    m = jnp.dot(y_refs[0][...], w_refs[0][...], preferred_element_type=F32)
    for p in range(1, nparts):
        m = m + jnp.dot(y_refs[p][...], w_refs[p][...], preferred_element_type=F32)
    x = x_ref[...] + mod_ref[2:3, :] * m
    xo_ref[...] = x
    h = _rms(x, g_ref[...]) * (1.0 + mod_ref[4:5, :]) + mod_ref[3:4, :]
    hx_ref[:, :HX_H] = _pack_pairs(h)
    logits = lax.dot_general(rw_ref[...], h, (((1,), (1,)), ((), ())), precision=HI,
                             preferred_element_type=F32) + rb_ref[...]
    z = jnp.exp(logits - jnp.max(logits, axis=0, keepdims=True))
    probs = z / jnp.sum(z, axis=0, keepdims=True)
    cmb, bucket = _route(probs)
    pad = jnp.zeros((LANES - N_EXPERTS, cmb.shape[1]), F32)
    hx_ref[:, HX_H:] = lax.bitcast_convert_type(jnp.concatenate([cmb, pad], axis=0).T, jnp.uint32)
    bkt_ref[...] = bucket


def _outproj_call(geo, tm, ys, ws, x, mod_l, g, rw_t, rb):
    t = geo.t
    per_seg = geo.seg // tm
    row = lambda i: (i, 0)
    const = lambda i: (0, 0)
    n = len(ys)
    return pl.pallas_call(
        functools.partial(_outproj_kernel, n),
        grid=(t // tm,),
        in_specs=([pl.BlockSpec((tm, y.shape[1]), row) for y in ys]
                  + [pl.BlockSpec(w.shape, const) for w in ws]
                  + [pl.BlockSpec((tm, D_MODEL), row),
                     pl.BlockSpec((None, 6, D_MODEL), lambda i: (i // per_seg, 0, 0)),
                     pl.BlockSpec((1, D_MODEL), const),
                     pl.BlockSpec((N_EXPERTS, D_MODEL), const),
                     pl.BlockSpec((N_EXPERTS, 1), const)]),
        out_specs=[pl.BlockSpec((tm, D_MODEL), row), pl.BlockSpec((tm, HX_W), row),
                   pl.BlockSpec((None, 1, tm), lambda i: (i, 0, 0))],
        out_shape=[jax.ShapeDtypeStruct((t, D_MODEL), F32), jax.ShapeDtypeStruct((t, HX_W), jnp.uint32),
                   jax.ShapeDtypeStruct((t // tm, 1, tm), jnp.int32)],
        compiler_params=_cparams(("parallel",)),
        name="outproj_router",
    )(*ys, *ws, x, mod_l, g, rw_t, rb)


def _plan_kernel(tm, bkt_ref, pos_ref, te_ref, nv_ref):
    nblk, _, blk = bkt_ref.shape
    nrow = 32
    bid = lax.broadcasted_iota(jnp.int32, (nrow, blk), 0)
    r = lax.broadcasted_iota(jnp.int32, (blk, blk), 0)
    c = lax.broadcasted_iota(jnp.int32, (blk, blk), 1)
    prefix = (r <= c).astype(MXU_DT)

    def count(b, acc):
        return acc + jnp.sum((bkt_ref[b] == bid).astype(F32), axis=1, keepdims=True)

    counts = lax.fori_loop(0, nblk, count, jnp.zeros((nrow, 1), F32))
    padded = jnp.floor((counts + (tm - 1.0)) / tm) * tm
    rows = lax.broadcasted_iota(jnp.int32, (nrow, 1), 0)
    offs = jnp.zeros((nrow, 1), F32)
    ends = []
    run = jnp.zeros((1, 1), F32)
    for b in range(N_BUCKETS):
        offs = jnp.where(rows == b, run, offs)
        run = run + padded[b:b + 1, :]
        ends.append(run)

    def place(b, carry):
        oh = (bkt_ref[b] == bid).astype(F32)
        pre = jnp.dot(oh.astype(MXU_DT), prefix, preferred_element_type=F32)
        pos = jnp.sum(oh * (offs + carry + pre - 1.0), axis=0, keepdims=True)
        pos_ref[b] = pos.astype(jnp.int32)
        return carry + pre[:, blk - 1:blk]

    lax.fori_loop(0, nblk, place, jnp.zeros((nrow, 1), F32))
    start = lax.broadcasted_iota(jnp.int32, (1, te_ref.shape[1]), 1).astype(F32) * tm
    tb = jnp.zeros(start.shape, F32)
    for b in range(N_BUCKETS - 1):
        tb = tb + (ends[b] <= start).astype(F32)
    grp = jnp.floor((tb + 0.5) / N_PAIRS)
    pair = tb - grp * N_PAIRS
    lo = (pair >= 3).astype(F32) + (pair >= 5).astype(F32)
    hi = jnp.where(pair == 0, 1.0, jnp.where((pair == 1) | (pair == 3), 2.0, 3.0))
    te_ref[0:1, :] = (grp * EPG + lo).astype(jnp.int32)
    te_ref[1:2, :] = (grp * EPG + hi).astype(jnp.int32)
    nv_ref[...] = jnp.broadcast_to(run / tm, nv_ref.shape).astype(jnp.int32)


def _plan_call(tm, bkt):
    nblk, _, blk = bkt.shape
    nt = nblk * blk // tm + N_BUCKETS
    ntp = -(-nt // LANES) * LANES
    pos, te, nv = pl.pallas_call(
        functools.partial(_plan_kernel, tm),
        out_shape=[jax.ShapeDtypeStruct(bkt.shape, jnp.int32), jax.ShapeDtypeStruct((2, ntp), jnp.int32),
                   jax.ShapeDtypeStruct((1, LANES), jnp.int32)],
        compiler_params=pltpu.CompilerParams(vmem_limit_bytes=VMEM_LIMIT),
        name="moe_plan",
    )(bkt)
    return pos, te[0, :nt], te[1, :nt], nv[0, :1]


def _dispatch_kernel(tb, pos_ref, hx_ref, init_ref, xs_ref, sem):
    del init_ref

    def issue(g, carry):
        base = pl.multiple_of(g * SUBLANES, SUBLANES)
        rows = hx_ref.at[pl.ds(base, SUBLANES)]
        for r in range(SUBLANES):
            pltpu.make_async_copy(rows.at[pl.ds(r, 1)], xs_ref.at[pl.ds(pos_ref[0, base + r], 1)],
                                  sem).start(priority=r % 2)
        return carry

    lax.fori_loop(0, tb // SUBLANES, issue, 0)
    pltpu.make_async_copy(hx_ref, xs_ref.at[pl.ds(0, tb)], sem).wait()


def _dispatch_call(tb, pos, hx, xs_init):
    nsteps = hx.shape[0] // tb
    return pl.pallas_call(
        functools.partial(_dispatch_kernel, tb),
        grid=(nsteps,),
        in_specs=[pl.BlockSpec((None, 1, tb), lambda i: (i, 0, 0), memory_space=pltpu.SMEM),
                  pl.BlockSpec((tb, HX_W), lambda i: (i, 0)),
                  pl.BlockSpec(memory_space=pl.ANY)],
        out_specs=pl.BlockSpec(memory_space=pl.ANY),
        out_shape=jax.ShapeDtypeStruct(xs_init.shape, xs_init.dtype),
        scratch_shapes=[pltpu.SemaphoreType.DMA(())],
        input_output_aliases={2: 0},
        compiler_params=pltpu.CompilerParams(dimension_semantics=("arbitrary",), disable_bounds_checks=True,
                                             has_side_effects=True),
        name="moe_dispatch",
    )(pos, hx, xs_init)


def _moe_kernel(te0_ref, te1_ref, nv_ref, xs_ref, w1a_ref, w3a_ref, w2a_ref, w1b_ref, w3b_ref, w2b_ref, o_ref):
    n = pl.program_id(0)

    @pl.when(n < nv_ref[0])
    def _():
        h = _unpack_pairs(xs_ref[:, :HX_H])
        cmb = lax.bitcast_convert_type(xs_ref[:, HX_H:], F32)
        lane = lax.broadcasted_iota(jnp.int32, cmb.shape, 1)

        def expert(e_id, w1_ref, w3_ref, w2_ref):
            a = jnp.dot(h, w1_ref[...], preferred_element_type=F32)
            b = jnp.dot(h, w3_ref[...], preferred_element_type=F32)
            c = jnp.sum(jnp.where(lane == e_id, cmb, 0.0), axis=-1, keepdims=True)
            return c * _mm(_silu(a) * b, w2_ref[...])

        o_ref[...] = (expert(te0_ref[n], w1a_ref, w3a_ref, w2a_ref)
                      + expert(te1_ref[n], w1b_ref, w3b_ref, w2b_ref))

    @pl.when(n >= nv_ref[0])
    def _():
        o_ref[...] = jnp.zeros(o_ref.shape, F32)


def _moe_call(tm, layer, te0, te1, nv, xs, w1, w3, w2):
    rows = xs.shape[0]
    nt = rows // tm
    tile = lambda n, te0, te1, nv: (jnp.minimum(n, nv[0] - 1), 0)
    wa = lambda n, te0, te1, nv: (layer, te0[jnp.minimum(n, nv[0] - 1)], 0, 0)
    wb = lambda n, te0, te1, nv: (layer, te1[jnp.minimum(n, nv[0] - 1)], 0, 0)
    up = lambda m: pl.BlockSpec((None, None, D_MODEL, D_FF), m)
    down = lambda m: pl.BlockSpec((None, None, D_FF, D_MODEL), m)
    return pl.pallas_call(
        _moe_kernel,
        grid_spec=pltpu.PrefetchScalarGridSpec(
            num_scalar_prefetch=3,
            grid=(nt,),
            in_specs=[pl.BlockSpec((tm, HX_W), tile), up(wa), up(wa), down(wa), up(wb), up(wb), down(wb)],
            out_specs=pl.BlockSpec((tm, D_MODEL), lambda n, te0, te1, nv: (n, 0))),
        out_shape=jax.ShapeDtypeStruct((rows, D_MODEL), F32),
        compiler_params=_cparams(("arbitrary",)),
        name="moe",
    )(te0, te1, nv, xs, w1, w3, w2, w1, w3, w2)


def _collect_kernel(tb, nsteps, pos_ref, posn_ref, ys_ref, x_ref, mod_ref, o_ref, buf, sem):
    i = pl.program_id(0)
    slot = i % 2

    def issue(p_ref, s):
        def body(g, carry):
            base = pl.multiple_of(g * SUBLANES, SUBLANES)
            rows = buf.at[s, pl.ds(base, SUBLANES)]
            for r in range(SUBLANES):
                pltpu.make_async_copy(ys_ref.at[pl.ds(p_ref[0, base + r], 1)], rows.at[pl.ds(r, 1)],
                                      sem.at[s]).start(priority=r % 2)
            return carry
        lax.fori_loop(0, tb // SUBLANES, body, 0)

    pl.when(i == 0)(lambda: issue(pos_ref, 0))
    pl.when(i + 1 < nsteps)(lambda: issue(posn_ref, 1 - slot))
    pltpu.make_async_copy(ys_ref.at[pl.ds(0, tb)], buf.at[slot], sem.at[slot]).wait()
    o_ref[...] = x_ref[...] + mod_ref[5:6, :] * buf[slot]


def _collect_call(geo, tb, pos, ys, x, mod_l):
    t = geo.t
    nsteps = t // tb
    per_seg = geo.seg // tb
    return pl.pallas_call(
        functools.partial(_collect_kernel, tb, nsteps),
        grid=(nsteps,),
        in_specs=[pl.BlockSpec((None, 1, tb), lambda i: (i, 0, 0), memory_space=pltpu.SMEM),
                  pl.BlockSpec((None, 1, tb), lambda i: (jnp.minimum(i + 1, nsteps - 1), 0, 0),
                               memory_space=pltpu.SMEM),
                  pl.BlockSpec(memory_space=pl.ANY),
                  pl.BlockSpec((tb, D_MODEL), lambda i: (i, 0)),
                  pl.BlockSpec((None, 6, D_MODEL), lambda i: (i // per_seg, 0, 0))],
        out_specs=pl.BlockSpec((tb, D_MODEL), lambda i: (i, 0)),
        out_shape=jax.ShapeDtypeStruct((t, D_MODEL), F32),
        scratch_shapes=[pltpu.VMEM((2, tb, D_MODEL), F32), pltpu.SemaphoreType.DMA((2,))],
        compiler_params=pltpu.CompilerParams(dimension_semantics=("arbitrary",), vmem_limit_bytes=VMEM_LIMIT,
                                             disable_bounds_checks=True),
        name="moe_collect",
    )(pos, pos, ys, x, mod_l)


def _tiles(geo):
    seg = geo.seg
    return dict(tm=min(512, seg), tmc=min(1024, seg), tq=min(1024, seg), tscan=min(512, seg), tmoe=min(512, seg))


def _forward(geo, tiles, x_prompt, x_sample, c_prompt, c_sample, rel_bias, router_w, router_b, ada_w, ada_b,
             norm_mix_g, norm_ffn_g, w_in_ab, w_out_ab, q_norm_g, k_norm_g, diff_lambda, diff_norm_g,
             mlstm_conv_w, mlstm_conv_b, mlstm_gate_b, mlstm_norm_g, w_in_c, w_out_c, ret_decay_logit,
             ret_norm_g, moe_w1, moe_w3, moe_w2):
    tm, tmc, tq, tscan, tmoe = tiles["tm"], tiles["tmc"], tiles["tq"], tiles["tscan"], tiles["tmoe"]
    x = jnp.concatenate([x_prompt.reshape(geo.tp, D_MODEL), x_sample.reshape(geo.t - geo.tp, D_MODEL)], axis=0)
    c_rows = jnp.concatenate([jnp.repeat(c_prompt, geo.sp // geo.seg, axis=0),
                              jnp.repeat(c_sample, geo.ss // geo.seg, axis=0)], axis=0)
    mod = _ada_call(c_rows, ada_w, ada_b)

    bias = _bias_call(rel_bias, tq)
    cos, sin = _rope_tables(max(geo.sp, geo.ss))
    rw_t = router_w.T
    rb = router_b.reshape(N_EXPERTS, 1)
    w1 = moe_w1.astype(MXU_DT)
    w3 = moe_w3.astype(MXU_DT)
    w2 = moe_w2.astype(MXU_DT)
    xs = jnp.zeros((geo.t + N_BUCKETS * tmoe, HX_W), jnp.uint32)

    for l in range(DEPTH):
        j = l // 2
        mod_l = mod[l]
        g_mix = norm_mix_g[l].reshape(1, D_MODEL)
        g_ffn = norm_ffn_g[l].reshape(1, D_MODEL)
        if l % 2 == 0:
            lam_init = 0.8 - 0.6 * math.exp(-0.3 * l)
            w_pad = jnp.pad(w_in_ab[j], ((0, 0), (0, AB_IN_PAD - AB_IN))).astype(MXU_DT)
            qg = jnp.tile(q_norm_g[j], 2).reshape(1, LANES)
            kg = jnp.tile(k_norm_g[j], 2).reshape(1, LANES)
            gate_b = jnp.pad(mlstm_gate_b[j].reshape(1, B_HEADS * N_GATES), ((0, 0), (0, LANES - B_HEADS * N_GATES)))
            qa, ka, va, qb, kb, vb, ob, gates = _inproj_ab_call(
                geo, tm, x, mod_l, g_mix, w_pad, qg, kg, gate_b, mlstm_conv_w[j],
                mlstm_conv_b[j].reshape(1, 2 * B_WIDTH))
            sc, bounded_ok = _attn_scalars(bias, tq, q_norm_g[j], k_norm_g[j])
            ya = lax.cond(bounded_ok,
                          functools.partial(_attn_call, geo, tq, lam_init, True),
                          functools.partial(_attn_call, geo, tq, lam_init, False),
                          qa, ka, va, bias, sc, diff_lambda[j], diff_norm_g[j].reshape(1, A_V_DIM))
            yb = _mlstm_call(geo, tscan, qb, kb, vb, gates, ob, mlstm_norm_g[j].reshape(1, B_DIM))
            w_o = w_out_ab[j].astype(MXU_DT)
            ys, ws = [ya, yb], [w_o[:A_WIDTH], w_o[A_WIDTH:]]
        else:
            q, k, v, gt = _inproj_c_call(geo, tmc, x, mod_l, g_mix, w_in_c[j].astype(MXU_DT), cos, sin)
            y = _ret_call(geo, tscan, q, k, v, ret_decay_logit[j], gt, ret_norm_g[j].reshape(1, C_V_DIM))
            ys, ws = [y], [w_out_c[j].astype(MXU_DT)]
        x, hx, bucket = _outproj_call(geo, tm, ys, ws, x, mod_l, g_ffn, rw_t, rb)
        pos, te0, te1, n_tiles = _plan_call(tmoe, bucket)
        xs = _dispatch_call(tm, pos, hx, xs)
        ysort = _moe_call(tmoe, l, te0, te1, n_tiles, xs, w1, w3, w2)
        x = _collect_call(geo, tm, pos, ysort, x, mod_l)

    y_prompt = x[:geo.tp].reshape(x_prompt.shape)
    y_sample = x[geo.tp:].reshape(x_sample.shape)
    return (y_prompt, y_sample)


def kernel(x_prompt, x_sample, c_prompt, c_sample, rel_bias, router_w, router_b, ada_w, ada_b, norm_mix_g, norm_ffn_g, w_in_ab, w_out_ab, q_norm_g, k_norm_g, diff_lambda, diff_norm_g, mlstm_conv_w, mlstm_conv_b, mlstm_gate_b, mlstm_norm_g, w_in_c, w_out_c, ret_decay_logit, ret_norm_g, moe_w1, moe_w3, moe_w2):
    geo = Geo(x_prompt.shape[0], x_prompt.shape[1], x_sample.shape[0], x_sample.shape[1])
    return _forward(geo, _tiles(geo), x_prompt, x_sample, c_prompt, c_sample, rel_bias, router_w, router_b,
                    ada_w, ada_b, norm_mix_g, norm_ffn_g, w_in_ab, w_out_ab, q_norm_g, k_norm_g, diff_lambda,
                    diff_norm_g, mlstm_conv_w, mlstm_conv_b, mlstm_gate_b, mlstm_norm_g, w_in_c, w_out_c,
                    ret_decay_logit, ret_norm_g, moe_w1, moe_w3, moe_w2)
```
